```python
import jax
import jax.numpy as jnp
from jax import lax
import numpy as np

D_MODEL = 2048
BATCH = 8
SEQ = 2048
DEPTH = 4

HEAD_DIM = 128
N_HEADS = D_MODEL // HEAD_DIM
A_HEADS = N_HEADS // 4
B_HEADS = N_HEADS // 4
C_HEADS = N_HEADS - A_HEADS - B_HEADS
IDX_HEADS = 16
IDX_DIM = 64
DSA_TOPK_MAX = 256
DILATED_PATTERNS = ((128, 1), (512, 4), (2048, 16))
C_KV_GROUPS = 2
C_GROUP_SIZE = C_HEADS // C_KV_GROUPS
CMP_BLOCK = 32
CMP_STRIDE = 16
CMP_HIDDEN = 256
SEL_BLOCK = 64
SEL_COUNT = 16
WIN_SIZE = 512
BAND_BLOCK = 128
Q_BLOCK = 128
SEL_Q_CHUNK = 32
D_FF = ((8 * D_MODEL + 2) // 3 + 255) // 256 * 256
ROPE_THETA = 10000.0
RMS_EPS = 1e-6
ATTN_SCALE = HEAD_DIM ** -0.5

IN_WIDTHS = (
    A_HEADS * HEAD_DIM, HEAD_DIM, HEAD_DIM,
    IDX_HEADS * IDX_DIM, IDX_DIM, IDX_HEADS,
    B_HEADS * HEAD_DIM, B_HEADS * HEAD_DIM, B_HEADS * HEAD_DIM,
    C_HEADS * HEAD_DIM,
    C_KV_GROUPS * HEAD_DIM, C_KV_GROUPS * HEAD_DIM,
    C_KV_GROUPS * HEAD_DIM, C_KV_GROUPS * HEAD_DIM,
    C_KV_GROUPS * HEAD_DIM, C_KV_GROUPS * HEAD_DIM,
    3 * C_HEADS,
)
IN_WIDTH = sum(IN_WIDTHS)

kernel_name = "hybrid_dsa_dilated_nsa_block"


def rms_norm(x, g):
    xf = x.astype(jnp.float32)
    y = xf * lax.rsqrt(jnp.mean(xf * xf, axis=-1, keepdims=True) + RMS_EPS)
    return (y * g.astype(jnp.float32)).astype(x.dtype)


def rope_tables(seq_len, dim):
    inv = 1.0 / (ROPE_THETA ** (jnp.arange(0, dim, 2, dtype=jnp.float32) / dim))
    ang = jnp.arange(seq_len, dtype=jnp.float32)[:, None] * inv[None, :]
    return jnp.cos(ang), jnp.sin(ang)


def apply_rope(x, cos, sin):
    xf = x.astype(jnp.float32)
    x1, x2 = jnp.split(xf, 2, axis=-1)
    return jnp.concatenate([x1 * cos - x2 * sin, x2 * cos + x1 * sin], axis=-1).astype(x.dtype)


def banded_attention(q, k, v, max_dist, blk):
    n, r, length, dh = q.shape
    nb = length // blk
    n_prev = -(-max_dist // blk)
    pad = ((0, 0), (n_prev * blk, 0), (0, 0))
    kb = jnp.pad(k, pad).reshape(n, nb + n_prev, blk, dh)
    vb = jnp.pad(v, pad).reshape(n, nb + n_prev, blk, dh)
    kw = jnp.concatenate([kb[:, i:i + nb] for i in range(n_prev + 1)], axis=2)
    vw = jnp.concatenate([vb[:, i:i + nb] for i in range(n_prev + 1)], axis=2)
    width = (n_prev + 1) * blk
    qb = q.reshape(n, r, nb, blk, dh)
    s = jnp.einsum('nrbqd,nbkd->nrbqk', qb, kw).astype(jnp.float32) * dh ** -0.5
    dist = n_prev * blk + jnp.arange(blk)[:, None] - jnp.arange(width)[None, :]
    kglob = (jnp.arange(nb)[:, None] - n_prev) * blk + jnp.arange(width)[None, :]
    mask = ((dist >= 0) & (dist <= max_dist))[None] & (kglob >= 0)[:, None, :]
    s = jnp.where(mask, s, -jnp.inf)
    mx = jnp.max(s, axis=-1, keepdims=True)
    e = jnp.exp(s - mx)
    den = jnp.sum(e, axis=-1, keepdims=True)
    o = jnp.einsum('nrbqk,nbkd->nrbqd', e, vw.astype(jnp.float32)) / den
    lse = (mx + jnp.log(den))[..., 0]
    return o.reshape(n, r, length, dh), lse.reshape(n, r, length)


def dilated_attention(q, k, v, window, dilation):
    b, h, s, dh = q.shape
    unit = dilation * BAND_BLOCK
    s_pad = -(-s // unit) * unit
    m = s_pad // dilation

    def split(t):
        t = jnp.pad(t, ((0, 0), (0, 0), (0, s_pad - s), (0, 0)))
        return t.reshape(b, h, m, dilation, dh).transpose(0, 1, 3, 2, 4).reshape(b * h * dilation, m, dh)

    o, lse = banded_attention(split(q)[:, None], split(k), split(v), window // dilation, BAND_BLOCK)
    o = o.reshape(b, h, dilation, m, dh).transpose(0, 1, 3, 2, 4).reshape(b, h, s_pad, dh)[:, :, :s]
    lse = lse.reshape(b, h, dilation, m).transpose(0, 1, 3, 2).reshape(b, h, s_pad)[:, :, :s]
    return o, lse


def dilated_mixture(q, k, v):
    outs, lses = [], []
    for window, dilation in DILATED_PATTERNS:
        o, lse = dilated_attention(q, k, v, window, dilation)
        outs.append(o)
        lses.append(lse)
    wts = jax.nn.softmax(jnp.stack(lses, axis=0), axis=0)
    return jnp.sum(wts[..., None] * jnp.stack(outs, axis=0), axis=0)


def dsa_attention(q, k, v, iq, ik, iw):
    b, ha, s, dh = q.shape
    topk = min(DSA_TOPK_MAX, s // 4)
    n_blocks = s // Q_BLOCK
    kpos = jnp.arange(s)
    bi = jnp.arange(b)[:, None, None]
    iwf = iw.astype(jnp.float32) * IDX_HEADS ** -0.5

    def block(i):
        start = i * Q_BLOCK
        qb = lax.dynamic_slice_in_dim(q, start, Q_BLOCK, axis=2)
        iqb = lax.dynamic_slice_in_dim(iq, start, Q_BLOCK, axis=2)
        iwb = lax.dynamic_slice_in_dim(iwf, start, Q_BLOCK, axis=1)
        qpos = start + jnp.arange(Q_BLOCK)
        logits = jnp.einsum('bhqd,bsd->bhqs', iqb, ik).astype(jnp.float32) * IDX_DIM ** -0.5
        score = jnp.einsum('bhqs,bqh->bqs', jax.nn.relu(logits), iwb)
        score = jnp.where((kpos[None, :] <= qpos[:, None])[None], score, -jnp.inf)
        _, idx = lax.top_k(score, topk)
        kg = k[bi, idx]
        vg = v[bi, idx]
        valid = idx <= qpos[None, :, None]
        sc = jnp.einsum('bhqd,bqkd->bhqk', qb, kg).astype(jnp.float32) * ATTN_SCALE
        p = jax.nn.softmax(jnp.where(valid[:, None], sc, -jnp.inf), axis=-1)
        return jnp.einsum('bhqk,bqkd->bhqd', p, vg.astype(jnp.float32))

    o = lax.map(block, jnp.arange(n_blocks))
    return o.transpose(1, 2, 0, 3, 4).reshape(b, ha, s, dh)


def compress_blocks(x, pe, w1, w2):
    b, g, s, dh = x.shape
    ratio = CMP_BLOCK // CMP_STRIDE
    chunks = x.reshape(b, g, s // CMP_STRIDE, CMP_STRIDE, dh)
    n_cmp = s // CMP_STRIDE - ratio + 1
    blocks = jnp.concatenate([chunks[:, :, j:j + n_cmp] for j in range(ratio)], axis=3)
    flat = (blocks + pe).reshape(b, g, n_cmp, CMP_BLOCK * dh)
    return jax.nn.gelu(flat @ w1) @ w2


def selected_block_attention(q, k, v, sel):
    b, g, r, s, dh = q.shape
    n_sel = sel.shape[-1]
    n_chunk = s // SEL_Q_CHUNK
    kb = k.reshape(b, g, s // SEL_BLOCK, SEL_BLOCK, dh)
    vb = v.reshape(b, g, s // SEL_BLOCK, SEL_BLOCK, dh)
    qc = q.reshape(b, g, r, n_chunk, SEL_Q_CHUNK, dh).transpose(3, 0, 1, 2, 4, 5)
    selc = sel.reshape(b, g, n_chunk, SEL_Q_CHUNK, n_sel).transpose(2, 0, 1, 3, 4)
    bi = jnp.arange(b)[:, None, None, None]
    gi = jnp.arange(g)[None, :, None, None]
    offs = jnp.arange(SEL_BLOCK)

    def chunk(args):
        qx, sx, ci = args
        kg = kb[bi, gi, sx].reshape(b, g, SEL_Q_CHUNK, n_sel * SEL_BLOCK, dh)
        vg = vb[bi, gi, sx].reshape(b, g, SEL_Q_CHUNK, n_sel * SEL_BLOCK, dh)
        kpos = (sx[..., None] * SEL_BLOCK + offs).reshape(b, g, SEL_Q_CHUNK, n_sel * SEL_BLOCK)
        qpos = ci * SEL_Q_CHUNK + jnp.arange(SEL_Q_CHUNK)
        valid = kpos <= qpos[None, None, :, None]
        sc = jnp.einsum('bgrqd,bgqkd->bgrqk', qx, kg).astype(jnp.float32) * ATTN_SCALE
        p = jax.nn.softmax(jnp.where(valid[:, :, None], sc, -jnp.inf), axis=-1)
        return jnp.einsum('bgrqk,bgqkd->bgrqd', p, vg.astype(jnp.float32))

    o = lax.map(chunk, (qc, selc, jnp.arange(n_chunk)))
    return o.transpose(1, 2, 3, 0, 4, 5).reshape(b, g, r, s, dh)


def nsa_attention(q, k_cmp, v_cmp, k_slc, v_slc, k_win, v_win, gate_logits, cos, sin,
                  pe_k, w1_k, w2_k, pe_v, w1_v, w2_v):
    b, hc, s, dh = q.shape
    g, r = C_KV_GROUPS, C_GROUP_SIZE
    qg = q.reshape(b, g, r, s, dh)
    t = jnp.arange(s)

    kc = compress_blocks(k_cmp, pe_k, w1_k, w2_k)
    vc = compress_blocks(v_cmp, pe_v, w1_v, w2_v)
    n_cmp = kc.shape[2]
    sc = jnp.einsum('bgrsd,bgcd->bgrsc', qg, kc).astype(jnp.float32) * ATTN_SCALE
    cmp_end = jnp.arange(n_cmp) * CMP_STRIDE + CMP_BLOCK - 1
    cmask = cmp_end[None, :] <= t[:, None]
    sc = jnp.where(cmask, sc, -jnp.inf)
    mx = jnp.max(sc, axis=-1, keepdims=True)
    mx = jnp.where(jnp.isfinite(mx), mx, 0.0)
    e = jnp.where(cmask, jnp.exp(sc - mx), 0.0)
    den = jnp.sum(e, axis=-1, keepdims=True)
    p_cmp = e / jnp.where(den > 0, den, 1.0)
    o_cmp = jnp.einsum('bgrsc,bgcd->bgrsd', p_cmp, vc.astype(jnp.float32))

    n_slc = s // SEL_BLOCK
    c_start = np.arange(n_cmp) * CMP_STRIDE
    n_start = np.arange(n_slc) * SEL_BLOCK
    intersect = ((c_start[:, None] < n_start[None, :] + SEL_BLOCK)
                 & (c_start[:, None] + CMP_BLOCK > n_start[None, :])).astype(np.float32)
    imp = jnp.einsum('bgrsc,cn->bgsn', p_cmp, jnp.asarray(intersect))
    blk = jnp.arange(n_slc)[None, :]
    cur = (t // SEL_BLOCK)[:, None]
    admissible = blk <= cur
    forced = (blk == 0) | (blk == cur) | (blk == cur - 1)
    imp = jnp.where(admissible, jnp.where(forced, jnp.inf, imp), -jnp.inf)
    _, sel = lax.top_k(imp, min(SEL_COUNT, n_slc))
    qr = apply_rope(qg, cos, sin)
    o_slc = selected_block_attention(qr, apply_rope(k_slc, cos, sin), v_slc, sel)

    o_win, _ = banded_attention(qr.reshape(b * g, r, s, dh),
                                apply_rope(k_win, cos, sin).reshape(b * g, s, dh),
                                v_win.reshape(b * g, s, dh), WIN_SIZE - 1, BAND_BLOCK)
    o_win = o_win.reshape(b, g, r, s, dh)

    gates = jax.nn.sigmoid(gate_logits.astype(jnp.float32)).reshape(b, s, g, r, 3)
    gates = gates.transpose(4, 0, 2, 3, 1)[..., None]
    o = gates[0] * o_cmp + gates[1] * o_slc + gates[2] * o_win
    return o.reshape(b, hc, s, dh)


def setup_inputs(seed: int = 0) -> dict:
    key = jax.random.key(seed)
    ks = jax.random.split(key, 16)
    f32 = jnp.float32

    def nrm(k, shape, fan_in):
        return jax.random.normal(k, shape, f32) * fan_in ** -0.5

    def gain(k):
        return 1.0 + 0.05 * jax.random.normal(k, (DEPTH, D_MODEL), f32)

    flat_cmp = CMP_BLOCK * HEAD_DIM
    mix_width = N_HEADS * HEAD_DIM
    return {
        'x': jax.random.normal(ks[0], (BATCH, SEQ, D_MODEL), f32),
        'w_in': nrm(ks[1], (DEPTH, D_MODEL, IN_WIDTH), D_MODEL),
        'w_out': nrm(ks[2], (DEPTH, mix_width, D_MODEL), mix_width),
        'cmp_pe_k': 0.1 * jax.random.normal(ks[3], (DEPTH, CMP_BLOCK, HEAD_DIM), f32),
        'cmp_w1_k': nrm(ks[4], (DEPTH, flat_cmp, CMP_HIDDEN), flat_cmp),
        'cmp_w2_k': nrm(ks[5], (DEPTH, CMP_HIDDEN, HEAD_DIM), CMP_HIDDEN),
        'cmp_pe_v': 0.1 * jax.random.normal(ks[6], (DEPTH, CMP_BLOCK, HEAD_DIM), f32),
        'cmp_w1_v': nrm(ks[7], (DEPTH, flat_cmp, CMP_HIDDEN), flat_cmp),
        'cmp_w2_v': nrm(ks[8], (DEPTH, CMP_HIDDEN, HEAD_DIM), CMP_HIDDEN),
        'w_gate': nrm(ks[9], (DEPTH, D_MODEL, D_FF), D_MODEL),
        'w_up': nrm(ks[10], (DEPTH, D_MODEL, D_FF), D_MODEL),
        'w_down': nrm(ks[11], (DEPTH, D_FF, D_MODEL), D_FF),
        'g_pre_mix': gain(ks[12]),
        'g_post_mix': gain(ks[13]),
        'g_pre_ffn': gain(ks[14]),
        'g_post_ffn': gain(ks[15]),
    }


def reference(x, w_in, w_out, cmp_pe_k, cmp_w1_k, cmp_w2_k, cmp_pe_v, cmp_w1_v, cmp_w2_v,
              w_gate, w_up, w_down, g_pre_mix, g_post_mix, g_pre_ffn, g_post_ffn):
    b, s, _ = x.shape
    cos, sin = rope_tables(s, HEAD_DIM)
    icos, isin = rope_tables(s, IDX_DIM)
    splits = [int(v) for v in np.cumsum(IN_WIDTHS)[:-1]]

    def heads(t, n):
        return t.reshape(b, s, n, -1).transpose(0, 2, 1, 3)

    def merge(o):
        return o.transpose(0, 2, 1, 3).reshape(b, s, -1)

    for layer in range(DEPTH):
        h = rms_norm(x, g_pre_mix[layer])
        (a_q, a_k, a_v, i_q, i_k, i_w, b_q, b_k, b_v, c_q,
         c_kc, c_vc, c_ks, c_vs, c_kw, c_vw, c_g) = jnp.split(h @ w_in[layer], splits, axis=-1)

        o_a = dsa_attention(apply_rope(heads(a_q, A_HEADS), cos, sin),
                            apply_rope(a_k, cos, sin), a_v,
                            apply_rope(heads(i_q, IDX_HEADS), icos, isin),
                            apply_rope(i_k, icos, isin), i_w)
        o_b = dilated_mixture(apply_rope(heads(b_q, B_HEADS), cos, sin),
                              apply_rope(heads(b_k, B_HEADS), cos, sin),
                              heads(b_v, B_HEADS))
        o_c = nsa_attention(heads(c_q, C_HEADS),
                            heads(c_kc, C_KV_GROUPS), heads(c_vc, C_KV_GROUPS),
                            heads(c_ks, C_KV_GROUPS), heads(c_vs, C_KV_GROUPS),
                            heads(c_kw, C_KV_GROUPS), heads(c_vw, C_KV_GROUPS),
                            c_g, cos, sin,
                            cmp_pe_k[layer], cmp_w1_k[layer], cmp_w2_k[layer],
                            cmp_pe_v[layer], cmp_w1_v[layer], cmp_w2_v[layer])

        mixed = jnp.concatenate([merge(o_a), merge(o_b), merge(o_c)], axis=-1).astype(x.dtype)
        x = x + rms_norm(mixed @ w_out[layer], g_post_mix[layer])

        h = rms_norm(x, g_pre_ffn[layer])
        f = (jax.nn.silu(h @ w_gate[layer]) * (h @ w_up[layer])) @ w_down[layer]
        x = x + rms_norm(f, g_post_ffn[layer])
    return x
```

```python
import functools

import numpy as np
import jax
import jax.numpy as jnp
from jax import lax
from jax.experimental import pallas as pl
from jax.experimental.pallas import tpu as pltpu

F32 = jnp.float32
BF16 = jnp.bfloat16

LANE = 128
HEAD_DIM = 128
A_HEADS = 4
B_HEADS = 4
C_HEADS = 8
C_KV_GROUPS = 2
C_GROUP_SIZE = C_HEADS // C_KV_GROUPS
IDX_HEADS = 16
IDX_DIM = 64
DSA_TOPK_MAX = 256
DILATED_PATTERNS = ((128, 1), (512, 4), (2048, 16))
CMP_BLOCK = 32
CMP_STRIDE = 16
CMP_HIDDEN = 256
SEL_BLOCK = 64
SEL_COUNT = 16
WIN_SIZE = 512
ROPE_THETA = 10000.0
RMS_EPS = 1e-6
ATTN_SCALE = HEAD_DIM ** -0.5
IDX_SCALE = IDX_DIM ** -0.5 * IDX_HEADS ** -0.5

NEG = -1e30
INT_MIN = -(2 ** 31)

MODE_NONE, MODE_ROPE, MODE_IROPE = 0, 1, 2

_SEGMENTS = (
    ("i_q", 8, MODE_IROPE, IDX_SCALE),
    ("c_q", 8, MODE_NONE, ATTN_SCALE),
    ("a_q", 4, MODE_ROPE, ATTN_SCALE),
    ("b_q", 4, MODE_ROPE, ATTN_SCALE),
    ("b_k", 4, MODE_ROPE, 1.0),
    ("b_v", 4, MODE_NONE, 1.0),
    ("c_kc", 2, MODE_NONE, 1.0),
    ("c_vc", 2, MODE_NONE, 1.0),
    ("c_ks", 2, MODE_ROPE, 1.0),
    ("c_vs", 2, MODE_NONE, 1.0),
    ("c_kw", 2, MODE_ROPE, 1.0),
    ("c_vw", 2, MODE_NONE, 1.0),
    ("a_k", 1, MODE_ROPE, 1.0),
    ("a_v", 1, MODE_NONE, 1.0),
    ("i_k", 1, MODE_IROPE, 1.0),
    ("i_w", 1, MODE_NONE, 1.0),
    ("c_g0", 1, MODE_NONE, 1.0),
    ("c_g1", 1, MODE_NONE, 1.0),
)
_OFF = {}
_o = 0
for _n, _w, _m, _s in _SEGMENTS:
    _OFF[_n] = _o
    _o += _w
NP_BLOCKS = _o
NP = NP_BLOCKS * LANE
_BLOCK_MODE = tuple(m for _, w, m, _ in _SEGMENTS for _ in range(w))
_COL_SCALE = np.repeat(np.array([s for _, w, _, s in _SEGMENTS for _ in range(w)], np.float32), LANE)[None, :]
_PROJ_CHUNK = 4


def _params(sem, vmem_mb):
    return pltpu.CompilerParams(dimension_semantics=sem, vmem_limit_bytes=vmem_mb * 1024 * 1024)


def _resident(block_shape, index_map):
    return pl.BlockSpec(block_shape, index_map, pipeline_mode=pl.Buffered(1))


def _rms(x, g):
    return x * lax.rsqrt(jnp.mean(x * x, axis=-1, keepdims=True) + RMS_EPS) * g


def _rms_kernel(x_ref, g_ref, h_ref):
    h_ref[...] = _rms(x_ref[...], g_ref[...]).astype(h_ref.dtype)


def _rms_call(x, g, tm):
    m, d = x.shape
    return pl.pallas_call(
        _rms_kernel,
        grid=(m // tm,),
        in_specs=[pl.BlockSpec((tm, d), lambda i: (i, 0)), pl.BlockSpec((1, d), lambda i: (0, 0))],
        out_specs=pl.BlockSpec((tm, d), lambda i: (i, 0)),
        out_shape=jax.ShapeDtypeStruct((m, d), BF16),
        compiler_params=_params(("parallel",), 32),
        name="rms_pre",
    )(x, g)


def _rope_full(a, cos, sin):
    return a * cos + pltpu.roll(a, HEAD_DIM // 2, 1) * sin


def _rope_idx(a, cos, sin, first_half):
    partner = jnp.where(first_half, pltpu.roll(a, LANE - IDX_DIM // 2, 1), pltpu.roll(a, IDX_DIM // 2, 1))
    return a * cos + partner * sin


def _proj_kernel(h_ref, w_ref, cs_ref, cos_ref, sin_ref, icos_ref, isin_ref, o_ref):
    h = h_ref[...]
    tm = h.shape[0]
    lane = lax.broadcasted_iota(jnp.int32, (tm, LANE), 1)
    first_half = (lane & (IDX_DIM - 1)) < IDX_DIM // 2
    for c0 in range(0, NP_BLOCKS, _PROJ_CHUNK):
        nb = min(_PROJ_CHUNK, NP_BLOCKS - c0)
        acc = jnp.dot(h, w_ref[:, c0 * LANE:(c0 + nb) * LANE], preferred_element_type=F32)
        for b in range(nb):
            col = slice((c0 + b) * LANE, (c0 + b + 1) * LANE)
            a = acc[:, b * LANE:(b + 1) * LANE] * cs_ref[:, col]
            mode = _BLOCK_MODE[c0 + b]
            if mode == MODE_ROPE:
                a = _rope_full(a, cos_ref[...], sin_ref[...])
            elif mode == MODE_IROPE:
                a = _rope_idx(a, icos_ref[...], isin_ref[...], first_half)
            o_ref[:, col] = a.astype(o_ref.dtype)


def _proj_call(h, w_in_p, layer, col_scale, tabs, seq, tm):
    m, d = h.shape
    tpb = seq // tm
    tab_spec = pl.BlockSpec((tm, LANE), lambda i: (i % tpb, 0))
    return pl.pallas_call(
        _proj_kernel,
        grid=(m // tm,),
        in_specs=[
            pl.BlockSpec((tm, d), lambda i: (i, 0)),
            _resident((None, d, NP), lambda i: (layer, 0, 0)),
            _resident((1, NP), lambda i: (0, 0)),
            tab_spec, tab_spec, tab_spec, tab_spec,
        ],
        out_specs=pl.BlockSpec((tm, NP), lambda i: (i, 0)),
        out_shape=jax.ShapeDtypeStruct((m, NP), BF16),
        compiler_params=_params(("parallel",), 52),
        name="in_proj",
    )(h, w_in_p, col_scale, *tabs)


def _transpose_into(dst_ref, src_ref):
    rows, cols = src_ref.shape
    for r0 in range(0, rows, LANE):
        r1 = min(r0 + LANE, rows)
        for c0 in range(0, cols, LANE):
            tile = src_ref[r0:r1, c0:c0 + LANE].astype(F32)
            dst_ref[c0:c0 + LANE, r0:r1] = tile.T.astype(dst_ref.dtype)


def _masked_attention(k, q, vt, mask, weight=None):
    s = lax.dot_general(k, q, (((1,), (1,)), ((), ())), preferred_element_type=F32)
    s = jnp.where(mask, s, NEG)
    mx = jnp.max(s, axis=0, keepdims=True)
    p = jnp.exp(s - mx)
    if weight is not None:
        p = p * weight
    den = jnp.sum(p, axis=0, keepdims=True)
    o = jnp.dot(vt, p.astype(BF16), preferred_element_type=F32)
    return o / den


def _count(x):
    return jnp.sum(x.astype(jnp.int32), axis=0, keepdims=True)


def _dsa_kernel(iq_ref, iw_ref, ik_ref, q_ref, k_ref, v_ref, o_ref, vt_ref, key_ref, cut_ref, *, topk):
    i = pl.program_id(1)
    tq = q_ref.shape[0]
    s_len = k_ref.shape[0]

    @pl.when(i == 0)
    def _():
        _transpose_into(vt_ref, v_ref)

    lane = lax.broadcasted_iota(jnp.int32, (s_len, LANE), 1)
    ik = ik_ref[...]
    ik_lo = jnp.where(lane < IDX_DIM, ik, jnp.zeros_like(ik))
    ik_hi = jnp.where(lane >= IDX_DIM, ik, jnp.zeros_like(ik))
    iw_t = iw_ref[...].astype(F32).T
    score = jnp.zeros((s_len, tq), F32)
    for p in range(IDX_HEADS // 2):
        blk = iq_ref[:, p * LANE:(p + 1) * LANE]
        for half, ikh in enumerate((ik_lo, ik_hi)):
            j = 2 * p + half
            lg = lax.dot_general(ikh, blk, (((1,), (1,)), ((), ())), preferred_element_type=F32)
            score = score + jnp.maximum(lg, 0.0) * iw_t[j:j + 1, :]

    srow = lax.broadcasted_iota(jnp.int32, (s_len, tq), 0)
    tcol = i * tq + lax.broadcasted_iota(jnp.int32, (s_len, tq), 1)
    causal = srow <= tcol

    bits = pltpu.bitcast(score, jnp.int32)
    key = bits ^ ((bits >> 31) & 0x7FFFFFFF)
    key_ref[...] = jnp.where(causal, key, INT_MIN)

    def bit_step(b, thr):
        cand = thr ^ jnp.left_shift(jnp.int32(1), 31 - b)
        cnt = _count(key_ref[...] >= cand)
        return jnp.where(cnt >= topk, cand, thr)

    thr = lax.fori_loop(0, 32, bit_step, jnp.full((1, tq), INT_MIN, jnp.int32))

    key = key_ref[...]
    need = topk - _count(key > thr)
    n_eq = _count(key == thr)
    cut_ref[...] = jnp.full((1, tq), s_len, jnp.int32)
    excess = jnp.where(n_eq > need, jnp.where(thr != INT_MIN, 1, 0), 0)

    @pl.when(jnp.max(excess) > 0)
    def _():
        eq = key_ref[...] == thr
        n_bits = max(1, (s_len - 1).bit_length())

        def idx_step(b, cut):
            cand = cut | jnp.left_shift(jnp.int32(1), n_bits - 1 - b)
            below = jnp.sum(jnp.where(eq, jnp.where(srow < cand, 1, 0), 0), axis=0, keepdims=True)
            return jnp.where(below < need, cand, cut)

        cut = lax.fori_loop(0, n_bits, idx_step, jnp.zeros((1, tq), jnp.int32))
        cut_ref[...] = jnp.where(excess > 0, cut, s_len)

    cut = cut_ref[...]
    chosen = jnp.where(key > thr, 1, jnp.where(key == thr, jnp.where(srow <= cut, 1, 0), 0))
    mask = jnp.where(causal, chosen, 0) > 0

    k = k_ref[...]
    vt = vt_ref[...]
    for h in range(A_HEADS):
        col = slice(h * HEAD_DIM, (h + 1) * HEAD_DIM)
        o_t = _masked_attention(k, q_ref[:, col], vt, mask)
        o_ref[:, col] = o_t.T.astype(o_ref.dtype)


def _dsa_call(p3, tq):
    b, s, _ = p3.shape
    topk = min(DSA_TOPK_MAX, s // 4)
    blk = lambda name, width: _OFF[name] // width
    return pl.pallas_call(
        functools.partial(_dsa_kernel, topk=topk),
        grid=(b, s // tq),
        in_specs=[
            pl.BlockSpec((None, tq, 8 * LANE), lambda bi, i: (bi, i, blk("i_q", 8))),
            pl.BlockSpec((None, tq, LANE), lambda bi, i: (bi, i, blk("i_w", 1))),
            pl.BlockSpec((None, s, LANE), lambda bi, i: (bi, 0, blk("i_k", 1))),
            pl.BlockSpec((None, tq, 4 * LANE), lambda bi, i: (bi, i, blk("a_q", 4))),
            pl.BlockSpec((None, s, LANE), lambda bi, i: (bi, 0, blk("a_k", 1))),
            pl.BlockSpec((None, s, LANE), lambda bi, i: (bi, 0, blk("a_v", 1))),
        ],
        out_specs=pl.BlockSpec((None, tq, A_HEADS * HEAD_DIM), lambda bi, i: (bi, i, 0)),
        out_shape=jax.ShapeDtypeStruct((b, s, A_HEADS * HEAD_DIM), BF16),
        scratch_shapes=[
            pltpu.VMEM((HEAD_DIM, s), BF16),
            pltpu.VMEM((s, tq), jnp.int32),
            pltpu.VMEM((1, tq), jnp.int32),
        ],
        compiler_params=_params(("parallel", "arbitrary"), 48),
        name="dsa",
    )(p3, p3, p3, p3, p3, p3)


def _dil_kernel(q_ref, k_ref, v_ref, o_ref, vt_ref):
    i = pl.program_id(1)
    tq = q_ref.shape[0]
    s_len = k_ref.shape[0]

    @pl.when(i == 0)
    def _():
        _transpose_into(vt_ref, v_ref)

    srow = lax.broadcasted_iota(jnp.int32, (s_len, tq), 0)
    tcol = i * tq + lax.broadcasted_iota(jnp.int32, (s_len, tq), 1)
    delta = tcol - srow
    mult = jnp.zeros((s_len, tq), jnp.int32)
    for window, dilation in DILATED_PATTERNS:
        on_stride = (delta & (dilation - 1)) == 0
        mult = mult + jnp.where(on_stride, jnp.where(delta <= window, 1, 0), 0)
    mult = jnp.where(delta >= 0, mult, 0)
    mask = mult > 0
    weight = mult.astype(F32)

    for h in range(B_HEADS):
        col = slice(h * HEAD_DIM, (h + 1) * HEAD_DIM)
        o_t = _masked_attention(k_ref[:, col], q_ref[:, col], vt_ref[col, :], mask, weight)
        o_ref[:, col] = o_t.T.astype(o_ref.dtype)


def _dil_call(p3, tq):
    b, s, _ = p3.shape
    width = B_HEADS * HEAD_DIM
    return pl.pallas_call(
        _dil_kernel,
        grid=(b, s // tq),
        in_specs=[
            pl.BlockSpec((None, tq, width), lambda bi, i: (bi, i, _OFF["b_q"] // 4)),
            pl.BlockSpec((None, s, width), lambda bi, i: (bi, 0, _OFF["b_k"] // 4)),
            pl.BlockSpec((None, s, width), lambda bi, i: (bi, 0, _OFF["b_v"] // 4)),
        ],
        out_specs=pl.BlockSpec((None, tq, width), lambda bi, i: (bi, i, 0)),
        out_shape=jax.ShapeDtypeStruct((b, s, width), BF16),
        scratch_shapes=[pltpu.VMEM((width, s), BF16)],
        compiler_params=_params(("parallel", "arbitrary"), 48),
        name="dilated",
    )(p3, p3, p3)


def _gelu_tanh(x):
    return 0.5 * x * (1.0 + jnp.tanh(np.float32(np.sqrt(2.0 / np.pi)) * (x + 0.044715 * (x * x * x))))


def _cmp_kernel(*refs):
    n = CMP_STRIDE
    xk, xv = refs[:n], refs[n:2 * n]
    pe_k, w1_k, w2_k, pe_v, w1_v, w2_v, kc_ref, vct_ref = refs[2 * n:]

    def branch(x_refs, pe_ref, w1_ref, w2_ref):
        lo = hi = None
        for j in range(n):
            xj = x_refs[j][...].astype(F32)
            a = (xj + pe_ref[j:j + 1, :]).astype(BF16)
            b = (xj + pe_ref[n + j:n + j + 1, :]).astype(BF16)
            dl = jnp.dot(a, w1_ref[j * HEAD_DIM:(j + 1) * HEAD_DIM, :], preferred_element_type=F32)
            dh = jnp.dot(b, w1_ref[(n + j) * HEAD_DIM:(n + j + 1) * HEAD_DIM, :], preferred_element_type=F32)
            lo = dl if lo is None else lo + dl
            hi = dh if hi is None else hi + dh
        nc = lo.shape[0]
        hid = lo + pltpu.roll(hi, nc - 1, 0)
        return jnp.dot(_gelu_tanh(hid).astype(BF16), w2_ref[...], preferred_element_type=F32)

    kc_ref[...] = branch(xk, pe_k, w1_k, w2_k).astype(kc_ref.dtype)
    vct_ref[...] = branch(xv, pe_v, w1_v, w2_v).T.astype(vct_ref.dtype)


def _cmp_call(p3, layer, pe_k, w1_k, w2_k, pe_v, w1_v, w2_v):
    b, s, _ = p3.shape
    nc = s // CMP_STRIDE
    p4 = p3.reshape(b, nc, CMP_STRIDE * NP)
    x_specs = []
    for name in ("c_kc", "c_vc"):
        for j in range(CMP_STRIDE):
            x_specs.append(pl.BlockSpec((None, nc, HEAD_DIM),
                                        lambda bi, g, j=j, name=name: (bi, 0, j * NP_BLOCKS + _OFF[name] + g)))
    flat = CMP_BLOCK * HEAD_DIM
    w_specs = [
        _resident((None, CMP_BLOCK, HEAD_DIM), lambda bi, g: (layer, 0, 0)),
        _resident((None, flat, CMP_HIDDEN), lambda bi, g: (layer, 0, 0)),
        _resident((None, CMP_HIDDEN, HEAD_DIM), lambda bi, g: (layer, 0, 0)),
    ]
    return pl.pallas_call(
        _cmp_kernel,
        grid=(b, C_KV_GROUPS),
        in_specs=x_specs + w_specs + w_specs,
        out_specs=[
            pl.BlockSpec((None, None, nc, HEAD_DIM), lambda bi, g: (bi, g, 0, 0)),
            pl.BlockSpec((None, None, HEAD_DIM, nc), lambda bi, g: (bi, g, 0, 0)),
        ],
        out_shape=[
            jax.ShapeDtypeStruct((b, C_KV_GROUPS, nc, HEAD_DIM), BF16),
            jax.ShapeDtypeStruct((b, C_KV_GROUPS, HEAD_DIM, nc), BF16),
        ],
        compiler_params=_params(("parallel", "parallel"), 32),
        name="nsa_compress",
    )(*([p4] * (2 * CMP_STRIDE)), pe_k, w1_k, w2_k, pe_v, w1_v, w2_v)


def _nsa_kernel(q_ref, kc_ref, vct_ref, ks_ref, vs_ref, kw_ref, vw_ref, g_ref, cos_ref, sin_ref,
                isect_ref, expand_ref, o_ref, vst_ref, vwt_ref, *, n_sel):
    i = pl.program_id(2)
    tq = q_ref.shape[0]
    s_len = ks_ref.shape[0]
    nc = kc_ref.shape[0]
    n_slc = isect_ref.shape[0]

    @pl.when(i == 0)
    def _():
        _transpose_into(vst_ref, vs_ref)
        _transpose_into(vwt_ref, vw_ref)

    t_row = i * tq + lax.broadcasted_iota(jnp.int32, (1, tq), 1)

    kc = kc_ref[...]
    vct = vct_ref[...]
    cend = lax.broadcasted_iota(jnp.int32, (nc, tq), 0) * CMP_STRIDE + (CMP_BLOCK - 1)
    cmask = cend <= t_row
    o_cmp = []
    p_sum = jnp.zeros((nc, tq), F32)
    for r in range(C_GROUP_SIZE):
        q_r = q_ref[:, r * HEAD_DIM:(r + 1) * HEAD_DIM]
        sc = lax.dot_general(kc, q_r, (((1,), (1,)), ((), ())), preferred_element_type=F32)
        sc = jnp.where(cmask, sc, NEG)
        mx = jnp.max(sc, axis=0, keepdims=True)
        e = jnp.where(cmask, jnp.exp(sc - mx), 0.0)
        den = jnp.sum(e, axis=0, keepdims=True)
        p = e / jnp.where(den > 0, den, 1.0)
        p_sum = p_sum + p
        o_cmp.append(jnp.dot(vct, p.astype(BF16), preferred_element_type=F32))

    isect = isect_ref[...]
    p_hi = p_sum.astype(BF16)
    p_lo = (p_sum - p_hi.astype(F32)).astype(BF16)
    imp = (jnp.dot(isect, p_hi, preferred_element_type=F32)
           + jnp.dot(isect, p_lo, preferred_element_type=F32))
    blk = lax.broadcasted_iota(jnp.int32, (n_slc, tq), 0)
    cur = t_row // SEL_BLOCK
    val = jnp.where(blk == 0, jnp.inf, jnp.where(blk == cur, jnp.inf, jnp.where(blk == cur - 1, jnp.inf, imp)))
    val = jnp.where(blk <= cur, val, -jnp.inf)
    rank = jnp.zeros((n_slc, tq), jnp.int32)
    for m in range(n_slc):
        vm = val[m:m + 1, :]
        before = jnp.where(vm > val, 1, jnp.where(vm == val, jnp.where(blk > m, 1, 0), 0))
        rank = rank + before
    sel = jnp.where(rank < n_sel, 1.0, 0.0)
    if n_slc < LANE:
        sel = jnp.concatenate([sel, jnp.zeros((LANE - n_slc, tq), F32)], axis=0)
    in_sel = jnp.dot(expand_ref[...], sel.astype(BF16), preferred_element_type=F32)

    srow = lax.broadcasted_iota(jnp.int32, (s_len, tq), 0)
    delta = t_row - srow
    slc_mask = jnp.where(delta >= 0, in_sel, 0.0) > 0.5
    win_mask = jnp.where(delta >= 0, jnp.where(delta <= WIN_SIZE - 1, 1, 0), 0) > 0

    gates = jax.nn.sigmoid(g_ref[...].astype(F32).T)
    ks, kw = ks_ref[...], kw_ref[...]
    vst, vwt = vst_ref[...], vwt_ref[...]
    cos, sin = cos_ref[...], sin_ref[...]
    for r in range(C_GROUP_SIZE):
        col = slice(r * HEAD_DIM, (r + 1) * HEAD_DIM)
        q_rot = _rope_full(q_ref[:, col].astype(F32), cos, sin).astype(BF16)
        o_slc = _masked_attention(ks, q_rot, vst, slc_mask)
        o_win = _masked_attention(kw, q_rot, vwt, win_mask)
        o_t = (gates[3 * r:3 * r + 1, :] * o_cmp[r] + gates[3 * r + 1:3 * r + 2, :] * o_slc
               + gates[3 * r + 2:3 * r + 3, :] * o_win)
        o_ref[:, col] = o_t.T.astype(o_ref.dtype)


def _nsa_call(p3, kc, vct, cos, sin, isect, expand, tq):
    b, s, _ = p3.shape
    nc = kc.shape[2]
    n_slc = s // SEL_BLOCK
    width = C_GROUP_SIZE * HEAD_DIM
    kv = lambda name: pl.BlockSpec((None, s, HEAD_DIM), lambda bi, g, i, name=name: (bi, 0, _OFF[name] + g))
    return pl.pallas_call(
        functools.partial(_nsa_kernel, n_sel=min(SEL_COUNT, n_slc)),
        grid=(b, C_KV_GROUPS, s // tq),
        in_specs=[
            pl.BlockSpec((None, tq, width), lambda bi, g, i: (bi, i, _OFF["c_q"] // 4 + g)),
            pl.BlockSpec((None, None, nc, HEAD_DIM), lambda bi, g, i: (bi, g, 0, 0)),
            pl.BlockSpec((None, None, HEAD_DIM, nc), lambda bi, g, i: (bi, g, 0, 0)),
            kv("c_ks"), kv("c_vs"), kv("c_kw"), kv("c_vw"),
            pl.BlockSpec((None, tq, LANE), lambda bi, g, i: (bi, i, _OFF["c_g0"] + g)),
            pl.BlockSpec((tq, LANE), lambda bi, g, i: (i, 0)),
            pl.BlockSpec((tq, LANE), lambda bi, g, i: (i, 0)),
            pl.BlockSpec(isect.shape, lambda bi, g, i: (0, 0)),
            pl.BlockSpec(expand.shape, lambda bi, g, i: (0, 0)),
        ],
        out_specs=pl.BlockSpec((None, tq, width), lambda bi, g, i: (bi, i, g)),
        out_shape=jax.ShapeDtypeStruct((b, s, C_HEADS * HEAD_DIM), BF16),
        scratch_shapes=[pltpu.VMEM((HEAD_DIM, s), BF16), pltpu.VMEM((HEAD_DIM, s), BF16)],
        compiler_params=_params(("parallel", "parallel", "arbitrary"), 48),
        name="nsa",
    )(p3, kc, vct, p3, p3, p3, p3, p3, cos, sin, isect, expand)


def _out_kernel(oa_ref, ob_ref, oc_ref, w_ref, x_ref, gpost_ref, gnext_ref, xo_ref, h_ref):
    na, nb = oa_ref.shape[1], ob_ref.shape[1]
    y = jnp.dot(oa_ref[...], w_ref[0:na, :], preferred_element_type=F32)
    y = y + jnp.dot(ob_ref[...], w_ref[na:na + nb, :], preferred_element_type=F32)
    y = y + jnp.dot(oc_ref[...], w_ref[na + nb:, :], preferred_element_type=F32)
    x = x_ref[...] + _rms(y, gpost_ref[...])
    xo_ref[...] = x
    h_ref[...] = _rms(x, gnext_ref[...]).astype(h_ref.dtype)


def _out_call(o_a, o_b, o_c, w_out_p, layer, x, g_post, g_next, tm):
    m, d = x.shape
    row = lambda width: pl.BlockSpec((tm, width), lambda i: (i, 0))
    vec = pl.BlockSpec((1, d), lambda i: (0, 0))
    return pl.pallas_call(
        _out_kernel,
        grid=(m // tm,),
        in_specs=[
            row(o_a.shape[1]), row(o_b.shape[1]), row(o_c.shape[1]),
            _resident((None, w_out_p.shape[1], d), lambda i: (layer, 0, 0)),
            row(d), vec, vec,
        ],
        out_specs=[row(d), row(d)],
        out_shape=[jax.ShapeDtypeStruct((m, d), F32), jax.ShapeDtypeStruct((m, d), BF16)],
        compiler_params=_params(("parallel",), 48),
        name="out_proj",
    )(o_a, o_b, o_c, w_out_p, x, g_post, g_next)


def _ffn_kernel(h_ref, wg_ref, wu_ref, wd_ref, x_ref, gpost_ref, gnext_ref, xo_ref, ho_ref, acc_ref):
    f = pl.program_id(1)
    h = h_ref[...]
    g = jnp.dot(h, wg_ref[...], preferred_element_type=F32)
    u = jnp.dot(h, wu_ref[...], preferred_element_type=F32)
    a = (g * jax.nn.sigmoid(g) * u).astype(BF16)
    part = jnp.dot(a, wd_ref[...], preferred_element_type=F32)

    @pl.when(f == 0)
    def _():
        acc_ref[...] = part

    @pl.when(f > 0)
    def _():
        acc_ref[...] += part

    @pl.when(f == pl.num_programs(1) - 1)
    def _():
        x = x_ref[...] + _rms(acc_ref[...], gpost_ref[...])
        xo_ref[...] = x
        ho_ref[...] = _rms(x, gnext_ref[...]).astype(ho_ref.dtype)


def _ffn_call(h, wg, wu, wd, layer, x, g_post, g_next, tm, tf):
    m, d = x.shape
    d_ff = wg.shape[2]
    row = pl.BlockSpec((tm, d), lambda i, f: (i, 0))
    vec = pl.BlockSpec((1, d), lambda i, f: (0, 0))
    return pl.pallas_call(
        _ffn_kernel,
        grid=(m // tm, d_ff // tf),
        in_specs=[
            row,
            pl.BlockSpec((None, d, tf), lambda i, f: (layer, 0, f)),
            pl.BlockSpec((None, d, tf), lambda i, f: (layer, 0, f)),
            pl.BlockSpec((None, tf, d), lambda i, f: (layer, f, 0)),
            row, vec, vec,
        ],
        out_specs=[row, row],
        out_shape=[jax.ShapeDtypeStruct((m, d), F32), jax.ShapeDtypeStruct((m, d), BF16)],
        scratch_shapes=[pltpu.VMEM((tm, d), F32)],
        compiler_params=_params(("parallel", "arbitrary"), 52),
        name="ffn",
    )(h, wg, wu, wd, x, g_post, g_next)


def _pad_cols(w, width):
    return jnp.pad(w, ((0, 0), (0, 0), (0, width - w.shape[-1])))


def _prep_w_in(w_in):
    widths = (A_HEADS * HEAD_DIM, HEAD_DIM, HEAD_DIM, IDX_HEADS * IDX_DIM, IDX_DIM, IDX_HEADS,
              B_HEADS * HEAD_DIM, B_HEADS * HEAD_DIM, B_HEADS * HEAD_DIM, C_HEADS * HEAD_DIM,
              C_KV_GROUPS * HEAD_DIM, C_KV_GROUPS * HEAD_DIM, C_KV_GROUPS * HEAD_DIM, C_KV_GROUPS * HEAD_DIM,
              C_KV_GROUPS * HEAD_DIM, C_KV_GROUPS * HEAD_DIM, 3 * C_HEADS)
    names = ("a_q", "a_k", "a_v", "i_q", "i_k", "i_w", "b_q", "b_k", "b_v", "c_q",
             "c_kc", "c_vc", "c_ks", "c_vs", "c_kw", "c_vw", "c_g")
    offs = np.concatenate([[0], np.cumsum(widths)])
    src = {n: w_in[:, :, int(offs[k]):int(offs[k + 1])] for k, n in enumerate(names)}
    per_group = 3 * C_GROUP_SIZE
    src["c_g0"] = _pad_cols(src["c_g"][:, :, :per_group], LANE)
    src["c_g1"] = _pad_cols(src["c_g"][:, :, per_group:], LANE)
    src["i_k"] = jnp.concatenate([src["i_k"], src["i_k"]], axis=-1)
    src["i_w"] = _pad_cols(src["i_w"], LANE)
    return jnp.concatenate([src[n] for n, _, _, _ in _SEGMENTS], axis=-1).astype(BF16)


def _rope_tables(seq):
    def tables(dim):
        inv = 1.0 / (ROPE_THETA ** (jnp.arange(0, dim, 2, dtype=F32) / dim))
        ang = jnp.arange(seq, dtype=F32)[:, None] * inv[None, :]
        cos, sin = jnp.cos(ang), jnp.sin(ang)
        reps = LANE // dim
        return (jnp.tile(jnp.concatenate([cos, cos], axis=-1), (1, reps)),
                jnp.tile(jnp.concatenate([-sin, sin], axis=-1), (1, reps)))
    return tables(HEAD_DIM) + tables(IDX_DIM)


def _selection_constants(seq):
    nc = seq // CMP_STRIDE
    n_slc = seq // SEL_BLOCK
    c_start = np.arange(nc) * CMP_STRIDE
    n_start = np.arange(n_slc) * SEL_BLOCK
    isect = ((c_start[None, :] < n_start[:, None] + SEL_BLOCK)
             & (c_start[None, :] + CMP_BLOCK > n_start[:, None])
             & (np.arange(nc)[None, :] < nc - CMP_BLOCK // CMP_STRIDE + 1)).astype(np.float32)
    expand = np.zeros((seq, max(n_slc, LANE)), np.float32)
    expand[np.arange(seq), np.arange(seq) // SEL_BLOCK] = 1.0
    return jnp.asarray(isect, BF16), jnp.asarray(expand, BF16)


def kernel(x, w_in, w_out, cmp_pe_k, cmp_w1_k, cmp_w2_k, cmp_pe_v, cmp_w1_v, cmp_w2_v,
           w_gate, w_up, w_down, g_pre_mix, g_post_mix, g_pre_ffn, g_post_ffn):
    b, s, d = x.shape
    depth = w_in.shape[0]
    m = b * s
    tq = LANE
    tm_proj = min(256, s)
    tm_out = min(512, s)
    tm_ffn = min(512, s)
    tf = 512

    w_in_p = _prep_w_in(w_in)
    w_out_p = w_out.astype(BF16)
    wg, wu, wd = w_gate.astype(BF16), w_up.astype(BF16), w_down.astype(BF16)
    w1_k, w2_k = cmp_w1_k.astype(BF16), cmp_w2_k.astype(BF16)
    w1_v, w2_v = cmp_w1_v.astype(BF16), cmp_w2_v.astype(BF16)
    col_scale = jnp.asarray(_COL_SCALE)
    tabs = _rope_tables(s)
    isect, expand = _selection_constants(s)
    vec = lambda g, layer: g[layer][None, :]

    xf = x.reshape(m, d)
    h = _rms_call(xf, vec(g_pre_mix, 0), tm_out)
    for layer in range(depth):
        p3 = _proj_call(h, w_in_p, layer, col_scale, tabs, s, tm_proj).reshape(b, s, NP)
        o_a = _dsa_call(p3, tq)
        o_b = _dil_call(p3, tq)
        kc, vct = _cmp_call(p3, layer, cmp_pe_k, w1_k, w2_k, cmp_pe_v, w1_v, w2_v)
        o_c = _nsa_call(p3, kc, vct, tabs[0], tabs[1], isect, expand, tq)
        xf, h = _out_call(o_a.reshape(m, -1), o_b.reshape(m, -1), o_c.reshape(m, -1), w_out_p, layer,
                          xf, vec(g_post_mix, layer), vec(g_pre_ffn, layer), tm_out)
        g_next = vec(g_pre_mix, (layer + 1) % depth)
        xf, h = _ffn_call(h, wg, wu, wd, layer, xf, vec(g_post_ffn, layer), g_next, tm_ffn, tf)
    return xf.reshape(b, s, d)
```

```python
import functools

import numpy as np
import jax
import jax.numpy as jnp
from jax import lax
from jax.experimental import pallas as pl
from jax.experimental.pallas import tpu as pltpu

F32 = jnp.float32
BF16 = jnp.bfloat16

LANE = 128
SUBLANE = 8
HEAD_DIM = 128
A_HEADS = 4
B_HEADS = 4
C_HEADS = 8
C_KV_GROUPS = 2
C_GROUP_SIZE = C_HEADS // C_KV_GROUPS
IDX_HEADS = 16
IDX_DIM = 64
DSA_TOPK_MAX = 256
DILATED_PATTERNS = ((128, 1), (512, 4), (2048, 16))
CMP_BLOCK = 32
CMP_STRIDE = 16
CMP_HIDDEN = 256
SEL_BLOCK = 64
SEL_COUNT = 16
WIN_SIZE = 512
ROPE_THETA = 10000.0
RMS_EPS = 1e-6
ATTN_SCALE = HEAD_DIM ** -0.5
IDX_SCALE = IDX_DIM ** -0.5 * IDX_HEADS ** -0.5

NEG = -1e30
INT_MIN = -(2 ** 31)

MODE_NONE, MODE_ROPE, MODE_IROPE = 0, 1, 2

_SEGMENTS = (
    ("i_q", 8, MODE_IROPE, IDX_SCALE),
    ("c_q", 8, MODE_NONE, ATTN_SCALE),
    ("a_q", 4, MODE_ROPE, ATTN_SCALE),
    ("b_q", 4, MODE_ROPE, ATTN_SCALE),
    ("b_k", 4, MODE_ROPE, 1.0),
    ("b_v", 4, MODE_NONE, 1.0),
    ("c_ks", 2, MODE_ROPE, 1.0),
    ("c_vs", 2, MODE_NONE, 1.0),
    ("c_kw", 2, MODE_ROPE, 1.0),
    ("c_vw", 2, MODE_NONE, 1.0),
    ("a_k", 1, MODE_ROPE, 1.0),
    ("a_v", 1, MODE_NONE, 1.0),
    ("i_k", 1, MODE_IROPE, 1.0),
    ("i_w", 1, MODE_NONE, 1.0),
    ("c_g0", 1, MODE_NONE, 1.0),
    ("c_g1", 1, MODE_NONE, 1.0),
)
_CMP_SEGMENTS = (("c_kc", 2), ("c_vc", 2))
_OFF = {}
_o = 0
for _n, _w, _m, _s in _SEGMENTS:
    _OFF[_n] = _o
    _o += _w
NP_BLOCKS = _o
NP = NP_BLOCKS * LANE
CMP_BLOCKS = sum(w for _, w in _CMP_SEGMENTS)
CMP_COLS = CMP_BLOCKS * LANE
NW = NP + CMP_COLS
_BLOCK_MODE = tuple(m for _, w, m, _ in _SEGMENTS for _ in range(w))
_COL_SCALE = np.repeat(np.array([s for _, w, _, s in _SEGMENTS for _ in range(w)], np.float32), LANE)[None, :]
_PROJ_CHUNK = 4


def _params(sem, vmem_mb):
    return pltpu.CompilerParams(dimension_semantics=sem, vmem_limit_bytes=vmem_mb * 1024 * 1024)


def _resident(block_shape, index_map):
    return pl.BlockSpec(block_shape, index_map, pipeline_mode=pl.Buffered(1))


def _rms(x, g):
    return x * lax.rsqrt(jnp.mean(x * x, axis=-1, keepdims=True) + RMS_EPS) * g


def _rms_kernel(x_ref, g_ref, h_ref):
    h_ref[...] = _rms(x_ref[...], g_ref[...]).astype(h_ref.dtype)


def _rms_call(x, g, tm):
    m, d = x.shape
    return pl.pallas_call(
        _rms_kernel,
        grid=(m // tm,),
        in_specs=[pl.BlockSpec((tm, d), lambda i: (i, 0)), pl.BlockSpec((1, d), lambda i: (0, 0))],
        out_specs=pl.BlockSpec((tm, d), lambda i: (i, 0)),
        out_shape=jax.ShapeDtypeStruct((m, d), BF16),
        compiler_params=_params(("parallel",), 32),
        name="rms_pre",
    )(x, g)


def _rope_full(a, cos, sin):
    return a * cos + pltpu.roll(a, HEAD_DIM // 2, 1) * sin


def _rope_idx(a, cos, sin, first_half):
    partner = jnp.where(first_half, pltpu.roll(a, LANE - IDX_DIM // 2, 1), pltpu.roll(a, IDX_DIM // 2, 1))
    return a * cos + partner * sin


def _proj_kernel(h_ref, w_ref, cs_ref, cos_ref, sin_ref, icos_ref, isin_ref, o_ref, oc_ref, cmp_ref):
    h = h_ref[...]
    tm = h.shape[0]
    lane = lax.broadcasted_iota(jnp.int32, (tm, LANE), 1)
    first_half = (lane & (IDX_DIM - 1)) < IDX_DIM // 2
    for c0 in range(0, NP_BLOCKS, _PROJ_CHUNK):
        nb = min(_PROJ_CHUNK, NP_BLOCKS - c0)
        acc = jnp.dot(h, w_ref[:, c0 * LANE:(c0 + nb) * LANE], preferred_element_type=F32)
        for b in range(nb):
            col = slice((c0 + b) * LANE, (c0 + b + 1) * LANE)
            a = acc[:, b * LANE:(b + 1) * LANE] * cs_ref[:, col]
            mode = _BLOCK_MODE[c0 + b]
            if mode == MODE_ROPE:
                a = _rope_full(a, cos_ref[...], sin_ref[...])
            elif mode == MODE_IROPE:
                a = _rope_idx(a, icos_ref[...], isin_ref[...], first_half)
            o_ref[:, col] = a.astype(o_ref.dtype)
    acc = jnp.dot(h, w_ref[:, NP:NW], preferred_element_type=F32)
    for b in range(CMP_BLOCKS):
        cmp_ref[b] = acc[:, b * LANE:(b + 1) * LANE]
    for j in range(CMP_STRIDE):
        for b in range(CMP_BLOCKS):
            rows = cmp_ref[b, pl.ds(j, tm // CMP_STRIDE, stride=CMP_STRIDE), :]
            oc_ref[:, j * CMP_COLS + b * LANE:j * CMP_COLS + (b + 1) * LANE] = rows.astype(oc_ref.dtype)


def _proj_call(h, w_in_p, layer, col_scale, tabs, seq, tm):
    m, d = h.shape
    tpb = seq // tm
    tab_spec = pl.BlockSpec((tm, LANE), lambda i: (i % tpb, 0))
    return pl.pallas_call(
        _proj_kernel,
        grid=(m // tm,),
        in_specs=[
            pl.BlockSpec((tm, d), lambda i: (i, 0)),
            _resident((None, d, NW), lambda i: (layer, 0, 0)),
            _resident((1, NP), lambda i: (0, 0)),
            tab_spec, tab_spec, tab_spec, tab_spec,
        ],
        out_specs=[
            pl.BlockSpec((tm, NP), lambda i: (i, 0)),
            pl.BlockSpec((tm // CMP_STRIDE, CMP_STRIDE * CMP_COLS), lambda i: (i, 0)),
        ],
        out_shape=[
            jax.ShapeDtypeStruct((m, NP), BF16),
            jax.ShapeDtypeStruct((m // CMP_STRIDE, CMP_STRIDE * CMP_COLS), BF16),
        ],
        scratch_shapes=[pltpu.VMEM((CMP_BLOCKS, tm, LANE), F32)],
        compiler_params=_params(("parallel",), 52),
        name="in_proj",
    )(h, w_in_p, col_scale, *tabs)


def _transpose_chunks(dst_ref, src_ref, ck):
    rows, cols = src_ref.shape
    for r0 in range(0, rows, LANE):
        c, off = divmod(r0, ck)
        for c0 in range(0, cols, LANE):
            tile = src_ref[r0:r0 + LANE, c0:c0 + LANE].astype(F32)
            dst_ref[c, c0:c0 + LANE, off:off + LANE] = tile.T.astype(dst_ref.dtype)


def _causal_chunks(i, tq, ck):
    return ((i + 1) * tq + ck - 1) // ck


def _flash(qs, k_ref, k_cols, vt_ref, vt_rows, acc_ref, lo, hi, ck, mask_fn):
    n_heads = len(qs)
    tq = qs[0].shape[0]
    acc_ref[...] = jnp.zeros(acc_ref.shape, F32)

    def body(c, carry):
        ms, ls = carry
        r0 = pl.multiple_of(c * ck, ck)
        mask, weight = mask_fn(r0)
        loaded = {}
        new_ms, new_ls = [], []
        for h in range(n_heads):
            kc, vr = k_cols[h], vt_rows[h]
            if (kc.start, vr.start) not in loaded:
                loaded[(kc.start, vr.start)] = (k_ref[pl.ds(r0, ck), kc], vt_ref[c, vr, :])
            k_c, vt_c = loaded[(kc.start, vr.start)]
            s = lax.dot_general(k_c, qs[h], (((1,), (1,)), ((), ())), preferred_element_type=F32)
            s = jnp.where(mask, s, NEG)
            m_new = jnp.maximum(ms[h], jnp.max(s, axis=0, keepdims=True))
            alpha = jnp.exp(ms[h] - m_new)
            p = jnp.exp(s - m_new)
            if weight is not None:
                p = p * weight
            new_ls.append(ls[h] * alpha + jnp.sum(p, axis=0, keepdims=True))
            new_ms.append(m_new)
            acc_ref[h] = acc_ref[h] * alpha + jnp.dot(vt_c, p.astype(BF16), preferred_element_type=F32)
        return tuple(new_ms), tuple(new_ls)

    init = (tuple(jnp.full((1, tq), NEG, F32) for _ in qs), tuple(jnp.zeros((1, tq), F32) for _ in qs))
    _, ls = lax.fori_loop(lo, hi, body, init)
    return [acc_ref[h] / ls[h] for h in range(n_heads)]


def _dsa_kernel(iq_ref, iw_ref, ik_ref, q_ref, k_ref, v_ref, o_ref, vt_ref, key_ref, cut_ref, acc_ref,
                *, topk, ck):
    i = pl.program_id(1)
    tq = q_ref.shape[0]
    s_len = k_ref.shape[0]
    n = _causal_chunks(i, tq, ck)

    @pl.when(i == 0)
    def _():
        _transpose_chunks(vt_ref, v_ref, ck)

    iw_t = iw_ref[...].astype(F32).T[:IDX_HEADS, :]
    lane = lax.broadcasted_iota(jnp.int32, (ck, LANE), 1)
    row = lax.broadcasted_iota(jnp.int32, (ck, tq), 0)
    tcol = i * tq + lax.broadcasted_iota(jnp.int32, (ck, tq), 1)

    def score_body(c, carry):
        r0 = pl.multiple_of(c * ck, ck)
        ik = ik_ref[pl.ds(r0, ck), :]
        halves = (jnp.where(lane < IDX_DIM, ik, jnp.zeros_like(ik)),
                  jnp.where(lane >= IDX_DIM, ik, jnp.zeros_like(ik)))
        score = jnp.zeros((ck, tq), F32)
        for p in range(IDX_HEADS // 2):
            blk = iq_ref[:, p * LANE:(p + 1) * LANE]
            for half in range(2):
                j = 2 * p + half
                lg = lax.dot_general(halves[half], blk, (((1,), (1,)), ((), ())), preferred_element_type=F32)
                score = score + jnp.maximum(lg, 0.0) * iw_t[j:j + 1, :]
        bits = pltpu.bitcast(score, jnp.int32)
        key = bits ^ ((bits >> 31) & 0x7FFFFFFF)
        key_ref[pl.ds(r0, ck), :] = jnp.where(r0 + row <= tcol, key, INT_MIN)
        return carry

    lax.fori_loop(0, n, score_body, 0)

    def count(fn):
        def body(c, acc):
            r0 = pl.multiple_of(c * ck, ck)
            hit = fn(key_ref[pl.ds(r0, ck), :], r0)
            return acc + hit.reshape(ck // SUBLANE, SUBLANE, tq).sum(axis=0)

        acc = lax.fori_loop(0, n, body, jnp.zeros((SUBLANE, tq), jnp.int32))
        return jnp.sum(acc, axis=0, keepdims=True)

    def bit_step(b, thr):
        cand = thr ^ jnp.left_shift(jnp.int32(1), 31 - b)
        cnt = count(lambda kc, r0: jnp.where(kc >= cand, 1, 0))
        return jnp.where(cnt >= topk, cand, thr)

    thr = lax.fori_loop(0, 32, bit_step, jnp.full((1, tq), INT_MIN, jnp.int32))
    short = thr == INT_MIN
    need = topk - count(lambda kc, r0: jnp.where(kc > thr, 1, 0))
    n_eq = count(lambda kc, r0: jnp.where(kc == thr, 1, 0))
    excess = jnp.where(n_eq > need, jnp.where(short, 0, 1), 0)
    cut_ref[...] = jnp.where(short, -1, s_len)

    @pl.when(jnp.max(excess) > 0)
    def _():
        n_bits = max(1, (s_len - 1).bit_length())

        def idx_step(b, cut):
            cand = cut | jnp.left_shift(jnp.int32(1), n_bits - 1 - b)
            below = count(lambda kc, r0: jnp.where(kc == thr, jnp.where(r0 + row < cand, 1, 0), 0))
            return jnp.where(below < need, cand, cut)

        cut = lax.fori_loop(0, n_bits, idx_step, jnp.zeros((1, tq), jnp.int32))
        cut_ref[...] = jnp.where(excess > 0, cut, cut_ref[...])

    cut = cut_ref[...]

    def mask_fn(r0):
        kc = key_ref[pl.ds(r0, ck), :]
        chosen = jnp.where(kc > thr, 1, jnp.where(kc == thr, jnp.where(r0 + row <= cut, 1, 0), 0))
        return chosen > 0, None

    cols = [slice(h * HEAD_DIM, (h + 1) * HEAD_DIM) for h in range(A_HEADS)]
    shared = [slice(0, HEAD_DIM)] * A_HEADS
    outs = _flash([q_ref[:, c] for c in cols], k_ref, shared, vt_ref, shared, acc_ref, 0, n, ck, mask_fn)
    for h, c in enumerate(cols):
        o_ref[:, c] = outs[h].T.astype(o_ref.dtype)


def _dsa_call(p3, tq, ck):
    b, s, _ = p3.shape
    topk = min(DSA_TOPK_MAX, s // 4)
    blk = lambda name, width: _OFF[name] // width
    return pl.pallas_call(
        functools.partial(_dsa_kernel, topk=topk, ck=ck),
        grid=(b, s // tq),
        in_specs=[
            pl.BlockSpec((None, tq, 8 * LANE), lambda bi, i: (bi, i, blk("i_q", 8))),
            pl.BlockSpec((None, tq, LANE), lambda bi, i: (bi, i, blk("i_w", 1))),
            pl.BlockSpec((None, s, LANE), lambda bi, i: (bi, 0, blk("i_k", 1))),
            pl.BlockSpec((None, tq, 4 * LANE), lambda bi, i: (bi, i, blk("a_q", 4))),
            pl.BlockSpec((None, s, LANE), lambda bi, i: (bi, 0, blk("a_k", 1))),
            pl.BlockSpec((None, s, LANE), lambda bi, i: (bi, 0, blk("a_v", 1))),
        ],
        out_specs=pl.BlockSpec((None, tq, A_HEADS * HEAD_DIM), lambda bi, i: (bi, i, 0)),
        out_shape=jax.ShapeDtypeStruct((b, s, A_HEADS * HEAD_DIM), BF16),
        scratch_shapes=[
            pltpu.VMEM((s // ck, HEAD_DIM, ck), BF16),
            pltpu.VMEM((s, tq), jnp.int32),
            pltpu.VMEM((1, tq), jnp.int32),
            pltpu.VMEM((A_HEADS, HEAD_DIM, tq), F32),
        ],
        compiler_params=_params(("parallel", "arbitrary"), 48),
        name="dsa",
    )(p3, p3, p3, p3, p3, p3)


def _dil_kernel(q_ref, k_ref, v_ref, o_ref, vt_ref, acc_ref, *, ck):
    i = pl.program_id(1)
    tq = q_ref.shape[0]

    @pl.when(i == 0)
    def _():
        _transpose_chunks(vt_ref, v_ref, ck)

    row = lax.broadcasted_iota(jnp.int32, (ck, tq), 0)
    tcol = i * tq + lax.broadcasted_iota(jnp.int32, (ck, tq), 1)

    def mask_fn(r0):
        delta = tcol - (r0 + row)
        mult = jnp.zeros((ck, tq), jnp.int32)
        for window, dilation in DILATED_PATTERNS:
            on_stride = (delta & (dilation - 1)) == 0
            mult = mult + jnp.where(on_stride, jnp.where(delta <= window, 1, 0), 0)
        mult = jnp.where(delta >= 0, mult, 0)
        return mult > 0, mult.astype(F32)

    cols = [slice(h * HEAD_DIM, (h + 1) * HEAD_DIM) for h in range(B_HEADS)]
    outs = _flash([q_ref[:, c] for c in cols], k_ref, cols, vt_ref, cols, acc_ref,
                  0, _causal_chunks(i, tq, ck), ck, mask_fn)
    for h, c in enumerate(cols):
        o_ref[:, c] = outs[h].T.astype(o_ref.dtype)


def _dil_call(p3, tq, ck):
    b, s, _ = p3.shape
    width = B_HEADS * HEAD_DIM
    return pl.pallas_call(
        functools.partial(_dil_kernel, ck=ck),
        grid=(b, s // tq),
        in_specs=[
            pl.BlockSpec((None, tq, width), lambda bi, i: (bi, i, _OFF["b_q"] // 4)),
            pl.BlockSpec((None, s, width), lambda bi, i: (bi, 0, _OFF["b_k"] // 4)),
            pl.BlockSpec((None, s, width), lambda bi, i: (bi, 0, _OFF["b_v"] // 4)),
        ],
        out_specs=pl.BlockSpec((None, tq, width), lambda bi, i: (bi, i, 0)),
        out_shape=jax.ShapeDtypeStruct((b, s, width), BF16),
        scratch_shapes=[pltpu.VMEM((s // ck, width, ck), BF16), pltpu.VMEM((B_HEADS, HEAD_DIM, tq), F32)],
        compiler_params=_params(("parallel", "arbitrary"), 48),
        name="dilated",
    )(p3, p3, p3)


def _gelu_tanh(x):
    return 0.5 * x * (1.0 + jnp.tanh(np.float32(np.sqrt(2.0 / np.pi)) * (x + 0.044715 * (x * x * x))))


def _cmp_kernel(*refs):
    n = CMP_STRIDE
    xk, xv = refs[:n], refs[n:2 * n]
    pe_k, w1_k, w2_k, pe_v, w1_v, w2_v, kc_ref, vct_ref = refs[2 * n:]

    def branch(x_refs, pe_ref, w1_ref, w2_ref):
        lo = hi = None
        for j in range(n):
            xj = x_refs[j][...].astype(F32)
            a = (xj + pe_ref[j:j + 1, :]).astype(BF16)
            b = (xj + pe_ref[n + j:n + j + 1, :]).astype(BF16)
            dl = jnp.dot(a, w1_ref[j * HEAD_DIM:(j + 1) * HEAD_DIM, :], preferred_element_type=F32)
            dh = jnp.dot(b, w1_ref[(n + j) * HEAD_DIM:(n + j + 1) * HEAD_DIM, :], preferred_element_type=F32)
            lo = dl if lo is None else lo + dl
            hi = dh if hi is None else hi + dh
        nc = lo.shape[0]
        hid = lo + pltpu.roll(hi, nc - 1, 0)
        return jnp.dot(_gelu_tanh(hid).astype(BF16), w2_ref[...], preferred_element_type=F32)

    kc_ref[...] = branch(xk, pe_k, w1_k, w2_k).astype(kc_ref.dtype)
    vct_ref[...] = branch(xv, pe_v, w1_v, w2_v).T.astype(vct_ref.dtype)


def _cmp_call(pc, b, layer, pe_k, w1_k, w2_k, pe_v, w1_v, w2_v):
    nc = pc.shape[0] // b
    pc3 = pc.reshape(b, nc, CMP_STRIDE * CMP_COLS)
    x_specs = []
    for first in (0, C_KV_GROUPS):
        for j in range(CMP_STRIDE):
            x_specs.append(pl.BlockSpec((None, nc, HEAD_DIM),
                                        lambda bi, g, j=j, first=first: (bi, 0, j * CMP_BLOCKS + first + g)))
    flat = CMP_BLOCK * HEAD_DIM
    w_specs = [
        _resident((None, CMP_BLOCK, HEAD_DIM), lambda bi, g: (layer, 0, 0)),
        _resident((None, flat, CMP_HIDDEN), lambda bi, g: (layer, 0, 0)),
        _resident((None, CMP_HIDDEN, HEAD_DIM), lambda bi, g: (layer, 0, 0)),
    ]
    return pl.pallas_call(
        _cmp_kernel,
        grid=(b, C_KV_GROUPS),
        in_specs=x_specs + w_specs + w_specs,
        out_specs=[
            pl.BlockSpec((None, None, nc, HEAD_DIM), lambda bi, g: (bi, g, 0, 0)),
            pl.BlockSpec((None, None, HEAD_DIM, nc), lambda bi, g: (bi, g, 0, 0)),
        ],
        out_shape=[
            jax.ShapeDtypeStruct((b, C_KV_GROUPS, nc, HEAD_DIM), BF16),
            jax.ShapeDtypeStruct((b, C_KV_GROUPS, HEAD_DIM, nc), BF16),
        ],
        compiler_params=_params(("parallel", "parallel"), 32),
        name="nsa_compress",
    )(*([pc3] * (2 * CMP_STRIDE)), pe_k, w1_k, w2_k, pe_v, w1_v, w2_v)


def _nsa_kernel(q_ref, kc_ref, vct_ref, ks_ref, vs_ref, kw_ref, vw_ref, g_ref, cos_ref, sin_ref,
                isect_ref, expand_ref, o_ref, vst_ref, vwt_ref, acc_ref, *, n_sel, ck):
    i = pl.program_id(2)
    tq = q_ref.shape[0]
    nc = kc_ref.shape[0]
    n_slc = isect_ref.shape[0]

    @pl.when(i == 0)
    def _():
        _transpose_chunks(vst_ref, vs_ref, ck)
        _transpose_chunks(vwt_ref, vw_ref, ck)

    t_row = i * tq + lax.broadcasted_iota(jnp.int32, (1, tq), 1)

    kc = kc_ref[...]
    vct = vct_ref[...]
    cend = lax.broadcasted_iota(jnp.int32, (nc, tq), 0) * CMP_STRIDE + (CMP_BLOCK - 1)
    cmask = cend <= t_row
    o_cmp = []
    p_sum = jnp.zeros((nc, tq), F32)
    for r in range(C_GROUP_SIZE):
        q_r = q_ref[:, r * HEAD_DIM:(r + 1) * HEAD_DIM]
        sc = lax.dot_general(kc, q_r, (((1,), (1,)), ((), ())), preferred_element_type=F32)
        sc = jnp.where(cmask, sc, NEG)
        mx = jnp.max(sc, axis=0, keepdims=True)
        e = jnp.where(cmask, jnp.exp(sc - mx), 0.0)
        den = jnp.sum(e, axis=0, keepdims=True)
        p = e / jnp.where(den > 0, den, 1.0)
        p_sum = p_sum + p
        o_cmp.append(jnp.dot(vct, p.astype(BF16), preferred_element_type=F32))

    isect = isect_ref[...]
    p_hi = p_sum.astype(BF16)
    p_lo = (p_sum - p_hi.astype(F32)).astype(BF16)
    imp = (jnp.dot(isect, p_hi, preferred_element_type=F32)
           + jnp.dot(isect, p_lo, preferred_element_type=F32))
    blk = lax.broadcasted_iota(jnp.int32, (n_slc, tq), 0)
    cur = t_row // SEL_BLOCK
    val = jnp.where(blk == 0, jnp.inf, jnp.where(blk == cur, jnp.inf, jnp.where(blk == cur - 1, jnp.inf, imp)))
    val = jnp.where(blk <= cur, val, -jnp.inf)
    rank = jnp.zeros((n_slc, tq), jnp.int32)
    for m in range(n_slc):
        vm = val[m:m + 1, :]
        before = jnp.where(vm > val, 1, jnp.where(vm == val, jnp.where(blk > m, 1, 0), 0))
        rank = rank + before
    sel = jnp.where(rank < n_sel, 1.0, 0.0)
    if n_slc < LANE:
        sel = jnp.concatenate([sel, jnp.zeros((LANE - n_slc, tq), F32)], axis=0)
    sel = sel.astype(BF16)

    row = lax.broadcasted_iota(jnp.int32, (ck, tq), 0)

    def slc_mask(r0):
        in_sel = jnp.dot(expand_ref[pl.ds(r0, ck), :], sel, preferred_element_type=F32)
        return jnp.where(t_row - (r0 + row) >= 0, in_sel, 0.0) > 0.5, None

    def win_mask(r0):
        delta = t_row - (r0 + row)
        return jnp.where(delta >= 0, jnp.where(delta <= WIN_SIZE - 1, 1, 0), 0) > 0, None

    cos, sin = cos_ref[...], sin_ref[...]
    cols = [slice(r * HEAD_DIM, (r + 1) * HEAD_DIM) for r in range(C_GROUP_SIZE)]
    shared = [slice(0, HEAD_DIM)] * C_GROUP_SIZE
    q_rot = [_rope_full(q_ref[:, c].astype(F32), cos, sin).astype(BF16) for c in cols]
    hi = _causal_chunks(i, tq, ck)
    o_slc = _flash(q_rot, ks_ref, shared, vst_ref, shared, acc_ref, 0, hi, ck, slc_mask)
    lo = jnp.maximum(i * tq - (WIN_SIZE - 1), 0) // ck
    o_win = _flash(q_rot, kw_ref, shared, vwt_ref, shared, acc_ref, lo, hi, ck, win_mask)

    gates = jax.nn.sigmoid(g_ref[...].astype(F32).T)
    for r, c in enumerate(cols):
        o_t = (gates[3 * r:3 * r + 1, :] * o_cmp[r] + gates[3 * r + 1:3 * r + 2, :] * o_slc[r]
               + gates[3 * r + 2:3 * r + 3, :] * o_win[r])
        o_ref[:, c] = o_t.T.astype(o_ref.dtype)


def _nsa_call(p3, kc, vct, cos, sin, isect, expand, tq, ck):
    b, s, _ = p3.shape
    nc = kc.shape[2]
    n_slc = s // SEL_BLOCK
    width = C_GROUP_SIZE * HEAD_DIM
    kv = lambda name: pl.BlockSpec((None, s, HEAD_DIM), lambda bi, g, i, name=name: (bi, 0, _OFF[name] + g))
    vt_scratch = pltpu.VMEM((s // ck, HEAD_DIM, ck), BF16)
    return pl.pallas_call(
        functools.partial(_nsa_kernel, n_sel=min(SEL_COUNT, n_slc), ck=ck),
        grid=(b, C_KV_GROUPS, s // tq),
        in_specs=[
            pl.BlockSpec((None, tq, width), lambda bi, g, i: (bi, i, _OFF["c_q"] // 4 + g)),
            pl.BlockSpec((None, None, nc, HEAD_DIM), lambda bi, g, i: (bi, g, 0, 0)),
            pl.BlockSpec((None, None, HEAD_DIM, nc), lambda bi, g, i: (bi, g, 0, 0)),
            kv("c_ks"), kv("c_vs"), kv("c_kw"), kv("c_vw"),
            pl.BlockSpec((None, tq, LANE), lambda bi, g, i: (bi, i, _OFF["c_g0"] + g)),
            pl.BlockSpec((tq, LANE), lambda bi, g, i: (i, 0)),
            pl.BlockSpec((tq, LANE), lambda bi, g, i: (i, 0)),
            pl.BlockSpec(isect.shape, lambda bi, g, i: (0, 0)),
            pl.BlockSpec(expand.shape, lambda bi, g, i: (0, 0)),
        ],
        out_specs=pl.BlockSpec((None, tq, width), lambda bi, g, i: (bi, i, g)),
        out_shape=jax.ShapeDtypeStruct((b, s, C_HEADS * HEAD_DIM), BF16),
        scratch_shapes=[vt_scratch, vt_scratch, pltpu.VMEM((C_GROUP_SIZE, HEAD_DIM, tq), F32)],
        compiler_params=_params(("parallel", "parallel", "arbitrary"), 48),
        name="nsa",
    )(p3, kc, vct, p3, p3, p3, p3, p3, cos, sin, isect, expand)


def _out_kernel(oa_ref, ob_ref, oc_ref, w_ref, x_ref, gpost_ref, gnext_ref, xo_ref, h_ref):
    na, nb = oa_ref.shape[1], ob_ref.shape[1]
    y = jnp.dot(oa_ref[...], w_ref[0:na, :], preferred_element_type=F32)
    y = y + jnp.dot(ob_ref[...], w_ref[na:na + nb, :], preferred_element_type=F32)
    y = y + jnp.dot(oc_ref[...], w_ref[na + nb:, :], preferred_element_type=F32)
    x = x_ref[...] + _rms(y, gpost_ref[...])
    xo_ref[...] = x
    h_ref[...] = _rms(x, gnext_ref[...]).astype(h_ref.dtype)


def _out_call(o_a, o_b, o_c, w_out_p, layer, x, g_post, g_next, tm):
    m, d = x.shape
    row = lambda width: pl.BlockSpec((tm, width), lambda i: (i, 0))
    vec = pl.BlockSpec((1, d), lambda i: (0, 0))
    return pl.pallas_call(
        _out_kernel,
        grid=(m // tm,),
        in_specs=[
            row(o_a.shape[1]), row(o_b.shape[1]), row(o_c.shape[1]),
            _resident((None, w_out_p.shape[1], d), lambda i: (layer, 0, 0)),
            row(d), vec, vec,
        ],
        out_specs=[row(d), row(d)],
        out_shape=[jax.ShapeDtypeStruct((m, d), F32), jax.ShapeDtypeStruct((m, d), BF16)],
        compiler_params=_params(("parallel",), 48),
        name="out_proj",
    )(o_a, o_b, o_c, w_out_p, x, g_post, g_next)


def _ffn_kernel(h_ref, wg_ref, wu_ref, wd_ref, x_ref, gpost_ref, gnext_ref, xo_ref, ho_ref, acc_ref):
    f = pl.program_id(1)
    h = h_ref[...]
    g = jnp.dot(h, wg_ref[...], preferred_element_type=F32)
    u = jnp.dot(h, wu_ref[...], preferred_element_type=F32)
    a = (g * jax.nn.sigmoid(g) * u).astype(BF16)
    part = jnp.dot(a, wd_ref[...], preferred_element_type=F32)

    @pl.when(f == 0)
    def _():
        acc_ref[...] = part

    @pl.when(f > 0)
    def _():
        acc_ref[...] += part

    @pl.when(f == pl.num_programs(1) - 1)
    def _():
        x = x_ref[...] + _rms(acc_ref[...], gpost_ref[...])
        xo_ref[...] = x
        ho_ref[...] = _rms(x, gnext_ref[...]).astype(ho_ref.dtype)


def _ffn_call(h, wg, wu, wd, layer, x, g_post, g_next, tm, tf):
    m, d = x.shape
    d_ff = wg.shape[2]
    row = pl.BlockSpec((tm, d), lambda i, f: (i, 0))
    vec = pl.BlockSpec((1, d), lambda i, f: (0, 0))
    return pl.pallas_call(
        _ffn_kernel,
        grid=(m // tm, d_ff // tf),
        in_specs=[
            row,
            pl.BlockSpec((None, d, tf), lambda i, f: (layer, 0, f)),
            pl.BlockSpec((None, d, tf), lambda i, f: (layer, 0, f)),
            pl.BlockSpec((None, tf, d), lambda i, f: (layer, f, 0)),
            row, vec, vec,
        ],
        out_specs=[row, row],
        out_shape=[jax.ShapeDtypeStruct((m, d), F32), jax.ShapeDtypeStruct((m, d), BF16)],
        scratch_shapes=[pltpu.VMEM((tm, d), F32)],
        compiler_params=_params(("parallel", "arbitrary"), 52),
        name="ffn",
    )(h, wg, wu, wd, x, g_post, g_next)


def _pad_cols(w, width):
    return jnp.pad(w, ((0, 0), (0, 0), (0, width - w.shape[-1])))


def _prep_w_in(w_in):
    widths = (A_HEADS * HEAD_DIM, HEAD_DIM, HEAD_DIM, IDX_HEADS * IDX_DIM, IDX_DIM, IDX_HEADS,
              B_HEADS * HEAD_DIM, B_HEADS * HEAD_DIM, B_HEADS * HEAD_DIM, C_HEADS * HEAD_DIM,
              C_KV_GROUPS * HEAD_DIM, C_KV_GROUPS * HEAD_DIM, C_KV_GROUPS * HEAD_DIM, C_KV_GROUPS * HEAD_DIM,
              C_KV_GROUPS * HEAD_DIM, C_KV_GROUPS * HEAD_DIM, 3 * C_HEADS)
    names = ("a_q", "a_k", "a_v", "i_q", "i_k", "i_w", "b_q", "b_k", "b_v", "c_q",
             "c_kc", "c_vc", "c_ks", "c_vs", "c_kw", "c_vw", "c_g")
    offs = np.concatenate([[0], np.cumsum(widths)])
    src = {n: w_in[:, :, int(offs[k]):int(offs[k + 1])] for k, n in enumerate(names)}
    per_group = 3 * C_GROUP_SIZE
    src["c_g0"] = _pad_cols(src["c_g"][:, :, :per_group], LANE)
    src["c_g1"] = _pad_cols(src["c_g"][:, :, per_group:], LANE)
    src["i_k"] = jnp.concatenate([src["i_k"], src["i_k"]], axis=-1)
    src["i_w"] = _pad_cols(src["i_w"], LANE)
    order = [n for n, _, _, _ in _SEGMENTS] + [n for n, _ in _CMP_SEGMENTS]
    return jnp.concatenate([src[n] for n in order], axis=-1).astype(BF16)


def _rope_tables(seq):
    def tables(dim):
        inv = 1.0 / (ROPE_THETA ** (jnp.arange(0, dim, 2, dtype=F32) / dim))
        ang = jnp.arange(seq, dtype=F32)[:, None] * inv[None, :]
        cos, sin = jnp.cos(ang), jnp.sin(ang)
        reps = LANE // dim
        return (jnp.tile(jnp.concatenate([cos, cos], axis=-1), (1, reps)),
                jnp.tile(jnp.concatenate([-sin, sin], axis=-1), (1, reps)))
    return tables(HEAD_DIM) + tables(IDX_DIM)


def _selection_constants(seq):
    nc = seq // CMP_STRIDE
    n_slc = seq // SEL_BLOCK
    c_start = np.arange(nc) * CMP_STRIDE
    n_start = np.arange(n_slc) * SEL_BLOCK
    isect = ((c_start[None, :] < n_start[:, None] + SEL_BLOCK)
             & (c_start[None, :] + CMP_BLOCK > n_start[:, None])
             & (np.arange(nc)[None, :] < nc - CMP_BLOCK // CMP_STRIDE + 1)).astype(np.float32)
    expand = np.zeros((seq, max(n_slc, LANE)), np.float32)
    expand[np.arange(seq), np.arange(seq) // SEL_BLOCK] = 1.0
    return jnp.asarray(isect, BF16), jnp.asarray(expand, BF16)


def _tiles(seq):
    return dict(tq=LANE, ck=min(256, seq), tm_proj=min(256, seq), tm_out=min(512, seq),
                tm_ffn=min(512, seq), tf=512)


def kernel(x, w_in, w_out, cmp_pe_k, cmp_w1_k, cmp_w2_k, cmp_pe_v, cmp_w1_v, cmp_w2_v,
           w_gate, w_up, w_down, g_pre_mix, g_post_mix, g_pre_ffn, g_post_ffn):
    b, s, d = x.shape
    depth = w_in.shape[0]
    m = b * s
    t = _tiles(s)
    tq, ck = t["tq"], t["ck"]

    w_in_p = _prep_w_in(w_in)
    w_out_p = w_out.astype(BF16)
    wg, wu, wd = w_gate.astype(BF16), w_up.astype(BF16), w_down.astype(BF16)
    w1_k, w2_k = cmp_w1_k.astype(BF16), cmp_w2_k.astype(BF16)
    w1_v, w2_v = cmp_w1_v.astype(BF16), cmp_w2_v.astype(BF16)
    col_scale = jnp.asarray(_COL_SCALE)
    tabs = _rope_tables(s)
    isect, expand = _selection_constants(s)
    vec = lambda g, layer: g[layer][None, :]

    xf = x.reshape(m, d)
    h = _rms_call(xf, vec(g_pre_mix, 0), t["tm_out"])
    for layer in range(depth):
        p, pc = _proj_call(h, w_in_p, layer, col_scale, tabs, s, t["tm_proj"])
        p3 = p.reshape(b, s, NP)
        o_a = _dsa_call(p3, tq, ck)
        o_b = _dil_call(p3, tq, ck)
        kc, vct = _cmp_call(pc, b, layer, cmp_pe_k, w1_k, w2_k, cmp_pe_v, w1_v, w2_v)
        o_c = _nsa_call(p3, kc, vct, tabs[0], tabs[1], isect, expand, tq, ck)
        xf, h = _out_call(o_a.reshape(m, -1), o_b.reshape(m, -1), o_c.reshape(m, -1), w_out_p, layer,
                          xf, vec(g_post_mix, layer), vec(g_pre_ffn, layer), t["tm_out"])
        g_next = vec(g_pre_mix, (layer + 1) % depth)
        xf, h = _ffn_call(h, wg, wu, wd, layer, xf, vec(g_post_ffn, layer), g_next, t["tm_ffn"], t["tf"])
    return xf.reshape(b, s, d)
```

```python
import functools

import numpy as np
import jax
import jax.numpy as jnp
from jax import lax
from jax.experimental import pallas as pl
from jax.experimental.pallas import tpu as pltpu

F32 = jnp.float32
BF16 = jnp.bfloat16

LANE = 128
SUBLANE = 8
HEAD_DIM = 128
A_HEADS = 4
B_HEADS = 4
C_HEADS = 8
C_KV_GROUPS = 2
C_GROUP_SIZE = C_HEADS // C_KV_GROUPS
IDX_HEADS = 16
IDX_DIM = 64
DSA_TOPK_MAX = 256
DILATED_PATTERNS = ((128, 1), (512, 4), (2048, 16))
CMP_BLOCK = 32
CMP_STRIDE = 16
CMP_HIDDEN = 256
SEL_BLOCK = 64
SEL_COUNT = 16
WIN_SIZE = 512
ROPE_THETA = 10000.0
RMS_EPS = 1e-6
ATTN_SCALE = HEAD_DIM ** -0.5
IDX_SCALE = IDX_DIM ** -0.5 * IDX_HEADS ** -0.5

NEG = -1e30
INT_MIN = -(2 ** 31)
COUNT_ROWS = 8 * SUBLANE
SCORE_ROWS = 256

MODE_NONE, MODE_ROPE, MODE_IROPE = 0, 1, 2

_SEGMENTS = (
    ("i_q", 8, MODE_IROPE, IDX_SCALE),
    ("c_q", 8, MODE_NONE, ATTN_SCALE),
    ("a_q", 4, MODE_ROPE, ATTN_SCALE),
    ("b_q", 4, MODE_ROPE, ATTN_SCALE),
    ("b_k", 4, MODE_ROPE, 1.0),
    ("b_v", 4, MODE_NONE, 1.0),
    ("c_ks", 2, MODE_ROPE, 1.0),
    ("c_vs", 2, MODE_NONE, 1.0),
    ("c_kw", 2, MODE_ROPE, 1.0),
    ("c_vw", 2, MODE_NONE, 1.0),
    ("a_k", 1, MODE_ROPE, 1.0),
    ("a_v", 1, MODE_NONE, 1.0),
    ("i_k", 1, MODE_IROPE, 1.0),
    ("i_w", 1, MODE_NONE, 1.0),
    ("c_g0", 1, MODE_NONE, 1.0),
    ("c_g1", 1, MODE_NONE, 1.0),
)
_CMP_SEGMENTS = (("c_kc", 2), ("c_vc", 2))
_OFF = {}
_o = 0
for _n, _w, _m, _s in _SEGMENTS:
    _OFF[_n] = _o
    _o += _w
NP_BLOCKS = _o
NP = NP_BLOCKS * LANE
CMP_BLOCKS = sum(w for _, w in _CMP_SEGMENTS)
CMP_COLS = CMP_BLOCKS * LANE
NW = NP + CMP_COLS
_BLOCK_MODE = tuple(m for _, w, m, _ in _SEGMENTS for _ in range(w))
_COL_SCALE = np.repeat(np.array([s for _, w, _, s in _SEGMENTS for _ in range(w)], np.float32), LANE)[None, :]
_PROJ_CHUNK = 4


def _params(sem, vmem_mb):
    return pltpu.CompilerParams(dimension_semantics=sem, vmem_limit_bytes=vmem_mb * 1024 * 1024)


def _resident(block_shape, index_map):
    return pl.BlockSpec(block_shape, index_map, pipeline_mode=pl.Buffered(1))


def _rms(x, g):
    return x * lax.rsqrt(jnp.mean(x * x, axis=-1, keepdims=True) + RMS_EPS) * g


def _rms_kernel(x_ref, g_ref, h_ref):
    h_ref[...] = _rms(x_ref[...], g_ref[...]).astype(h_ref.dtype)


def _rms_call(x, g, tm):
    m, d = x.shape
    return pl.pallas_call(
        _rms_kernel,
        grid=(m // tm,),
        in_specs=[pl.BlockSpec((tm, d), lambda i: (i, 0)), pl.BlockSpec((1, d), lambda i: (0, 0))],
        out_specs=pl.BlockSpec((tm, d), lambda i: (i, 0)),
        out_shape=jax.ShapeDtypeStruct((m, d), BF16),
        compiler_params=_params(("parallel",), 32),
        name="rms_pre",
    )(x, g)


def _rope_full(a, cos, sin):
    return a * cos + pltpu.roll(a, HEAD_DIM // 2, 1) * sin


def _rope_idx(a, cos, sin, first_half):
    partner = jnp.where(first_half, pltpu.roll(a, LANE - IDX_DIM // 2, 1), pltpu.roll(a, IDX_DIM // 2, 1))
    return a * cos + partner * sin


def _proj_kernel(h_ref, w_ref, cs_ref, cos_ref, sin_ref, icos_ref, isin_ref, o_ref, oc_ref, cmp_ref):
    h = h_ref[...]
    tm = h.shape[0]
    lane = lax.broadcasted_iota(jnp.int32, (tm, LANE), 1)
    first_half = (lane & (IDX_DIM - 1)) < IDX_DIM // 2
    for c0 in range(0, NP_BLOCKS, _PROJ_CHUNK):
        nb = min(_PROJ_CHUNK, NP_BLOCKS - c0)
        acc = jnp.dot(h, w_ref[:, c0 * LANE:(c0 + nb) * LANE], preferred_element_type=F32)
        for b in range(nb):
            col = slice((c0 + b) * LANE, (c0 + b + 1) * LANE)
            a = acc[:, b * LANE:(b + 1) * LANE] * cs_ref[:, col]
            mode = _BLOCK_MODE[c0 + b]
            if mode == MODE_ROPE:
                a = _rope_full(a, cos_ref[...], sin_ref[...])
            elif mode == MODE_IROPE:
                a = _rope_idx(a, icos_ref[...], isin_ref[...], first_half)
            o_ref[:, col] = a.astype(o_ref.dtype)
    acc = jnp.dot(h, w_ref[:, NP:NW], preferred_element_type=F32)
    for b in range(CMP_BLOCKS):
        cmp_ref[b] = acc[:, b * LANE:(b + 1) * LANE]
    for j in range(CMP_STRIDE):
        for b in range(CMP_BLOCKS):
            rows = cmp_ref[b, pl.ds(j, tm // CMP_STRIDE, stride=CMP_STRIDE), :]
            oc_ref[:, j * CMP_COLS + b * LANE:j * CMP_COLS + (b + 1) * LANE] = rows.astype(oc_ref.dtype)


def _proj_call(h, w_in_p, layer, col_scale, tabs, seq, tm):
    m, d = h.shape
    tpb = seq // tm
    tab_spec = pl.BlockSpec((tm, LANE), lambda i: (i % tpb, 0))
    return pl.pallas_call(
        _proj_kernel,
        grid=(m // tm,),
        in_specs=[
            pl.BlockSpec((tm, d), lambda i: (i, 0)),
            _resident((None, d, NW), lambda i: (layer, 0, 0)),
            _resident((1, NP), lambda i: (0, 0)),
            tab_spec, tab_spec, tab_spec, tab_spec,
        ],
        out_specs=[
            pl.BlockSpec((tm, NP), lambda i: (i, 0)),
            pl.BlockSpec((tm // CMP_STRIDE, CMP_STRIDE * CMP_COLS), lambda i: (i, 0)),
        ],
        out_shape=[
            jax.ShapeDtypeStruct((m, NP), BF16),
            jax.ShapeDtypeStruct((m // CMP_STRIDE, CMP_STRIDE * CMP_COLS), BF16),
        ],
        scratch_shapes=[pltpu.VMEM((CMP_BLOCKS, tm, LANE), F32)],
        compiler_params=_params(("parallel",), 52),
        name="in_proj",
    )(h, w_in_p, col_scale, *tabs)


def _transpose_into(dst_ref, src_ref):
    rows, cols = src_ref.shape
    for r0 in range(0, rows, LANE):
        for c0 in range(0, cols, LANE):
            tile = src_ref[r0:r0 + LANE, c0:c0 + LANE].astype(F32)
            dst_ref[c0:c0 + LANE, r0:r0 + LANE] = tile.T.astype(dst_ref.dtype)


def _transpose_tiles(dst_ref, src_ref):
    rows, _ = src_ref.shape
    for r in range(rows // LANE):
        tile = src_ref[r * LANE:(r + 1) * LANE, :].astype(F32)
        dst_ref[r] = tile.T.astype(dst_ref.dtype)


def _for_causal_prefix(i, tq, s_len, step, body):
    n = ((i + 1) * tq + step - 1) // step
    for v in range(1, s_len // step + 1):
        pl.when(n == v)(functools.partial(body, v * step))


def _attend(k, q, pv, mask, weight=None):
    s = lax.dot_general(k, q, (((1,), (1,)), ((), ())), preferred_element_type=F32)
    s = jnp.where(mask, s, NEG)
    mx = _col_reduce(s, jnp.maximum)
    p = jnp.exp(s - mx)
    if weight is not None:
        p = p * weight
    den = _col_reduce(p, jnp.add)
    return pv(p.astype(BF16)) / den


def _col_reduce(x, op):
    rows = x.shape[0]
    slab = min(COUNT_ROWS, rows)
    parts = [x[r:r + slab] for r in range(0, rows, slab)]
    while len(parts) > 1:
        pairs = [op(parts[j], parts[j + 1]) for j in range(0, len(parts) - 1, 2)]
        parts = pairs + parts[len(parts) - len(parts) % 2:]
    x = parts[0]
    while x.shape[0] > SUBLANE:
        half = x.shape[0] // 2
        x = op(x[:half], x[half:])
    if op is jnp.maximum:
        return jnp.max(x, axis=0, keepdims=True)
    return jnp.sum(x, axis=0, keepdims=True)


def _count(hit):
    return _col_reduce(hit, jnp.add)


def _dsa_prefix(rows, i, iq_ref, iw_t, ik_ref, q_ref, k_ref, vt_ref, o_ref, key_ref, cut_ref, topk):
    tq = q_ref.shape[0]
    chunk = min(SCORE_ROWS, rows)
    lane = lax.broadcasted_iota(jnp.int32, (chunk, LANE), 1)
    crow = lax.broadcasted_iota(jnp.int32, (chunk, tq), 0)
    ctcol = i * tq + lax.broadcasted_iota(jnp.int32, (chunk, tq), 1)

    for r0 in range(0, rows, chunk):
        ik = ik_ref[r0:r0 + chunk, :]
        halves = (jnp.where(lane < IDX_DIM, ik, jnp.zeros_like(ik)),
                  jnp.where(lane >= IDX_DIM, ik, jnp.zeros_like(ik)))
        score = jnp.zeros((chunk, tq), F32)
        for p in range(IDX_HEADS // 2):
            blk = iq_ref[:, p * LANE:(p + 1) * LANE]
            for half in range(2):
                j = 2 * p + half
                lg = lax.dot_general(halves[half], blk, (((1,), (1,)), ((), ())), preferred_element_type=F32)
                score = score + jnp.maximum(lg, 0.0) * iw_t[j:j + 1, :]
        bits = pltpu.bitcast(score, jnp.int32)
        key = bits ^ ((bits >> 31) & 0x7FFFFFFF)
        key_ref[r0:r0 + chunk, :] = jnp.where(r0 + crow <= ctcol, key, INT_MIN)

    def bit_step(b, thr):
        cand = thr ^ jnp.left_shift(jnp.int32(1), 31 - b)
        cnt = _count(jnp.where(key_ref[0:rows, :] >= cand, 1, 0))
        return jnp.where(cnt >= topk, cand, thr)

    thr = lax.fori_loop(0, 32, bit_step, jnp.full((1, tq), INT_MIN, jnp.int32))
    key = key_ref[0:rows, :]
    srow = lax.broadcasted_iota(jnp.int32, (rows, tq), 0)
    short = thr == INT_MIN
    need = topk - _count(jnp.where(key > thr, 1, 0))
    n_eq = _count(jnp.where(key == thr, 1, 0))
    excess = jnp.where(n_eq > need, jnp.where(short, 0, 1), 0)
    cut_ref[...] = jnp.where(short, -1, rows)

    @pl.when(jnp.max(excess) > 0)
    def _():
        n_bits = max(1, (rows - 1).bit_length())
        eq = key_ref[0:rows, :] == thr

        def idx_step(b, cut):
            cand = cut | jnp.left_shift(jnp.int32(1), n_bits - 1 - b)
            below = _count(jnp.where(eq, jnp.where(srow < cand, 1, 0), 0))
            return jnp.where(below < need, cand, cut)

        cut = lax.fori_loop(0, n_bits, idx_step, jnp.zeros((1, tq), jnp.int32))
        cut_ref[...] = jnp.where(excess > 0, cut, cut_ref[...])

    cut = cut_ref[...]
    chosen = jnp.where(key > thr, 1, jnp.where(key == thr, jnp.where(srow <= cut, 1, 0), 0))
    mask = chosen > 0

    k = k_ref[0:rows, :]
    vt = vt_ref[:, 0:rows]
    pv = lambda p: jnp.dot(vt, p, preferred_element_type=F32)
    for h in range(A_HEADS):
        col = slice(h * HEAD_DIM, (h + 1) * HEAD_DIM)
        o_ref[:, col] = _attend(k, q_ref[:, col], pv, mask).T.astype(o_ref.dtype)


def _dsa_kernel(iq_ref, iw_ref, ik_ref, q_ref, k_ref, v_ref, o_ref, vt_ref, key_ref, cut_ref, *, topk, step):
    i = pl.program_id(1)
    tq = q_ref.shape[0]
    s_len = k_ref.shape[0]

    @pl.when(i == 0)
    def _():
        _transpose_into(vt_ref, v_ref)

    iw_t = iw_ref[...].astype(F32).T[:IDX_HEADS, :]
    _for_causal_prefix(i, tq, s_len, step, functools.partial(
        _dsa_prefix, i=i, iq_ref=iq_ref, iw_t=iw_t, ik_ref=ik_ref, q_ref=q_ref, k_ref=k_ref, vt_ref=vt_ref,
        o_ref=o_ref, key_ref=key_ref, cut_ref=cut_ref, topk=topk))


def _dsa_call(p3, tq, step):
    b, s, _ = p3.shape
    topk = min(DSA_TOPK_MAX, s // 4)
    blk = lambda name, width: _OFF[name] // width
    return pl.pallas_call(
        functools.partial(_dsa_kernel, topk=topk, step=step),
        grid=(b, s // tq),
        in_specs=[
            pl.BlockSpec((None, tq, 8 * LANE), lambda bi, i: (bi, i, blk("i_q", 8))),
            pl.BlockSpec((None, tq, LANE), lambda bi, i: (bi, i, blk("i_w", 1))),
            pl.BlockSpec((None, s, LANE), lambda bi, i: (bi, 0, blk("i_k", 1))),
            pl.BlockSpec((None, tq, 4 * LANE), lambda bi, i: (bi, i, blk("a_q", 4))),
            pl.BlockSpec((None, s, LANE), lambda bi, i: (bi, 0, blk("a_k", 1))),
            pl.BlockSpec((None, s, LANE), lambda bi, i: (bi, 0, blk("a_v", 1))),
        ],
        out_specs=pl.BlockSpec((None, tq, A_HEADS * HEAD_DIM), lambda bi, i: (bi, i, 0)),
        out_shape=jax.ShapeDtypeStruct((b, s, A_HEADS * HEAD_DIM), BF16),
        scratch_shapes=[
            pltpu.VMEM((HEAD_DIM, s), BF16),
            pltpu.VMEM((s, tq), jnp.int32),
            pltpu.VMEM((1, tq), jnp.int32),
        ],
        compiler_params=_params(("parallel", "arbitrary"), 48),
        name="dsa",
    )(p3, p3, p3, p3, p3, p3)


def _dil_prefix(rows, i, q_ref, k_ref, vt_ref, o_ref):
    tq = q_ref.shape[0]
    srow = lax.broadcasted_iota(jnp.int32, (rows, tq), 0)
    tcol = i * tq + lax.broadcasted_iota(jnp.int32, (rows, tq), 1)
    delta = tcol - srow
    mult = jnp.zeros((rows, tq), jnp.int32)
    for window, dilation in DILATED_PATTERNS:
        on_stride = (delta & (dilation - 1)) == 0
        mult = mult + jnp.where(on_stride, jnp.where(delta <= window, 1, 0), 0)
    mult = jnp.where(delta >= 0, mult, 0)
    mask = mult > 0
    weight = mult.astype(F32)
    for h in range(B_HEADS):
        col = slice(h * HEAD_DIM, (h + 1) * HEAD_DIM)
        vt = vt_ref[col, 0:rows]
        pv = lambda p, vt=vt: jnp.dot(vt, p, preferred_element_type=F32)
        o_ref[:, col] = _attend(k_ref[0:rows, col], q_ref[:, col], pv, mask, weight).T.astype(o_ref.dtype)


def _dil_kernel(q_ref, k_ref, v_ref, o_ref, vt_ref, *, step):
    i = pl.program_id(1)

    @pl.when(i == 0)
    def _():
        _transpose_into(vt_ref, v_ref)

    _for_causal_prefix(i, q_ref.shape[0], k_ref.shape[0], step, functools.partial(
        _dil_prefix, i=i, q_ref=q_ref, k_ref=k_ref, vt_ref=vt_ref, o_ref=o_ref))


def _dil_call(p3, tq, step):
    b, s, _ = p3.shape
    width = B_HEADS * HEAD_DIM
    return pl.pallas_call(
        functools.partial(_dil_kernel, step=step),
        grid=(b, s // tq),
        in_specs=[
            pl.BlockSpec((None, tq, width), lambda bi, i: (bi, i, _OFF["b_q"] // 4)),
            pl.BlockSpec((None, s, width), lambda bi, i: (bi, 0, _OFF["b_k"] // 4)),
            pl.BlockSpec((None, s, width), lambda bi, i: (bi, 0, _OFF["b_v"] // 4)),
        ],
        out_specs=pl.BlockSpec((None, tq, width), lambda bi, i: (bi, i, 0)),
        out_shape=jax.ShapeDtypeStruct((b, s, width), BF16),
        scratch_shapes=[pltpu.VMEM((width, s), BF16)],
        compiler_params=_params(("parallel", "arbitrary"), 48),
        name="dilated",
    )(p3, p3, p3)


def _gelu_tanh(x):
    return 0.5 * x * (1.0 + jnp.tanh(np.float32(np.sqrt(2.0 / np.pi)) * (x + 0.044715 * (x * x * x))))


def _cmp_kernel(*refs):
    n = CMP_STRIDE
    xk, xv = refs[:n], refs[n:2 * n]
    pe_k, w1_k, w2_k, pe_v, w1_v, w2_v, kc_ref, vct_ref = refs[2 * n:]

    def branch(x_refs, pe_ref, w1_ref, w2_ref):
        lo = hi = None
        for j in range(n):
            xj = x_refs[j][...].astype(F32)
            a = (xj + pe_ref[j:j + 1, :]).astype(BF16)
            b = (xj + pe_ref[n + j:n + j + 1, :]).astype(BF16)
            dl = jnp.dot(a, w1_ref[j * HEAD_DIM:(j + 1) * HEAD_DIM, :], preferred_element_type=F32)
            dh = jnp.dot(b, w1_ref[(n + j) * HEAD_DIM:(n + j + 1) * HEAD_DIM, :], preferred_element_type=F32)
            lo = dl if lo is None else lo + dl
            hi = dh if hi is None else hi + dh
        nc = lo.shape[0]
        hid = lo + pltpu.roll(hi, nc - 1, 0)
        return jnp.dot(_gelu_tanh(hid).astype(BF16), w2_ref[...], preferred_element_type=F32)

    kc_ref[...] = branch(xk, pe_k, w1_k, w2_k).astype(kc_ref.dtype)
    vct_ref[...] = branch(xv, pe_v, w1_v, w2_v).T.astype(vct_ref.dtype)


def _cmp_call(pc, b, layer, pe_k, w1_k, w2_k, pe_v, w1_v, w2_v):
    nc = pc.shape[0] // b
    pc3 = pc.reshape(b, nc, CMP_STRIDE * CMP_COLS)
    x_specs = []
    for first in (0, C_KV_GROUPS):
        for j in range(CMP_STRIDE):
            x_specs.append(pl.BlockSpec((None, nc, HEAD_DIM),
                                        lambda bi, g, j=j, first=first: (bi, 0, j * CMP_BLOCKS + first + g)))
    flat = CMP_BLOCK * HEAD_DIM
    w_specs = [
        _resident((None, CMP_BLOCK, HEAD_DIM), lambda bi, g: (layer, 0, 0)),
        _resident((None, flat, CMP_HIDDEN), lambda bi, g: (layer, 0, 0)),
        _resident((None, CMP_HIDDEN, HEAD_DIM), lambda bi, g: (layer, 0, 0)),
    ]
    return pl.pallas_call(
        _cmp_kernel,
        grid=(b, C_KV_GROUPS),
        in_specs=x_specs + w_specs + w_specs,
        out_specs=[
            pl.BlockSpec((None, None, nc, HEAD_DIM), lambda bi, g: (bi, g, 0, 0)),
            pl.BlockSpec((None, None, HEAD_DIM, nc), lambda bi, g: (bi, g, 0, 0)),
        ],
        out_shape=[
            jax.ShapeDtypeStruct((b, C_KV_GROUPS, nc, HEAD_DIM), BF16),
            jax.ShapeDtypeStruct((b, C_KV_GROUPS, HEAD_DIM, nc), BF16),
        ],
        compiler_params=_params(("parallel", "parallel"), 32),
        name="nsa_compress",
    )(*([pc3] * (2 * CMP_STRIDE)), pe_k, w1_k, w2_k, pe_v, w1_v, w2_v)


def _nsa_selected(rows, t_row, sel, q_rot, ks_ref, vst_ref, expand_ref, acc_ref):
    tq = sel.shape[1]
    in_sel = jnp.dot(expand_ref[0:rows, :], sel, preferred_element_type=F32)
    srow = lax.broadcasted_iota(jnp.int32, (rows, tq), 0)
    mask = jnp.where(t_row - srow >= 0, in_sel, 0.0) > 0.5
    k = ks_ref[0:rows, :]
    vt = vst_ref[:, 0:rows]
    pv = lambda p: jnp.dot(vt, p, preferred_element_type=F32)
    for r in range(C_GROUP_SIZE):
        acc_ref[r] = _attend(k, q_rot[r], pv, mask)


def _nsa_kernel(q_ref, kc_ref, vct_ref, ks_ref, vs_ref, kw_ref, vw_ref, g_ref, cos_ref, sin_ref,
                isect_ref, expand_ref, o_ref, vst_ref, vwt_ref, acc_ref, *, n_sel, step):
    i = pl.program_id(2)
    tq = q_ref.shape[0]
    s_len = ks_ref.shape[0]
    nc = kc_ref.shape[0]
    n_slc = isect_ref.shape[0]

    @pl.when(i == 0)
    def _():
        _transpose_into(vst_ref, vs_ref)
        _transpose_tiles(vwt_ref, vw_ref)

    t_row = i * tq + lax.broadcasted_iota(jnp.int32, (1, tq), 1)

    kc = kc_ref[...]
    vct = vct_ref[...]
    cend = lax.broadcasted_iota(jnp.int32, (nc, tq), 0) * CMP_STRIDE + (CMP_BLOCK - 1)
    cmask = cend <= t_row
    o_cmp = []
    p_sum = jnp.zeros((nc, tq), F32)
    for r in range(C_GROUP_SIZE):
        q_r = q_ref[:, r * HEAD_DIM:(r + 1) * HEAD_DIM]
        sc = lax.dot_general(kc, q_r, (((1,), (1,)), ((), ())), preferred_element_type=F32)
        sc = jnp.where(cmask, sc, NEG)
        mx = jnp.max(sc, axis=0, keepdims=True)
        e = jnp.where(cmask, jnp.exp(sc - mx), 0.0)
        den = jnp.sum(e, axis=0, keepdims=True)
        p = e / jnp.where(den > 0, den, 1.0)
        p_sum = p_sum + p
        o_cmp.append(jnp.dot(vct, p.astype(BF16), preferred_element_type=F32))

    isect = isect_ref[...]
    p_hi = p_sum.astype(BF16)
    p_lo = (p_sum - p_hi.astype(F32)).astype(BF16)
    imp = (jnp.dot(isect, p_hi, preferred_element_type=F32)
           + jnp.dot(isect, p_lo, preferred_element_type=F32))
    blk = lax.broadcasted_iota(jnp.int32, (n_slc, tq), 0)
    cur = t_row // SEL_BLOCK
    val = jnp.where(blk == 0, jnp.inf, jnp.where(blk == cur, jnp.inf, jnp.where(blk == cur - 1, jnp.inf, imp)))
    val = jnp.where(blk <= cur, val, -jnp.inf)
    rank = jnp.zeros((n_slc, tq), jnp.int32)
    for m in range(n_slc):
        vm = val[m:m + 1, :]
        before = jnp.where(vm > val, 1, jnp.where(vm == val, jnp.where(blk > m, 1, 0), 0))
        rank = rank + before
    sel = jnp.where(rank < n_sel, 1.0, 0.0)
    if n_slc < LANE:
        sel = jnp.concatenate([sel, jnp.zeros((LANE - n_slc, tq), F32)], axis=0)
    sel = sel.astype(BF16)

    cos, sin = cos_ref[...], sin_ref[...]
    cols = [slice(r * HEAD_DIM, (r + 1) * HEAD_DIM) for r in range(C_GROUP_SIZE)]
    q_rot = [_rope_full(q_ref[:, c].astype(F32), cos, sin).astype(BF16) for c in cols]

    _for_causal_prefix(i, tq, s_len, step, functools.partial(
        _nsa_selected, t_row=t_row, sel=sel, q_rot=q_rot, ks_ref=ks_ref, vst_ref=vst_ref,
        expand_ref=expand_ref, acc_ref=acc_ref))

    w_tiles = min((WIN_SIZE - 1 + tq - 1) // LANE + 1, s_len // LANE)
    w_rows = w_tiles * LANE
    w0 = pl.multiple_of(jnp.maximum(i * tq + tq - w_rows, 0), LANE)
    t0 = w0 // LANE
    delta = t_row - (w0 + lax.broadcasted_iota(jnp.int32, (w_rows, tq), 0))
    win_mask = jnp.where(delta >= 0, jnp.where(delta <= WIN_SIZE - 1, 1, 0), 0) > 0
    kw = kw_ref[pl.ds(w0, w_rows), :]

    def pv_win(p):
        out = None
        for w in range(w_tiles):
            part = jnp.dot(vwt_ref[t0 + w], p[w * LANE:(w + 1) * LANE, :], preferred_element_type=F32)
            out = part if out is None else out + part
        return out

    gates = jax.nn.sigmoid(g_ref[...].astype(F32).T)
    for r, c in enumerate(cols):
        o_win = _attend(kw, q_rot[r], pv_win, win_mask)
        o_t = (gates[3 * r:3 * r + 1, :] * o_cmp[r] + gates[3 * r + 1:3 * r + 2, :] * acc_ref[r]
               + gates[3 * r + 2:3 * r + 3, :] * o_win)
        o_ref[:, c] = o_t.T.astype(o_ref.dtype)


def _nsa_call(p3, kc, vct, cos, sin, isect, expand, tq, step):
    b, s, _ = p3.shape
    nc = kc.shape[2]
    n_slc = s // SEL_BLOCK
    width = C_GROUP_SIZE * HEAD_DIM
    kv = lambda name: pl.BlockSpec((None, s, HEAD_DIM), lambda bi, g, i, name=name: (bi, 0, _OFF[name] + g))
    return pl.pallas_call(
        functools.partial(_nsa_kernel, n_sel=min(SEL_COUNT, n_slc), step=step),
        grid=(b, C_KV_GROUPS, s // tq),
        in_specs=[
            pl.BlockSpec((None, tq, width), lambda bi, g, i: (bi, i, _OFF["c_q"] // 4 + g)),
            pl.BlockSpec((None, None, nc, HEAD_DIM), lambda bi, g, i: (bi, g, 0, 0)),
            pl.BlockSpec((None, None, HEAD_DIM, nc), lambda bi, g, i: (bi, g, 0, 0)),
            kv("c_ks"), kv("c_vs"), kv("c_kw"), kv("c_vw"),
            pl.BlockSpec((None, tq, LANE), lambda bi, g, i: (bi, i, _OFF["c_g0"] + g)),
            pl.BlockSpec((tq, LANE), lambda bi, g, i: (i, 0)),
            pl.BlockSpec((tq, LANE), lambda bi, g, i: (i, 0)),
            pl.BlockSpec(isect.shape, lambda bi, g, i: (0, 0)),
            pl.BlockSpec(expand.shape, lambda bi, g, i: (0, 0)),
        ],
        out_specs=pl.BlockSpec((None, tq, width), lambda bi, g, i: (bi, i, g)),
        out_shape=jax.ShapeDtypeStruct((b, s, C_HEADS * HEAD_DIM), BF16),
        scratch_shapes=[
            pltpu.VMEM((HEAD_DIM, s), BF16),
            pltpu.VMEM((s // LANE, HEAD_DIM, LANE), BF16),
            pltpu.VMEM((C_GROUP_SIZE, HEAD_DIM, tq), F32),
        ],
        compiler_params=_params(("parallel", "parallel", "arbitrary"), 48),
        name="nsa",
    )(p3, kc, vct, p3, p3, p3, p3, p3, cos, sin, isect, expand)


def _out_kernel(oa_ref, ob_ref, oc_ref, w_ref, x_ref, gpost_ref, gnext_ref, xo_ref, h_ref):
    na, nb = oa_ref.shape[1], ob_ref.shape[1]
    y = jnp.dot(oa_ref[...], w_ref[0:na, :], preferred_element_type=F32)
    y = y + jnp.dot(ob_ref[...], w_ref[na:na + nb, :], preferred_element_type=F32)
    y = y + jnp.dot(oc_ref[...], w_ref[na + nb:, :], preferred_element_type=F32)
    x = x_ref[...] + _rms(y, gpost_ref[...])
    xo_ref[...] = x
    h_ref[...] = _rms(x, gnext_ref[...]).astype(h_ref.dtype)


def _out_call(o_a, o_b, o_c, w_out_p, layer, x, g_post, g_next, tm):
    m, d = x.shape
    row = lambda width: pl.BlockSpec((tm, width), lambda i: (i, 0))
    vec = pl.BlockSpec((1, d), lambda i: (0, 0))
    return pl.pallas_call(
        _out_kernel,
        grid=(m // tm,),
        in_specs=[
            row(o_a.shape[1]), row(o_b.shape[1]), row(o_c.shape[1]),
            _resident((None, w_out_p.shape[1], d), lambda i: (layer, 0, 0)),
            row(d), vec, vec,
        ],
        out_specs=[row(d), row(d)],
        out_shape=[jax.ShapeDtypeStruct((m, d), F32), jax.ShapeDtypeStruct((m, d), BF16)],
        compiler_params=_params(("parallel",), 48),
        name="out_proj",
    )(o_a, o_b, o_c, w_out_p, x, g_post, g_next)


def _ffn_kernel(h_ref, wg_ref, wu_ref, wd_ref, x_ref, gpost_ref, gnext_ref, xo_ref, ho_ref, acc_ref):
    f = pl.program_id(1)
    h = h_ref[...]
    g = jnp.dot(h, wg_ref[...], preferred_element_type=F32)
    u = jnp.dot(h, wu_ref[...], preferred_element_type=F32)
    a = (g * jax.nn.sigmoid(g) * u).astype(BF16)
    part = jnp.dot(a, wd_ref[...], preferred_element_type=F32)

    @pl.when(f == 0)
    def _():
        acc_ref[...] = part

    @pl.when(f > 0)
    def _():
        acc_ref[...] += part

    @pl.when(f == pl.num_programs(1) - 1)
    def _():
        x = x_ref[...] + _rms(acc_ref[...], gpost_ref[...])
        xo_ref[...] = x
        ho_ref[...] = _rms(x, gnext_ref[...]).astype(ho_ref.dtype)


def _ffn_call(h, wg, wu, wd, layer, x, g_post, g_next, tm, tf):
    m, d = x.shape
    d_ff = wg.shape[2]
    row = pl.BlockSpec((tm, d), lambda i, f: (i, 0))
    vec = pl.BlockSpec((1, d), lambda i, f: (0, 0))
    return pl.pallas_call(
        _ffn_kernel,
        grid=(m // tm, d_ff // tf),
        in_specs=[
            row,
            pl.BlockSpec((None, d, tf), lambda i, f: (layer, 0, f)),
            pl.BlockSpec((None, d, tf), lambda i, f: (layer, 0, f)),
            pl.BlockSpec((None, tf, d), lambda i, f: (layer, f, 0)),
            row, vec, vec,
        ],
        out_specs=[row, row],
        out_shape=[jax.ShapeDtypeStruct((m, d), F32), jax.ShapeDtypeStruct((m, d), BF16)],
        scratch_shapes=[pltpu.VMEM((tm, d), F32)],
        compiler_params=_params(("parallel", "arbitrary"), 52),
        name="ffn",
    )(h, wg, wu, wd, x, g_post, g_next)


def _pad_cols(w, width):
    return jnp.pad(w, ((0, 0), (0, 0), (0, width - w.shape[-1])))


def _prep_w_in(w_in):
    widths = (A_HEADS * HEAD_DIM, HEAD_DIM, HEAD_DIM, IDX_HEADS * IDX_DIM, IDX_DIM, IDX_HEADS,
              B_HEADS * HEAD_DIM, B_HEADS * HEAD_DIM, B_HEADS * HEAD_DIM, C_HEADS * HEAD_DIM,
              C_KV_GROUPS * HEAD_DIM, C_KV_GROUPS * HEAD_DIM, C_KV_GROUPS * HEAD_DIM, C_KV_GROUPS * HEAD_DIM,
              C_KV_GROUPS * HEAD_DIM, C_KV_GROUPS * HEAD_DIM, 3 * C_HEADS)
    names = ("a_q", "a_k", "a_v", "i_q", "i_k", "i_w", "b_q", "b_k", "b_v", "c_q",
             "c_kc", "c_vc", "c_ks", "c_vs", "c_kw", "c_vw", "c_g")
    offs = np.concatenate([[0], np.cumsum(widths)])
    src = {n: w_in[:, :, int(offs[k]):int(offs[k + 1])] for k, n in enumerate(names)}
    per_group = 3 * C_GROUP_SIZE
    src["c_g0"] = _pad_cols(src["c_g"][:, :, :per_group], LANE)
    src["c_g1"] = _pad_cols(src["c_g"][:, :, per_group:], LANE)
    src["i_k"] = jnp.concatenate([src["i_k"], src["i_k"]], axis=-1)
    src["i_w"] = _pad_cols(src["i_w"], LANE)
    order = [n for n, _, _, _ in _SEGMENTS] + [n for n, _ in _CMP_SEGMENTS]
    return jnp.concatenate([src[n] for n in order], axis=-1).astype(BF16)


def _rope_tables(seq):
    def tables(dim):
        inv = 1.0 / (ROPE_THETA ** (jnp.arange(0, dim, 2, dtype=F32) / dim))
        ang = jnp.arange(seq, dtype=F32)[:, None] * inv[None, :]
        cos, sin = jnp.cos(ang), jnp.sin(ang)
        reps = LANE // dim
        return (jnp.tile(jnp.concatenate([cos, cos], axis=-1), (1, reps)),
                jnp.tile(jnp.concatenate([-sin, sin], axis=-1), (1, reps)))
    return tables(HEAD_DIM) + tables(IDX_DIM)


def _selection_constants(seq):
    nc = seq // CMP_STRIDE
    n_slc = seq // SEL_BLOCK
    c_start = np.arange(nc) * CMP_STRIDE
    n_start = np.arange(n_slc) * SEL_BLOCK
    isect = ((c_start[None, :] < n_start[:, None] + SEL_BLOCK)
             & (c_start[None, :] + CMP_BLOCK > n_start[:, None])
             & (np.arange(nc)[None, :] < nc - CMP_BLOCK // CMP_STRIDE + 1)).astype(np.float32)
    expand = np.zeros((seq, max(n_slc, LANE)), np.float32)
    expand[np.arange(seq), np.arange(seq) // SEL_BLOCK] = 1.0
    return jnp.asarray(isect, BF16), jnp.asarray(expand, BF16)


def _tiles(seq):
    return dict(tq=LANE, step=min(512, seq), tm_proj=min(256, seq), tm_out=min(512, seq),
                tm_ffn=min(512, seq), tf=512)


def kernel(x, w_in, w_out, cmp_pe_k, cmp_w1_k, cmp_w2_k, cmp_pe_v, cmp_w1_v, cmp_w2_v,
           w_gate, w_up, w_down, g_pre_mix, g_post_mix, g_pre_ffn, g_post_ffn):
    b, s, d = x.shape
    depth = w_in.shape[0]
    m = b * s
    t = _tiles(s)
    tq, step = t["tq"], t["step"]

    w_in_p = _prep_w_in(w_in)
    w_out_p = w_out.astype(BF16)
    wg, wu, wd = w_gate.astype(BF16), w_up.astype(BF16), w_down.astype(BF16)
    w1_k, w2_k = cmp_w1_k.astype(BF16), cmp_w2_k.astype(BF16)
    w1_v, w2_v = cmp_w1_v.astype(BF16), cmp_w2_v.astype(BF16)
    col_scale = jnp.asarray(_COL_SCALE)
    tabs = _rope_tables(s)
    isect, expand = _selection_constants(s)
    vec = lambda g, layer: g[layer][None, :]

    xf = x.reshape(m, d)
    h = _rms_call(xf, vec(g_pre_mix, 0), t["tm_out"])
    for layer in range(depth):
        p, pc = _proj_call(h, w_in_p, layer, col_scale, tabs, s, t["tm_proj"])
        p3 = p.reshape(b, s, NP)
        o_a = _dsa_call(p3, tq, step)
        o_b = _dil_call(p3, tq, step)
        kc, vct = _cmp_call(pc, b, layer, cmp_pe_k, w1_k, w2_k, cmp_pe_v, w1_v, w2_v)
        o_c = _nsa_call(p3, kc, vct, tabs[0], tabs[1], isect, expand, tq, step)
        xf, h = _out_call(o_a.reshape(m, -1), o_b.reshape(m, -1), o_c.reshape(m, -1), w_out_p, layer,
                          xf, vec(g_post_mix, layer), vec(g_pre_ffn, layer), t["tm_out"])
        g_next = vec(g_pre_mix, (layer + 1) % depth)
        xf, h = _ffn_call(h, wg, wu, wd, layer, xf, vec(g_post_ffn, layer), g_next, t["tm_ffn"], t["tf"])
    return xf.reshape(b, s, d)
```

```python
import functools

import numpy as np
import jax
import jax.numpy as jnp
from jax import lax
from jax.experimental import pallas as pl
from jax.experimental.pallas import tpu as pltpu

F32 = jnp.float32
BF16 = jnp.bfloat16

LANE = 128
SUBLANE = 8
HEAD_DIM = 128
A_HEADS = 4
B_HEADS = 4
C_HEADS = 8
C_KV_GROUPS = 2
C_GROUP_SIZE = C_HEADS // C_KV_GROUPS
IDX_HEADS = 16
IDX_DIM = 64
DSA_TOPK_MAX = 256
DILATED_PATTERNS = ((128, 1), (512, 4), (2048, 16))
CMP_BLOCK = 32
CMP_STRIDE = 16
CMP_HIDDEN = 256
SEL_BLOCK = 64
SEL_COUNT = 16
WIN_SIZE = 512
ROPE_THETA = 10000.0
RMS_EPS = 1e-6
ATTN_SCALE = HEAD_DIM ** -0.5 * float(np.log2(np.e))
IDX_SCALE = IDX_DIM ** -0.5 * IDX_HEADS ** -0.5

NEG = -1e30
INT_MIN = -(2 ** 31)
COUNT_ROWS = 8 * SUBLANE
SCORE_ROWS = 256
ATT_ROWS = 128

MODE_NONE, MODE_ROPE, MODE_IROPE = 0, 1, 2

_SEGMENTS = (
    ("i_q", 8, MODE_IROPE, IDX_SCALE),
    ("c_q", 8, MODE_NONE, ATTN_SCALE),
    ("a_q", 4, MODE_ROPE, ATTN_SCALE),
    ("b_q", 4, MODE_ROPE, ATTN_SCALE),
    ("b_k", 4, MODE_ROPE, 1.0),
    ("b_v", 4, MODE_NONE, 1.0),
    ("c_ks", 2, MODE_ROPE, 1.0),
    ("c_vs", 2, MODE_NONE, 1.0),
    ("c_kw", 2, MODE_ROPE, 1.0),
    ("c_vw", 2, MODE_NONE, 1.0),
    ("a_k", 1, MODE_ROPE, 1.0),
    ("a_v", 1, MODE_NONE, 1.0),
    ("i_k", 1, MODE_IROPE, 1.0),
    ("i_w", 1, MODE_NONE, 1.0),
    ("c_g0", 1, MODE_NONE, 1.0),
    ("c_g1", 1, MODE_NONE, 1.0),
)
_CMP_SEGMENTS = (("c_kc", 2), ("c_vc", 2))
_OFF = {}
_o = 0
for _n, _w, _m, _s in _SEGMENTS:
    _OFF[_n] = _o
    _o += _w
NP_BLOCKS = _o
NP = NP_BLOCKS * LANE
CMP_BLOCKS = sum(w for _, w in _CMP_SEGMENTS)
CMP_COLS = CMP_BLOCKS * LANE
NW = NP + CMP_COLS
_BLOCK_MODE = tuple(m for _, w, m, _ in _SEGMENTS for _ in range(w))
_COL_SCALE = np.repeat(np.array([s for _, w, _, s in _SEGMENTS for _ in range(w)], np.float32), LANE)[None, :]
_PROJ_CHUNK = 4


def _params(sem, vmem_mb):
    return pltpu.CompilerParams(dimension_semantics=sem, vmem_limit_bytes=vmem_mb * 1024 * 1024)


def _resident(block_shape, index_map):
    return pl.BlockSpec(block_shape, index_map, pipeline_mode=pl.Buffered(1))


def _rms(x, g):
    return x * lax.rsqrt(jnp.mean(x * x, axis=-1, keepdims=True) + RMS_EPS) * g


def _rms_kernel(x_ref, g_ref, h_ref):
    h_ref[...] = _rms(x_ref[...], g_ref[...]).astype(h_ref.dtype)


def _rms_call(x, g, tm):
    m, d = x.shape
    return pl.pallas_call(
        _rms_kernel,
        grid=(m // tm,),
        in_specs=[pl.BlockSpec((tm, d), lambda i: (i, 0)), pl.BlockSpec((1, d), lambda i: (0, 0))],
        out_specs=pl.BlockSpec((tm, d), lambda i: (i, 0)),
        out_shape=jax.ShapeDtypeStruct((m, d), BF16),
        compiler_params=_params(("parallel",), 32),
        name="rms_pre",
    )(x, g)


def _rope_full(a, cos, sin):
    return a * cos + pltpu.roll(a, HEAD_DIM // 2, 1) * sin


def _rope_idx(a, cos, sin, first_half):
    partner = jnp.where(first_half, pltpu.roll(a, LANE - IDX_DIM // 2, 1), pltpu.roll(a, IDX_DIM // 2, 1))
    return a * cos + partner * sin


def _proj_kernel(h_ref, w_ref, cs_ref, cos_ref, sin_ref, icos_ref, isin_ref, o_ref, oc_ref, cmp_ref):
    h = h_ref[...]
    tm = h.shape[0]
    lane = lax.broadcasted_iota(jnp.int32, (tm, LANE), 1)
    first_half = (lane & (IDX_DIM - 1)) < IDX_DIM // 2
    for c0 in range(0, NP_BLOCKS, _PROJ_CHUNK):
        nb = min(_PROJ_CHUNK, NP_BLOCKS - c0)
        acc = jnp.dot(h, w_ref[:, c0 * LANE:(c0 + nb) * LANE], preferred_element_type=F32)
        for b in range(nb):
            col = slice((c0 + b) * LANE, (c0 + b + 1) * LANE)
            a = acc[:, b * LANE:(b + 1) * LANE] * cs_ref[:, col]
            mode = _BLOCK_MODE[c0 + b]
            if mode == MODE_ROPE:
                a = _rope_full(a, cos_ref[...], sin_ref[...])
            elif mode == MODE_IROPE:
                a = _rope_idx(a, icos_ref[...], isin_ref[...], first_half)
            o_ref[:, col] = a.astype(o_ref.dtype)
    acc = jnp.dot(h, w_ref[:, NP:NW], preferred_element_type=F32)
    for b in range(CMP_BLOCKS):
        cmp_ref[b] = acc[:, b * LANE:(b + 1) * LANE]
    for j in range(CMP_STRIDE):
        for b in range(CMP_BLOCKS):
            rows = cmp_ref[b, pl.ds(j, tm // CMP_STRIDE, stride=CMP_STRIDE), :]
            oc_ref[:, j * CMP_COLS + b * LANE:j * CMP_COLS + (b + 1) * LANE] = rows.astype(oc_ref.dtype)


def _proj_call(h, w_in_p, layer, col_scale, tabs, seq, tm):
    m, d = h.shape
    tpb = seq // tm
    tab_spec = pl.BlockSpec((tm, LANE), lambda i: (i % tpb, 0))
    return pl.pallas_call(
        _proj_kernel,
        grid=(m // tm,),
        in_specs=[
            pl.BlockSpec((tm, d), lambda i: (i, 0)),
            _resident((None, d, NW), lambda i: (layer, 0, 0)),
            _resident((1, NP), lambda i: (0, 0)),
            tab_spec, tab_spec, tab_spec, tab_spec,
        ],
        out_specs=[
            pl.BlockSpec((tm, NP), lambda i: (i, 0)),
            pl.BlockSpec((tm // CMP_STRIDE, CMP_STRIDE * CMP_COLS), lambda i: (i, 0)),
        ],
        out_shape=[
            jax.ShapeDtypeStruct((m, NP), BF16),
            jax.ShapeDtypeStruct((m // CMP_STRIDE, CMP_STRIDE * CMP_COLS), BF16),
        ],
        scratch_shapes=[pltpu.VMEM((CMP_BLOCKS, tm, LANE), F32)],
        compiler_params=_params(("parallel",), 52),
        name="in_proj",
    )(h, w_in_p, col_scale, *tabs)


def _transpose_into(dst_ref, src_ref):
    rows, cols = src_ref.shape
    for r0 in range(0, rows, LANE):
        for c0 in range(0, cols, LANE):
            tile = src_ref[r0:r0 + LANE, c0:c0 + LANE].astype(F32)
            dst_ref[c0:c0 + LANE, r0:r0 + LANE] = tile.T.astype(dst_ref.dtype)


def _transpose_tiles(dst_ref, src_ref):
    rows, _ = src_ref.shape
    for r in range(rows // LANE):
        tile = src_ref[r * LANE:(r + 1) * LANE, :].astype(F32)
        dst_ref[r] = tile.T.astype(dst_ref.dtype)


def _for_causal_prefix(i, tq, s_len, step, body):
    n = ((i + 1) * tq + step - 1) // step
    for v in range(1, s_len // step + 1):
        pl.when(n == v)(functools.partial(body, v * step))


def _fold(x, op):
    out = x[0:COUNT_ROWS]
    for r0 in range(COUNT_ROWS, x.shape[0], COUNT_ROWS):
        out = op(out, x[r0:r0 + COUNT_ROWS])
    return out


def _fold_tree(x, op):
    while x.shape[0] > SUBLANE:
        half = x.shape[0] // 2
        x = op(x[:half], x[half:])
    return x


def _row_chunk(rows):
    return SCORE_ROWS if rows % SCORE_ROWS == 0 else LANE


def _attend_heads(qs, k_of, pv_of, rows, mask_of, s_ref, p_ref, w_ref=None):
    n_heads = len(qs)
    tq = qs[0].shape[0]
    chunk = ATT_ROWS * LANE // tq
    for h in range(n_heads):
        s_ref[h, 0:rows, :] = lax.dot_general(k_of(h, 0, rows), qs[h], (((1,), (1,)), ((), ())),
                                              preferred_element_type=F32)
    m_acc = [jnp.full((SUBLANE, tq), NEG, F32) for _ in qs]
    weighted = False
    for r0 in range(0, rows, chunk):
        mask, weight = mask_of(r0, chunk)
        if weight is not None:
            weighted = True
            w_ref[r0:r0 + chunk, :] = weight
        for h in range(n_heads):
            s = jnp.where(mask, s_ref[h, r0:r0 + chunk, :], NEG)
            s_ref[h, r0:r0 + chunk, :] = s
            m_acc[h] = jnp.maximum(m_acc[h], _fold_tree(s, jnp.maximum))
    outs = []
    for h in range(n_heads):
        mx = jnp.max(m_acc[h], axis=0, keepdims=True)
        l_acc = jnp.zeros((SUBLANE, tq), F32)
        for r0 in range(0, rows, chunk):
            p = jnp.exp2(s_ref[h, r0:r0 + chunk, :] - mx)
            if weighted:
                p = p * w_ref[r0:r0 + chunk, :]
            l_acc = l_acc + _fold_tree(p, jnp.add)
            p_ref[h, r0:r0 + chunk, :] = p.astype(p_ref.dtype)
        den = jnp.sum(l_acc, axis=0, keepdims=True)
        outs.append(pv_of(h, p_ref[h, 0:rows, :]) / den)
    return outs


def _count_rows(key_ref, rows, pred):
    chunk = _row_chunk(rows)
    acc = None
    for r0 in range(0, rows, chunk):
        part = _fold(pred(key_ref[r0:r0 + chunk, :], r0), jnp.add)
        acc = part if acc is None else acc + part
    return jnp.sum(acc, axis=0, keepdims=True)


def _dsa_prefix(rows, i, iq_ref, iw_t, ik_ref, q_ref, k_ref, vt_ref, o_ref, key_ref, cut_ref, s_ref, p_ref, topk):
    tq = q_ref.shape[0]
    chunk = min(SCORE_ROWS * LANE // tq, rows)
    lane = lax.broadcasted_iota(jnp.int32, (chunk, LANE), 1)
    crow = lax.broadcasted_iota(jnp.int32, (chunk, tq), 0)
    ctcol = i * tq + lax.broadcasted_iota(jnp.int32, (chunk, tq), 1)

    for r0 in range(0, rows, chunk):
        ik = ik_ref[r0:r0 + chunk, :]
        halves = (jnp.where(lane < IDX_DIM, ik, jnp.zeros_like(ik)),
                  jnp.where(lane >= IDX_DIM, ik, jnp.zeros_like(ik)))
        score = jnp.zeros((chunk, tq), F32)
        for p in range(IDX_HEADS // 2):
            blk = iq_ref[:, p * LANE:(p + 1) * LANE]
            for half in range(2):
                j = 2 * p + half
                lg = lax.dot_general(halves[half], blk, (((1,), (1,)), ((), ())), preferred_element_type=F32)
                score = score + jnp.maximum(lg, 0.0) * iw_t[j:j + 1, :]
        bits = pltpu.bitcast(score, jnp.int32)
        key = bits ^ ((bits >> 31) & 0x7FFFFFFF)
        key_ref[r0:r0 + chunk, :] = jnp.where(r0 + crow <= ctcol, key, INT_MIN)

    def bit_step(b, thr):
        cand = thr ^ jnp.left_shift(jnp.int32(1), 31 - b)
        cnt = _count_rows(key_ref, rows, lambda kc, r0: jnp.where(kc >= cand, 1, 0))
        return jnp.where(cnt >= topk, cand, thr)

    thr = lax.fori_loop(0, 32, bit_step, jnp.full((1, tq), INT_MIN, jnp.int32))
    short = thr == INT_MIN
    need = topk - _count_rows(key_ref, rows, lambda kc, r0: jnp.where(kc > thr, 1, 0))
    n_eq = _count_rows(key_ref, rows, lambda kc, r0: jnp.where(kc == thr, 1, 0))
    excess = jnp.where(n_eq > need, jnp.where(short, 0, 1), 0)
    cut_ref[...] = jnp.where(short, -1, rows)

    def row_ids(r0, n):
        return r0 + lax.broadcasted_iota(jnp.int32, (n, tq), 0)

    @pl.when(jnp.max(excess) > 0)
    def _():
        n_bits = max(1, (rows - 1).bit_length())

        def idx_step(b, cut):
            cand = cut | jnp.left_shift(jnp.int32(1), n_bits - 1 - b)
            below = _count_rows(key_ref, rows, lambda kc, r0: jnp.where(
                kc == thr, jnp.where(row_ids(r0, kc.shape[0]) < cand, 1, 0), 0))
            return jnp.where(below < need, cand, cut)

        cut = lax.fori_loop(0, n_bits, idx_step, jnp.zeros((1, tq), jnp.int32))
        cut_ref[...] = jnp.where(excess > 0, cut, cut_ref[...])

    cut = cut_ref[...]

    def mask_of(r0, n):
        kc = key_ref[r0:r0 + n, :]
        chosen = jnp.where(kc > thr, 1, jnp.where(kc == thr, jnp.where(row_ids(r0, n) <= cut, 1, 0), 0))
        return chosen > 0, None

    vt = vt_ref[:, 0:rows]
    cols = [slice(h * HEAD_DIM, (h + 1) * HEAD_DIM) for h in range(A_HEADS)]
    outs = _attend_heads([q_ref[:, c] for c in cols], lambda h, r0, n: k_ref[r0:r0 + n, :],
                         lambda h, p: jnp.dot(vt, p, preferred_element_type=F32),
                         rows, mask_of, s_ref, p_ref)
    for h, c in enumerate(cols):
        o_ref[:, c] = outs[h].T.astype(o_ref.dtype)


def _dsa_kernel(iq_ref, iw_ref, ik_ref, q_ref, k_ref, v_ref, o_ref, vt_ref, key_ref, cut_ref, s_ref, p_ref,
                *, topk, step):
    i = pl.program_id(1)
    tq = q_ref.shape[0]
    s_len = k_ref.shape[0]

    @pl.when(i == 0)
    def _():
        _transpose_into(vt_ref, v_ref)

    iw_t = iw_ref[...].astype(F32).T[:IDX_HEADS, :]
    _for_causal_prefix(i, tq, s_len, step, functools.partial(
        _dsa_prefix, i=i, iq_ref=iq_ref, iw_t=iw_t, ik_ref=ik_ref, q_ref=q_ref, k_ref=k_ref, vt_ref=vt_ref,
        o_ref=o_ref, key_ref=key_ref, cut_ref=cut_ref, s_ref=s_ref, p_ref=p_ref, topk=topk))


def _dsa_call(p3, tq, step):
    b, s, _ = p3.shape
    topk = min(DSA_TOPK_MAX, s // 4)
    blk = lambda name, width: _OFF[name] // width
    return pl.pallas_call(
        functools.partial(_dsa_kernel, topk=topk, step=step),
        grid=(b, s // tq),
        in_specs=[
            pl.BlockSpec((None, tq, 8 * LANE), lambda bi, i: (bi, i, blk("i_q", 8))),
            pl.BlockSpec((None, tq, LANE), lambda bi, i: (bi, i, blk("i_w", 1))),
            pl.BlockSpec((None, s, LANE), lambda bi, i: (bi, 0, blk("i_k", 1))),
            pl.BlockSpec((None, tq, 4 * LANE), lambda bi, i: (bi, i, blk("a_q", 4))),
            pl.BlockSpec((None, s, LANE), lambda bi, i: (bi, 0, blk("a_k", 1))),
            pl.BlockSpec((None, s, LANE), lambda bi, i: (bi, 0, blk("a_v", 1))),
        ],
        out_specs=pl.BlockSpec((None, tq, A_HEADS * HEAD_DIM), lambda bi, i: (bi, i, 0)),
        out_shape=jax.ShapeDtypeStruct((b, s, A_HEADS * HEAD_DIM), BF16),
        scratch_shapes=[
            pltpu.VMEM((HEAD_DIM, s), BF16),
            pltpu.VMEM((s, tq), jnp.int32),
            pltpu.VMEM((1, tq), jnp.int32),
            pltpu.VMEM((A_HEADS, s, tq), F32),
            pltpu.VMEM((A_HEADS, s, tq), BF16),
        ],
        compiler_params=_params(("parallel", "arbitrary"), 48),
        name="dsa",
    )(p3, p3, p3, p3, p3, p3)


def _dil_prefix(rows, i, q_ref, k_ref, vt_ref, o_ref, s_ref, p_ref, w_ref):
    tq = q_ref.shape[0]

    def mask_of(r0, n):
        srow = r0 + lax.broadcasted_iota(jnp.int32, (n, tq), 0)
        tcol = i * tq + lax.broadcasted_iota(jnp.int32, (n, tq), 1)
        delta = tcol - srow
        mult = jnp.zeros((n, tq), jnp.int32)
        for window, dilation in DILATED_PATTERNS:
            on_stride = (delta & (dilation - 1)) == 0
            mult = mult + jnp.where(on_stride, jnp.where(delta <= window, 1, 0), 0)
        mult = jnp.where(delta >= 0, mult, 0)
        return mult > 0, mult.astype(F32)

    cols = [slice(h * HEAD_DIM, (h + 1) * HEAD_DIM) for h in range(B_HEADS)]
    outs = _attend_heads([q_ref[:, c] for c in cols], lambda h, r0, n: k_ref[r0:r0 + n, cols[h]],
                         lambda h, p: jnp.dot(vt_ref[cols[h], 0:rows], p, preferred_element_type=F32),
                         rows, mask_of, s_ref, p_ref, w_ref)
    for h, c in enumerate(cols):
        o_ref[:, c] = outs[h].T.astype(o_ref.dtype)


def _dil_kernel(q_ref, k_ref, v_ref, o_ref, vt_ref, s_ref, p_ref, w_ref, *, step):
    i = pl.program_id(1)

    @pl.when(i == 0)
    def _():
        _transpose_into(vt_ref, v_ref)

    _for_causal_prefix(i, q_ref.shape[0], k_ref.shape[0], step, functools.partial(
        _dil_prefix, i=i, q_ref=q_ref, k_ref=k_ref, vt_ref=vt_ref, o_ref=o_ref,
        s_ref=s_ref, p_ref=p_ref, w_ref=w_ref))


def _dil_call(p3, tq, step):
    b, s, _ = p3.shape
    width = B_HEADS * HEAD_DIM
    return pl.pallas_call(
        functools.partial(_dil_kernel, step=step),
        grid=(b, s // tq),
        in_specs=[
            pl.BlockSpec((None, tq, width), lambda bi, i: (bi, i, _OFF["b_q"] // 4)),
            pl.BlockSpec((None, s, width), lambda bi, i: (bi, 0, _OFF["b_k"] // 4)),
            pl.BlockSpec((None, s, width), lambda bi, i: (bi, 0, _OFF["b_v"] // 4)),
        ],
        out_specs=pl.BlockSpec((None, tq, width), lambda bi, i: (bi, i, 0)),
        out_shape=jax.ShapeDtypeStruct((b, s, width), BF16),
        scratch_shapes=[
            pltpu.VMEM((width, s), BF16),
            pltpu.VMEM((B_HEADS, s, tq), F32),
            pltpu.VMEM((B_HEADS, s, tq), BF16),
            pltpu.VMEM((s, tq), F32),
        ],
        compiler_params=_params(("parallel", "arbitrary"), 48),
        name="dilated",
    )(p3, p3, p3)


def _gelu_tanh(x):
    return 0.5 * x * (1.0 + jnp.tanh(np.float32(np.sqrt(2.0 / np.pi)) * (x + 0.044715 * (x * x * x))))


def _cmp_kernel(*refs):
    n = CMP_STRIDE
    xk, xv = refs[:n], refs[n:2 * n]
    pe_k, w1_k, w2_k, pe_v, w1_v, w2_v, kc_ref, vct_ref = refs[2 * n:]

    def branch(x_refs, pe_ref, w1_ref, w2_ref):
        lo = hi = None
        for j in range(n):
            xj = x_refs[j][...].astype(F32)
            a = (xj + pe_ref[j:j + 1, :]).astype(BF16)
            b = (xj + pe_ref[n + j:n + j + 1, :]).astype(BF16)
            dl = jnp.dot(a, w1_ref[j * HEAD_DIM:(j + 1) * HEAD_DIM, :], preferred_element_type=F32)
            dh = jnp.dot(b, w1_ref[(n + j) * HEAD_DIM:(n + j + 1) * HEAD_DIM, :], preferred_element_type=F32)
            lo = dl if lo is None else lo + dl
            hi = dh if hi is None else hi + dh
        nc = lo.shape[0]
        hid = lo + pltpu.roll(hi, nc - 1, 0)
        return jnp.dot(_gelu_tanh(hid).astype(BF16), w2_ref[...], preferred_element_type=F32)

    kc_ref[...] = branch(xk, pe_k, w1_k, w2_k).astype(kc_ref.dtype)
    vct_ref[...] = branch(xv, pe_v, w1_v, w2_v).T.astype(vct_ref.dtype)


def _cmp_call(pc, b, layer, pe_k, w1_k, w2_k, pe_v, w1_v, w2_v):
    nc = pc.shape[0] // b
    pc3 = pc.reshape(b, nc, CMP_STRIDE * CMP_COLS)
    x_specs = []
    for first in (0, C_KV_GROUPS):
        for j in range(CMP_STRIDE):
            x_specs.append(pl.BlockSpec((None, nc, HEAD_DIM),
                                        lambda bi, g, j=j, first=first: (bi, 0, j * CMP_BLOCKS + first + g)))
    flat = CMP_BLOCK * HEAD_DIM
    w_specs = [
        _resident((None, CMP_BLOCK, HEAD_DIM), lambda bi, g: (layer, 0, 0)),
        _resident((None, flat, CMP_HIDDEN), lambda bi, g: (layer, 0, 0)),
        _resident((None, CMP_HIDDEN, HEAD_DIM), lambda bi, g: (layer, 0, 0)),
    ]
    return pl.pallas_call(
        _cmp_kernel,
        grid=(b, C_KV_GROUPS),
        in_specs=x_specs + w_specs + w_specs,
        out_specs=[
            pl.BlockSpec((None, None, nc, HEAD_DIM), lambda bi, g: (bi, g, 0, 0)),
            pl.BlockSpec((None, None, HEAD_DIM, nc), lambda bi, g: (bi, g, 0, 0)),
        ],
        out_shape=[
            jax.ShapeDtypeStruct((b, C_KV_GROUPS, nc, HEAD_DIM), BF16),
            jax.ShapeDtypeStruct((b, C_KV_GROUPS, HEAD_DIM, nc), BF16),
        ],
        compiler_params=_params(("parallel", "parallel"), 32),
        name="nsa_compress",
    )(*([pc3] * (2 * CMP_STRIDE)), pe_k, w1_k, w2_k, pe_v, w1_v, w2_v)


def _nsa_selected(rows, t_row, sel, q_rot, ks_ref, vst_ref, expand_ref, acc_ref, s_ref, p_ref):
    tq = sel.shape[1]

    def mask_of(r0, n):
        in_sel = jnp.dot(expand_ref[r0:r0 + n, :], sel, preferred_element_type=F32)
        srow = r0 + lax.broadcasted_iota(jnp.int32, (n, tq), 0)
        return jnp.where(t_row - srow >= 0, in_sel, 0.0) > 0.5, None

    vt = vst_ref[:, 0:rows]
    outs = _attend_heads(q_rot, lambda h, r0, n: ks_ref[r0:r0 + n, :],
                         lambda h, p: jnp.dot(vt, p, preferred_element_type=F32),
                         rows, mask_of, s_ref, p_ref)
    for r in range(C_GROUP_SIZE):
        acc_ref[r] = outs[r]


def _nsa_kernel(q_ref, kc_ref, vct_ref, ks_ref, vs_ref, kw_ref, vw_ref, g_ref, cos_ref, sin_ref,
                isect_ref, expand_ref, o_ref, vst_ref, vwt_ref, acc_ref, s_ref, p_ref, *, n_sel, step):
    i = pl.program_id(2)
    tq = q_ref.shape[0]
    s_len = ks_ref.shape[0]
    nc = kc_ref.shape[0]
    n_slc = isect_ref.shape[0]

    @pl.when(i == 0)
    def _():
        _transpose_into(vst_ref, vs_ref)
        _transpose_tiles(vwt_ref, vw_ref)

    t_row = i * tq + lax.broadcasted_iota(jnp.int32, (1, tq), 1)

    kc = kc_ref[...]
    vct = vct_ref[...]
    cend = lax.broadcasted_iota(jnp.int32, (nc, tq), 0) * CMP_STRIDE + (CMP_BLOCK - 1)
    cmask = cend <= t_row
    o_cmp = []
    p_sum = jnp.zeros((nc, tq), F32)
    for r in range(C_GROUP_SIZE):
        q_r = q_ref[:, r * HEAD_DIM:(r + 1) * HEAD_DIM]
        sc = lax.dot_general(kc, q_r, (((1,), (1,)), ((), ())), preferred_element_type=F32)
        sc = jnp.where(cmask, sc, NEG)
        mx = jnp.max(sc, axis=0, keepdims=True)
        e = jnp.where(cmask, jnp.exp2(sc - mx), 0.0)
        den = jnp.sum(e, axis=0, keepdims=True)
        p = e / jnp.where(den > 0, den, 1.0)
        p_sum = p_sum + p
        o_cmp.append(jnp.dot(vct, p.astype(BF16), preferred_element_type=F32))

    isect = isect_ref[...]
    p_hi = p_sum.astype(BF16)
    p_lo = (p_sum - p_hi.astype(F32)).astype(BF16)
    imp = (jnp.dot(isect, p_hi, preferred_element_type=F32)
           + jnp.dot(isect, p_lo, preferred_element_type=F32))
    blk = lax.broadcasted_iota(jnp.int32, (n_slc, tq), 0)
    cur = t_row // SEL_BLOCK
    val = jnp.where(blk == 0, jnp.inf, jnp.where(blk == cur, jnp.inf, jnp.where(blk == cur - 1, jnp.inf, imp)))
    val = jnp.where(blk <= cur, val, -jnp.inf)
    rank = jnp.zeros((n_slc, tq), jnp.int32)
    for m in range(n_slc):
        vm = val[m:m + 1, :]
        before = jnp.where(vm > val, 1, jnp.where(vm == val, jnp.where(blk > m, 1, 0), 0))
        rank = rank + before
    sel = jnp.where(rank < n_sel, 1.0, 0.0)
    if n_slc < LANE:
        sel = jnp.concatenate([sel, jnp.zeros((LANE - n_slc, tq), F32)], axis=0)
    sel = sel.astype(BF16)

    cos, sin = cos_ref[...], sin_ref[...]
    cols = [slice(r * HEAD_DIM, (r + 1) * HEAD_DIM) for r in range(C_GROUP_SIZE)]
    q_rot = [_rope_full(q_ref[:, c].astype(F32), cos, sin).astype(BF16) for c in cols]

    _for_causal_prefix(i, tq, s_len, step, functools.partial(
        _nsa_selected, t_row=t_row, sel=sel, q_rot=q_rot, ks_ref=ks_ref, vst_ref=vst_ref,
        expand_ref=expand_ref, acc_ref=acc_ref, s_ref=s_ref, p_ref=p_ref))

    w_tiles = min((WIN_SIZE - 1 + tq - 1) // LANE + 1, s_len // LANE)
    w_rows = w_tiles * LANE
    w0 = pl.multiple_of(jnp.maximum(i * tq + tq - w_rows, 0), LANE)
    t0 = w0 // LANE

    def win_mask(r0, n):
        delta = t_row - (w0 + r0 + lax.broadcasted_iota(jnp.int32, (n, tq), 0))
        return jnp.where(delta >= 0, jnp.where(delta <= WIN_SIZE - 1, 1, 0), 0) > 0, None

    def pv_win(h, p):
        out = None
        for w in range(w_tiles):
            part = jnp.dot(vwt_ref[t0 + w], p[w * LANE:(w + 1) * LANE, :], preferred_element_type=F32)
            out = part if out is None else out + part
        return out

    o_win = _attend_heads(q_rot, lambda h, r0, n: kw_ref[pl.ds(w0 + r0, n), :], pv_win,
                          w_rows, win_mask, s_ref, p_ref)

    gates = jax.nn.sigmoid(g_ref[...].astype(F32).T)
    for r, c in enumerate(cols):
        o_t = (gates[3 * r:3 * r + 1, :] * o_cmp[r] + gates[3 * r + 1:3 * r + 2, :] * acc_ref[r]
               + gates[3 * r + 2:3 * r + 3, :] * o_win[r])
        o_ref[:, c] = o_t.T.astype(o_ref.dtype)


def _nsa_call(p3, kc, vct, cos, sin, isect, expand, tq, step):
    b, s, _ = p3.shape
    nc = kc.shape[2]
    n_slc = s // SEL_BLOCK
    width = C_GROUP_SIZE * HEAD_DIM
    kv = lambda name: pl.BlockSpec((None, s, HEAD_DIM), lambda bi, g, i, name=name: (bi, 0, _OFF[name] + g))
    return pl.pallas_call(
        functools.partial(_nsa_kernel, n_sel=min(SEL_COUNT, n_slc), step=step),
        grid=(b, C_KV_GROUPS, s // tq),
        in_specs=[
            pl.BlockSpec((None, tq, width), lambda bi, g, i: (bi, i, _OFF["c_q"] // 4 + g)),
            pl.BlockSpec((None, None, nc, HEAD_DIM), lambda bi, g, i: (bi, g, 0, 0)),
            pl.BlockSpec((None, None, HEAD_DIM, nc), lambda bi, g, i: (bi, g, 0, 0)),
            kv("c_ks"), kv("c_vs"), kv("c_kw"), kv("c_vw"),
            pl.BlockSpec((None, tq, LANE), lambda bi, g, i: (bi, i, _OFF["c_g0"] + g)),
            pl.BlockSpec((tq, LANE), lambda bi, g, i: (i, 0)),
            pl.BlockSpec((tq, LANE), lambda bi, g, i: (i, 0)),
            pl.BlockSpec(isect.shape, lambda bi, g, i: (0, 0)),
            pl.BlockSpec(expand.shape, lambda bi, g, i: (0, 0)),
        ],
        out_specs=pl.BlockSpec((None, tq, width), lambda bi, g, i: (bi, i, g)),
        out_shape=jax.ShapeDtypeStruct((b, s, C_HEADS * HEAD_DIM), BF16),
        scratch_shapes=[
            pltpu.VMEM((HEAD_DIM, s), BF16),
            pltpu.VMEM((s // LANE, HEAD_DIM, LANE), BF16),
            pltpu.VMEM((C_GROUP_SIZE, HEAD_DIM, tq), F32),
            pltpu.VMEM((C_GROUP_SIZE, s, tq), F32),
            pltpu.VMEM((C_GROUP_SIZE, s, tq), BF16),
        ],
        compiler_params=_params(("parallel", "parallel", "arbitrary"), 48),
        name="nsa",
    )(p3, kc, vct, p3, p3, p3, p3, p3, cos, sin, isect, expand)


def _out_kernel(oa_ref, ob_ref, oc_ref, w_ref, x_ref, gpost_ref, gnext_ref, xo_ref, h_ref):
    na, nb = oa_ref.shape[1], ob_ref.shape[1]
    y = jnp.dot(oa_ref[...], w_ref[0:na, :], preferred_element_type=F32)
    y = y + jnp.dot(ob_ref[...], w_ref[na:na + nb, :], preferred_element_type=F32)
    y = y + jnp.dot(oc_ref[...], w_ref[na + nb:, :], preferred_element_type=F32)
    x = x_ref[...] + _rms(y, gpost_ref[...])
    xo_ref[...] = x
    h_ref[...] = _rms(x, gnext_ref[...]).astype(h_ref.dtype)


def _out_call(o_a, o_b, o_c, w_out_p, layer, x, g_post, g_next, tm):
    m, d = x.shape
    row = lambda width: pl.BlockSpec((tm, width), lambda i: (i, 0))
    vec = pl.BlockSpec((1, d), lambda i: (0, 0))
    return pl.pallas_call(
        _out_kernel,
        grid=(m // tm,),
        in_specs=[
            row(o_a.shape[1]), row(o_b.shape[1]), row(o_c.shape[1]),
            _resident((None, w_out_p.shape[1], d), lambda i: (layer, 0, 0)),
            row(d), vec, vec,
        ],
        out_specs=[row(d), row(d)],
        out_shape=[jax.ShapeDtypeStruct((m, d), F32), jax.ShapeDtypeStruct((m, d), BF16)],
        compiler_params=_params(("parallel",), 48),
        name="out_proj",
    )(o_a, o_b, o_c, w_out_p, x, g_post, g_next)


def _ffn_kernel(h_ref, wg_ref, wu_ref, wd_ref, x_ref, gpost_ref, gnext_ref, xo_ref, ho_ref, acc_ref):
    f = pl.program_id(1)

    @pl.when(f == 0)
    def _():
        acc_ref[...] = jnp.zeros(acc_ref.shape, F32)

    h = h_ref[...]
    g = jnp.dot(h, wg_ref[...], preferred_element_type=F32)
    u = jnp.dot(h, wu_ref[...], preferred_element_type=F32)
    a = (g * jax.nn.sigmoid(g) * u).astype(BF16)
    acc_ref[...] += jnp.dot(a, wd_ref[...], preferred_element_type=F32)

    @pl.when(f == pl.num_programs(1) - 1)
    def _():
        x = x_ref[...] + _rms(acc_ref[...], gpost_ref[...])
        xo_ref[...] = x
        ho_ref[...] = _rms(x, gnext_ref[...]).astype(ho_ref.dtype)


def _ffn_call(h, wg, wu, wd, layer, x, g_post, g_next, tm, tf):
    m, d = x.shape
    d_ff = wg.shape[2]
    row = pl.BlockSpec((tm, d), lambda i, f: (i, 0))
    vec = pl.BlockSpec((1, d), lambda i, f: (0, 0))
    return pl.pallas_call(
        _ffn_kernel,
        grid=(m // tm, d_ff // tf),
        in_specs=[
            row,
            pl.BlockSpec((None, d, tf), lambda i, f: (layer, 0, f)),
            pl.BlockSpec((None, d, tf), lambda i, f: (layer, 0, f)),
            pl.BlockSpec((None, tf, d), lambda i, f: (layer, f, 0)),
            row, vec, vec,
        ],
        out_specs=[row, row],
        out_shape=[jax.ShapeDtypeStruct((m, d), F32), jax.ShapeDtypeStruct((m, d), BF16)],
        scratch_shapes=[pltpu.VMEM((tm, d), F32)],
        compiler_params=_params(("parallel", "arbitrary"), 52),
        name="ffn",
    )(h, wg, wu, wd, x, g_post, g_next)


def _pad_cols(w, width):
    return jnp.pad(w, ((0, 0), (0, 0), (0, width - w.shape[-1])))


def _prep_w_in(w_in):
    widths = (A_HEADS * HEAD_DIM, HEAD_DIM, HEAD_DIM, IDX_HEADS * IDX_DIM, IDX_DIM, IDX_HEADS,
              B_HEADS * HEAD_DIM, B_HEADS * HEAD_DIM, B_HEADS * HEAD_DIM, C_HEADS * HEAD_DIM,
              C_KV_GROUPS * HEAD_DIM, C_KV_GROUPS * HEAD_DIM, C_KV_GROUPS * HEAD_DIM, C_KV_GROUPS * HEAD_DIM,
              C_KV_GROUPS * HEAD_DIM, C_KV_GROUPS * HEAD_DIM, 3 * C_HEADS)
    names = ("a_q", "a_k", "a_v", "i_q", "i_k", "i_w", "b_q", "b_k", "b_v", "c_q",
             "c_kc", "c_vc", "c_ks", "c_vs", "c_kw", "c_vw", "c_g")
    offs = np.concatenate([[0], np.cumsum(widths)])
    src = {n: w_in[:, :, int(offs[k]):int(offs[k + 1])] for k, n in enumerate(names)}
    per_group = 3 * C_GROUP_SIZE
    src["c_g0"] = _pad_cols(src["c_g"][:, :, :per_group], LANE)
    src["c_g1"] = _pad_cols(src["c_g"][:, :, per_group:], LANE)
    src["i_k"] = jnp.concatenate([src["i_k"], src["i_k"]], axis=-1)
    src["i_w"] = _pad_cols(src["i_w"], LANE)
    order = [n for n, _, _, _ in _SEGMENTS] + [n for n, _ in _CMP_SEGMENTS]
    return jnp.concatenate([src[n] for n in order], axis=-1).astype(BF16)


def _rope_tables(seq):
    def tables(dim):
        inv = 1.0 / (ROPE_THETA ** (jnp.arange(0, dim, 2, dtype=F32) / dim))
        ang = jnp.arange(seq, dtype=F32)[:, None] * inv[None, :]
        cos, sin = jnp.cos(ang), jnp.sin(ang)
        reps = LANE // dim
        return (jnp.tile(jnp.concatenate([cos, cos], axis=-1), (1, reps)),
                jnp.tile(jnp.concatenate([-sin, sin], axis=-1), (1, reps)))
    return tables(HEAD_DIM) + tables(IDX_DIM)


def _selection_constants(seq):
    nc = seq // CMP_STRIDE
    n_slc = seq // SEL_BLOCK
    c_start = np.arange(nc) * CMP_STRIDE
    n_start = np.arange(n_slc) * SEL_BLOCK
    isect = ((c_start[None, :] < n_start[:, None] + SEL_BLOCK)
             & (c_start[None, :] + CMP_BLOCK > n_start[:, None])
             & (np.arange(nc)[None, :] < nc - CMP_BLOCK // CMP_STRIDE + 1)).astype(np.float32)
    expand = np.zeros((seq, max(n_slc, LANE)), np.float32)
    expand[np.arange(seq), np.arange(seq) // SEL_BLOCK] = 1.0
    return jnp.asarray(isect, BF16), jnp.asarray(expand, BF16)


def _tiles(seq):
    return dict(tq=min(2 * LANE, seq), step=min(512, seq), tm_proj=min(256, seq), tm_out=min(512, seq),
                tm_ffn=min(512, seq), tf=512)


def kernel(x, w_in, w_out, cmp_pe_k, cmp_w1_k, cmp_w2_k, cmp_pe_v, cmp_w1_v, cmp_w2_v,
           w_gate, w_up, w_down, g_pre_mix, g_post_mix, g_pre_ffn, g_post_ffn):
    b, s, d = x.shape
    depth = w_in.shape[0]
    m = b * s
    t = _tiles(s)
    tq, step = t["tq"], t["step"]

    w_in_p = _prep_w_in(w_in)
    w_out_p = w_out.astype(BF16)
    wg, wu, wd = w_gate.astype(BF16), w_up.astype(BF16), w_down.astype(BF16)
    w1_k, w2_k = cmp_w1_k.astype(BF16), cmp_w2_k.astype(BF16)
    w1_v, w2_v = cmp_w1_v.astype(BF16), cmp_w2_v.astype(BF16)
    col_scale = jnp.asarray(_COL_SCALE)
    tabs = _rope_tables(s)
    isect, expand = _selection_constants(s)
    vec = lambda g, layer: g[layer][None, :]

    xf = x.reshape(m, d)
    h = _rms_call(xf, vec(g_pre_mix, 0), t["tm_out"])
    for layer in range(depth):
        p, pc = _proj_call(h, w_in_p, layer, col_scale, tabs, s, t["tm_proj"])
        p3 = p.reshape(b, s, NP)
        o_a = _dsa_call(p3, tq, step)
        o_b = _dil_call(p3, tq, step)
        kc, vct = _cmp_call(pc, b, layer, cmp_pe_k, w1_k, w2_k, cmp_pe_v, w1_v, w2_v)
        o_c = _nsa_call(p3, kc, vct, tabs[0], tabs[1], isect, expand, tq, step)
        xf, h = _out_call(o_a.reshape(m, -1), o_b.reshape(m, -1), o_c.reshape(m, -1), w_out_p, layer,
                          xf, vec(g_post_mix, layer), vec(g_pre_ffn, layer), t["tm_out"])
        g_next = vec(g_pre_mix, (layer + 1) % depth)
        xf, h = _ffn_call(h, wg, wu, wd, layer, xf, vec(g_post_ffn, layer), g_next, t["tm_ffn"], t["tf"])
    return xf.reshape(b, s, d)
```

```python
import functools

import numpy as np
import jax
import jax.numpy as jnp
from jax import lax
from jax.experimental import pallas as pl
from jax.experimental.pallas import tpu as pltpu

F32 = jnp.float32
BF16 = jnp.bfloat16

LANE = 128
SUBLANE = 8
HEAD_DIM = 128
A_HEADS = 4
B_HEADS = 4
C_HEADS = 8
C_KV_GROUPS = 2
C_GROUP_SIZE = C_HEADS // C_KV_GROUPS
IDX_HEADS = 16
IDX_DIM = 64
DSA_TOPK_MAX = 256
DILATED_PATTERNS = ((128, 1), (512, 4), (2048, 16))
CMP_BLOCK = 32
CMP_STRIDE = 16
CMP_HIDDEN = 256
SEL_BLOCK = 64
SEL_COUNT = 16
WIN_SIZE = 512
ROPE_THETA = 10000.0
RMS_EPS = 1e-6
ATTN_SCALE = HEAD_DIM ** -0.5 * float(np.log2(np.e))
IDX_SCALE = IDX_DIM ** -0.5 * IDX_HEADS ** -0.5

NEG = -1e30
INT_MIN = -(2 ** 31)
COUNT_ROWS = 8 * SUBLANE
SCORE_ROWS = 256
ATT_ROWS = 128

MODE_NONE, MODE_ROPE, MODE_IROPE = 0, 1, 2

_SEGMENTS = (
    ("i_q", 8, MODE_IROPE, IDX_SCALE),
    ("c_q", 8, MODE_NONE, ATTN_SCALE),
    ("a_q", 4, MODE_ROPE, ATTN_SCALE),
    ("b_q", 4, MODE_ROPE, ATTN_SCALE),
    ("b_k", 4, MODE_ROPE, 1.0),
    ("b_v", 4, MODE_NONE, 1.0),
    ("c_ks", 2, MODE_ROPE, 1.0),
    ("c_vs", 2, MODE_NONE, 1.0),
    ("c_kw", 2, MODE_ROPE, 1.0),
    ("c_vw", 2, MODE_NONE, 1.0),
    ("a_k", 1, MODE_ROPE, 1.0),
    ("a_v", 1, MODE_NONE, 1.0),
    ("i_k", 1, MODE_IROPE, 1.0),
    ("i_w", 1, MODE_NONE, 1.0),
    ("c_g0", 1, MODE_NONE, 1.0),
    ("c_g1", 1, MODE_NONE, 1.0),
)
_CMP_SEGMENTS = (("c_kc", 2), ("c_vc", 2))
_OFF = {}
_o = 0
for _n, _w, _m, _s in _SEGMENTS:
    _OFF[_n] = _o
    _o += _w
NP_BLOCKS = _o
NP = NP_BLOCKS * LANE
CMP_BLOCKS = sum(w for _, w in _CMP_SEGMENTS)
CMP_COLS = CMP_BLOCKS * LANE
NW = NP + CMP_COLS
_BLOCK_MODE = tuple(m for _, w, m, _ in _SEGMENTS for _ in range(w))
_COL_SCALE = np.repeat(np.array([s for _, w, _, s in _SEGMENTS for _ in range(w)], np.float32), LANE)[None, :]
_PROJ_CHUNK = 4


def _params(sem, vmem_mb):
    return pltpu.CompilerParams(dimension_semantics=sem, vmem_limit_bytes=vmem_mb * 1024 * 1024)


def _resident(block_shape, index_map):
    return pl.BlockSpec(block_shape, index_map, pipeline_mode=pl.Buffered(1))


def _rms(x, g):
    return x * lax.rsqrt(jnp.mean(x * x, axis=-1, keepdims=True) + RMS_EPS) * g


def _rms_kernel(x_ref, g_ref, h_ref):
    h_ref[...] = _rms(x_ref[...], g_ref[...]).astype(h_ref.dtype)


def _rms_call(x, g, tm):
    m, d = x.shape
    return pl.pallas_call(
        _rms_kernel,
        grid=(m // tm,),
        in_specs=[pl.BlockSpec((tm, d), lambda i: (i, 0)), pl.BlockSpec((1, d), lambda i: (0, 0))],
        out_specs=pl.BlockSpec((tm, d), lambda i: (i, 0)),
        out_shape=jax.ShapeDtypeStruct((m, d), BF16),
        compiler_params=_params(("parallel",), 32),
        name="rms_pre",
    )(x, g)


def _rope_full(a, cos, sin):
    return a * cos + pltpu.roll(a, HEAD_DIM // 2, 1) * sin


def _rope_idx(a, cos, sin, first_half):
    partner = jnp.where(first_half, pltpu.roll(a, LANE - IDX_DIM // 2, 1), pltpu.roll(a, IDX_DIM // 2, 1))
    return a * cos + partner * sin


def _proj_kernel(*refs, after_ffn):
    if after_ffn:
        (y_ref, x_ref, gpost_ref, gpre_ref, w_ref, cs_ref, cos_ref, sin_ref, icos_ref, isin_ref,
         xo_ref, o_ref, oc_ref, cmp_ref) = refs
        x = x_ref[...] + _rms(y_ref[...], gpost_ref[...])
        xo_ref[...] = x
        h = _rms(x, gpre_ref[...]).astype(BF16)
    else:
        h_ref, w_ref, cs_ref, cos_ref, sin_ref, icos_ref, isin_ref, o_ref, oc_ref, cmp_ref = refs
        h = h_ref[...]
    tm = h.shape[0]
    lane = lax.broadcasted_iota(jnp.int32, (tm, LANE), 1)
    first_half = (lane & (IDX_DIM - 1)) < IDX_DIM // 2
    for c0 in range(0, NP_BLOCKS, _PROJ_CHUNK):
        nb = min(_PROJ_CHUNK, NP_BLOCKS - c0)
        acc = jnp.dot(h, w_ref[:, c0 * LANE:(c0 + nb) * LANE], preferred_element_type=F32)
        for b in range(nb):
            col = slice((c0 + b) * LANE, (c0 + b + 1) * LANE)
            a = acc[:, b * LANE:(b + 1) * LANE] * cs_ref[:, col]
            mode = _BLOCK_MODE[c0 + b]
            if mode == MODE_ROPE:
                a = _rope_full(a, cos_ref[...], sin_ref[...])
            elif mode == MODE_IROPE:
                a = _rope_idx(a, icos_ref[...], isin_ref[...], first_half)
            o_ref[:, col] = a.astype(o_ref.dtype)
    acc = jnp.dot(h, w_ref[:, NP:NW], preferred_element_type=F32)
    for b in range(CMP_BLOCKS):
        cmp_ref[b] = acc[:, b * LANE:(b + 1) * LANE]
    for j in range(CMP_STRIDE):
        for b in range(CMP_BLOCKS):
            rows = cmp_ref[b, pl.ds(j, tm // CMP_STRIDE, stride=CMP_STRIDE), :]
            oc_ref[:, j * CMP_COLS + b * LANE:j * CMP_COLS + (b + 1) * LANE] = rows.astype(oc_ref.dtype)


def _proj_call(rows_in, w_in_p, layer, col_scale, tabs, seq, tm):
    after_ffn = len(rows_in) > 1
    m, d = rows_in[0].shape
    tpb = seq // tm
    tab_spec = pl.BlockSpec((tm, LANE), lambda i: (i % tpb, 0))
    row = pl.BlockSpec((tm, d), lambda i: (i, 0))
    vec = pl.BlockSpec((1, d), lambda i: (0, 0))
    out_specs = [
        pl.BlockSpec((tm, NP), lambda i: (i, 0)),
        pl.BlockSpec((tm // CMP_STRIDE, CMP_STRIDE * CMP_COLS), lambda i: (i, 0)),
    ]
    out_shape = [
        jax.ShapeDtypeStruct((m, NP), BF16),
        jax.ShapeDtypeStruct((m // CMP_STRIDE, CMP_STRIDE * CMP_COLS), BF16),
    ]
    if after_ffn:
        out_specs = [row] + out_specs
        out_shape = [jax.ShapeDtypeStruct((m, d), F32)] + out_shape
    return pl.pallas_call(
        functools.partial(_proj_kernel, after_ffn=after_ffn),
        grid=(m // tm,),
        in_specs=([row, row, vec, vec] if after_ffn else [row]) + [
            _resident((None, d, NW), lambda i: (layer, 0, 0)),
            _resident((1, NP), lambda i: (0, 0)),
            tab_spec, tab_spec, tab_spec, tab_spec,
        ],
        out_specs=out_specs,
        out_shape=out_shape,
        scratch_shapes=[pltpu.VMEM((CMP_BLOCKS, tm, LANE), F32)],
        compiler_params=_params(("parallel",), 54),
        name="in_proj",
    )(*rows_in, w_in_p, col_scale, *tabs)


def _transpose_into(dst_ref, src_ref):
    rows, cols = src_ref.shape
    for r0 in range(0, rows, LANE):
        for c0 in range(0, cols, LANE):
            tile = src_ref[r0:r0 + LANE, c0:c0 + LANE].astype(F32)
            dst_ref[c0:c0 + LANE, r0:r0 + LANE] = tile.T.astype(dst_ref.dtype)


def _transpose_tiles(dst_ref, src_ref):
    rows, _ = src_ref.shape
    for r in range(rows // LANE):
        tile = src_ref[r * LANE:(r + 1) * LANE, :].astype(F32)
        dst_ref[r] = tile.T.astype(dst_ref.dtype)


def _for_causal_prefix(i, tq, s_len, step, body):
    n = ((i + 1) * tq + step - 1) // step
    for v in range(1, s_len // step + 1):
        pl.when(n == v)(functools.partial(body, v * step))


def _fold(x, op):
    out = x[0:COUNT_ROWS]
    for r0 in range(COUNT_ROWS, x.shape[0], COUNT_ROWS):
        out = op(out, x[r0:r0 + COUNT_ROWS])
    return out


def _fold_tree(x, op):
    while x.shape[0] > SUBLANE:
        half = x.shape[0] // 2
        x = op(x[:half], x[half:])
    return x


def _row_chunk(rows):
    return SCORE_ROWS if rows % SCORE_ROWS == 0 else LANE


def _attend_heads(qs, k_of, pv_of, rows, mask_of, s_ref, p_ref, w_ref=None):
    n_heads = len(qs)
    tq = qs[0].shape[0]
    chunk = ATT_ROWS * LANE // tq
    for h in range(n_heads):
        s_ref[h, 0:rows, :] = lax.dot_general(k_of(h, 0, rows), qs[h], (((1,), (1,)), ((), ())),
                                              preferred_element_type=F32)
    m_acc = [jnp.full((SUBLANE, tq), NEG, F32) for _ in qs]
    weighted = False
    for r0 in range(0, rows, chunk):
        mask, weight = mask_of(r0, chunk)
        if weight is not None:
            weighted = True
            w_ref[r0:r0 + chunk, :] = weight
        for h in range(n_heads):
            s = jnp.where(mask, s_ref[h, r0:r0 + chunk, :], NEG)
            s_ref[h, r0:r0 + chunk, :] = s
            m_acc[h] = jnp.maximum(m_acc[h], _fold_tree(s, jnp.maximum))
    outs = []
    for h in range(n_heads):
        mx = jnp.max(m_acc[h], axis=0, keepdims=True)
        l_acc = jnp.zeros((SUBLANE, tq), F32)
        for r0 in range(0, rows, chunk):
            p = jnp.exp2(s_ref[h, r0:r0 + chunk, :] - mx)
            if weighted:
                p = p * w_ref[r0:r0 + chunk, :]
            l_acc = l_acc + _fold_tree(p, jnp.add)
            p_ref[h, r0:r0 + chunk, :] = p.astype(p_ref.dtype)
        den = jnp.sum(l_acc, axis=0, keepdims=True)
        outs.append(pv_of(h, p_ref[h, 0:rows, :]) / den)
    return outs


def _count_rows(key_ref, rows, pred):
    chunk = _row_chunk(rows)
    acc = None
    for r0 in range(0, rows, chunk):
        part = _fold(pred(key_ref[r0:r0 + chunk, :], r0), jnp.add)
        acc = part if acc is None else acc + part
    return jnp.sum(acc, axis=0, keepdims=True)


def _dsa_prefix(rows, i, iq_ref, iw_t, ik_ref, q_ref, k_ref, vt_ref, o_ref, key_ref, cut_ref, s_ref, p_ref, topk):
    tq = q_ref.shape[0]
    chunk = min(SCORE_ROWS * LANE // tq, rows)
    lane = lax.broadcasted_iota(jnp.int32, (chunk, LANE), 1)
    crow = lax.broadcasted_iota(jnp.int32, (chunk, tq), 0)
    ctcol = i * tq + lax.broadcasted_iota(jnp.int32, (chunk, tq), 1)

    for r0 in range(0, rows, chunk):
        ik = ik_ref[r0:r0 + chunk, :]
        halves = (jnp.where(lane < IDX_DIM, ik, jnp.zeros_like(ik)),
                  jnp.where(lane >= IDX_DIM, ik, jnp.zeros_like(ik)))
        score = jnp.zeros((chunk, tq), F32)
        for p in range(IDX_HEADS // 2):
            blk = iq_ref[:, p * LANE:(p + 1) * LANE]
            for half in range(2):
                j = 2 * p + half
                lg = lax.dot_general(halves[half], blk, (((1,), (1,)), ((), ())), preferred_element_type=F32)
                score = score + jnp.maximum(lg, 0.0) * iw_t[j:j + 1, :]
        bits = pltpu.bitcast(score, jnp.int32)
        key = bits ^ ((bits >> 31) & 0x7FFFFFFF)
        key_ref[r0:r0 + chunk, :] = jnp.where(r0 + crow <= ctcol, key, INT_MIN)

    def bit_step(b, thr):
        cand = thr ^ jnp.left_shift(jnp.int32(1), 31 - b)
        cnt = _count_rows(key_ref, rows, lambda kc, r0: jnp.where(kc >= cand, 1, 0))
        return jnp.where(cnt >= topk, cand, thr)

    thr = lax.fori_loop(0, 32, bit_step, jnp.full((1, tq), INT_MIN, jnp.int32))
    short = thr == INT_MIN
    need = topk - _count_rows(key_ref, rows, lambda kc, r0: jnp.where(kc > thr, 1, 0))
    n_eq = _count_rows(key_ref, rows, lambda kc, r0: jnp.where(kc == thr, 1, 0))
    excess = jnp.where(n_eq > need, jnp.where(short, 0, 1), 0)
    cut_ref[...] = jnp.where(short, -1, rows)

    def row_ids(r0, n):
        return r0 + lax.broadcasted_iota(jnp.int32, (n, tq), 0)

    @pl.when(jnp.max(excess) > 0)
    def _():
        n_bits = max(1, (rows - 1).bit_length())

        def idx_step(b, cut):
            cand = cut | jnp.left_shift(jnp.int32(1), n_bits - 1 - b)
            below = _count_rows(key_ref, rows, lambda kc, r0: jnp.where(
                kc == thr, jnp.where(row_ids(r0, kc.shape[0]) < cand, 1, 0), 0))
            return jnp.where(below < need, cand, cut)

        cut = lax.fori_loop(0, n_bits, idx_step, jnp.zeros((1, tq), jnp.int32))
        cut_ref[...] = jnp.where(excess > 0, cut, cut_ref[...])

    cut = cut_ref[...]

    def mask_of(r0, n):
        kc = key_ref[r0:r0 + n, :]
        chosen = jnp.where(kc > thr, 1, jnp.where(kc == thr, jnp.where(row_ids(r0, n) <= cut, 1, 0), 0))
        return chosen > 0, None

    vt = vt_ref[:, 0:rows]
    cols = [slice(h * HEAD_DIM, (h + 1) * HEAD_DIM) for h in range(A_HEADS)]
    outs = _attend_heads([q_ref[:, c] for c in cols], lambda h, r0, n: k_ref[r0:r0 + n, :],
                         lambda h, p: jnp.dot(vt, p, preferred_element_type=F32),
                         rows, mask_of, s_ref, p_ref)
    for h, c in enumerate(cols):
        o_ref[:, c] = outs[h].T.astype(o_ref.dtype)


def _dsa_kernel(iq_ref, iw_ref, ik_ref, q_ref, k_ref, v_ref, o_ref, vt_ref, key_ref, cut_ref, s_ref, p_ref,
                *, topk, step):
    i = pl.program_id(1)
    tq = q_ref.shape[0]
    s_len = k_ref.shape[0]

    @pl.when(i == 0)
    def _():
        _transpose_into(vt_ref, v_ref)

    iw_t = iw_ref[...].astype(F32).T[:IDX_HEADS, :]
    _for_causal_prefix(i, tq, s_len, step, functools.partial(
        _dsa_prefix, i=i, iq_ref=iq_ref, iw_t=iw_t, ik_ref=ik_ref, q_ref=q_ref, k_ref=k_ref, vt_ref=vt_ref,
        o_ref=o_ref, key_ref=key_ref, cut_ref=cut_ref, s_ref=s_ref, p_ref=p_ref, topk=topk))


def _dsa_call(p3, tq, step):
    b, s, _ = p3.shape
    topk = min(DSA_TOPK_MAX, s // 4)
    blk = lambda name, width: _OFF[name] // width
    return pl.pallas_call(
        functools.partial(_dsa_kernel, topk=topk, step=step),
        grid=(b, s // tq),
        in_specs=[
            pl.BlockSpec((None, tq, 8 * LANE), lambda bi, i: (bi, i, blk("i_q", 8))),
            pl.BlockSpec((None, tq, LANE), lambda bi, i: (bi, i, blk("i_w", 1))),
            pl.BlockSpec((None, s, LANE), lambda bi, i: (bi, 0, blk("i_k", 1))),
            pl.BlockSpec((None, tq, 4 * LANE), lambda bi, i: (bi, i, blk("a_q", 4))),
            pl.BlockSpec((None, s, LANE), lambda bi, i: (bi, 0, blk("a_k", 1))),
            pl.BlockSpec((None, s, LANE), lambda bi, i: (bi, 0, blk("a_v", 1))),
        ],
        out_specs=pl.BlockSpec((None, tq, A_HEADS * HEAD_DIM), lambda bi, i: (bi, i, 0)),
        out_shape=jax.ShapeDtypeStruct((b, s, A_HEADS * HEAD_DIM), BF16),
        scratch_shapes=[
            pltpu.VMEM((HEAD_DIM, s), BF16),
            pltpu.VMEM((s, tq), jnp.int32),
            pltpu.VMEM((1, tq), jnp.int32),
            pltpu.VMEM((A_HEADS, s, tq), F32),
            pltpu.VMEM((A_HEADS, s, tq), BF16),
        ],
        compiler_params=_params(("parallel", "arbitrary"), 48),
        name="dsa",
    )(p3, p3, p3, p3, p3, p3)


def _dil_prefix(rows, i, q_ref, k_ref, vt_ref, o_ref, s_ref, p_ref, w_ref):
    tq = q_ref.shape[0]

    def mask_of(r0, n):
        srow = r0 + lax.broadcasted_iota(jnp.int32, (n, tq), 0)
        tcol = i * tq + lax.broadcasted_iota(jnp.int32, (n, tq), 1)
        delta = tcol - srow
        mult = jnp.zeros((n, tq), jnp.int32)
        for window, dilation in DILATED_PATTERNS:
            on_stride = (delta & (dilation - 1)) == 0
            mult = mult + jnp.where(on_stride, jnp.where(delta <= window, 1, 0), 0)
        mult = jnp.where(delta >= 0, mult, 0)
        return mult > 0, mult.astype(F32)

    cols = [slice(h * HEAD_DIM, (h + 1) * HEAD_DIM) for h in range(B_HEADS)]
    outs = _attend_heads([q_ref[:, c] for c in cols], lambda h, r0, n: k_ref[r0:r0 + n, cols[h]],
                         lambda h, p: jnp.dot(vt_ref[cols[h], 0:rows], p, preferred_element_type=F32),
                         rows, mask_of, s_ref, p_ref, w_ref)
    for h, c in enumerate(cols):
        o_ref[:, c] = outs[h].T.astype(o_ref.dtype)


def _dil_kernel(q_ref, k_ref, v_ref, o_ref, vt_ref, s_ref, p_ref, w_ref, *, step):
    i = pl.program_id(1)

    @pl.when(i == 0)
    def _():
        _transpose_into(vt_ref, v_ref)

    _for_causal_prefix(i, q_ref.shape[0], k_ref.shape[0], step, functools.partial(
        _dil_prefix, i=i, q_ref=q_ref, k_ref=k_ref, vt_ref=vt_ref, o_ref=o_ref,
        s_ref=s_ref, p_ref=p_ref, w_ref=w_ref))


def _dil_call(p3, tq, step):
    b, s, _ = p3.shape
    width = B_HEADS * HEAD_DIM
    return pl.pallas_call(
        functools.partial(_dil_kernel, step=step),
        grid=(b, s // tq),
        in_specs=[
            pl.BlockSpec((None, tq, width), lambda bi, i: (bi, i, _OFF["b_q"] // 4)),
            pl.BlockSpec((None, s, width), lambda bi, i: (bi, 0, _OFF["b_k"] // 4)),
            pl.BlockSpec((None, s, width), lambda bi, i: (bi, 0, _OFF["b_v"] // 4)),
        ],
        out_specs=pl.BlockSpec((None, tq, width), lambda bi, i: (bi, i, 0)),
        out_shape=jax.ShapeDtypeStruct((b, s, width), BF16),
        scratch_shapes=[
            pltpu.VMEM((width, s), BF16),
            pltpu.VMEM((B_HEADS, s, tq), F32),
            pltpu.VMEM((B_HEADS, s, tq), BF16),
            pltpu.VMEM((s, tq), F32),
        ],
        compiler_params=_params(("parallel", "arbitrary"), 48),
        name="dilated",
    )(p3, p3, p3)


def _gelu_tanh(x):
    return 0.5 * x * (1.0 + jnp.tanh(np.float32(np.sqrt(2.0 / np.pi)) * (x + 0.044715 * (x * x * x))))


def _cmp_kernel(*refs):
    n = CMP_STRIDE
    xk, xv = refs[:n], refs[n:2 * n]
    pe_k, w1_k, w2_k, pe_v, w1_v, w2_v, kc_ref, vct_ref = refs[2 * n:]

    def branch(x_refs, pe_ref, w1_ref, w2_ref):
        lo = hi = None
        for j in range(n):
            xj = x_refs[j][...].astype(F32)
            a = (xj + pe_ref[j:j + 1, :]).astype(BF16)
            b = (xj + pe_ref[n + j:n + j + 1, :]).astype(BF16)
            dl = jnp.dot(a, w1_ref[j * HEAD_DIM:(j + 1) * HEAD_DIM, :], preferred_element_type=F32)
            dh = jnp.dot(b, w1_ref[(n + j) * HEAD_DIM:(n + j + 1) * HEAD_DIM, :], preferred_element_type=F32)
            lo = dl if lo is None else lo + dl
            hi = dh if hi is None else hi + dh
        nc = lo.shape[0]
        hid = lo + pltpu.roll(hi, nc - 1, 0)
        return jnp.dot(_gelu_tanh(hid).astype(BF16), w2_ref[...], preferred_element_type=F32)

    kc_ref[...] = branch(xk, pe_k, w1_k, w2_k).astype(kc_ref.dtype)
    vct_ref[...] = branch(xv, pe_v, w1_v, w2_v).T.astype(vct_ref.dtype)


def _cmp_call(pc, b, layer, pe_k, w1_k, w2_k, pe_v, w1_v, w2_v):
    nc = pc.shape[0] // b
    pc3 = pc.reshape(b, nc, CMP_STRIDE * CMP_COLS)
    x_specs = []
    for first in (0, C_KV_GROUPS):
        for j in range(CMP_STRIDE):
            x_specs.append(pl.BlockSpec((None, nc, HEAD_DIM),
                                        lambda bi, g, j=j, first=first: (bi, 0, j * CMP_BLOCKS + first + g)))
    flat = CMP_BLOCK * HEAD_DIM
    w_specs = [
        _resident((None, CMP_BLOCK, HEAD_DIM), lambda bi, g: (layer, 0, 0)),
        _resident((None, flat, CMP_HIDDEN), lambda bi, g: (layer, 0, 0)),
        _resident((None, CMP_HIDDEN, HEAD_DIM), lambda bi, g: (layer, 0, 0)),
    ]
    return pl.pallas_call(
        _cmp_kernel,
        grid=(b, C_KV_GROUPS),
        in_specs=x_specs + w_specs + w_specs,
        out_specs=[
            pl.BlockSpec((None, None, nc, HEAD_DIM), lambda bi, g: (bi, g, 0, 0)),
            pl.BlockSpec((None, None, HEAD_DIM, nc), lambda bi, g: (bi, g, 0, 0)),
        ],
        out_shape=[
            jax.ShapeDtypeStruct((b, C_KV_GROUPS, nc, HEAD_DIM), BF16),
            jax.ShapeDtypeStruct((b, C_KV_GROUPS, HEAD_DIM, nc), BF16),
        ],
        compiler_params=_params(("parallel", "parallel"), 32),
        name="nsa_compress",
    )(*([pc3] * (2 * CMP_STRIDE)), pe_k, w1_k, w2_k, pe_v, w1_v, w2_v)


def _nsa_selected(rows, t_row, sel, q_rot, ks_ref, vst_ref, expand_ref, acc_ref, s_ref, p_ref):
    tq = sel.shape[1]

    def mask_of(r0, n):
        in_sel = jnp.dot(expand_ref[r0:r0 + n, :], sel, preferred_element_type=F32)
        srow = r0 + lax.broadcasted_iota(jnp.int32, (n, tq), 0)
        return jnp.where(t_row - srow >= 0, in_sel, 0.0) > 0.5, None

    vt = vst_ref[:, 0:rows]
    outs = _attend_heads(q_rot, lambda h, r0, n: ks_ref[r0:r0 + n, :],
                         lambda h, p: jnp.dot(vt, p, preferred_element_type=F32),
                         rows, mask_of, s_ref, p_ref)
    for r in range(C_GROUP_SIZE):
        acc_ref[r] = outs[r]


def _nsa_kernel(q_ref, kc_ref, vct_ref, ks_ref, vs_ref, kw_ref, vw_ref, g_ref, cos_ref, sin_ref,
                isect_ref, expand_ref, o_ref, vst_ref, vwt_ref, acc_ref, s_ref, p_ref, *, n_sel, step):
    i = pl.program_id(2)
    tq = q_ref.shape[0]
    s_len = ks_ref.shape[0]
    nc = kc_ref.shape[0]
    n_slc = isect_ref.shape[0]

    @pl.when(i == 0)
    def _():
        _transpose_into(vst_ref, vs_ref)
        _transpose_tiles(vwt_ref, vw_ref)

    t_row = i * tq + lax.broadcasted_iota(jnp.int32, (1, tq), 1)

    kc = kc_ref[...]
    vct = vct_ref[...]
    cend = lax.broadcasted_iota(jnp.int32, (nc, tq), 0) * CMP_STRIDE + (CMP_BLOCK - 1)
    cmask = cend <= t_row
    o_cmp = []
    p_sum = jnp.zeros((nc, tq), F32)
    for r in range(C_GROUP_SIZE):
        q_r = q_ref[:, r * HEAD_DIM:(r + 1) * HEAD_DIM]
        sc = lax.dot_general(kc, q_r, (((1,), (1,)), ((), ())), preferred_element_type=F32)
        sc = jnp.where(cmask, sc, NEG)
        mx = jnp.max(sc, axis=0, keepdims=True)
        e = jnp.where(cmask, jnp.exp2(sc - mx), 0.0)
        den = jnp.sum(e, axis=0, keepdims=True)
        p = e / jnp.where(den > 0, den, 1.0)
        p_sum = p_sum + p
        o_cmp.append(jnp.dot(vct, p.astype(BF16), preferred_element_type=F32))

    isect = isect_ref[...]
    p_hi = p_sum.astype(BF16)
    p_lo = (p_sum - p_hi.astype(F32)).astype(BF16)
    imp = (jnp.dot(isect, p_hi, preferred_element_type=F32)
           + jnp.dot(isect, p_lo, preferred_element_type=F32))
    blk = lax.broadcasted_iota(jnp.int32, (n_slc, tq), 0)
    cur = t_row // SEL_BLOCK
    val = jnp.where(blk == 0, jnp.inf, jnp.where(blk == cur, jnp.inf, jnp.where(blk == cur - 1, jnp.inf, imp)))
    val = jnp.where(blk <= cur, val, -jnp.inf)
    rank = jnp.zeros((n_slc, tq), jnp.int32)
    for m in range(n_slc):
        vm = val[m:m + 1, :]
        before = jnp.where(vm > val, 1, jnp.where(vm == val, jnp.where(blk > m, 1, 0), 0))
        rank = rank + before
    sel = jnp.where(rank < n_sel, 1.0, 0.0)
    if n_slc < LANE:
        sel = jnp.concatenate([sel, jnp.zeros((LANE - n_slc, tq), F32)], axis=0)
    sel = sel.astype(BF16)

    cos, sin = cos_ref[...], sin_ref[...]
    cols = [slice(r * HEAD_DIM, (r + 1) * HEAD_DIM) for r in range(C_GROUP_SIZE)]
    q_rot = [_rope_full(q_ref[:, c].astype(F32), cos, sin).astype(BF16) for c in cols]

    _for_causal_prefix(i, tq, s_len, step, functools.partial(
        _nsa_selected, t_row=t_row, sel=sel, q_rot=q_rot, ks_ref=ks_ref, vst_ref=vst_ref,
        expand_ref=expand_ref, acc_ref=acc_ref, s_ref=s_ref, p_ref=p_ref))

    w_tiles = min((WIN_SIZE - 1 + tq - 1) // LANE + 1, s_len // LANE)
    w_rows = w_tiles * LANE
    w0 = pl.multiple_of(jnp.maximum(i * tq + tq - w_rows, 0), LANE)
    t0 = w0 // LANE

    def win_mask(r0, n):
        delta = t_row - (w0 + r0 + lax.broadcasted_iota(jnp.int32, (n, tq), 0))
        return jnp.where(delta >= 0, jnp.where(delta <= WIN_SIZE - 1, 1, 0), 0) > 0, None

    def pv_win(h, p):
        out = None
        for w in range(w_tiles):
            part = jnp.dot(vwt_ref[t0 + w], p[w * LANE:(w + 1) * LANE, :], preferred_element_type=F32)
            out = part if out is None else out + part
        return out

    o_win = _attend_heads(q_rot, lambda h, r0, n: kw_ref[pl.ds(w0 + r0, n), :], pv_win,
                          w_rows, win_mask, s_ref, p_ref)

    gates = jax.nn.sigmoid(g_ref[...].astype(F32).T)
    for r, c in enumerate(cols):
        o_t = (gates[3 * r:3 * r + 1, :] * o_cmp[r] + gates[3 * r + 1:3 * r + 2, :] * acc_ref[r]
               + gates[3 * r + 2:3 * r + 3, :] * o_win[r])
        o_ref[:, c] = o_t.T.astype(o_ref.dtype)


def _nsa_call(p3, kc, vct, cos, sin, isect, expand, tq, step):
    b, s, _ = p3.shape
    nc = kc.shape[2]
    n_slc = s // SEL_BLOCK
    width = C_GROUP_SIZE * HEAD_DIM
    kv = lambda name: pl.BlockSpec((None, s, HEAD_DIM), lambda bi, g, i, name=name: (bi, 0, _OFF[name] + g))
    return pl.pallas_call(
        functools.partial(_nsa_kernel, n_sel=min(SEL_COUNT, n_slc), step=step),
        grid=(b, C_KV_GROUPS, s // tq),
        in_specs=[
            pl.BlockSpec((None, tq, width), lambda bi, g, i: (bi, i, _OFF["c_q"] // 4 + g)),
            pl.BlockSpec((None, None, nc, HEAD_DIM), lambda bi, g, i: (bi, g, 0, 0)),
            pl.BlockSpec((None, None, HEAD_DIM, nc), lambda bi, g, i: (bi, g, 0, 0)),
            kv("c_ks"), kv("c_vs"), kv("c_kw"), kv("c_vw"),
            pl.BlockSpec((None, tq, LANE), lambda bi, g, i: (bi, i, _OFF["c_g0"] + g)),
            pl.BlockSpec((tq, LANE), lambda bi, g, i: (i, 0)),
            pl.BlockSpec((tq, LANE), lambda bi, g, i: (i, 0)),
            pl.BlockSpec(isect.shape, lambda bi, g, i: (0, 0)),
            pl.BlockSpec(expand.shape, lambda bi, g, i: (0, 0)),
        ],
        out_specs=pl.BlockSpec((None, tq, width), lambda bi, g, i: (bi, i, g)),
        out_shape=jax.ShapeDtypeStruct((b, s, C_HEADS * HEAD_DIM), BF16),
        scratch_shapes=[
            pltpu.VMEM((HEAD_DIM, s), BF16),
            pltpu.VMEM((s // LANE, HEAD_DIM, LANE), BF16),
            pltpu.VMEM((C_GROUP_SIZE, HEAD_DIM, tq), F32),
            pltpu.VMEM((C_GROUP_SIZE, s, tq), F32),
            pltpu.VMEM((C_GROUP_SIZE, s, tq), BF16),
        ],
        compiler_params=_params(("parallel", "parallel", "arbitrary"), 48),
        name="nsa",
    )(p3, kc, vct, p3, p3, p3, p3, p3, cos, sin, isect, expand)


def _out_kernel(oa_ref, ob_ref, oc_ref, w_ref, x_ref, gpost_ref, gnext_ref, xo_ref, h_ref):
    na, nb = oa_ref.shape[1], ob_ref.shape[1]
    y = jnp.dot(oa_ref[...], w_ref[0:na, :], preferred_element_type=F32)
    y = y + jnp.dot(ob_ref[...], w_ref[na:na + nb, :], preferred_element_type=F32)
    y = y + jnp.dot(oc_ref[...], w_ref[na + nb:, :], preferred_element_type=F32)
    x = x_ref[...] + _rms(y, gpost_ref[...])
    xo_ref[...] = x
    h_ref[...] = _rms(x, gnext_ref[...]).astype(h_ref.dtype)


def _out_call(o_a, o_b, o_c, w_out_p, layer, x, g_post, g_next, tm):
    m, d = x.shape
    row = lambda width: pl.BlockSpec((tm, width), lambda i: (i, 0))
    vec = pl.BlockSpec((1, d), lambda i: (0, 0))
    return pl.pallas_call(
        _out_kernel,
        grid=(m // tm,),
        in_specs=[
            row(o_a.shape[1]), row(o_b.shape[1]), row(o_c.shape[1]),
            _resident((None, w_out_p.shape[1], d), lambda i: (layer, 0, 0)),
            row(d), vec, vec,
        ],
        out_specs=[row(d), row(d)],
        out_shape=[jax.ShapeDtypeStruct((m, d), F32), jax.ShapeDtypeStruct((m, d), BF16)],
        compiler_params=_params(("parallel",), 48),
        name="out_proj",
    )(o_a, o_b, o_c, w_out_p, x, g_post, g_next)


def _ffn_kernel(h_ref, wg_ref, wu_ref, wd_ref, y_ref):
    @pl.when(pl.program_id(1) == 0)
    def _():
        y_ref[...] = jnp.zeros(y_ref.shape, F32)

    h = h_ref[...]
    g = jnp.dot(h, wg_ref[...], preferred_element_type=F32)
    u = jnp.dot(h, wu_ref[...], preferred_element_type=F32)
    a = (g * jax.nn.sigmoid(g) * u).astype(BF16)
    y_ref[...] += jnp.dot(a, wd_ref[...], preferred_element_type=F32)


def _ffn_call(h, wg, wu, wd, layer, tm, tf):
    m, d = h.shape
    d_ff = wg.shape[2]
    row = pl.BlockSpec((tm, d), lambda i, f: (i, 0))
    return pl.pallas_call(
        _ffn_kernel,
        grid=(m // tm, d_ff // tf),
        in_specs=[
            row,
            pl.BlockSpec((None, d, tf), lambda i, f: (layer, 0, f)),
            pl.BlockSpec((None, d, tf), lambda i, f: (layer, 0, f)),
            pl.BlockSpec((None, tf, d), lambda i, f: (layer, f, 0)),
        ],
        out_specs=row,
        out_shape=jax.ShapeDtypeStruct((m, d), F32),
        compiler_params=_params(("parallel", "arbitrary"), 52),
        name="ffn",
    )(h, wg, wu, wd)


def _residual_kernel(y_ref, x_ref, g_ref, o_ref):
    o_ref[...] = x_ref[...] + _rms(y_ref[...], g_ref[...])


def _residual_call(y, x, g, tm):
    m, d = x.shape
    row = pl.BlockSpec((tm, d), lambda i: (i, 0))
    return pl.pallas_call(
        _residual_kernel,
        grid=(m // tm,),
        in_specs=[row, row, pl.BlockSpec((1, d), lambda i: (0, 0))],
        out_specs=row,
        out_shape=jax.ShapeDtypeStruct((m, d), F32),
        compiler_params=_params(("parallel",), 32),
        name="ffn_residual",
    )(y, x, g)


def _pad_cols(w, width):
    return jnp.pad(w, ((0, 0), (0, 0), (0, width - w.shape[-1])))


def _prep_w_in(w_in):
    widths = (A_HEADS * HEAD_DIM, HEAD_DIM, HEAD_DIM, IDX_HEADS * IDX_DIM, IDX_DIM, IDX_HEADS,
              B_HEADS * HEAD_DIM, B_HEADS * HEAD_DIM, B_HEADS * HEAD_DIM, C_HEADS * HEAD_DIM,
              C_KV_GROUPS * HEAD_DIM, C_KV_GROUPS * HEAD_DIM, C_KV_GROUPS * HEAD_DIM, C_KV_GROUPS * HEAD_DIM,
              C_KV_GROUPS * HEAD_DIM, C_KV_GROUPS * HEAD_DIM, 3 * C_HEADS)
    names = ("a_q", "a_k", "a_v", "i_q", "i_k", "i_w", "b_q", "b_k", "b_v", "c_q",
             "c_kc", "c_vc", "c_ks", "c_vs", "c_kw", "c_vw", "c_g")
    offs = np.concatenate([[0], np.cumsum(widths)])
    src = {n: w_in[:, :, int(offs[k]):int(offs[k + 1])] for k, n in enumerate(names)}
    per_group = 3 * C_GROUP_SIZE
    src["c_g0"] = _pad_cols(src["c_g"][:, :, :per_group], LANE)
    src["c_g1"] = _pad_cols(src["c_g"][:, :, per_group:], LANE)
    src["i_k"] = jnp.concatenate([src["i_k"], src["i_k"]], axis=-1)
    src["i_w"] = _pad_cols(src["i_w"], LANE)
    order = [n for n, _, _, _ in _SEGMENTS] + [n for n, _ in _CMP_SEGMENTS]
    return jnp.concatenate([src[n] for n in order], axis=-1).astype(BF16)


def _rope_tables(seq):
    def tables(dim):
        inv = 1.0 / (ROPE_THETA ** (jnp.arange(0, dim, 2, dtype=F32) / dim))
        ang = jnp.arange(seq, dtype=F32)[:, None] * inv[None, :]
        cos, sin = jnp.cos(ang), jnp.sin(ang)
        reps = LANE // dim
        return (jnp.tile(jnp.concatenate([cos, cos], axis=-1), (1, reps)),
                jnp.tile(jnp.concatenate([-sin, sin], axis=-1), (1, reps)))
    return tables(HEAD_DIM) + tables(IDX_DIM)


def _selection_constants(seq):
    nc = seq // CMP_STRIDE
    n_slc = seq // SEL_BLOCK
    c_start = np.arange(nc) * CMP_STRIDE
    n_start = np.arange(n_slc) * SEL_BLOCK
    isect = ((c_start[None, :] < n_start[:, None] + SEL_BLOCK)
             & (c_start[None, :] + CMP_BLOCK > n_start[:, None])
             & (np.arange(nc)[None, :] < nc - CMP_BLOCK // CMP_STRIDE + 1)).astype(np.float32)
    expand = np.zeros((seq, max(n_slc, LANE)), np.float32)
    expand[np.arange(seq), np.arange(seq) // SEL_BLOCK] = 1.0
    return jnp.asarray(isect, BF16), jnp.asarray(expand, BF16)


def _tiles(seq):
    return dict(tq=min(2 * LANE, seq), step=min(512, seq), tm_proj=min(256, seq), tm_out=min(512, seq),
                tm_ffn=min(1024, seq), tf=512)


def kernel(x, w_in, w_out, cmp_pe_k, cmp_w1_k, cmp_w2_k, cmp_pe_v, cmp_w1_v, cmp_w2_v,
           w_gate, w_up, w_down, g_pre_mix, g_post_mix, g_pre_ffn, g_post_ffn):
    b, s, d = x.shape
    depth = w_in.shape[0]
    m = b * s
    t = _tiles(s)
    tq, step = t["tq"], t["step"]

    w_in_p = _prep_w_in(w_in)
    w_out_p = w_out.astype(BF16)
    wg, wu, wd = w_gate.astype(BF16), w_up.astype(BF16), w_down.astype(BF16)
    w1_k, w2_k = cmp_w1_k.astype(BF16), cmp_w2_k.astype(BF16)
    w1_v, w2_v = cmp_w1_v.astype(BF16), cmp_w2_v.astype(BF16)
    col_scale = jnp.asarray(_COL_SCALE)
    tabs = _rope_tables(s)
    isect, expand = _selection_constants(s)
    vec = lambda g, layer: g[layer][None, :]

    xf = x.reshape(m, d)
    y = None
    for layer in range(depth):
        if layer == 0:
            rows_in = (_rms_call(xf, vec(g_pre_mix, 0), t["tm_out"]),)
            p, pc = _proj_call(rows_in, w_in_p, layer, col_scale, tabs, s, t["tm_proj"])
        else:
            rows_in = (y, xf, vec(g_post_ffn, layer - 1), vec(g_pre_mix, layer))
            xf, p, pc = _proj_call(rows_in, w_in_p, layer, col_scale, tabs, s, t["tm_proj"])
        p3 = p.reshape(b, s, NP)
        o_a = _dsa_call(p3, tq, step)
        o_b = _dil_call(p3, tq, step)
        kc, vct = _cmp_call(pc, b, layer, cmp_pe_k, w1_k, w2_k, cmp_pe_v, w1_v, w2_v)
        o_c = _nsa_call(p3, kc, vct, tabs[0], tabs[1], isect, expand, tq, step)
        xf, h = _out_call(o_a.reshape(m, -1), o_b.reshape(m, -1), o_c.reshape(m, -1), w_out_p, layer,
                          xf, vec(g_post_mix, layer), vec(g_pre_ffn, layer), t["tm_out"])
        y = _ffn_call(h, wg, wu, wd, layer, t["tm_ffn"], t["tf"])
    return _residual_call(y, xf, vec(g_post_ffn, depth - 1), t["tm_out"]).reshape(b, s, d)
```

```python
import functools

import numpy as np
import jax
import jax.numpy as jnp
from jax import lax
from jax.experimental import pallas as pl
from jax.experimental.pallas import tpu as pltpu

F32 = jnp.float32
BF16 = jnp.bfloat16

LANE = 128
SUBLANE = 8
HEAD_DIM = 128
A_HEADS = 4
B_HEADS = 4
C_HEADS = 8
C_KV_GROUPS = 2
C_GROUP_SIZE = C_HEADS // C_KV_GROUPS
IDX_HEADS = 16
IDX_DIM = 64
DSA_TOPK_MAX = 256
DILATED_PATTERNS = ((128, 1), (512, 4), (2048, 16))
CMP_BLOCK = 32
CMP_STRIDE = 16
CMP_HIDDEN = 256
SEL_BLOCK = 64
SEL_COUNT = 16
WIN_SIZE = 512
ROPE_THETA = 10000.0
RMS_EPS = 1e-6
ATTN_SCALE = HEAD_DIM ** -0.5 * float(np.log2(np.e))
IDX_SCALE = IDX_DIM ** -0.5 * IDX_HEADS ** -0.5

NEG = -1e30
INT_MIN = -(2 ** 31)
COUNT_ROWS = 8 * SUBLANE
SCORE_ROWS = 256
ATT_ROWS = 128

MODE_NONE, MODE_ROPE, MODE_IROPE = 0, 1, 2

_SEGMENTS = (
    ("i_q", 8, MODE_IROPE, IDX_SCALE),
    ("c_q", 8, MODE_NONE, ATTN_SCALE),
    ("a_q", 4, MODE_ROPE, ATTN_SCALE),
    ("b_q", 4, MODE_ROPE, ATTN_SCALE),
    ("b_k", 4, MODE_ROPE, 1.0),
    ("b_v", 4, MODE_NONE, 1.0),
    ("c_ks", 2, MODE_ROPE, 1.0),
    ("c_vs", 2, MODE_NONE, 1.0),
    ("c_kw", 2, MODE_ROPE, 1.0),
    ("c_vw", 2, MODE_NONE, 1.0),
    ("a_k", 1, MODE_ROPE, 1.0),
    ("a_v", 1, MODE_NONE, 1.0),
    ("i_k", 1, MODE_IROPE, 1.0),
    ("i_w", 1, MODE_NONE, 1.0),
    ("c_g0", 1, MODE_NONE, 1.0),
    ("c_g1", 1, MODE_NONE, 1.0),
)
_CMP_SEGMENTS = (("c_kc", 2), ("c_vc", 2))
_OFF = {}
_o = 0
for _n, _w, _m, _s in _SEGMENTS:
    _OFF[_n] = _o
    _o += _w
NP_BLOCKS = _o
NP = NP_BLOCKS * LANE
CMP_BLOCKS = sum(w for _, w in _CMP_SEGMENTS)
CMP_COLS = CMP_BLOCKS * LANE
NW = NP + CMP_COLS
_BLOCK_MODE = tuple(m for _, w, m, _ in _SEGMENTS for _ in range(w))
_COL_SCALE = np.repeat(np.array([s for _, w, _, s in _SEGMENTS for _ in range(w)], np.float32), LANE)[None, :]
_PROJ_CHUNK = 4


def _params(sem, vmem_mb):
    return pltpu.CompilerParams(dimension_semantics=sem, vmem_limit_bytes=vmem_mb * 1024 * 1024)


def _resident(block_shape, index_map):
    return pl.BlockSpec(block_shape, index_map, pipeline_mode=pl.Buffered(1))


def _rms(x, g):
    return x * lax.rsqrt(jnp.mean(x * x, axis=-1, keepdims=True) + RMS_EPS) * g


def _rms_kernel(x_ref, g_ref, h_ref):
    h_ref[...] = _rms(x_ref[...], g_ref[...]).astype(h_ref.dtype)


def _rms_call(x, g, tm):
    m, d = x.shape
    return pl.pallas_call(
        _rms_kernel,
        grid=(m // tm,),
        in_specs=[pl.BlockSpec((tm, d), lambda i: (i, 0)), pl.BlockSpec((1, d), lambda i: (0, 0))],
        out_specs=pl.BlockSpec((tm, d), lambda i: (i, 0)),
        out_shape=jax.ShapeDtypeStruct((m, d), BF16),
        compiler_params=_params(("parallel",), 32),
        name="rms_pre",
    )(x, g)


def _rope_full(a, cos, sin):
    return a * cos + pltpu.roll(a, HEAD_DIM // 2, 1) * sin


def _rope_idx(a, cos, sin, first_half):
    partner = jnp.where(first_half, pltpu.roll(a, LANE - IDX_DIM // 2, 1), pltpu.roll(a, IDX_DIM // 2, 1))
    return a * cos + partner * sin


def _proj_kernel(*refs, after_ffn):
    if after_ffn:
        (y_ref, x_ref, gpost_ref, gpre_ref, w_ref, cs_ref, cos_ref, sin_ref, icos_ref, isin_ref,
         xo_ref, o_ref, oc_ref, cmp_ref) = refs
        x = x_ref[...] + _rms(y_ref[...], gpost_ref[...])
        xo_ref[...] = x
        h = _rms(x, gpre_ref[...]).astype(BF16)
    else:
        h_ref, w_ref, cs_ref, cos_ref, sin_ref, icos_ref, isin_ref, o_ref, oc_ref, cmp_ref = refs
        h = h_ref[...]
    tm = h.shape[0]
    lane = lax.broadcasted_iota(jnp.int32, (tm, LANE), 1)
    first_half = (lane & (IDX_DIM - 1)) < IDX_DIM // 2
    for c0 in range(0, NP_BLOCKS, _PROJ_CHUNK):
        nb = min(_PROJ_CHUNK, NP_BLOCKS - c0)
        acc = jnp.dot(h, w_ref[:, c0 * LANE:(c0 + nb) * LANE], preferred_element_type=F32)
        for b in range(nb):
            col = slice((c0 + b) * LANE, (c0 + b + 1) * LANE)
            a = acc[:, b * LANE:(b + 1) * LANE] * cs_ref[:, col]
            mode = _BLOCK_MODE[c0 + b]
            if mode == MODE_ROPE:
                a = _rope_full(a, cos_ref[...], sin_ref[...])
            elif mode == MODE_IROPE:
                a = _rope_idx(a, icos_ref[...], isin_ref[...], first_half)
            o_ref[:, col] = a.astype(o_ref.dtype)
    acc = jnp.dot(h, w_ref[:, NP:NW], preferred_element_type=F32)
    for b in range(CMP_BLOCKS):
        cmp_ref[b] = acc[:, b * LANE:(b + 1) * LANE]
    for j in range(CMP_STRIDE):
        for b in range(CMP_BLOCKS):
            rows = cmp_ref[b, pl.ds(j, tm // CMP_STRIDE, stride=CMP_STRIDE), :]
            oc_ref[:, j * CMP_COLS + b * LANE:j * CMP_COLS + (b + 1) * LANE] = rows.astype(oc_ref.dtype)


def _proj_call(rows_in, w_in_p, layer, col_scale, tabs, seq, tm):
    after_ffn = len(rows_in) > 1
    m, d = rows_in[0].shape
    tpb = seq // tm
    tab_spec = pl.BlockSpec((tm, LANE), lambda i: (i % tpb, 0))
    row = pl.BlockSpec((tm, d), lambda i: (i, 0))
    vec = pl.BlockSpec((1, d), lambda i: (0, 0))
    out_specs = [
        pl.BlockSpec((tm, NP), lambda i: (i, 0)),
        pl.BlockSpec((tm // CMP_STRIDE, CMP_STRIDE * CMP_COLS), lambda i: (i, 0)),
    ]
    out_shape = [
        jax.ShapeDtypeStruct((m, NP), BF16),
        jax.ShapeDtypeStruct((m // CMP_STRIDE, CMP_STRIDE * CMP_COLS), BF16),
    ]
    if after_ffn:
        out_specs = [row] + out_specs
        out_shape = [jax.ShapeDtypeStruct((m, d), F32)] + out_shape
    return pl.pallas_call(
        functools.partial(_proj_kernel, after_ffn=after_ffn),
        grid=(m // tm,),
        in_specs=([row, row, vec, vec] if after_ffn else [row]) + [
            _resident((None, d, NW), lambda i: (layer, 0, 0)),
            _resident((1, NP), lambda i: (0, 0)),
            tab_spec, tab_spec, tab_spec, tab_spec,
        ],
        out_specs=out_specs,
        out_shape=out_shape,
        scratch_shapes=[pltpu.VMEM((CMP_BLOCKS, tm, LANE), F32)],
        compiler_params=_params(("parallel",), 54),
        name="in_proj",
    )(*rows_in, w_in_p, col_scale, *tabs)


def _transpose_into(dst_ref, src_ref):
    rows, cols = src_ref.shape
    for r0 in range(0, rows, LANE):
        for c0 in range(0, cols, LANE):
            tile = src_ref[r0:r0 + LANE, c0:c0 + LANE].astype(F32)
            dst_ref[c0:c0 + LANE, r0:r0 + LANE] = tile.T.astype(dst_ref.dtype)


def _transpose_tiles(dst_ref, src_ref):
    rows, _ = src_ref.shape
    for r in range(rows // LANE):
        tile = src_ref[r * LANE:(r + 1) * LANE, :].astype(F32)
        dst_ref[r] = tile.T.astype(dst_ref.dtype)


def _for_causal_prefix(i, tq, s_len, step, body):
    n = ((i + 1) * tq + step - 1) // step
    for v in range(1, s_len // step + 1):
        pl.when(n == v)(functools.partial(body, v * step))


def _fold(x, op):
    out = x[0:COUNT_ROWS]
    for r0 in range(COUNT_ROWS, x.shape[0], COUNT_ROWS):
        out = op(out, x[r0:r0 + COUNT_ROWS])
    return out


def _fold_tree(x, op):
    while x.shape[0] > SUBLANE:
        half = x.shape[0] // 2
        x = op(x[:half], x[half:])
    return x


def _row_chunk(rows):
    return SCORE_ROWS if rows % SCORE_ROWS == 0 else LANE


def _attend_heads(qs, k_of, pv_of, rows, mask_of, s_ref, p_ref, w_ref=None):
    n_heads = len(qs)
    tq = qs[0].shape[0]
    chunk = ATT_ROWS * LANE // tq
    for h in range(n_heads):
        s_ref[h, 0:rows, :] = lax.dot_general(k_of(h, 0, rows), qs[h], (((1,), (1,)), ((), ())),
                                              preferred_element_type=F32)
    m_acc = [jnp.full((SUBLANE, tq), NEG, F32) for _ in qs]
    weighted = False
    for r0 in range(0, rows, chunk):
        mask, weight = mask_of(r0, chunk)
        if weight is not None:
            weighted = True
            w_ref[r0:r0 + chunk, :] = weight
        for h in range(n_heads):
            s = jnp.where(mask, s_ref[h, r0:r0 + chunk, :], NEG)
            s_ref[h, r0:r0 + chunk, :] = s
            m_acc[h] = jnp.maximum(m_acc[h], _fold_tree(s, jnp.maximum))
    outs = []
    for h in range(n_heads):
        mx = jnp.max(m_acc[h], axis=0, keepdims=True)
        l_acc = jnp.zeros((SUBLANE, tq), F32)
        for r0 in range(0, rows, chunk):
            p = jnp.exp2(s_ref[h, r0:r0 + chunk, :] - mx)
            if weighted:
                p = p * w_ref[r0:r0 + chunk, :]
            l_acc = l_acc + _fold_tree(p, jnp.add)
            p_ref[h, r0:r0 + chunk, :] = p.astype(p_ref.dtype)
        den = jnp.sum(l_acc, axis=0, keepdims=True)
        outs.append(pv_of(h, p_ref[h, 0:rows, :]) / den)
    return outs


def _count_rows(key_ref, rows, pred):
    chunk = _row_chunk(rows)
    acc = None
    for r0 in range(0, rows, chunk):
        part = _fold(pred(key_ref[r0:r0 + chunk, :], r0), jnp.add)
        acc = part if acc is None else acc + part
    return jnp.sum(acc, axis=0, keepdims=True)


def _dsa_prefix(rows, i, iq_ref, iw_t, ik_ref, q_ref, k_ref, vt_ref, o_ref, key_ref, cut_ref, s_ref, p_ref, topk):
    tq = q_ref.shape[0]
    chunk = min(SCORE_ROWS * LANE // tq, rows)
    lane = lax.broadcasted_iota(jnp.int32, (chunk, LANE), 1)
    crow = lax.broadcasted_iota(jnp.int32, (chunk, tq), 0)
    ctcol = i * tq + lax.broadcasted_iota(jnp.int32, (chunk, tq), 1)

    for r0 in range(0, rows, chunk):
        ik = ik_ref[r0:r0 + chunk, :]
        halves = (jnp.where(lane < IDX_DIM, ik, jnp.zeros_like(ik)),
                  jnp.where(lane >= IDX_DIM, ik, jnp.zeros_like(ik)))
        score = jnp.zeros((chunk, tq), F32)
        for p in range(IDX_HEADS // 2):
            blk = iq_ref[:, p * LANE:(p + 1) * LANE]
            for half in range(2):
                j = 2 * p + half
                lg = lax.dot_general(halves[half], blk, (((1,), (1,)), ((), ())), preferred_element_type=F32)
                score = score + jnp.maximum(lg, 0.0) * iw_t[j:j + 1, :]
        bits = pltpu.bitcast(score, jnp.int32)
        key = bits ^ ((bits >> 31) & 0x7FFFFFFF)
        key_ref[r0:r0 + chunk, :] = jnp.where(r0 + crow <= ctcol, key, INT_MIN)

    def bit_step(b, thr):
        cand = thr ^ jnp.left_shift(jnp.int32(1), 31 - b)
        cnt = _count_rows(key_ref, rows, lambda kc, r0: jnp.where(kc >= cand, 1, 0))
        return jnp.where(cnt >= topk, cand, thr)

    thr = lax.fori_loop(0, 32, bit_step, jnp.full((1, tq), INT_MIN, jnp.int32))
    short = thr == INT_MIN
    need = topk - _count_rows(key_ref, rows, lambda kc, r0: jnp.where(kc > thr, 1, 0))
    n_eq = _count_rows(key_ref, rows, lambda kc, r0: jnp.where(kc == thr, 1, 0))
    excess = jnp.where(n_eq > need, jnp.where(short, 0, 1), 0)
    cut_ref[...] = jnp.where(short, -1, rows)

    def row_ids(r0, n):
        return r0 + lax.broadcasted_iota(jnp.int32, (n, tq), 0)

    @pl.when(jnp.max(excess) > 0)
    def _():
        n_bits = max(1, (rows - 1).bit_length())

        def idx_step(b, cut):
            cand = cut | jnp.left_shift(jnp.int32(1), n_bits - 1 - b)
            below = _count_rows(key_ref, rows, lambda kc, r0: jnp.where(
                kc == thr, jnp.where(row_ids(r0, kc.shape[0]) < cand, 1, 0), 0))
            return jnp.where(below < need, cand, cut)

        cut = lax.fori_loop(0, n_bits, idx_step, jnp.zeros((1, tq), jnp.int32))
        cut_ref[...] = jnp.where(excess > 0, cut, cut_ref[...])

    cut = cut_ref[...]

    def mask_of(r0, n):
        kc = key_ref[r0:r0 + n, :]
        chosen = jnp.where(kc > thr, 1, jnp.where(kc == thr, jnp.where(row_ids(r0, n) <= cut, 1, 0), 0))
        return chosen > 0, None

    vt = vt_ref[:, 0:rows]
    cols = [slice(h * HEAD_DIM, (h + 1) * HEAD_DIM) for h in range(A_HEADS)]
    outs = _attend_heads([q_ref[:, c] for c in cols], lambda h, r0, n: k_ref[r0:r0 + n, :],
                         lambda h, p: jnp.dot(vt, p, preferred_element_type=F32),
                         rows, mask_of, s_ref, p_ref)
    for h, c in enumerate(cols):
        o_ref[:, c] = outs[h].T.astype(o_ref.dtype)


def _dsa_kernel(iq_ref, iw_ref, ik_ref, q_ref, k_ref, v_ref, o_ref, vt_ref, key_ref, cut_ref, s_ref, p_ref,
                *, topk, step):
    i = pl.program_id(1)
    tq = q_ref.shape[0]
    s_len = k_ref.shape[0]

    @pl.when(i == 0)
    def _():
        _transpose_into(vt_ref, v_ref)

    iw_t = iw_ref[...].astype(F32).T[:IDX_HEADS, :]
    _for_causal_prefix(i, tq, s_len, step, functools.partial(
        _dsa_prefix, i=i, iq_ref=iq_ref, iw_t=iw_t, ik_ref=ik_ref, q_ref=q_ref, k_ref=k_ref, vt_ref=vt_ref,
        o_ref=o_ref, key_ref=key_ref, cut_ref=cut_ref, s_ref=s_ref, p_ref=p_ref, topk=topk))


def _dsa_call(p3, tq, step):
    b, s, _ = p3.shape
    topk = min(DSA_TOPK_MAX, s // 4)
    blk = lambda name, width: _OFF[name] // width
    return pl.pallas_call(
        functools.partial(_dsa_kernel, topk=topk, step=step),
        grid=(b, s // tq),
        in_specs=[
            pl.BlockSpec((None, tq, 8 * LANE), lambda bi, i: (bi, i, blk("i_q", 8))),
            pl.BlockSpec((None, tq, LANE), lambda bi, i: (bi, i, blk("i_w", 1))),
            pl.BlockSpec((None, s, LANE), lambda bi, i: (bi, 0, blk("i_k", 1))),
            pl.BlockSpec((None, tq, 4 * LANE), lambda bi, i: (bi, i, blk("a_q", 4))),
            pl.BlockSpec((None, s, LANE), lambda bi, i: (bi, 0, blk("a_k", 1))),
            pl.BlockSpec((None, s, LANE), lambda bi, i: (bi, 0, blk("a_v", 1))),
        ],
        out_specs=pl.BlockSpec((None, tq, A_HEADS * HEAD_DIM), lambda bi, i: (bi, i, 0)),
        out_shape=jax.ShapeDtypeStruct((b, s, A_HEADS * HEAD_DIM), BF16),
        scratch_shapes=[
            pltpu.VMEM((HEAD_DIM, s), BF16),
            pltpu.VMEM((s, tq), jnp.int32),
            pltpu.VMEM((1, tq), jnp.int32),
            pltpu.VMEM((A_HEADS, s, tq), F32),
            pltpu.VMEM((A_HEADS, s, tq), BF16),
        ],
        compiler_params=_params(("parallel", "arbitrary"), 48),
        name="dsa",
    )(p3, p3, p3, p3, p3, p3)


def _dil_prefix(rows, i, q_ref, k_ref, vt_ref, o_ref, s_ref, p_ref, w_ref):
    tq = q_ref.shape[0]

    def mask_of(r0, n):
        srow = r0 + lax.broadcasted_iota(jnp.int32, (n, tq), 0)
        tcol = i * tq + lax.broadcasted_iota(jnp.int32, (n, tq), 1)
        delta = tcol - srow
        mult = jnp.zeros((n, tq), jnp.int32)
        for window, dilation in DILATED_PATTERNS:
            on_stride = (delta & (dilation - 1)) == 0
            mult = mult + jnp.where(on_stride, jnp.where(delta <= window, 1, 0), 0)
        mult = jnp.where(delta >= 0, mult, 0)
        return mult > 0, mult.astype(F32)

    cols = [slice(h * HEAD_DIM, (h + 1) * HEAD_DIM) for h in range(B_HEADS)]
    outs = _attend_heads([q_ref[:, c] for c in cols], lambda h, r0, n: k_ref[r0:r0 + n, cols[h]],
                         lambda h, p: jnp.dot(vt_ref[cols[h], 0:rows], p, preferred_element_type=F32),
                         rows, mask_of, s_ref, p_ref, w_ref)
    for h, c in enumerate(cols):
        o_ref[:, c] = outs[h].T.astype(o_ref.dtype)


def _dil_kernel(q_ref, k_ref, v_ref, o_ref, vt_ref, s_ref, p_ref, w_ref, *, step):
    i = pl.program_id(1)

    @pl.when(i == 0)
    def _():
        _transpose_into(vt_ref, v_ref)

    _for_causal_prefix(i, q_ref.shape[0], k_ref.shape[0], step, functools.partial(
        _dil_prefix, i=i, q_ref=q_ref, k_ref=k_ref, vt_ref=vt_ref, o_ref=o_ref,
        s_ref=s_ref, p_ref=p_ref, w_ref=w_ref))


def _dil_call(p3, tq, step):
    b, s, _ = p3.shape
    width = B_HEADS * HEAD_DIM
    return pl.pallas_call(
        functools.partial(_dil_kernel, step=step),
        grid=(b, s // tq),
        in_specs=[
            pl.BlockSpec((None, tq, width), lambda bi, i: (bi, i, _OFF["b_q"] // 4)),
            pl.BlockSpec((None, s, width), lambda bi, i: (bi, 0, _OFF["b_k"] // 4)),
            pl.BlockSpec((None, s, width), lambda bi, i: (bi, 0, _OFF["b_v"] // 4)),
        ],
        out_specs=pl.BlockSpec((None, tq, width), lambda bi, i: (bi, i, 0)),
        out_shape=jax.ShapeDtypeStruct((b, s, width), BF16),
        scratch_shapes=[
            pltpu.VMEM((width, s), BF16),
            pltpu.VMEM((B_HEADS, s, tq), F32),
            pltpu.VMEM((B_HEADS, s, tq), BF16),
            pltpu.VMEM((s, tq), F32),
        ],
        compiler_params=_params(("parallel", "arbitrary"), 48),
        name="dilated",
    )(p3, p3, p3)


def _gelu_tanh(x):
    return 0.5 * x * (1.0 + jnp.tanh(np.float32(np.sqrt(2.0 / np.pi)) * (x + 0.044715 * (x * x * x))))


def _cmp_kernel(*refs):
    n = CMP_STRIDE
    xk, xv = refs[:n], refs[n:2 * n]
    pe_k, w1_k, w2_k, pe_v, w1_v, w2_v, kc_ref, vct_ref = refs[2 * n:]

    def branch(x_refs, pe_ref, w1_ref, w2_ref):
        lo = hi = None
        for j in range(n):
            xj = x_refs[j][...].astype(F32)
            a = (xj + pe_ref[j:j + 1, :]).astype(BF16)
            b = (xj + pe_ref[n + j:n + j + 1, :]).astype(BF16)
            dl = jnp.dot(a, w1_ref[j * HEAD_DIM:(j + 1) * HEAD_DIM, :], preferred_element_type=F32)
            dh = jnp.dot(b, w1_ref[(n + j) * HEAD_DIM:(n + j + 1) * HEAD_DIM, :], preferred_element_type=F32)
            lo = dl if lo is None else lo + dl
            hi = dh if hi is None else hi + dh
        nc = lo.shape[0]
        hid = lo + pltpu.roll(hi, nc - 1, 0)
        return jnp.dot(_gelu_tanh(hid).astype(BF16), w2_ref[...], preferred_element_type=F32)

    kc_ref[...] = branch(xk, pe_k, w1_k, w2_k).astype(kc_ref.dtype)
    vct_ref[...] = branch(xv, pe_v, w1_v, w2_v).T.astype(vct_ref.dtype)


def _cmp_call(pc, b, layer, pe_k, w1_k, w2_k, pe_v, w1_v, w2_v):
    nc = pc.shape[0] // b
    pc3 = pc.reshape(b, nc, CMP_STRIDE * CMP_COLS)
    x_specs = []
    for first in (0, C_KV_GROUPS):
        for j in range(CMP_STRIDE):
            x_specs.append(pl.BlockSpec((None, nc, HEAD_DIM),
                                        lambda bi, g, j=j, first=first: (bi, 0, j * CMP_BLOCKS + first + g)))
    flat = CMP_BLOCK * HEAD_DIM
    w_specs = [
        _resident((None, CMP_BLOCK, HEAD_DIM), lambda bi, g: (layer, 0, 0)),
        _resident((None, flat, CMP_HIDDEN), lambda bi, g: (layer, 0, 0)),
        _resident((None, CMP_HIDDEN, HEAD_DIM), lambda bi, g: (layer, 0, 0)),
    ]
    return pl.pallas_call(
        _cmp_kernel,
        grid=(b, C_KV_GROUPS),
        in_specs=x_specs + w_specs + w_specs,
        out_specs=[
            pl.BlockSpec((None, None, nc, HEAD_DIM), lambda bi, g: (bi, g, 0, 0)),
            pl.BlockSpec((None, None, HEAD_DIM, nc), lambda bi, g: (bi, g, 0, 0)),
        ],
        out_shape=[
            jax.ShapeDtypeStruct((b, C_KV_GROUPS, nc, HEAD_DIM), BF16),
            jax.ShapeDtypeStruct((b, C_KV_GROUPS, HEAD_DIM, nc), BF16),
        ],
        compiler_params=_params(("parallel", "parallel"), 32),
        name="nsa_compress",
    )(*([pc3] * (2 * CMP_STRIDE)), pe_k, w1_k, w2_k, pe_v, w1_v, w2_v)


def _nsa_selected(rows, t_row, sel, q_rot, ks_ref, vst_ref, expand_ref, acc_ref, s_ref, p_ref):
    tq = sel.shape[1]

    def mask_of(r0, n):
        in_sel = jnp.dot(expand_ref[r0:r0 + n, :], sel, preferred_element_type=F32)
        srow = r0 + lax.broadcasted_iota(jnp.int32, (n, tq), 0)
        return jnp.where(t_row - srow >= 0, in_sel, 0.0) > 0.5, None

    vt = vst_ref[:, 0:rows]
    outs = _attend_heads(q_rot, lambda h, r0, n: ks_ref[r0:r0 + n, :],
                         lambda h, p: jnp.dot(vt, p, preferred_element_type=F32),
                         rows, mask_of, s_ref, p_ref)
    for r in range(C_GROUP_SIZE):
        acc_ref[r] = outs[r]


def _nsa_kernel(q_ref, kc_ref, vct_ref, ks_ref, vs_ref, kw_ref, vw_ref, g_ref, cos_ref, sin_ref,
                isect_ref, expand_ref, o_ref, vst_ref, vwt_ref, acc_ref, s_ref, p_ref, *, n_sel, step):
    i = pl.program_id(2)
    tq = q_ref.shape[0]
    s_len = ks_ref.shape[0]
    nc = kc_ref.shape[0]
    n_slc = isect_ref.shape[0]

    @pl.when(i == 0)
    def _():
        _transpose_into(vst_ref, vs_ref)
        _transpose_tiles(vwt_ref, vw_ref)

    t_row = i * tq + lax.broadcasted_iota(jnp.int32, (1, tq), 1)

    kc = kc_ref[...]
    vct = vct_ref[...]
    cend = lax.broadcasted_iota(jnp.int32, (nc, tq), 0) * CMP_STRIDE + (CMP_BLOCK - 1)
    cmask = cend <= t_row
    o_cmp = []
    p_sum = jnp.zeros((nc, tq), F32)
    for r in range(C_GROUP_SIZE):
        q_r = q_ref[:, r * HEAD_DIM:(r + 1) * HEAD_DIM]
        sc = lax.dot_general(kc, q_r, (((1,), (1,)), ((), ())), preferred_element_type=F32)
        sc = jnp.where(cmask, sc, NEG)
        mx = jnp.max(sc, axis=0, keepdims=True)
        e = jnp.where(cmask, jnp.exp2(sc - mx), 0.0)
        den = jnp.sum(e, axis=0, keepdims=True)
        p = e / jnp.where(den > 0, den, 1.0)
        p_sum = p_sum + p
        o_cmp.append(jnp.dot(vct, p.astype(BF16), preferred_element_type=F32))

    isect = isect_ref[...]
    p_hi = p_sum.astype(BF16)
    p_lo = (p_sum - p_hi.astype(F32)).astype(BF16)
    imp = (jnp.dot(isect, p_hi, preferred_element_type=F32)
           + jnp.dot(isect, p_lo, preferred_element_type=F32))
    blk = lax.broadcasted_iota(jnp.int32, (n_slc, tq), 0)
    cur = t_row // SEL_BLOCK
    val = jnp.where(blk == 0, jnp.inf, jnp.where(blk == cur, jnp.inf, jnp.where(blk == cur - 1, jnp.inf, imp)))
    val = jnp.where(blk <= cur, val, -jnp.inf)
    rank = jnp.zeros((n_slc, tq), jnp.int32)
    for m in range(n_slc):
        vm = val[m:m + 1, :]
        before = jnp.where(vm > val, 1, jnp.where(vm == val, jnp.where(blk > m, 1, 0), 0))
        rank = rank + before
    sel = jnp.where(rank < n_sel, 1.0, 0.0)
    if n_slc < LANE:
        sel = jnp.concatenate([sel, jnp.zeros((LANE - n_slc, tq), F32)], axis=0)
    sel = sel.astype(BF16)

    cos, sin = cos_ref[...], sin_ref[...]
    cols = [slice(r * HEAD_DIM, (r + 1) * HEAD_DIM) for r in range(C_GROUP_SIZE)]
    q_rot = [_rope_full(q_ref[:, c].astype(F32), cos, sin).astype(BF16) for c in cols]

    _for_causal_prefix(i, tq, s_len, step, functools.partial(
        _nsa_selected, t_row=t_row, sel=sel, q_rot=q_rot, ks_ref=ks_ref, vst_ref=vst_ref,
        expand_ref=expand_ref, acc_ref=acc_ref, s_ref=s_ref, p_ref=p_ref))

    w_tiles = min((WIN_SIZE - 1 + tq - 1) // LANE + 1, s_len // LANE)
    w_rows = w_tiles * LANE
    w0 = pl.multiple_of(jnp.maximum(i * tq + tq - w_rows, 0), LANE)
    t0 = w0 // LANE

    def win_mask(r0, n):
        delta = t_row - (w0 + r0 + lax.broadcasted_iota(jnp.int32, (n, tq), 0))
        return jnp.where(delta >= 0, jnp.where(delta <= WIN_SIZE - 1, 1, 0), 0) > 0, None

    def pv_win(h, p):
        out = None
        for w in range(w_tiles):
            part = jnp.dot(vwt_ref[t0 + w], p[w * LANE:(w + 1) * LANE, :], preferred_element_type=F32)
            out = part if out is None else out + part
        return out

    o_win = _attend_heads(q_rot, lambda h, r0, n: kw_ref[pl.ds(w0 + r0, n), :], pv_win,
                          w_rows, win_mask, s_ref, p_ref)

    gates = jax.nn.sigmoid(g_ref[...].astype(F32).T)
    for r, c in enumerate(cols):
        o_t = (gates[3 * r:3 * r + 1, :] * o_cmp[r] + gates[3 * r + 1:3 * r + 2, :] * acc_ref[r]
               + gates[3 * r + 2:3 * r + 3, :] * o_win[r])
        o_ref[:, c] = o_t.T.astype(o_ref.dtype)


def _nsa_call(p3, kc, vct, cos, sin, isect, expand, tq, step):
    b, s, _ = p3.shape
    nc = kc.shape[2]
    n_slc = s // SEL_BLOCK
    width = C_GROUP_SIZE * HEAD_DIM
    kv = lambda name: pl.BlockSpec((None, s, HEAD_DIM), lambda bi, g, i, name=name: (bi, 0, _OFF[name] + g))
    return pl.pallas_call(
        functools.partial(_nsa_kernel, n_sel=min(SEL_COUNT, n_slc), step=step),
        grid=(b, C_KV_GROUPS, s // tq),
        in_specs=[
            pl.BlockSpec((None, tq, width), lambda bi, g, i: (bi, i, _OFF["c_q"] // 4 + g)),
            pl.BlockSpec((None, None, nc, HEAD_DIM), lambda bi, g, i: (bi, g, 0, 0)),
            pl.BlockSpec((None, None, HEAD_DIM, nc), lambda bi, g, i: (bi, g, 0, 0)),
            kv("c_ks"), kv("c_vs"), kv("c_kw"), kv("c_vw"),
            pl.BlockSpec((None, tq, LANE), lambda bi, g, i: (bi, i, _OFF["c_g0"] + g)),
            pl.BlockSpec((tq, LANE), lambda bi, g, i: (i, 0)),
            pl.BlockSpec((tq, LANE), lambda bi, g, i: (i, 0)),
            pl.BlockSpec(isect.shape, lambda bi, g, i: (0, 0)),
            pl.BlockSpec(expand.shape, lambda bi, g, i: (0, 0)),
        ],
        out_specs=pl.BlockSpec((None, tq, width), lambda bi, g, i: (bi, i, g)),
        out_shape=jax.ShapeDtypeStruct((b, s, C_HEADS * HEAD_DIM), BF16),
        scratch_shapes=[
            pltpu.VMEM((HEAD_DIM, s), BF16),
            pltpu.VMEM((s // LANE, HEAD_DIM, LANE), BF16),
            pltpu.VMEM((C_GROUP_SIZE, HEAD_DIM, tq), F32),
            pltpu.VMEM((C_GROUP_SIZE, s, tq), F32),
            pltpu.VMEM((C_GROUP_SIZE, s, tq), BF16),
        ],
        compiler_params=_params(("parallel", "parallel", "arbitrary"), 48),
        name="nsa",
    )(p3, kc, vct, p3, p3, p3, p3, p3, cos, sin, isect, expand)


def _out_kernel(oa_ref, ob_ref, oc_ref, w_ref, x_ref, gpost_ref, gnext_ref, xo_ref, h_ref):
    na, nb = oa_ref.shape[1], ob_ref.shape[1]
    y = jnp.dot(oa_ref[...], w_ref[0:na, :], preferred_element_type=F32)
    y = y + jnp.dot(ob_ref[...], w_ref[na:na + nb, :], preferred_element_type=F32)
    y = y + jnp.dot(oc_ref[...], w_ref[na + nb:, :], preferred_element_type=F32)
    x = x_ref[...] + _rms(y, gpost_ref[...])
    xo_ref[...] = x
    h_ref[...] = _rms(x, gnext_ref[...]).astype(h_ref.dtype)


def _out_call(o_a, o_b, o_c, w_out_p, layer, x, g_post, g_next, tm):
    m, d = x.shape
    row = lambda width: pl.BlockSpec((tm, width), lambda i: (i, 0))
    vec = pl.BlockSpec((1, d), lambda i: (0, 0))
    return pl.pallas_call(
        _out_kernel,
        grid=(m // tm,),
        in_specs=[
            row(o_a.shape[1]), row(o_b.shape[1]), row(o_c.shape[1]),
            _resident((None, w_out_p.shape[1], d), lambda i: (layer, 0, 0)),
            row(d), vec, vec,
        ],
        out_specs=[row(d), row(d)],
        out_shape=[jax.ShapeDtypeStruct((m, d), F32), jax.ShapeDtypeStruct((m, d), BF16)],
        compiler_params=_params(("parallel",), 48),
        name="out_proj",
    )(o_a, o_b, o_c, w_out_p, x, g_post, g_next)


def _ffn_kernel(h_ref, wg_ref, wu_ref, wd_ref, y_ref):
    @pl.when(pl.program_id(1) == 0)
    def _():
        y_ref[...] = jnp.zeros(y_ref.shape, F32)

    h = h_ref[...]
    g = jnp.dot(h, wg_ref[...], preferred_element_type=F32)
    u = jnp.dot(h, wu_ref[...], preferred_element_type=F32)
    a = (g * jax.nn.sigmoid(g) * u).astype(BF16)
    y_ref[...] += jnp.dot(a, wd_ref[...], preferred_element_type=F32)


def _ffn_call(h, wg, wu, wd, layer, tm, tf):
    m, d = h.shape
    d_ff = wg.shape[2]
    row = pl.BlockSpec((tm, d), lambda i, f: (i, 0))
    return pl.pallas_call(
        _ffn_kernel,
        grid=(m // tm, d_ff // tf),
        in_specs=[
            row,
            pl.BlockSpec((None, d, tf), lambda i, f: (layer, 0, f)),
            pl.BlockSpec((None, d, tf), lambda i, f: (layer, 0, f)),
            pl.BlockSpec((None, tf, d), lambda i, f: (layer, f, 0)),
        ],
        out_specs=row,
        out_shape=jax.ShapeDtypeStruct((m, d), F32),
        compiler_params=_params(("parallel", "arbitrary"), 52),
        name="ffn",
    )(h, wg, wu, wd)


def _residual_kernel(y_ref, x_ref, g_ref, o_ref):
    o_ref[...] = x_ref[...] + _rms(y_ref[...], g_ref[...])


def _residual_call(y, x, g, tm):
    m, d = x.shape
    row = pl.BlockSpec((tm, d), lambda i: (i, 0))
    return pl.pallas_call(
        _residual_kernel,
        grid=(m // tm,),
        in_specs=[row, row, pl.BlockSpec((1, d), lambda i: (0, 0))],
        out_specs=row,
        out_shape=jax.ShapeDtypeStruct((m, d), F32),
        compiler_params=_params(("parallel",), 32),
        name="ffn_residual",
    )(y, x, g)


def _pad_cols(w, width):
    return jnp.pad(w, ((0, 0), (0, 0), (0, width - w.shape[-1])))


def _prep_w_in(w_in):
    widths = (A_HEADS * HEAD_DIM, HEAD_DIM, HEAD_DIM, IDX_HEADS * IDX_DIM, IDX_DIM, IDX_HEADS,
              B_HEADS * HEAD_DIM, B_HEADS * HEAD_DIM, B_HEADS * HEAD_DIM, C_HEADS * HEAD_DIM,
              C_KV_GROUPS * HEAD_DIM, C_KV_GROUPS * HEAD_DIM, C_KV_GROUPS * HEAD_DIM, C_KV_GROUPS * HEAD_DIM,
              C_KV_GROUPS * HEAD_DIM, C_KV_GROUPS * HEAD_DIM, 3 * C_HEADS)
    names = ("a_q", "a_k", "a_v", "i_q", "i_k", "i_w", "b_q", "b_k", "b_v", "c_q",
             "c_kc", "c_vc", "c_ks", "c_vs", "c_kw", "c_vw", "c_g")
    offs = np.concatenate([[0], np.cumsum(widths)])
    src = {n: w_in[:, :, int(offs[k]):int(offs[k + 1])] for k, n in enumerate(names)}
    per_group = 3 * C_GROUP_SIZE
    src["c_g0"] = _pad_cols(src["c_g"][:, :, :per_group], LANE)
    src["c_g1"] = _pad_cols(src["c_g"][:, :, per_group:], LANE)
    src["i_k"] = jnp.concatenate([src["i_k"], src["i_k"]], axis=-1)
    src["i_w"] = _pad_cols(src["i_w"], LANE)
    order = [n for n, _, _, _ in _SEGMENTS] + [n for n, _ in _CMP_SEGMENTS]
    return jnp.concatenate([src[n] for n in order], axis=-1).astype(BF16)


def _rope_tables(seq):
    def tables(dim):
        inv = 1.0 / (ROPE_THETA ** (jnp.arange(0, dim, 2, dtype=F32) / dim))
        ang = jnp.arange(seq, dtype=F32)[:, None] * inv[None, :]
        cos, sin = jnp.cos(ang), jnp.sin(ang)
        reps = LANE // dim
        return (jnp.tile(jnp.concatenate([cos, cos], axis=-1), (1, reps)),
                jnp.tile(jnp.concatenate([-sin, sin], axis=-1), (1, reps)))
    return tables(HEAD_DIM) + tables(IDX_DIM)


def _selection_constants(seq):
    nc = seq // CMP_STRIDE
    n_slc = seq // SEL_BLOCK
    c_start = np.arange(nc) * CMP_STRIDE
    n_start = np.arange(n_slc) * SEL_BLOCK
    isect = ((c_start[None, :] < n_start[:, None] + SEL_BLOCK)
             & (c_start[None, :] + CMP_BLOCK > n_start[:, None])
             & (np.arange(nc)[None, :] < nc - CMP_BLOCK // CMP_STRIDE + 1)).astype(np.float32)
    expand = np.zeros((seq, max(n_slc, LANE)), np.float32)
    expand[np.arange(seq), np.arange(seq) // SEL_BLOCK] = 1.0
    return jnp.asarray(isect, BF16), jnp.asarray(expand, BF16)


def _tiles(seq):
    return dict(tq=min(2 * LANE, seq), step=min(2 * LANE, seq), tm_proj=min(256, seq), tm_out=min(512, seq),
                tm_ffn=min(1024, seq), tf=512)


def kernel(x, w_in, w_out, cmp_pe_k, cmp_w1_k, cmp_w2_k, cmp_pe_v, cmp_w1_v, cmp_w2_v,
           w_gate, w_up, w_down, g_pre_mix, g_post_mix, g_pre_ffn, g_post_ffn):
    b, s, d = x.shape
    depth = w_in.shape[0]
    m = b * s
    t = _tiles(s)
    tq, step = t["tq"], t["step"]

    w_in_p = _prep_w_in(w_in)
    w_out_p = w_out.astype(BF16)
    wg, wu, wd = w_gate.astype(BF16), w_up.astype(BF16), w_down.astype(BF16)
    w1_k, w2_k = cmp_w1_k.astype(BF16), cmp_w2_k.astype(BF16)
    w1_v, w2_v = cmp_w1_v.astype(BF16), cmp_w2_v.astype(BF16)
    col_scale = jnp.asarray(_COL_SCALE)
    tabs = _rope_tables(s)
    isect, expand = _selection_constants(s)
    vec = lambda g, layer: g[layer][None, :]

    xf = x.reshape(m, d)
    y = None
    for layer in range(depth):
        if layer == 0:
            rows_in = (_rms_call(xf, vec(g_pre_mix, 0), t["tm_out"]),)
            p, pc = _proj_call(rows_in, w_in_p, layer, col_scale, tabs, s, t["tm_proj"])
        else:
            rows_in = (y, xf, vec(g_post_ffn, layer - 1), vec(g_pre_mix, layer))
            xf, p, pc = _proj_call(rows_in, w_in_p, layer, col_scale, tabs, s, t["tm_proj"])
        p3 = p.reshape(b, s, NP)
        o_a = _dsa_call(p3, tq, step)
        o_b = _dil_call(p3, tq, step)
        kc, vct = _cmp_call(pc, b, layer, cmp_pe_k, w1_k, w2_k, cmp_pe_v, w1_v, w2_v)
        o_c = _nsa_call(p3, kc, vct, tabs[0], tabs[1], isect, expand, tq, step)
        xf, h = _out_call(o_a.reshape(m, -1), o_b.reshape(m, -1), o_c.reshape(m, -1), w_out_p, layer,
                          xf, vec(g_post_mix, layer), vec(g_pre_ffn, layer), t["tm_out"])
        y = _ffn_call(h, wg, wu, wd, layer, t["tm_ffn"], t["tf"])
    return _residual_call(y, xf, vec(g_post_ffn, depth - 1), t["tm_out"]).reshape(b, s, d)
```

```python
import functools

import numpy as np
import jax
import jax.numpy as jnp
from jax import lax
from jax.experimental import pallas as pl
from jax.experimental.pallas import tpu as pltpu

F32 = jnp.float32
BF16 = jnp.bfloat16

LANE = 128
SUBLANE = 8
HEAD_DIM = 128
A_HEADS = 4
B_HEADS = 4
C_HEADS = 8
C_KV_GROUPS = 2
C_GROUP_SIZE = C_HEADS // C_KV_GROUPS
IDX_HEADS = 16
IDX_DIM = 64
DSA_TOPK_MAX = 256
DILATED_PATTERNS = ((128, 1), (512, 4), (2048, 16))
CMP_BLOCK = 32
CMP_STRIDE = 16
CMP_HIDDEN = 256
SEL_BLOCK = 64
SEL_COUNT = 16
WIN_SIZE = 512
ROPE_THETA = 10000.0
RMS_EPS = 1e-6
ATTN_SCALE = HEAD_DIM ** -0.5 * float(np.log2(np.e))
IDX_SCALE = IDX_DIM ** -0.5 * IDX_HEADS ** -0.5

NEG = -1e30
INT_MIN = -(2 ** 31)
COUNT_ROWS = 8 * SUBLANE
SCORE_ROWS = 256
ATT_ROWS = 128

MODE_NONE, MODE_ROPE, MODE_IROPE = 0, 1, 2

_SEGMENTS = (
    ("i_q", 8, MODE_IROPE, IDX_SCALE),
    ("c_q", 8, MODE_NONE, ATTN_SCALE),
    ("a_q", 4, MODE_ROPE, ATTN_SCALE),
    ("b_q", 4, MODE_ROPE, ATTN_SCALE),
    ("b_k", 4, MODE_ROPE, 1.0),
    ("b_v", 4, MODE_NONE, 1.0),
    ("c_ks", 2, MODE_ROPE, 1.0),
    ("c_vs", 2, MODE_NONE, 1.0),
    ("c_kw", 2, MODE_ROPE, 1.0),
    ("c_vw", 2, MODE_NONE, 1.0),
    ("a_k", 1, MODE_ROPE, 1.0),
    ("a_v", 1, MODE_NONE, 1.0),
    ("i_k", 1, MODE_IROPE, 1.0),
    ("i_w", 1, MODE_NONE, 1.0),
    ("c_g0", 1, MODE_NONE, 1.0),
    ("c_g1", 1, MODE_NONE, 1.0),
)
_CMP_SEGMENTS = (("c_kc", 2), ("c_vc", 2))
_OFF = {}
_o = 0
for _n, _w, _m, _s in _SEGMENTS:
    _OFF[_n] = _o
    _o += _w
NP_BLOCKS = _o
NP = NP_BLOCKS * LANE
CMP_BLOCKS = sum(w for _, w in _CMP_SEGMENTS)
CMP_COLS = CMP_BLOCKS * LANE
NW = NP + CMP_COLS
_BLOCK_MODE = tuple(m for _, w, m, _ in _SEGMENTS for _ in range(w))
_COL_SCALE = np.repeat(np.array([s for _, w, _, s in _SEGMENTS for _ in range(w)], np.float32), LANE)[None, :]
_PROJ_CHUNK = 4


def _params(sem, vmem_mb):
    return pltpu.CompilerParams(dimension_semantics=sem, vmem_limit_bytes=vmem_mb * 1024 * 1024)


def _resident(block_shape, index_map):
    return pl.BlockSpec(block_shape, index_map, pipeline_mode=pl.Buffered(1))


def _rms(x, g):
    return x * lax.rsqrt(jnp.mean(x * x, axis=-1, keepdims=True) + RMS_EPS) * g


def _rms_kernel(x_ref, g_ref, h_ref):
    h_ref[...] = _rms(x_ref[...], g_ref[...]).astype(h_ref.dtype)


def _rms_call(x, g, tm):
    m, d = x.shape
    return pl.pallas_call(
        _rms_kernel,
        grid=(m // tm,),
        in_specs=[pl.BlockSpec((tm, d), lambda i: (i, 0)), pl.BlockSpec((1, d), lambda i: (0, 0))],
        out_specs=pl.BlockSpec((tm, d), lambda i: (i, 0)),
        out_shape=jax.ShapeDtypeStruct((m, d), BF16),
        compiler_params=_params(("parallel",), 32),
        name="rms_pre",
    )(x, g)


def _rope_full(a, cos, sin):
    return a * cos + pltpu.roll(a, HEAD_DIM // 2, 1) * sin


def _rope_idx(a, cos, sin, first_half):
    partner = jnp.where(first_half, pltpu.roll(a, LANE - IDX_DIM // 2, 1), pltpu.roll(a, IDX_DIM // 2, 1))
    return a * cos + partner * sin


def _proj_kernel(*refs, after_ffn):
    if after_ffn:
        (y_ref, x_ref, gpost_ref, gpre_ref, w_ref, cs_ref, cos_ref, sin_ref, icos_ref, isin_ref,
         xo_ref, o_ref, oc_ref, cmp_ref) = refs
        x = x_ref[...] + _rms(y_ref[...], gpost_ref[...])
        xo_ref[...] = x
        h = _rms(x, gpre_ref[...]).astype(BF16)
    else:
        x_ref, gpre_ref, w_ref, cs_ref, cos_ref, sin_ref, icos_ref, isin_ref, o_ref, oc_ref, cmp_ref = refs
        h = _rms(x_ref[...], gpre_ref[...]).astype(BF16)
    tm = h.shape[0]
    lane = lax.broadcasted_iota(jnp.int32, (tm, LANE), 1)
    first_half = (lane & (IDX_DIM - 1)) < IDX_DIM // 2
    for c0 in range(0, NP_BLOCKS, _PROJ_CHUNK):
        nb = min(_PROJ_CHUNK, NP_BLOCKS - c0)
        acc = jnp.dot(h, w_ref[:, c0 * LANE:(c0 + nb) * LANE], preferred_element_type=F32)
        for b in range(nb):
            col = slice((c0 + b) * LANE, (c0 + b + 1) * LANE)
            a = acc[:, b * LANE:(b + 1) * LANE] * cs_ref[:, col]
            mode = _BLOCK_MODE[c0 + b]
            if mode == MODE_ROPE:
                a = _rope_full(a, cos_ref[...], sin_ref[...])
            elif mode == MODE_IROPE:
                a = _rope_idx(a, icos_ref[...], isin_ref[...], first_half)
            o_ref[:, col] = a.astype(o_ref.dtype)
    acc = jnp.dot(h, w_ref[:, NP:NW], preferred_element_type=F32)
    for b in range(CMP_BLOCKS):
        cmp_ref[b] = acc[:, b * LANE:(b + 1) * LANE]
    for j in range(CMP_STRIDE):
        for b in range(CMP_BLOCKS):
            rows = cmp_ref[b, pl.ds(j, tm // CMP_STRIDE, stride=CMP_STRIDE), :]
            oc_ref[:, j * CMP_COLS + b * LANE:j * CMP_COLS + (b + 1) * LANE] = rows.astype(oc_ref.dtype)


def _proj_call(rows_in, w_in_p, layer, col_scale, tabs, seq, tm):
    after_ffn = len(rows_in) > 2
    m, d = rows_in[0].shape
    tpb = seq // tm
    tab_spec = pl.BlockSpec((tm, LANE), lambda i: (i % tpb, 0))
    row = pl.BlockSpec((tm, d), lambda i: (i, 0))
    vec = pl.BlockSpec((1, d), lambda i: (0, 0))
    out_specs = [
        pl.BlockSpec((tm, NP), lambda i: (i, 0)),
        pl.BlockSpec((tm // CMP_STRIDE, CMP_STRIDE * CMP_COLS), lambda i: (i, 0)),
    ]
    out_shape = [
        jax.ShapeDtypeStruct((m, NP), BF16),
        jax.ShapeDtypeStruct((m // CMP_STRIDE, CMP_STRIDE * CMP_COLS), BF16),
    ]
    if after_ffn:
        out_specs = [row] + out_specs
        out_shape = [jax.ShapeDtypeStruct((m, d), F32)] + out_shape
    return pl.pallas_call(
        functools.partial(_proj_kernel, after_ffn=after_ffn),
        grid=(m // tm,),
        in_specs=([row, row, vec, vec] if after_ffn else [row, vec]) + [
            _resident((None, d, NW), lambda i: (layer, 0, 0)),
            _resident((1, NP), lambda i: (0, 0)),
            tab_spec, tab_spec, tab_spec, tab_spec,
        ],
        out_specs=out_specs,
        out_shape=out_shape,
        scratch_shapes=[pltpu.VMEM((CMP_BLOCKS, tm, LANE), F32)],
        compiler_params=_params(("parallel",), 54),
        name="in_proj",
    )(*rows_in, w_in_p, col_scale, *tabs)


def _transpose_into(dst_ref, src_ref):
    rows, cols = src_ref.shape
    for r0 in range(0, rows, LANE):
        for c0 in range(0, cols, LANE):
            tile = src_ref[r0:r0 + LANE, c0:c0 + LANE].astype(F32)
            dst_ref[c0:c0 + LANE, r0:r0 + LANE] = tile.T.astype(dst_ref.dtype)


def _transpose_tiles(dst_ref, src_ref):
    rows, _ = src_ref.shape
    for r in range(rows // LANE):
        tile = src_ref[r * LANE:(r + 1) * LANE, :].astype(F32)
        dst_ref[r] = tile.T.astype(dst_ref.dtype)


def _for_causal_prefix(i, tq, s_len, step, body):
    n = ((i + 1) * tq + step - 1) // step
    for v in range(1, s_len // step + 1):
        pl.when(n == v)(functools.partial(body, v * step))


def _fold(x, op):
    out = x[0:COUNT_ROWS]
    for r0 in range(COUNT_ROWS, x.shape[0], COUNT_ROWS):
        out = op(out, x[r0:r0 + COUNT_ROWS])
    return out


def _fold_tree(x, op):
    while x.shape[0] > SUBLANE:
        half = x.shape[0] // 2
        x = op(x[:half], x[half:])
    return x


def _row_chunk(rows):
    return SCORE_ROWS if rows % SCORE_ROWS == 0 else LANE


def _attend_heads(qs, k_of, pv_of, rows, mask_of, s_ref, p_ref, w_ref=None, plain_rows=0):
    n_heads = len(qs)
    tq = qs[0].shape[0]
    chunk = ATT_ROWS * LANE // tq
    for h in range(n_heads):
        s_ref[h, 0:rows, :] = lax.dot_general(k_of(h, 0, rows), qs[h], (((1,), (1,)), ((), ())),
                                              preferred_element_type=F32)
    m_acc = [jnp.full((SUBLANE, tq), NEG, F32) for _ in qs]
    weighted = False
    for r0 in range(0, rows, chunk):
        if r0 + chunk <= plain_rows:
            for h in range(n_heads):
                m_acc[h] = jnp.maximum(m_acc[h], _fold_tree(s_ref[h, r0:r0 + chunk, :], jnp.maximum))
            continue
        mask, weight = mask_of(r0, chunk)
        if weight is not None:
            weighted = True
            w_ref[r0:r0 + chunk, :] = weight
        for h in range(n_heads):
            s = jnp.where(mask, s_ref[h, r0:r0 + chunk, :], NEG)
            s_ref[h, r0:r0 + chunk, :] = s
            m_acc[h] = jnp.maximum(m_acc[h], _fold_tree(s, jnp.maximum))
    outs = []
    for h in range(n_heads):
        mx = jnp.max(m_acc[h], axis=0, keepdims=True)
        l_acc = jnp.zeros((SUBLANE, tq), F32)
        for r0 in range(0, rows, chunk):
            p = jnp.exp2(s_ref[h, r0:r0 + chunk, :] - mx)
            if weighted:
                p = p * w_ref[r0:r0 + chunk, :]
            l_acc = l_acc + _fold_tree(p, jnp.add)
            p_ref[h, r0:r0 + chunk, :] = p.astype(p_ref.dtype)
        den = jnp.sum(l_acc, axis=0, keepdims=True)
        outs.append(pv_of(h, p_ref[h, 0:rows, :]) / den)
    return outs


def _count_rows(key_ref, rows, pred):
    chunk = _row_chunk(rows)
    acc = None
    for r0 in range(0, rows, chunk):
        part = _fold(pred(key_ref[r0:r0 + chunk, :], r0), jnp.add)
        acc = part if acc is None else acc + part
    return jnp.sum(acc, axis=0, keepdims=True)


def _dsa_prefix(rows, i, iq_ref, iw_t, ik_ref, q_ref, k_ref, vt_ref, o_ref, key_ref, cut_ref, s_ref, p_ref, topk):
    tq = q_ref.shape[0]
    chunk = min(SCORE_ROWS * LANE // tq, rows)
    lane = lax.broadcasted_iota(jnp.int32, (chunk, LANE), 1)
    crow = lax.broadcasted_iota(jnp.int32, (chunk, tq), 0)
    ctcol = i * tq + lax.broadcasted_iota(jnp.int32, (chunk, tq), 1)

    for r0 in range(0, rows, chunk):
        ik = ik_ref[r0:r0 + chunk, :]
        halves = (jnp.where(lane < IDX_DIM, ik, jnp.zeros_like(ik)),
                  jnp.where(lane >= IDX_DIM, ik, jnp.zeros_like(ik)))
        score = jnp.zeros((chunk, tq), F32)
        for p in range(IDX_HEADS // 2):
            blk = iq_ref[:, p * LANE:(p + 1) * LANE]
            for half in range(2):
                j = 2 * p + half
                lg = lax.dot_general(halves[half], blk, (((1,), (1,)), ((), ())), preferred_element_type=F32)
                score = score + jnp.maximum(lg, 0.0) * iw_t[j:j + 1, :]
        bits = pltpu.bitcast(score, jnp.int32)
        key = bits ^ ((bits >> 31) & 0x7FFFFFFF)
        key_ref[r0:r0 + chunk, :] = jnp.where(r0 + crow <= ctcol, key, INT_MIN)

    def bit_step(b, thr):
        cand = thr ^ jnp.left_shift(jnp.int32(1), 31 - b)
        cnt = _count_rows(key_ref, rows, lambda kc, r0: jnp.where(kc >= cand, 1, 0))
        return jnp.where(cnt >= topk, cand, thr)

    thr = lax.fori_loop(0, 32, bit_step, jnp.full((1, tq), INT_MIN, jnp.int32))
    short = thr == INT_MIN
    need = topk - _count_rows(key_ref, rows, lambda kc, r0: jnp.where(kc > thr, 1, 0))
    n_eq = _count_rows(key_ref, rows, lambda kc, r0: jnp.where(kc == thr, 1, 0))
    excess = jnp.where(n_eq > need, jnp.where(short, 0, 1), 0)
    cut_ref[...] = jnp.where(short, -1, rows)

    def row_ids(r0, n):
        return r0 + lax.broadcasted_iota(jnp.int32, (n, tq), 0)

    @pl.when(jnp.max(excess) > 0)
    def _():
        n_bits = max(1, (rows - 1).bit_length())

        def idx_step(b, cut):
            cand = cut | jnp.left_shift(jnp.int32(1), n_bits - 1 - b)
            below = _count_rows(key_ref, rows, lambda kc, r0: jnp.where(
                kc == thr, jnp.where(row_ids(r0, kc.shape[0]) < cand, 1, 0), 0))
            return jnp.where(below < need, cand, cut)

        cut = lax.fori_loop(0, n_bits, idx_step, jnp.zeros((1, tq), jnp.int32))
        cut_ref[...] = jnp.where(excess > 0, cut, cut_ref[...])

    cut = cut_ref[...]

    def mask_of(r0, n):
        kc = key_ref[r0:r0 + n, :]
        chosen = jnp.where(kc > thr, 1, jnp.where(kc == thr, jnp.where(row_ids(r0, n) <= cut, 1, 0), 0))
        return chosen > 0, None

    vt = vt_ref[:, 0:rows]
    cols = [slice(h * HEAD_DIM, (h + 1) * HEAD_DIM) for h in range(A_HEADS)]
    outs = _attend_heads([q_ref[:, c] for c in cols], lambda h, r0, n: k_ref[r0:r0 + n, :],
                         lambda h, p: jnp.dot(vt, p, preferred_element_type=F32),
                         rows, mask_of, s_ref, p_ref)
    for h, c in enumerate(cols):
        o_ref[:, c] = outs[h].T.astype(o_ref.dtype)


def _dsa_kernel(iq_ref, iw_ref, ik_ref, q_ref, k_ref, v_ref, o_ref, vt_ref, key_ref, cut_ref, s_ref, p_ref,
                *, topk, step):
    i = pl.program_id(1)
    tq = q_ref.shape[0]
    s_len = k_ref.shape[0]

    @pl.when(i == 0)
    def _():
        _transpose_into(vt_ref, v_ref)

    iw_t = iw_ref[...].astype(F32).T[:IDX_HEADS, :]
    _for_causal_prefix(i, tq, s_len, step, functools.partial(
        _dsa_prefix, i=i, iq_ref=iq_ref, iw_t=iw_t, ik_ref=ik_ref, q_ref=q_ref, k_ref=k_ref, vt_ref=vt_ref,
        o_ref=o_ref, key_ref=key_ref, cut_ref=cut_ref, s_ref=s_ref, p_ref=p_ref, topk=topk))


def _dsa_call(p3, tq, step):
    b, s, _ = p3.shape
    topk = min(DSA_TOPK_MAX, s // 4)
    blk = lambda name, width: _OFF[name] // width
    return pl.pallas_call(
        functools.partial(_dsa_kernel, topk=topk, step=step),
        grid=(b, s // tq),
        in_specs=[
            pl.BlockSpec((None, tq, 8 * LANE), lambda bi, i: (bi, i, blk("i_q", 8))),
            pl.BlockSpec((None, tq, LANE), lambda bi, i: (bi, i, blk("i_w", 1))),
            pl.BlockSpec((None, s, LANE), lambda bi, i: (bi, 0, blk("i_k", 1))),
            pl.BlockSpec((None, tq, 4 * LANE), lambda bi, i: (bi, i, blk("a_q", 4))),
            pl.BlockSpec((None, s, LANE), lambda bi, i: (bi, 0, blk("a_k", 1))),
            pl.BlockSpec((None, s, LANE), lambda bi, i: (bi, 0, blk("a_v", 1))),
        ],
        out_specs=pl.BlockSpec((None, tq, A_HEADS * HEAD_DIM), lambda bi, i: (bi, i, 0)),
        out_shape=jax.ShapeDtypeStruct((b, s, A_HEADS * HEAD_DIM), BF16),
        scratch_shapes=[
            pltpu.VMEM((HEAD_DIM, s), BF16),
            pltpu.VMEM((s, tq), jnp.int32),
            pltpu.VMEM((1, tq), jnp.int32),
            pltpu.VMEM((A_HEADS, s, tq), F32),
            pltpu.VMEM((A_HEADS, s, tq), BF16),
        ],
        compiler_params=_params(("parallel", "arbitrary"), 48),
        name="dsa",
    )(p3, p3, p3, p3, p3, p3)


def _dil_prefix(rows, i, q_ref, k_ref, vt_ref, o_ref, s_ref, p_ref, w_ref):
    tq = q_ref.shape[0]

    def mask_of(r0, n):
        srow = r0 + lax.broadcasted_iota(jnp.int32, (n, tq), 0)
        tcol = i * tq + lax.broadcasted_iota(jnp.int32, (n, tq), 1)
        delta = tcol - srow
        mult = jnp.zeros((n, tq), jnp.int32)
        for window, dilation in DILATED_PATTERNS:
            on_stride = (delta & (dilation - 1)) == 0
            mult = mult + jnp.where(on_stride, jnp.where(delta <= window, 1, 0), 0)
        mult = jnp.where(delta >= 0, mult, 0)
        return mult > 0, mult.astype(F32)

    cols = [slice(h * HEAD_DIM, (h + 1) * HEAD_DIM) for h in range(B_HEADS)]
    outs = _attend_heads([q_ref[:, c] for c in cols], lambda h, r0, n: k_ref[r0:r0 + n, cols[h]],
                         lambda h, p: jnp.dot(vt_ref[cols[h], 0:rows], p, preferred_element_type=F32),
                         rows, mask_of, s_ref, p_ref, w_ref)
    for h, c in enumerate(cols):
        o_ref[:, c] = outs[h].T.astype(o_ref.dtype)


def _dil_kernel(q_ref, k_ref, v_ref, o_ref, vt_ref, s_ref, p_ref, w_ref, *, step):
    i = pl.program_id(1)

    @pl.when(i == 0)
    def _():
        _transpose_into(vt_ref, v_ref)

    _for_causal_prefix(i, q_ref.shape[0], k_ref.shape[0], step, functools.partial(
        _dil_prefix, i=i, q_ref=q_ref, k_ref=k_ref, vt_ref=vt_ref, o_ref=o_ref,
        s_ref=s_ref, p_ref=p_ref, w_ref=w_ref))


def _dil_call(p3, tq, step):
    b, s, _ = p3.shape
    width = B_HEADS * HEAD_DIM
    return pl.pallas_call(
        functools.partial(_dil_kernel, step=step),
        grid=(b, s // tq),
        in_specs=[
            pl.BlockSpec((None, tq, width), lambda bi, i: (bi, i, _OFF["b_q"] // 4)),
            pl.BlockSpec((None, s, width), lambda bi, i: (bi, 0, _OFF["b_k"] // 4)),
            pl.BlockSpec((None, s, width), lambda bi, i: (bi, 0, _OFF["b_v"] // 4)),
        ],
        out_specs=pl.BlockSpec((None, tq, width), lambda bi, i: (bi, i, 0)),
        out_shape=jax.ShapeDtypeStruct((b, s, width), BF16),
        scratch_shapes=[
            pltpu.VMEM((width, s), BF16),
            pltpu.VMEM((B_HEADS, s, tq), F32),
            pltpu.VMEM((B_HEADS, s, tq), BF16),
            pltpu.VMEM((s, tq), F32),
        ],
        compiler_params=_params(("parallel", "arbitrary"), 48),
        name="dilated",
    )(p3, p3, p3)


def _gelu_tanh(x):
    return 0.5 * x * (1.0 + jnp.tanh(np.float32(np.sqrt(2.0 / np.pi)) * (x + 0.044715 * (x * x * x))))


def _cmp_kernel(*refs):
    n = CMP_STRIDE
    xk, xv = refs[:n], refs[n:2 * n]
    pe_k, w1_k, w2_k, pe_v, w1_v, w2_v, kc_ref, vct_ref = refs[2 * n:]

    def branch(x_refs, pe_ref, w1_ref, w2_ref):
        lo = hi = None
        for j in range(n):
            xj = x_refs[j][...].astype(F32)
            a = (xj + pe_ref[j:j + 1, :]).astype(BF16)
            b = (xj + pe_ref[n + j:n + j + 1, :]).astype(BF16)
            dl = jnp.dot(a, w1_ref[j * HEAD_DIM:(j + 1) * HEAD_DIM, :], preferred_element_type=F32)
            dh = jnp.dot(b, w1_ref[(n + j) * HEAD_DIM:(n + j + 1) * HEAD_DIM, :], preferred_element_type=F32)
            lo = dl if lo is None else lo + dl
            hi = dh if hi is None else hi + dh
        nc = lo.shape[0]
        hid = lo + pltpu.roll(hi, nc - 1, 0)
        return jnp.dot(_gelu_tanh(hid).astype(BF16), w2_ref[...], preferred_element_type=F32)

    kc_ref[...] = branch(xk, pe_k, w1_k, w2_k).astype(kc_ref.dtype)
    vct_ref[...] = branch(xv, pe_v, w1_v, w2_v).T.astype(vct_ref.dtype)


def _cmp_call(pc, b, layer, pe_k, w1_k, w2_k, pe_v, w1_v, w2_v):
    nc = pc.shape[0] // b
    pc3 = pc.reshape(b, nc, CMP_STRIDE * CMP_COLS)
    x_specs = []
    for first in (0, C_KV_GROUPS):
        for j in range(CMP_STRIDE):
            x_specs.append(pl.BlockSpec((None, nc, HEAD_DIM),
                                        lambda bi, g, j=j, first=first: (bi, 0, j * CMP_BLOCKS + first + g)))
    flat = CMP_BLOCK * HEAD_DIM
    w_specs = [
        _resident((None, CMP_BLOCK, HEAD_DIM), lambda bi, g: (layer, 0, 0)),
        _resident((None, flat, CMP_HIDDEN), lambda bi, g: (layer, 0, 0)),
        _resident((None, CMP_HIDDEN, HEAD_DIM), lambda bi, g: (layer, 0, 0)),
    ]
    return pl.pallas_call(
        _cmp_kernel,
        grid=(b, C_KV_GROUPS),
        in_specs=x_specs + w_specs + w_specs,
        out_specs=[
            pl.BlockSpec((None, None, nc, HEAD_DIM), lambda bi, g: (bi, g, 0, 0)),
            pl.BlockSpec((None, None, HEAD_DIM, nc), lambda bi, g: (bi, g, 0, 0)),
        ],
        out_shape=[
            jax.ShapeDtypeStruct((b, C_KV_GROUPS, nc, HEAD_DIM), BF16),
            jax.ShapeDtypeStruct((b, C_KV_GROUPS, HEAD_DIM, nc), BF16),
        ],
        compiler_params=_params(("parallel", "parallel"), 32),
        name="nsa_compress",
    )(*([pc3] * (2 * CMP_STRIDE)), pe_k, w1_k, w2_k, pe_v, w1_v, w2_v)


def _nsa_selected(rows, t_row, q_sel, ke_ref, vst_ref, acc_ref, s_ref, p_ref, step):
    tq = q_sel[0].shape[0]

    def mask_of(r0, n):
        srow = r0 + lax.broadcasted_iota(jnp.int32, (n, tq), 0)
        return t_row - srow >= 0, None

    vt = vst_ref[:, 0:rows]
    outs = _attend_heads(q_sel, lambda h, r0, n: ke_ref[r0:r0 + n, :],
                         lambda h, p: jnp.dot(vt, p, preferred_element_type=F32),
                         rows, mask_of, s_ref, p_ref, plain_rows=rows - max(step, tq))
    for r in range(C_GROUP_SIZE):
        acc_ref[r] = outs[r]


def _nsa_kernel(q_ref, kc_ref, vct_ref, ks_ref, vs_ref, kw_ref, vw_ref, g_ref, cos_ref, sin_ref,
                isect_ref, expand_ref, o_ref, vst_ref, vwt_ref, acc_ref, s_ref, p_ref, ke_ref, *, n_sel, step):
    i = pl.program_id(2)
    tq = q_ref.shape[0]
    s_len = ks_ref.shape[0]
    nc = kc_ref.shape[0]
    n_slc = isect_ref.shape[0]

    @pl.when(i == 0)
    def _():
        _transpose_into(vst_ref, vs_ref)
        _transpose_tiles(vwt_ref, vw_ref)
        ke_ref[:, 0:HEAD_DIM] = ks_ref[...]
        ke_ref[:, HEAD_DIM:] = expand_ref[...]

    t_row = i * tq + lax.broadcasted_iota(jnp.int32, (1, tq), 1)

    kc = kc_ref[...]
    vct = vct_ref[...]
    cend = lax.broadcasted_iota(jnp.int32, (nc, tq), 0) * CMP_STRIDE + (CMP_BLOCK - 1)
    cmask = cend <= t_row
    o_cmp = []
    p_sum = jnp.zeros((nc, tq), F32)
    for r in range(C_GROUP_SIZE):
        q_r = q_ref[:, r * HEAD_DIM:(r + 1) * HEAD_DIM]
        sc = lax.dot_general(kc, q_r, (((1,), (1,)), ((), ())), preferred_element_type=F32)
        sc = jnp.where(cmask, sc, NEG)
        mx = jnp.max(sc, axis=0, keepdims=True)
        e = jnp.where(cmask, jnp.exp2(sc - mx), 0.0)
        den = jnp.sum(e, axis=0, keepdims=True)
        p = e / jnp.where(den > 0, den, 1.0)
        p_sum = p_sum + p
        o_cmp.append(jnp.dot(vct, p.astype(BF16), preferred_element_type=F32))

    isect = isect_ref[...]
    p_hi = p_sum.astype(BF16)
    p_lo = (p_sum - p_hi.astype(F32)).astype(BF16)
    imp = (jnp.dot(isect, p_hi, preferred_element_type=F32)
           + jnp.dot(isect, p_lo, preferred_element_type=F32))
    blk = lax.broadcasted_iota(jnp.int32, (n_slc, tq), 0)
    cur = t_row // SEL_BLOCK
    val = jnp.where(blk == 0, jnp.inf, jnp.where(blk == cur, jnp.inf, jnp.where(blk == cur - 1, jnp.inf, imp)))
    val = jnp.where(blk <= cur, val, -jnp.inf)
    rank = jnp.zeros((n_slc, tq), jnp.int32)
    for m in range(n_slc):
        vm = val[m:m + 1, :]
        before = jnp.where(vm > val, 1, jnp.where(vm == val, jnp.where(blk > m, 1, 0), 0))
        rank = rank + before
    bias = jnp.where(rank < n_sel, 0.0, NEG)
    if n_slc < LANE:
        bias = jnp.concatenate([bias, jnp.zeros((LANE - n_slc, tq), F32)], axis=0)
    bias_t = bias.T.astype(BF16)

    cos, sin = cos_ref[...], sin_ref[...]
    cols = [slice(r * HEAD_DIM, (r + 1) * HEAD_DIM) for r in range(C_GROUP_SIZE)]
    q_rot = [_rope_full(q_ref[:, c].astype(F32), cos, sin).astype(BF16) for c in cols]
    q_sel = [jnp.concatenate([q, bias_t], axis=1) for q in q_rot]

    _for_causal_prefix(i, tq, s_len, step, functools.partial(
        _nsa_selected, t_row=t_row, q_sel=q_sel, ke_ref=ke_ref, vst_ref=vst_ref,
        acc_ref=acc_ref, s_ref=s_ref, p_ref=p_ref, step=step))

    w_tiles = min((WIN_SIZE - 1 + tq - 1) // LANE + 1, s_len // LANE)
    w_rows = w_tiles * LANE
    w0 = pl.multiple_of(jnp.maximum(i * tq + tq - w_rows, 0), LANE)
    t0 = w0 // LANE

    def win_mask(r0, n):
        delta = t_row - (w0 + r0 + lax.broadcasted_iota(jnp.int32, (n, tq), 0))
        return jnp.where(delta >= 0, jnp.where(delta <= WIN_SIZE - 1, 1, 0), 0) > 0, None

    def pv_win(h, p):
        out = None
        for w in range(w_tiles):
            part = jnp.dot(vwt_ref[t0 + w], p[w * LANE:(w + 1) * LANE, :], preferred_element_type=F32)
            out = part if out is None else out + part
        return out

    o_win = _attend_heads(q_rot, lambda h, r0, n: kw_ref[pl.ds(w0 + r0, n), :], pv_win,
                          w_rows, win_mask, s_ref, p_ref)

    gates = jax.nn.sigmoid(g_ref[...].astype(F32).T)
    for r, c in enumerate(cols):
        o_t = (gates[3 * r:3 * r + 1, :] * o_cmp[r] + gates[3 * r + 1:3 * r + 2, :] * acc_ref[r]
               + gates[3 * r + 2:3 * r + 3, :] * o_win[r])
        o_ref[:, c] = o_t.T.astype(o_ref.dtype)


def _nsa_call(p3, kc, vct, cos, sin, isect, expand, tq, step):
    b, s, _ = p3.shape
    nc = kc.shape[2]
    n_slc = s // SEL_BLOCK
    width = C_GROUP_SIZE * HEAD_DIM
    kv = lambda name: pl.BlockSpec((None, s, HEAD_DIM), lambda bi, g, i, name=name: (bi, 0, _OFF[name] + g))
    return pl.pallas_call(
        functools.partial(_nsa_kernel, n_sel=min(SEL_COUNT, n_slc), step=step),
        grid=(b, C_KV_GROUPS, s // tq),
        in_specs=[
            pl.BlockSpec((None, tq, width), lambda bi, g, i: (bi, i, _OFF["c_q"] // 4 + g)),
            pl.BlockSpec((None, None, nc, HEAD_DIM), lambda bi, g, i: (bi, g, 0, 0)),
            pl.BlockSpec((None, None, HEAD_DIM, nc), lambda bi, g, i: (bi, g, 0, 0)),
            kv("c_ks"), kv("c_vs"), kv("c_kw"), kv("c_vw"),
            pl.BlockSpec((None, tq, LANE), lambda bi, g, i: (bi, i, _OFF["c_g0"] + g)),
            pl.BlockSpec((tq, LANE), lambda bi, g, i: (i, 0)),
            pl.BlockSpec((tq, LANE), lambda bi, g, i: (i, 0)),
            pl.BlockSpec(isect.shape, lambda bi, g, i: (0, 0)),
            pl.BlockSpec(expand.shape, lambda bi, g, i: (0, 0)),
        ],
        out_specs=pl.BlockSpec((None, tq, width), lambda bi, g, i: (bi, i, g)),
        out_shape=jax.ShapeDtypeStruct((b, s, C_HEADS * HEAD_DIM), BF16),
        scratch_shapes=[
            pltpu.VMEM((HEAD_DIM, s), BF16),
            pltpu.VMEM((s // LANE, HEAD_DIM, LANE), BF16),
            pltpu.VMEM((C_GROUP_SIZE, HEAD_DIM, tq), F32),
            pltpu.VMEM((C_GROUP_SIZE, s, tq), F32),
            pltpu.VMEM((C_GROUP_SIZE, s, tq), BF16),
            pltpu.VMEM((s, HEAD_DIM + expand.shape[1]), BF16),
        ],
        compiler_params=_params(("parallel", "parallel", "arbitrary"), 48),
        name="nsa",
    )(p3, kc, vct, p3, p3, p3, p3, p3, cos, sin, isect, expand)


def _out_kernel(oa_ref, ob_ref, oc_ref, w_ref, x_ref, gpost_ref, gnext_ref, xo_ref, h_ref):
    na, nb = oa_ref.shape[1], ob_ref.shape[1]
    y = jnp.dot(oa_ref[...], w_ref[0:na, :], preferred_element_type=F32)
    y = y + jnp.dot(ob_ref[...], w_ref[na:na + nb, :], preferred_element_type=F32)
    y = y + jnp.dot(oc_ref[...], w_ref[na + nb:, :], preferred_element_type=F32)
    x = x_ref[...] + _rms(y, gpost_ref[...])
    xo_ref[...] = x
    h_ref[...] = _rms(x, gnext_ref[...]).astype(h_ref.dtype)


def _out_call(o_a, o_b, o_c, w_out_p, layer, x, g_post, g_next, tm):
    m, d = x.shape
    row = lambda width: pl.BlockSpec((tm, width), lambda i: (i, 0))
    vec = pl.BlockSpec((1, d), lambda i: (0, 0))
    return pl.pallas_call(
        _out_kernel,
        grid=(m // tm,),
        in_specs=[
            row(o_a.shape[1]), row(o_b.shape[1]), row(o_c.shape[1]),
            _resident((None, w_out_p.shape[1], d), lambda i: (layer, 0, 0)),
            row(d), vec, vec,
        ],
        out_specs=[row(d), row(d)],
        out_shape=[jax.ShapeDtypeStruct((m, d), F32), jax.ShapeDtypeStruct((m, d), BF16)],
        compiler_params=_params(("parallel",), 48),
        name="out_proj",
    )(o_a, o_b, o_c, w_out_p, x, g_post, g_next)


def _ffn_kernel(h_ref, wg_ref, wu_ref, wd_ref, y_ref):
    @pl.when(pl.program_id(1) == 0)
    def _():
        y_ref[...] = jnp.zeros(y_ref.shape, F32)

    h = h_ref[...]
    g = jnp.dot(h, wg_ref[...], preferred_element_type=F32)
    u = jnp.dot(h, wu_ref[...], preferred_element_type=F32)
    a = (g * jax.nn.sigmoid(g) * u).astype(BF16)
    y_ref[...] += jnp.dot(a, wd_ref[...], preferred_element_type=F32)


def _ffn_call(h, wg, wu, wd, layer, tm, tf):
    m, d = h.shape
    d_ff = wg.shape[2]
    row = pl.BlockSpec((tm, d), lambda i, f: (i, 0))
    return pl.pallas_call(
        _ffn_kernel,
        grid=(m // tm, d_ff // tf),
        in_specs=[
            row,
            pl.BlockSpec((None, d, tf), lambda i, f: (layer, 0, f)),
            pl.BlockSpec((None, d, tf), lambda i, f: (layer, 0, f)),
            pl.BlockSpec((None, tf, d), lambda i, f: (layer, f, 0)),
        ],
        out_specs=row,
        out_shape=jax.ShapeDtypeStruct((m, d), F32),
        compiler_params=_params(("parallel", "arbitrary"), 52),
        name="ffn",
    )(h, wg, wu, wd)


def _residual_kernel(y_ref, x_ref, g_ref, o_ref):
    o_ref[...] = x_ref[...] + _rms(y_ref[...], g_ref[...])


def _residual_call(y, x, g, tm):
    m, d = x.shape
    row = pl.BlockSpec((tm, d), lambda i: (i, 0))
    return pl.pallas_call(
        _residual_kernel,
        grid=(m // tm,),
        in_specs=[row, row, pl.BlockSpec((1, d), lambda i: (0, 0))],
        out_specs=row,
        out_shape=jax.ShapeDtypeStruct((m, d), F32),
        compiler_params=_params(("parallel",), 32),
        name="ffn_residual",
    )(y, x, g)


def _pad_cols(w, width):
    return jnp.pad(w, ((0, 0), (0, 0), (0, width - w.shape[-1])))


def _prep_w_in(w_in):
    widths = (A_HEADS * HEAD_DIM, HEAD_DIM, HEAD_DIM, IDX_HEADS * IDX_DIM, IDX_DIM, IDX_HEADS,
              B_HEADS * HEAD_DIM, B_HEADS * HEAD_DIM, B_HEADS * HEAD_DIM, C_HEADS * HEAD_DIM,
              C_KV_GROUPS * HEAD_DIM, C_KV_GROUPS * HEAD_DIM, C_KV_GROUPS * HEAD_DIM, C_KV_GROUPS * HEAD_DIM,
              C_KV_GROUPS * HEAD_DIM, C_KV_GROUPS * HEAD_DIM, 3 * C_HEADS)
    names = ("a_q", "a_k", "a_v", "i_q", "i_k", "i_w", "b_q", "b_k", "b_v", "c_q",
             "c_kc", "c_vc", "c_ks", "c_vs", "c_kw", "c_vw", "c_g")
    offs = np.concatenate([[0], np.cumsum(widths)])
    src = {n: w_in[:, :, int(offs[k]):int(offs[k + 1])] for k, n in enumerate(names)}
    per_group = 3 * C_GROUP_SIZE
    src["c_g0"] = _pad_cols(src["c_g"][:, :, :per_group], LANE)
    src["c_g1"] = _pad_cols(src["c_g"][:, :, per_group:], LANE)
    src["i_k"] = jnp.concatenate([src["i_k"], src["i_k"]], axis=-1)
    src["i_w"] = _pad_cols(src["i_w"], LANE)
    order = [n for n, _, _, _ in _SEGMENTS] + [n for n, _ in _CMP_SEGMENTS]
    return jnp.concatenate([src[n] for n in order], axis=-1).astype(BF16)


def _rope_tables(seq):
    def tables(dim):
        inv = 1.0 / (ROPE_THETA ** (jnp.arange(0, dim, 2, dtype=F32) / dim))
        ang = jnp.arange(seq, dtype=F32)[:, None] * inv[None, :]
        cos, sin = jnp.cos(ang), jnp.sin(ang)
        reps = LANE // dim
        return (jnp.tile(jnp.concatenate([cos, cos], axis=-1), (1, reps)),
                jnp.tile(jnp.concatenate([-sin, sin], axis=-1), (1, reps)))
    return tables(HEAD_DIM) + tables(IDX_DIM)


def _selection_constants(seq):
    nc = seq // CMP_STRIDE
    n_slc = seq // SEL_BLOCK
    c_start = np.arange(nc) * CMP_STRIDE
    n_start = np.arange(n_slc) * SEL_BLOCK
    isect = ((c_start[None, :] < n_start[:, None] + SEL_BLOCK)
             & (c_start[None, :] + CMP_BLOCK > n_start[:, None])
             & (np.arange(nc)[None, :] < nc - CMP_BLOCK // CMP_STRIDE + 1)).astype(np.float32)
    expand = np.zeros((seq, max(n_slc, LANE)), np.float32)
    expand[np.arange(seq), np.arange(seq) // SEL_BLOCK] = 1.0
    return jnp.asarray(isect, BF16), jnp.asarray(expand, BF16)


def _tiles(seq):
    return dict(tq=min(2 * LANE, seq), step_dsa=min(4 * LANE, seq), step=min(2 * LANE, seq),
                tm_proj=min(256, seq), tm_out=min(512, seq), tm_ffn=min(1024, seq), tf=512)


def kernel(x, w_in, w_out, cmp_pe_k, cmp_w1_k, cmp_w2_k, cmp_pe_v, cmp_w1_v, cmp_w2_v,
           w_gate, w_up, w_down, g_pre_mix, g_post_mix, g_pre_ffn, g_post_ffn):
    b, s, d = x.shape
    depth = w_in.shape[0]
    m = b * s
    t = _tiles(s)
    tq, step = t["tq"], t["step"]

    w_in_p = _prep_w_in(w_in)
    w_out_p = w_out.astype(BF16)
    wg, wu, wd = w_gate.astype(BF16), w_up.astype(BF16), w_down.astype(BF16)
    w1_k, w2_k = cmp_w1_k.astype(BF16), cmp_w2_k.astype(BF16)
    w1_v, w2_v = cmp_w1_v.astype(BF16), cmp_w2_v.astype(BF16)
    col_scale = jnp.asarray(_COL_SCALE)
    tabs = _rope_tables(s)
    isect, expand = _selection_constants(s)
    vec = lambda g, layer: g[layer][None, :]

    xf = x.reshape(m, d)
    y = None
    for layer in range(depth):
        if layer == 0:
            p, pc = _proj_call((xf, vec(g_pre_mix, 0)), w_in_p, layer, col_scale, tabs, s, t["tm_proj"])
        else:
            rows_in = (y, xf, vec(g_post_ffn, layer - 1), vec(g_pre_mix, layer))
            xf, p, pc = _proj_call(rows_in, w_in_p, layer, col_scale, tabs, s, t["tm_proj"])
        p3 = p.reshape(b, s, NP)
        o_a = _dsa_call(p3, tq, t["step_dsa"])
        o_b = _dil_call(p3, tq, step)
        kc, vct = _cmp_call(pc, b, layer, cmp_pe_k, w1_k, w2_k, cmp_pe_v, w1_v, w2_v)
        o_c = _nsa_call(p3, kc, vct, tabs[0], tabs[1], isect, expand, tq, step)
        xf, h = _out_call(o_a.reshape(m, -1), o_b.reshape(m, -1), o_c.reshape(m, -1), w_out_p, layer,
                          xf, vec(g_post_mix, layer), vec(g_pre_ffn, layer), t["tm_out"])
        y = _ffn_call(h, wg, wu, wd, layer, t["tm_ffn"], t["tf"])
    return _residual_call(y, xf, vec(g_post_ffn, depth - 1), t["tm_out"]).reshape(b, s, d)
```

```python
import functools

import numpy as np
import jax
import jax.numpy as jnp
from jax import lax
from jax.experimental import pallas as pl
from jax.experimental.pallas import tpu as pltpu

F32 = jnp.float32
BF16 = jnp.bfloat16

LANE = 128
SUBLANE = 8
HEAD_DIM = 128
A_HEADS = 4
B_HEADS = 4
C_HEADS = 8
C_KV_GROUPS = 2
C_GROUP_SIZE = C_HEADS // C_KV_GROUPS
IDX_HEADS = 16
IDX_DIM = 64
DSA_TOPK_MAX = 256
DILATED_PATTERNS = ((128, 1), (512, 4), (2048, 16))
CMP_BLOCK = 32
CMP_STRIDE = 16
CMP_HIDDEN = 256
SEL_BLOCK = 64
SEL_COUNT = 16
WIN_SIZE = 512
ROPE_THETA = 10000.0
RMS_EPS = 1e-6
ATTN_SCALE = HEAD_DIM ** -0.5 * float(np.log2(np.e))
IDX_SCALE = IDX_DIM ** -0.5 * IDX_HEADS ** -0.5

NEG = -1e30
INT_MIN = -(2 ** 31)
HALF16 = 2 ** 15
COUNT_ROWS = 8 * SUBLANE
SCORE_ROWS = 256
ATT_ROWS = 128

MODE_NONE, MODE_ROPE, MODE_IROPE = 0, 1, 2

_SEGMENTS = (
    ("i_q", 8, MODE_IROPE, IDX_SCALE),
    ("c_q", 8, MODE_NONE, ATTN_SCALE),
    ("a_q", 4, MODE_ROPE, ATTN_SCALE),
    ("b_q", 4, MODE_ROPE, ATTN_SCALE),
    ("b_k", 4, MODE_ROPE, 1.0),
    ("b_v", 4, MODE_NONE, 1.0),
    ("c_ks", 2, MODE_ROPE, 1.0),
    ("c_vs", 2, MODE_NONE, 1.0),
    ("c_kw", 2, MODE_ROPE, 1.0),
    ("c_vw", 2, MODE_NONE, 1.0),
    ("a_k", 1, MODE_ROPE, 1.0),
    ("a_v", 1, MODE_NONE, 1.0),
    ("i_k", 1, MODE_IROPE, 1.0),
    ("i_w", 1, MODE_NONE, 1.0),
    ("c_g0", 1, MODE_NONE, 1.0),
    ("c_g1", 1, MODE_NONE, 1.0),
)
_CMP_SEGMENTS = (("c_kc", 2), ("c_vc", 2))
_OFF = {}
_o = 0
for _n, _w, _m, _s in _SEGMENTS:
    _OFF[_n] = _o
    _o += _w
NP_BLOCKS = _o
NP = NP_BLOCKS * LANE
CMP_BLOCKS = sum(w for _, w in _CMP_SEGMENTS)
CMP_COLS = CMP_BLOCKS * LANE
NW = NP + CMP_COLS
_BLOCK_MODE = tuple(m for _, w, m, _ in _SEGMENTS for _ in range(w))
_COL_SCALE = np.repeat(np.array([s for _, w, _, s in _SEGMENTS for _ in range(w)], np.float32), LANE)[None, :]
_PROJ_CHUNK = 4


def _params(sem, vmem_mb):
    return pltpu.CompilerParams(dimension_semantics=sem, vmem_limit_bytes=vmem_mb * 1024 * 1024)


def _resident(block_shape, index_map):
    return pl.BlockSpec(block_shape, index_map, pipeline_mode=pl.Buffered(1))


def _rms(x, g):
    return x * lax.rsqrt(jnp.mean(x * x, axis=-1, keepdims=True) + RMS_EPS) * g


def _rms_kernel(x_ref, g_ref, h_ref):
    h_ref[...] = _rms(x_ref[...], g_ref[...]).astype(h_ref.dtype)


def _rms_call(x, g, tm):
    m, d = x.shape
    return pl.pallas_call(
        _rms_kernel,
        grid=(m // tm,),
        in_specs=[pl.BlockSpec((tm, d), lambda i: (i, 0)), pl.BlockSpec((1, d), lambda i: (0, 0))],
        out_specs=pl.BlockSpec((tm, d), lambda i: (i, 0)),
        out_shape=jax.ShapeDtypeStruct((m, d), BF16),
        compiler_params=_params(("parallel",), 32),
        name="rms_pre",
    )(x, g)


def _rope_full(a, cos, sin):
    return a * cos + pltpu.roll(a, HEAD_DIM // 2, 1) * sin


def _rope_idx(a, cos, sin, first_half):
    partner = jnp.where(first_half, pltpu.roll(a, LANE - IDX_DIM // 2, 1), pltpu.roll(a, IDX_DIM // 2, 1))
    return a * cos + partner * sin


def _proj_kernel(*refs, after_ffn):
    if after_ffn:
        (y_ref, x_ref, gpost_ref, gpre_ref, w_ref, cs_ref, cos_ref, sin_ref, icos_ref, isin_ref,
         xo_ref, o_ref, oc_ref, cmp_ref) = refs
        x = x_ref[...] + _rms(y_ref[...], gpost_ref[...])
        xo_ref[...] = x
        h = _rms(x, gpre_ref[...]).astype(BF16)
    else:
        x_ref, gpre_ref, w_ref, cs_ref, cos_ref, sin_ref, icos_ref, isin_ref, o_ref, oc_ref, cmp_ref = refs
        h = _rms(x_ref[...], gpre_ref[...]).astype(BF16)
    tm = h.shape[0]
    lane = lax.broadcasted_iota(jnp.int32, (tm, LANE), 1)
    first_half = (lane & (IDX_DIM - 1)) < IDX_DIM // 2
    for c0 in range(0, NP_BLOCKS, _PROJ_CHUNK):
        nb = min(_PROJ_CHUNK, NP_BLOCKS - c0)
        acc = jnp.dot(h, w_ref[:, c0 * LANE:(c0 + nb) * LANE], preferred_element_type=F32)
        for b in range(nb):
            col = slice((c0 + b) * LANE, (c0 + b + 1) * LANE)
            a = acc[:, b * LANE:(b + 1) * LANE] * cs_ref[:, col]
            mode = _BLOCK_MODE[c0 + b]
            if mode == MODE_ROPE:
                a = _rope_full(a, cos_ref[...], sin_ref[...])
            elif mode == MODE_IROPE:
                a = _rope_idx(a, icos_ref[...], isin_ref[...], first_half)
            o_ref[:, col] = a.astype(o_ref.dtype)
    acc = jnp.dot(h, w_ref[:, NP:NW], preferred_element_type=F32)
    for b in range(CMP_BLOCKS):
        cmp_ref[b] = acc[:, b * LANE:(b + 1) * LANE]
    for j in range(CMP_STRIDE):
        for b in range(CMP_BLOCKS):
            rows = cmp_ref[b, pl.ds(j, tm // CMP_STRIDE, stride=CMP_STRIDE), :]
            oc_ref[:, j * CMP_COLS + b * LANE:j * CMP_COLS + (b + 1) * LANE] = rows.astype(oc_ref.dtype)


def _proj_call(rows_in, w_in_p, layer, col_scale, tabs, seq, tm):
    after_ffn = len(rows_in) > 2
    m, d = rows_in[0].shape
    tpb = seq // tm
    tab_spec = pl.BlockSpec((tm, LANE), lambda i: (i % tpb, 0))
    row = pl.BlockSpec((tm, d), lambda i: (i, 0))
    vec = pl.BlockSpec((1, d), lambda i: (0, 0))
    out_specs = [
        pl.BlockSpec((tm, NP), lambda i: (i, 0)),
        pl.BlockSpec((tm // CMP_STRIDE, CMP_STRIDE * CMP_COLS), lambda i: (i, 0)),
    ]
    out_shape = [
        jax.ShapeDtypeStruct((m, NP), BF16),
        jax.ShapeDtypeStruct((m // CMP_STRIDE, CMP_STRIDE * CMP_COLS), BF16),
    ]
    if after_ffn:
        out_specs = [row] + out_specs
        out_shape = [jax.ShapeDtypeStruct((m, d), F32)] + out_shape
    return pl.pallas_call(
        functools.partial(_proj_kernel, after_ffn=after_ffn),
        grid=(m // tm,),
        in_specs=([row, row, vec, vec] if after_ffn else [row, vec]) + [
            _resident((None, d, NW), lambda i: (layer, 0, 0)),
            _resident((1, NP), lambda i: (0, 0)),
            tab_spec, tab_spec, tab_spec, tab_spec,
        ],
        out_specs=out_specs,
        out_shape=out_shape,
        scratch_shapes=[pltpu.VMEM((CMP_BLOCKS, tm, LANE), F32)],
        compiler_params=_params(("parallel",), 54),
        name="in_proj",
    )(*rows_in, w_in_p, col_scale, *tabs)


def _transpose_into(dst_ref, src_ref):
    rows, cols = src_ref.shape
    for r0 in range(0, rows, LANE):
        for c0 in range(0, cols, LANE):
            tile = src_ref[r0:r0 + LANE, c0:c0 + LANE].astype(F32)
            dst_ref[c0:c0 + LANE, r0:r0 + LANE] = tile.T.astype(dst_ref.dtype)


def _transpose_tiles(dst_ref, src_ref):
    rows, _ = src_ref.shape
    for r in range(rows // LANE):
        tile = src_ref[r * LANE:(r + 1) * LANE, :].astype(F32)
        dst_ref[r] = tile.T.astype(dst_ref.dtype)


def _for_causal_prefix(i, tq, s_len, step, body):
    n = ((i + 1) * tq + step - 1) // step
    for v in range(1, s_len // step + 1):
        pl.when(n == v)(functools.partial(body, v * step))


def _fold(x, op):
    out = x[0:COUNT_ROWS]
    for r0 in range(COUNT_ROWS, x.shape[0], COUNT_ROWS):
        out = op(out, x[r0:r0 + COUNT_ROWS])
    return out


def _fold_tree(x, op):
    while x.shape[0] > SUBLANE:
        half = x.shape[0] // 2
        x = op(x[:half], x[half:])
    return x


def _row_chunk(rows):
    return SCORE_ROWS if rows % SCORE_ROWS == 0 else LANE


def _attend_heads(qs, k_of, pv_of, rows, mask_of, s_ref, p_ref, w_ref=None, plain_rows=0):
    n_heads = len(qs)
    tq = qs[0].shape[0]
    chunk = ATT_ROWS * LANE // tq
    for h in range(n_heads):
        s_ref[h, 0:rows, :] = lax.dot_general(k_of(h, 0, rows), qs[h], (((1,), (1,)), ((), ())),
                                              preferred_element_type=F32)
    m_acc = [jnp.full((SUBLANE, tq), NEG, F32) for _ in qs]
    weighted = False
    for r0 in range(0, rows, chunk):
        if r0 + chunk <= plain_rows:
            for h in range(n_heads):
                m_acc[h] = jnp.maximum(m_acc[h], _fold_tree(s_ref[h, r0:r0 + chunk, :], jnp.maximum))
            continue
        mask, weight = mask_of(r0, chunk)
        if weight is not None:
            weighted = True
            w_ref[r0:r0 + chunk, :] = weight
        for h in range(n_heads):
            s = jnp.where(mask, s_ref[h, r0:r0 + chunk, :], NEG)
            s_ref[h, r0:r0 + chunk, :] = s
            m_acc[h] = jnp.maximum(m_acc[h], _fold_tree(s, jnp.maximum))
    outs = []
    for h in range(n_heads):
        mx = jnp.max(m_acc[h], axis=0, keepdims=True)
        l_acc = jnp.zeros((SUBLANE, tq), F32)
        for r0 in range(0, rows, chunk):
            p = jnp.exp2(s_ref[h, r0:r0 + chunk, :] - mx)
            if weighted:
                p = p * w_ref[r0:r0 + chunk, :]
            l_acc = l_acc + _fold_tree(p, jnp.add)
            p_ref[h, r0:r0 + chunk, :] = p.astype(p_ref.dtype)
        den = jnp.sum(l_acc, axis=0, keepdims=True)
        outs.append(pv_of(h, p_ref[h, 0:rows, :]) / den)
    return outs


def _count_rows(key_ref, rows, pred):
    chunk = _row_chunk(rows)
    acc = None
    for r0 in range(0, rows, chunk):
        part = _fold(pred(key_ref[r0:r0 + chunk, :], r0), jnp.add)
        acc = part if acc is None else acc + part
    return jnp.sum(acc.astype(jnp.int32), axis=0, keepdims=True)


def _dsa_prefix(rows, i, iq_ref, iw_t, ik_ref, q_ref, k_ref, vt_ref, o_ref, key_ref, cut_ref, s_ref, p_ref,
                hi_ref, lo_ref, topk):
    tq = q_ref.shape[0]
    chunk = min(SCORE_ROWS * LANE // tq, rows)
    lane = lax.broadcasted_iota(jnp.int32, (chunk, LANE), 1)
    crow = lax.broadcasted_iota(jnp.int32, (chunk, tq), 0)
    ctcol = i * tq + lax.broadcasted_iota(jnp.int32, (chunk, tq), 1)

    for r0 in range(0, rows, chunk):
        ik = ik_ref[r0:r0 + chunk, :]
        halves = (jnp.where(lane < IDX_DIM, ik, jnp.zeros_like(ik)),
                  jnp.where(lane >= IDX_DIM, ik, jnp.zeros_like(ik)))
        score = jnp.zeros((chunk, tq), F32)
        for p in range(IDX_HEADS // 2):
            blk = iq_ref[:, p * LANE:(p + 1) * LANE]
            for half in range(2):
                j = 2 * p + half
                lg = lax.dot_general(halves[half], blk, (((1,), (1,)), ((), ())), preferred_element_type=F32)
                score = score + jnp.maximum(lg, 0.0) * iw_t[j:j + 1, :]
        bits = pltpu.bitcast(score, jnp.int32)
        key = bits ^ ((bits >> 31) & 0x7FFFFFFF)
        key = jnp.where(r0 + crow <= ctcol, key, INT_MIN)
        key_ref[r0:r0 + chunk, :] = key
        hi_ref[r0:r0 + chunk, :] = (key >> 16).astype(jnp.int16)
        lo_ref[r0:r0 + chunk, :] = ((key & 0xFFFF) - HALF16).astype(jnp.int16)

    def search16(ref, target):
        def bit_step(b, off):
            cand_off = off | jnp.left_shift(jnp.int32(1), 15 - b)
            cand = (cand_off - HALF16).astype(jnp.int16)
            cnt = _count_rows(ref, rows, lambda c, r0: jnp.where(c >= cand, jnp.int16(1), jnp.int16(0)))
            return jnp.where(cnt >= target, cand_off, off)

        return lax.fori_loop(0, 16, bit_step, jnp.zeros((1, tq), jnp.int32)) - HALF16

    thr_hi = search16(hi_ref, topk)
    thr_hi16 = thr_hi.astype(jnp.int16)
    above = _count_rows(hi_ref, rows, lambda c, r0: jnp.where(c > thr_hi16, jnp.int16(1), jnp.int16(0)))
    chunk16 = _row_chunk(rows)
    for r0 in range(0, rows, chunk16):
        same = hi_ref[r0:r0 + chunk16, :] == thr_hi16
        hi_ref[r0:r0 + chunk16, :] = jnp.where(same, lo_ref[r0:r0 + chunk16, :], jnp.int16(-HALF16))
    thr_lo = search16(hi_ref, topk - above)
    thr = jnp.left_shift(thr_hi, 16) | (thr_lo + HALF16)
    short = thr == INT_MIN
    need = topk - _count_rows(key_ref, rows, lambda kc, r0: jnp.where(kc > thr, 1, 0))
    n_eq = _count_rows(key_ref, rows, lambda kc, r0: jnp.where(kc == thr, 1, 0))
    excess = jnp.where(n_eq > need, jnp.where(short, 0, 1), 0)
    cut_ref[...] = jnp.where(short, -1, rows)

    def row_ids(r0, n):
        return r0 + lax.broadcasted_iota(jnp.int32, (n, tq), 0)

    @pl.when(jnp.max(excess) > 0)
    def _():
        n_bits = max(1, (rows - 1).bit_length())

        def idx_step(b, cut):
            cand = cut | jnp.left_shift(jnp.int32(1), n_bits - 1 - b)
            below = _count_rows(key_ref, rows, lambda kc, r0: jnp.where(
                kc == thr, jnp.where(row_ids(r0, kc.shape[0]) < cand, 1, 0), 0))
            return jnp.where(below < need, cand, cut)

        cut = lax.fori_loop(0, n_bits, idx_step, jnp.zeros((1, tq), jnp.int32))
        cut_ref[...] = jnp.where(excess > 0, cut, cut_ref[...])

    cut = cut_ref[...]

    def mask_of(r0, n):
        kc = key_ref[r0:r0 + n, :]
        chosen = jnp.where(kc > thr, 1, jnp.where(kc == thr, jnp.where(row_ids(r0, n) <= cut, 1, 0), 0))
        return chosen > 0, None

    vt = vt_ref[:, 0:rows]
    cols = [slice(h * HEAD_DIM, (h + 1) * HEAD_DIM) for h in range(A_HEADS)]
    outs = _attend_heads([q_ref[:, c] for c in cols], lambda h, r0, n: k_ref[r0:r0 + n, :],
                         lambda h, p: jnp.dot(vt, p, preferred_element_type=F32),
                         rows, mask_of, s_ref, p_ref)
    for h, c in enumerate(cols):
        o_ref[:, c] = outs[h].T.astype(o_ref.dtype)


def _dsa_kernel(iq_ref, iw_ref, ik_ref, q_ref, k_ref, v_ref, o_ref, vt_ref, key_ref, cut_ref, s_ref, p_ref,
                hi_ref, lo_ref, *, topk, step):
    i = pl.program_id(1)
    tq = q_ref.shape[0]
    s_len = k_ref.shape[0]

    @pl.when(i == 0)
    def _():
        _transpose_into(vt_ref, v_ref)

    iw_t = iw_ref[...].astype(F32).T[:IDX_HEADS, :]
    _for_causal_prefix(i, tq, s_len, step, functools.partial(
        _dsa_prefix, i=i, iq_ref=iq_ref, iw_t=iw_t, ik_ref=ik_ref, q_ref=q_ref, k_ref=k_ref, vt_ref=vt_ref,
        o_ref=o_ref, key_ref=key_ref, cut_ref=cut_ref, s_ref=s_ref, p_ref=p_ref,
        hi_ref=hi_ref, lo_ref=lo_ref, topk=topk))


def _dsa_call(p3, tq, step):
    b, s, _ = p3.shape
    topk = min(DSA_TOPK_MAX, s // 4)
    blk = lambda name, width: _OFF[name] // width
    return pl.pallas_call(
        functools.partial(_dsa_kernel, topk=topk, step=step),
        grid=(b, s // tq),
        in_specs=[
            pl.BlockSpec((None, tq, 8 * LANE), lambda bi, i: (bi, i, blk("i_q", 8))),
            pl.BlockSpec((None, tq, LANE), lambda bi, i: (bi, i, blk("i_w", 1))),
            pl.BlockSpec((None, s, LANE), lambda bi, i: (bi, 0, blk("i_k", 1))),
            pl.BlockSpec((None, tq, 4 * LANE), lambda bi, i: (bi, i, blk("a_q", 4))),
            pl.BlockSpec((None, s, LANE), lambda bi, i: (bi, 0, blk("a_k", 1))),
            pl.BlockSpec((None, s, LANE), lambda bi, i: (bi, 0, blk("a_v", 1))),
        ],
        out_specs=pl.BlockSpec((None, tq, A_HEADS * HEAD_DIM), lambda bi, i: (bi, i, 0)),
        out_shape=jax.ShapeDtypeStruct((b, s, A_HEADS * HEAD_DIM), BF16),
        scratch_shapes=[
            pltpu.VMEM((HEAD_DIM, s), BF16),
            pltpu.VMEM((s, tq), jnp.int32),
            pltpu.VMEM((1, tq), jnp.int32),
            pltpu.VMEM((A_HEADS, s, tq), F32),
            pltpu.VMEM((A_HEADS, s, tq), BF16),
            pltpu.VMEM((s, tq), jnp.int16),
            pltpu.VMEM((s, tq), jnp.int16),
        ],
        compiler_params=_params(("parallel", "arbitrary"), 48),
        name="dsa",
    )(p3, p3, p3, p3, p3, p3)


def _dil_prefix(rows, i, q_ref, k_ref, vt_ref, o_ref, s_ref, p_ref, w_ref):
    tq = q_ref.shape[0]

    def mask_of(r0, n):
        srow = r0 + lax.broadcasted_iota(jnp.int32, (n, tq), 0)
        tcol = i * tq + lax.broadcasted_iota(jnp.int32, (n, tq), 1)
        delta = tcol - srow
        mult = jnp.zeros((n, tq), jnp.int32)
        for window, dilation in DILATED_PATTERNS:
            on_stride = (delta & (dilation - 1)) == 0
            mult = mult + jnp.where(on_stride, jnp.where(delta <= window, 1, 0), 0)
        mult = jnp.where(delta >= 0, mult, 0)
        return mult > 0, mult.astype(F32)

    cols = [slice(h * HEAD_DIM, (h + 1) * HEAD_DIM) for h in range(B_HEADS)]
    outs = _attend_heads([q_ref[:, c] for c in cols], lambda h, r0, n: k_ref[r0:r0 + n, cols[h]],
                         lambda h, p: jnp.dot(vt_ref[cols[h], 0:rows], p, preferred_element_type=F32),
                         rows, mask_of, s_ref, p_ref, w_ref)
    for h, c in enumerate(cols):
        o_ref[:, c] = outs[h].T.astype(o_ref.dtype)


def _dil_kernel(q_ref, k_ref, v_ref, o_ref, vt_ref, s_ref, p_ref, w_ref, *, step):
    i = pl.program_id(1)

    @pl.when(i == 0)
    def _():
        _transpose_into(vt_ref, v_ref)

    _for_causal_prefix(i, q_ref.shape[0], k_ref.shape[0], step, functools.partial(
        _dil_prefix, i=i, q_ref=q_ref, k_ref=k_ref, vt_ref=vt_ref, o_ref=o_ref,
        s_ref=s_ref, p_ref=p_ref, w_ref=w_ref))


def _dil_call(p3, tq, step):
    b, s, _ = p3.shape
    width = B_HEADS * HEAD_DIM
    return pl.pallas_call(
        functools.partial(_dil_kernel, step=step),
        grid=(b, s // tq),
        in_specs=[
            pl.BlockSpec((None, tq, width), lambda bi, i: (bi, i, _OFF["b_q"] // 4)),
            pl.BlockSpec((None, s, width), lambda bi, i: (bi, 0, _OFF["b_k"] // 4)),
            pl.BlockSpec((None, s, width), lambda bi, i: (bi, 0, _OFF["b_v"] // 4)),
        ],
        out_specs=pl.BlockSpec((None, tq, width), lambda bi, i: (bi, i, 0)),
        out_shape=jax.ShapeDtypeStruct((b, s, width), BF16),
        scratch_shapes=[
            pltpu.VMEM((width, s), BF16),
            pltpu.VMEM((B_HEADS, s, tq), F32),
            pltpu.VMEM((B_HEADS, s, tq), BF16),
            pltpu.VMEM((s, tq), F32),
        ],
        compiler_params=_params(("parallel", "arbitrary"), 48),
        name="dilated",
    )(p3, p3, p3)


def _gelu_tanh(x):
    return 0.5 * x * (1.0 + jnp.tanh(np.float32(np.sqrt(2.0 / np.pi)) * (x + 0.044715 * (x * x * x))))


def _cmp_kernel(*refs):
    n = CMP_STRIDE
    xk, xv = refs[:n], refs[n:2 * n]
    pe_k, w1_k, w2_k, pe_v, w1_v, w2_v, kc_ref, vct_ref = refs[2 * n:]

    def branch(x_refs, pe_ref, w1_ref, w2_ref):
        lo = hi = None
        for j in range(n):
            xj = x_refs[j][...].astype(F32)
            a = (xj + pe_ref[j:j + 1, :]).astype(BF16)
            b = (xj + pe_ref[n + j:n + j + 1, :]).astype(BF16)
            dl = jnp.dot(a, w1_ref[j * HEAD_DIM:(j + 1) * HEAD_DIM, :], preferred_element_type=F32)
            dh = jnp.dot(b, w1_ref[(n + j) * HEAD_DIM:(n + j + 1) * HEAD_DIM, :], preferred_element_type=F32)
            lo = dl if lo is None else lo + dl
            hi = dh if hi is None else hi + dh
        nc = lo.shape[0]
        hid = lo + pltpu.roll(hi, nc - 1, 0)
        return jnp.dot(_gelu_tanh(hid).astype(BF16), w2_ref[...], preferred_element_type=F32)

    kc_ref[...] = branch(xk, pe_k, w1_k, w2_k).astype(kc_ref.dtype)
    vct_ref[...] = branch(xv, pe_v, w1_v, w2_v).T.astype(vct_ref.dtype)


def _cmp_call(pc, b, layer, pe_k, w1_k, w2_k, pe_v, w1_v, w2_v):
    nc = pc.shape[0] // b
    pc3 = pc.reshape(b, nc, CMP_STRIDE * CMP_COLS)
    x_specs = []
    for first in (0, C_KV_GROUPS):
        for j in range(CMP_STRIDE):
            x_specs.append(pl.BlockSpec((None, nc, HEAD_DIM),
                                        lambda bi, g, j=j, first=first: (bi, 0, j * CMP_BLOCKS + first + g)))
    flat = CMP_BLOCK * HEAD_DIM
    w_specs = [
        _resident((None, CMP_BLOCK, HEAD_DIM), lambda bi, g: (layer, 0, 0)),
        _resident((None, flat, CMP_HIDDEN), lambda bi, g: (layer, 0, 0)),
        _resident((None, CMP_HIDDEN, HEAD_DIM), lambda bi, g: (layer, 0, 0)),
    ]
    return pl.pallas_call(
        _cmp_kernel,
        grid=(b, C_KV_GROUPS),
        in_specs=x_specs + w_specs + w_specs,
        out_specs=[
            pl.BlockSpec((None, None, nc, HEAD_DIM), lambda bi, g: (bi, g, 0, 0)),
            pl.BlockSpec((None, None, HEAD_DIM, nc), lambda bi, g: (bi, g, 0, 0)),
        ],
        out_shape=[
            jax.ShapeDtypeStruct((b, C_KV_GROUPS, nc, HEAD_DIM), BF16),
            jax.ShapeDtypeStruct((b, C_KV_GROUPS, HEAD_DIM, nc), BF16),
        ],
        compiler_params=_params(("parallel", "parallel"), 32),
        name="nsa_compress",
    )(*([pc3] * (2 * CMP_STRIDE)), pe_k, w1_k, w2_k, pe_v, w1_v, w2_v)


def _nsa_selected(rows, t_row, q_sel, ke_ref, vst_ref, acc_ref, s_ref, p_ref, step):
    tq = q_sel[0].shape[0]

    def mask_of(r0, n):
        srow = r0 + lax.broadcasted_iota(jnp.int32, (n, tq), 0)
        return t_row - srow >= 0, None

    vt = vst_ref[:, 0:rows]
    outs = _attend_heads(q_sel, lambda h, r0, n: ke_ref[r0:r0 + n, :],
                         lambda h, p: jnp.dot(vt, p, preferred_element_type=F32),
                         rows, mask_of, s_ref, p_ref, plain_rows=rows - max(step, tq))
    for r in range(C_GROUP_SIZE):
        acc_ref[r] = outs[r]


def _nsa_kernel(q_ref, kc_ref, vct_ref, ks_ref, vs_ref, kw_ref, vw_ref, g_ref, cos_ref, sin_ref,
                isect_ref, expand_ref, o_ref, vst_ref, vwt_ref, acc_ref, s_ref, p_ref, ke_ref, *, n_sel, step):
    i = pl.program_id(2)
    tq = q_ref.shape[0]
    s_len = ks_ref.shape[0]
    nc = kc_ref.shape[0]
    n_slc = isect_ref.shape[0]

    @pl.when(i == 0)
    def _():
        _transpose_into(vst_ref, vs_ref)
        _transpose_tiles(vwt_ref, vw_ref)
        ke_ref[:, 0:HEAD_DIM] = ks_ref[...]
        ke_ref[:, HEAD_DIM:] = expand_ref[...]

    t_row = i * tq + lax.broadcasted_iota(jnp.int32, (1, tq), 1)

    kc = kc_ref[...]
    vct = vct_ref[...]
    cend = lax.broadcasted_iota(jnp.int32, (nc, tq), 0) * CMP_STRIDE + (CMP_BLOCK - 1)
    cmask = cend <= t_row
    o_cmp = []
    p_sum = jnp.zeros((nc, tq), F32)
    for r in range(C_GROUP_SIZE):
        q_r = q_ref[:, r * HEAD_DIM:(r + 1) * HEAD_DIM]
        sc = lax.dot_general(kc, q_r, (((1,), (1,)), ((), ())), preferred_element_type=F32)
        sc = jnp.where(cmask, sc, NEG)
        mx = jnp.max(sc, axis=0, keepdims=True)
        e = jnp.where(cmask, jnp.exp2(sc - mx), 0.0)
        den = jnp.sum(e, axis=0, keepdims=True)
        p = e / jnp.where(den > 0, den, 1.0)
        p_sum = p_sum + p
        o_cmp.append(jnp.dot(vct, p.astype(BF16), preferred_element_type=F32))

    isect = isect_ref[...]
    p_hi = p_sum.astype(BF16)
    p_lo = (p_sum - p_hi.astype(F32)).astype(BF16)
    imp = (jnp.dot(isect, p_hi, preferred_element_type=F32)
           + jnp.dot(isect, p_lo, preferred_element_type=F32))
    blk = lax.broadcasted_iota(jnp.int32, (n_slc, tq), 0)
    cur = t_row // SEL_BLOCK
    val = jnp.where(blk == 0, jnp.inf, jnp.where(blk == cur, jnp.inf, jnp.where(blk == cur - 1, jnp.inf, imp)))
    val = jnp.where(blk <= cur, val, -jnp.inf)
    rank = jnp.zeros((n_slc, tq), jnp.int32)
    for m in range(n_slc):
        vm = val[m:m + 1, :]
        before = jnp.where(vm > val, 1, jnp.where(vm == val, jnp.where(blk > m, 1, 0), 0))
        rank = rank + before
    bias = jnp.where(rank < n_sel, 0.0, NEG)
    if n_slc < LANE:
        bias = jnp.concatenate([bias, jnp.zeros((LANE - n_slc, tq), F32)], axis=0)
    bias_t = bias.T.astype(BF16)

    cos, sin = cos_ref[...], sin_ref[...]
    cols = [slice(r * HEAD_DIM, (r + 1) * HEAD_DIM) for r in range(C_GROUP_SIZE)]
    q_rot = [_rope_full(q_ref[:, c].astype(F32), cos, sin).astype(BF16) for c in cols]
    q_sel = [jnp.concatenate([q, bias_t], axis=1) for q in q_rot]

    _for_causal_prefix(i, tq, s_len, step, functools.partial(
        _nsa_selected, t_row=t_row, q_sel=q_sel, ke_ref=ke_ref, vst_ref=vst_ref,
        acc_ref=acc_ref, s_ref=s_ref, p_ref=p_ref, step=step))

    w_tiles = min((WIN_SIZE - 1 + tq - 1) // LANE + 1, s_len // LANE)
    w_rows = w_tiles * LANE
    w0 = pl.multiple_of(jnp.maximum(i * tq + tq - w_rows, 0), LANE)
    t0 = w0 // LANE

    def win_mask(r0, n):
        delta = t_row - (w0 + r0 + lax.broadcasted_iota(jnp.int32, (n, tq), 0))
        return jnp.where(delta >= 0, jnp.where(delta <= WIN_SIZE - 1, 1, 0), 0) > 0, None

    def pv_win(h, p):
        out = None
        for w in range(w_tiles):
            part = jnp.dot(vwt_ref[t0 + w], p[w * LANE:(w + 1) * LANE, :], preferred_element_type=F32)
            out = part if out is None else out + part
        return out

    o_win = _attend_heads(q_rot, lambda h, r0, n: kw_ref[pl.ds(w0 + r0, n), :], pv_win,
                          w_rows, win_mask, s_ref, p_ref)

    gates = jax.nn.sigmoid(g_ref[...].astype(F32).T)
    for r, c in enumerate(cols):
        o_t = (gates[3 * r:3 * r + 1, :] * o_cmp[r] + gates[3 * r + 1:3 * r + 2, :] * acc_ref[r]
               + gates[3 * r + 2:3 * r + 3, :] * o_win[r])
        o_ref[:, c] = o_t.T.astype(o_ref.dtype)


def _nsa_call(p3, kc, vct, cos, sin, isect, expand, tq, step):
    b, s, _ = p3.shape
    nc = kc.shape[2]
    n_slc = s // SEL_BLOCK
    width = C_GROUP_SIZE * HEAD_DIM
    kv = lambda name: pl.BlockSpec((None, s, HEAD_DIM), lambda bi, g, i, name=name: (bi, 0, _OFF[name] + g))
    return pl.pallas_call(
        functools.partial(_nsa_kernel, n_sel=min(SEL_COUNT, n_slc), step=step),
        grid=(b, C_KV_GROUPS, s // tq),
        in_specs=[
            pl.BlockSpec((None, tq, width), lambda bi, g, i: (bi, i, _OFF["c_q"] // 4 + g)),
            pl.BlockSpec((None, None, nc, HEAD_DIM), lambda bi, g, i: (bi, g, 0, 0)),
            pl.BlockSpec((None, None, HEAD_DIM, nc), lambda bi, g, i: (bi, g, 0, 0)),
            kv("c_ks"), kv("c_vs"), kv("c_kw"), kv("c_vw"),
            pl.BlockSpec((None, tq, LANE), lambda bi, g, i: (bi, i, _OFF["c_g0"] + g)),
            pl.BlockSpec((tq, LANE), lambda bi, g, i: (i, 0)),
            pl.BlockSpec((tq, LANE), lambda bi, g, i: (i, 0)),
            pl.BlockSpec(isect.shape, lambda bi, g, i: (0, 0)),
            pl.BlockSpec(expand.shape, lambda bi, g, i: (0, 0)),
        ],
        out_specs=pl.BlockSpec((None, tq, width), lambda bi, g, i: (bi, i, g)),
        out_shape=jax.ShapeDtypeStruct((b, s, C_HEADS * HEAD_DIM), BF16),
        scratch_shapes=[
            pltpu.VMEM((HEAD_DIM, s), BF16),
            pltpu.VMEM((s // LANE, HEAD_DIM, LANE), BF16),
            pltpu.VMEM((C_GROUP_SIZE, HEAD_DIM, tq), F32),
            pltpu.VMEM((C_GROUP_SIZE, s, tq), F32),
            pltpu.VMEM((C_GROUP_SIZE, s, tq), BF16),
            pltpu.VMEM((s, HEAD_DIM + expand.shape[1]), BF16),
        ],
        compiler_params=_params(("parallel", "parallel", "arbitrary"), 48),
        name="nsa",
    )(p3, kc, vct, p3, p3, p3, p3, p3, cos, sin, isect, expand)


def _out_kernel(oa_ref, ob_ref, oc_ref, w_ref, x_ref, gpost_ref, gnext_ref, xo_ref, h_ref):
    na, nb = oa_ref.shape[1], ob_ref.shape[1]
    y = jnp.dot(oa_ref[...], w_ref[0:na, :], preferred_element_type=F32)
    y = y + jnp.dot(ob_ref[...], w_ref[na:na + nb, :], preferred_element_type=F32)
    y = y + jnp.dot(oc_ref[...], w_ref[na + nb:, :], preferred_element_type=F32)
    x = x_ref[...] + _rms(y, gpost_ref[...])
    xo_ref[...] = x
    h_ref[...] = _rms(x, gnext_ref[...]).astype(h_ref.dtype)


def _out_call(o_a, o_b, o_c, w_out_p, layer, x, g_post, g_next, tm):
    m, d = x.shape
    row = lambda width: pl.BlockSpec((tm, width), lambda i: (i, 0))
    vec = pl.BlockSpec((1, d), lambda i: (0, 0))
    return pl.pallas_call(
        _out_kernel,
        grid=(m // tm,),
        in_specs=[
            row(o_a.shape[1]), row(o_b.shape[1]), row(o_c.shape[1]),
            _resident((None, w_out_p.shape[1], d), lambda i: (layer, 0, 0)),
            row(d), vec, vec,
        ],
        out_specs=[row(d), row(d)],
        out_shape=[jax.ShapeDtypeStruct((m, d), F32), jax.ShapeDtypeStruct((m, d), BF16)],
        compiler_params=_params(("parallel",), 48),
        name="out_proj",
    )(o_a, o_b, o_c, w_out_p, x, g_post, g_next)


def _ffn_kernel(h_ref, wg_ref, wu_ref, wd_ref, y_ref):
    @pl.when(pl.program_id(1) == 0)
    def _():
        y_ref[...] = jnp.zeros(y_ref.shape, F32)

    h = h_ref[...]
    g = jnp.dot(h, wg_ref[...], preferred_element_type=F32)
    u = jnp.dot(h, wu_ref[...], preferred_element_type=F32)
    a = (g * jax.nn.sigmoid(g) * u).astype(BF16)
    y_ref[...] += jnp.dot(a, wd_ref[...], preferred_element_type=F32)


def _ffn_call(h, wg, wu, wd, layer, tm, tf):
    m, d = h.shape
    d_ff = wg.shape[2]
    row = pl.BlockSpec((tm, d), lambda i, f: (i, 0))
    return pl.pallas_call(
        _ffn_kernel,
        grid=(m // tm, d_ff // tf),
        in_specs=[
            row,
            pl.BlockSpec((None, d, tf), lambda i, f: (layer, 0, f)),
            pl.BlockSpec((None, d, tf), lambda i, f: (layer, 0, f)),
            pl.BlockSpec((None, tf, d), lambda i, f: (layer, f, 0)),
        ],
        out_specs=row,
        out_shape=jax.ShapeDtypeStruct((m, d), F32),
        compiler_params=_params(("parallel", "arbitrary"), 52),
        name="ffn",
    )(h, wg, wu, wd)


def _residual_kernel(y_ref, x_ref, g_ref, o_ref):
    o_ref[...] = x_ref[...] + _rms(y_ref[...], g_ref[...])


def _residual_call(y, x, g, tm):
    m, d = x.shape
    row = pl.BlockSpec((tm, d), lambda i: (i, 0))
    return pl.pallas_call(
        _residual_kernel,
        grid=(m // tm,),
        in_specs=[row, row, pl.BlockSpec((1, d), lambda i: (0, 0))],
        out_specs=row,
        out_shape=jax.ShapeDtypeStruct((m, d), F32),
        compiler_params=_params(("parallel",), 32),
        name="ffn_residual",
    )(y, x, g)


def _pad_cols(w, width):
    return jnp.pad(w, ((0, 0), (0, 0), (0, width - w.shape[-1])))


def _prep_w_in(w_in):
    widths = (A_HEADS * HEAD_DIM, HEAD_DIM, HEAD_DIM, IDX_HEADS * IDX_DIM, IDX_DIM, IDX_HEADS,
              B_HEADS * HEAD_DIM, B_HEADS * HEAD_DIM, B_HEADS * HEAD_DIM, C_HEADS * HEAD_DIM,
              C_KV_GROUPS * HEAD_DIM, C_KV_GROUPS * HEAD_DIM, C_KV_GROUPS * HEAD_DIM, C_KV_GROUPS * HEAD_DIM,
              C_KV_GROUPS * HEAD_DIM, C_KV_GROUPS * HEAD_DIM, 3 * C_HEADS)
    names = ("a_q", "a_k", "a_v", "i_q", "i_k", "i_w", "b_q", "b_k", "b_v", "c_q",
             "c_kc", "c_vc", "c_ks", "c_vs", "c_kw", "c_vw", "c_g")
    offs = np.concatenate([[0], np.cumsum(widths)])
    src = {n: w_in[:, :, int(offs[k]):int(offs[k + 1])] for k, n in enumerate(names)}
    per_group = 3 * C_GROUP_SIZE
    src["c_g0"] = _pad_cols(src["c_g"][:, :, :per_group], LANE)
    src["c_g1"] = _pad_cols(src["c_g"][:, :, per_group:], LANE)
    src["i_k"] = jnp.concatenate([src["i_k"], src["i_k"]], axis=-1)
    src["i_w"] = _pad_cols(src["i_w"], LANE)
    order = [n for n, _, _, _ in _SEGMENTS] + [n for n, _ in _CMP_SEGMENTS]
    return jnp.concatenate([src[n] for n in order], axis=-1).astype(BF16)


def _rope_tables(seq):
    def tables(dim):
        inv = 1.0 / (ROPE_THETA ** (jnp.arange(0, dim, 2, dtype=F32) / dim))
        ang = jnp.arange(seq, dtype=F32)[:, None] * inv[None, :]
        cos, sin = jnp.cos(ang), jnp.sin(ang)
        reps = LANE // dim
        return (jnp.tile(jnp.concatenate([cos, cos], axis=-1), (1, reps)),
                jnp.tile(jnp.concatenate([-sin, sin], axis=-1), (1, reps)))
    return tables(HEAD_DIM) + tables(IDX_DIM)


def _selection_constants(seq):
    nc = seq // CMP_STRIDE
    n_slc = seq // SEL_BLOCK
    c_start = np.arange(nc) * CMP_STRIDE
    n_start = np.arange(n_slc) * SEL_BLOCK
    isect = ((c_start[None, :] < n_start[:, None] + SEL_BLOCK)
             & (c_start[None, :] + CMP_BLOCK > n_start[:, None])
             & (np.arange(nc)[None, :] < nc - CMP_BLOCK // CMP_STRIDE + 1)).astype(np.float32)
    expand = np.zeros((seq, max(n_slc, LANE)), np.float32)
    expand[np.arange(seq), np.arange(seq) // SEL_BLOCK] = 1.0
    return jnp.asarray(isect, BF16), jnp.asarray(expand, BF16)


def _tiles(seq):
    return dict(tq=min(2 * LANE, seq), step_dsa=min(4 * LANE, seq), step=min(2 * LANE, seq),
                tm_proj=min(256, seq), tm_out=min(512, seq), tm_ffn=min(1024, seq), tf=512)


def kernel(x, w_in, w_out, cmp_pe_k, cmp_w1_k, cmp_w2_k, cmp_pe_v, cmp_w1_v, cmp_w2_v,
           w_gate, w_up, w_down, g_pre_mix, g_post_mix, g_pre_ffn, g_post_ffn):
    b, s, d = x.shape
    depth = w_in.shape[0]
    m = b * s
    t = _tiles(s)
    tq, step = t["tq"], t["step"]

    w_in_p = _prep_w_in(w_in)
    w_out_p = w_out.astype(BF16)
    wg, wu, wd = w_gate.astype(BF16), w_up.astype(BF16), w_down.astype(BF16)
    w1_k, w2_k = cmp_w1_k.astype(BF16), cmp_w2_k.astype(BF16)
    w1_v, w2_v = cmp_w1_v.astype(BF16), cmp_w2_v.astype(BF16)
    col_scale = jnp.asarray(_COL_SCALE)
    tabs = _rope_tables(s)
    isect, expand = _selection_constants(s)
    vec = lambda g, layer: g[layer][None, :]

    xf = x.reshape(m, d)
    y = None
    for layer in range(depth):
        if layer == 0:
            p, pc = _proj_call((xf, vec(g_pre_mix, 0)), w_in_p, layer, col_scale, tabs, s, t["tm_proj"])
        else:
            rows_in = (y, xf, vec(g_post_ffn, layer - 1), vec(g_pre_mix, layer))
            xf, p, pc = _proj_call(rows_in, w_in_p, layer, col_scale, tabs, s, t["tm_proj"])
        p3 = p.reshape(b, s, NP)
        o_a = _dsa_call(p3, tq, t["step_dsa"])
        o_b = _dil_call(p3, tq, step)
        kc, vct = _cmp_call(pc, b, layer, cmp_pe_k, w1_k, w2_k, cmp_pe_v, w1_v, w2_v)
        o_c = _nsa_call(p3, kc, vct, tabs[0], tabs[1], isect, expand, tq, step)
        xf, h = _out_call(o_a.reshape(m, -1), o_b.reshape(m, -1), o_c.reshape(m, -1), w_out_p, layer,
                          xf, vec(g_post_mix, layer), vec(g_pre_ffn, layer), t["tm_out"])
        y = _ffn_call(h, wg, wu, wd, layer, t["tm_ffn"], t["tf"])
    return _residual_call(y, xf, vec(g_post_ffn, depth - 1), t["tm_out"]).reshape(b, s, d)
```

```python
import functools

import numpy as np
import jax
import jax.numpy as jnp
from jax import lax
from jax.experimental import pallas as pl
from jax.experimental.pallas import tpu as pltpu

F32 = jnp.float32
BF16 = jnp.bfloat16

LANE = 128
SUBLANE = 8
HEAD_DIM = 128
A_HEADS = 4
B_HEADS = 4
C_HEADS = 8
C_KV_GROUPS = 2
C_GROUP_SIZE = C_HEADS // C_KV_GROUPS
IDX_HEADS = 16
IDX_DIM = 64
DSA_TOPK_MAX = 256
DILATED_PATTERNS = ((128, 1), (512, 4), (2048, 16))
CMP_BLOCK = 32
CMP_STRIDE = 16
CMP_HIDDEN = 256
SEL_BLOCK = 64
SEL_COUNT = 16
WIN_SIZE = 512
ROPE_THETA = 10000.0
RMS_EPS = 1e-6
ATTN_SCALE = HEAD_DIM ** -0.5 * float(np.log2(np.e))
IDX_SCALE = IDX_DIM ** -0.5 * IDX_HEADS ** -0.5

NEG = -1e30
INT_MIN = -(2 ** 31)
HALF16 = 2 ** 15
COUNT_ROWS = 8 * SUBLANE
SCORE_ROWS = 256
ATT_ROWS = 128
ONES_ROWS = 16

MODE_NONE, MODE_ROPE, MODE_IROPE = 0, 1, 2

_SEGMENTS = (
    ("i_q", 8, MODE_IROPE, IDX_SCALE),
    ("c_q", 8, MODE_NONE, ATTN_SCALE),
    ("a_q", 4, MODE_ROPE, ATTN_SCALE),
    ("b_q", 4, MODE_ROPE, ATTN_SCALE),
    ("b_k", 4, MODE_ROPE, 1.0),
    ("b_v", 4, MODE_NONE, 1.0),
    ("c_ks", 2, MODE_ROPE, 1.0),
    ("c_vs", 2, MODE_NONE, 1.0),
    ("c_kw", 2, MODE_ROPE, 1.0),
    ("c_vw", 2, MODE_NONE, 1.0),
    ("a_k", 1, MODE_ROPE, 1.0),
    ("a_v", 1, MODE_NONE, 1.0),
    ("i_k", 1, MODE_IROPE, 1.0),
    ("i_w", 1, MODE_NONE, 1.0),
    ("c_g0", 1, MODE_NONE, 1.0),
    ("c_g1", 1, MODE_NONE, 1.0),
)
_CMP_SEGMENTS = (("c_kc", 2), ("c_vc", 2))
_OFF = {}
_o = 0
for _n, _w, _m, _s in _SEGMENTS:
    _OFF[_n] = _o
    _o += _w
NP_BLOCKS = _o
NP = NP_BLOCKS * LANE
CMP_BLOCKS = sum(w for _, w in _CMP_SEGMENTS)
CMP_COLS = CMP_BLOCKS * LANE
NW = NP + CMP_COLS
_BLOCK_MODE = tuple(m for _, w, m, _ in _SEGMENTS for _ in range(w))
_COL_SCALE = np.repeat(np.array([s for _, w, _, s in _SEGMENTS for _ in range(w)], np.float32), LANE)[None, :]
_PROJ_CHUNK = 4


def _params(sem, vmem_mb):
    return pltpu.CompilerParams(dimension_semantics=sem, vmem_limit_bytes=vmem_mb * 1024 * 1024)


def _resident(block_shape, index_map):
    return pl.BlockSpec(block_shape, index_map, pipeline_mode=pl.Buffered(1))


def _rms(x, g):
    return x * lax.rsqrt(jnp.mean(x * x, axis=-1, keepdims=True) + RMS_EPS) * g


def _rope_full(a, cos, sin):
    return a * cos + pltpu.roll(a, HEAD_DIM // 2, 1) * sin


def _rope_idx(a, cos, sin, first_half):
    partner = jnp.where(first_half, pltpu.roll(a, LANE - IDX_DIM // 2, 1), pltpu.roll(a, IDX_DIM // 2, 1))
    return a * cos + partner * sin


def _proj_kernel(*refs, after_ffn):
    if after_ffn:
        (y_ref, x_ref, gpost_ref, gpre_ref, w_ref, cs_ref, cos_ref, sin_ref, icos_ref, isin_ref,
         xo_ref, o_ref, oc_ref, cmp_ref) = refs
        x = x_ref[...] + _rms(y_ref[...], gpost_ref[...])
        xo_ref[...] = x
        h = _rms(x, gpre_ref[...]).astype(BF16)
    else:
        x_ref, gpre_ref, w_ref, cs_ref, cos_ref, sin_ref, icos_ref, isin_ref, o_ref, oc_ref, cmp_ref = refs
        h = _rms(x_ref[...], gpre_ref[...]).astype(BF16)
    tm = h.shape[0]
    lane = lax.broadcasted_iota(jnp.int32, (tm, LANE), 1)
    first_half = (lane & (IDX_DIM - 1)) < IDX_DIM // 2
    for c0 in range(0, NP_BLOCKS, _PROJ_CHUNK):
        nb = min(_PROJ_CHUNK, NP_BLOCKS - c0)
        acc = jnp.dot(h, w_ref[:, c0 * LANE:(c0 + nb) * LANE], preferred_element_type=F32)
        for b in range(nb):
            col = slice((c0 + b) * LANE, (c0 + b + 1) * LANE)
            a = acc[:, b * LANE:(b + 1) * LANE] * cs_ref[:, col]
            mode = _BLOCK_MODE[c0 + b]
            if mode == MODE_ROPE:
                a = _rope_full(a, cos_ref[...], sin_ref[...])
            elif mode == MODE_IROPE:
                a = _rope_idx(a, icos_ref[...], isin_ref[...], first_half)
            o_ref[:, col] = a.astype(o_ref.dtype)
    acc = jnp.dot(h, w_ref[:, NP:NW], preferred_element_type=F32)
    for b in range(CMP_BLOCKS):
        cmp_ref[b] = acc[:, b * LANE:(b + 1) * LANE]
    for j in range(CMP_STRIDE):
        for b in range(CMP_BLOCKS):
            rows = cmp_ref[b, pl.ds(j, tm // CMP_STRIDE, stride=CMP_STRIDE), :]
            oc_ref[:, j * CMP_COLS + b * LANE:j * CMP_COLS + (b + 1) * LANE] = rows.astype(oc_ref.dtype)


def _proj_call(rows_in, w_in_p, layer, col_scale, tabs, seq, tm):
    after_ffn = len(rows_in) > 2
    m, d = rows_in[0].shape
    tpb = seq // tm
    tab_spec = pl.BlockSpec((tm, LANE), lambda i: (i % tpb, 0))
    row = pl.BlockSpec((tm, d), lambda i: (i, 0))
    vec = pl.BlockSpec((1, d), lambda i: (0, 0))
    out_specs = [
        pl.BlockSpec((tm, NP), lambda i: (i, 0)),
        pl.BlockSpec((tm // CMP_STRIDE, CMP_STRIDE * CMP_COLS), lambda i: (i, 0)),
    ]
    out_shape = [
        jax.ShapeDtypeStruct((m, NP), BF16),
        jax.ShapeDtypeStruct((m // CMP_STRIDE, CMP_STRIDE * CMP_COLS), BF16),
    ]
    if after_ffn:
        out_specs = [row] + out_specs
        out_shape = [jax.ShapeDtypeStruct((m, d), F32)] + out_shape
    return pl.pallas_call(
        functools.partial(_proj_kernel, after_ffn=after_ffn),
        grid=(m // tm,),
        in_specs=([row, row, vec, vec] if after_ffn else [row, vec]) + [
            _resident((None, d, NW), lambda i: (layer, 0, 0)),
            _resident((1, NP), lambda i: (0, 0)),
            tab_spec, tab_spec, tab_spec, tab_spec,
        ],
        out_specs=out_specs,
        out_shape=out_shape,
        scratch_shapes=[pltpu.VMEM((CMP_BLOCKS, tm, LANE), F32)],
        compiler_params=_params(("parallel",), 54),
        name="in_proj",
    )(*rows_in, w_in_p, col_scale, *tabs)


def _transpose_v(dst_ref, src_ref, col0=0):
    rows = src_ref.shape[0]
    for r0 in range(0, rows, LANE):
        tile = src_ref[r0:r0 + LANE, col0:col0 + HEAD_DIM].astype(F32)
        dst_ref[0:HEAD_DIM, r0:r0 + LANE] = tile.T.astype(dst_ref.dtype)
    dst_ref[HEAD_DIM:, :] = jnp.ones((ONES_ROWS, rows), dst_ref.dtype)


def _transpose_v_tiles(dst_ref, src_ref):
    rows, _ = src_ref.shape
    for r in range(rows // LANE):
        tile = src_ref[r * LANE:(r + 1) * LANE, :].astype(F32)
        dst_ref[r, 0:HEAD_DIM, :] = tile.T.astype(dst_ref.dtype)
        dst_ref[r, HEAD_DIM:, :] = jnp.ones((ONES_ROWS, LANE), dst_ref.dtype)


def _for_causal_prefix(i, tq, s_len, step, body):
    n = ((i + 1) * tq + step - 1) // step
    for v in range(1, s_len // step + 1):
        pl.when(n == v)(functools.partial(body, v * step))


def _fold(x, op):
    out = x[0:COUNT_ROWS]
    for r0 in range(COUNT_ROWS, x.shape[0], COUNT_ROWS):
        out = op(out, x[r0:r0 + COUNT_ROWS])
    return out


def _fold_tree(x, op):
    while x.shape[0] > SUBLANE:
        half = x.shape[0] // 2
        x = op(x[:half], x[half:])
    return x


def _row_chunk(rows):
    return SCORE_ROWS if rows % SCORE_ROWS == 0 else LANE


def _attend_heads(qs, k_of, pv_of, rows, mask_of, s_ref, p_ref, w_ref=None, plain_rows=0):
    n_heads = len(qs)
    tq = qs[0].shape[0]
    chunk = ATT_ROWS * LANE // tq
    for h in range(n_heads):
        s_ref[h, 0:rows, :] = lax.dot_general(k_of(h, 0, rows), qs[h], (((1,), (1,)), ((), ())),
                                              preferred_element_type=F32)
    m_acc = [jnp.full((SUBLANE, tq), NEG, F32) for _ in qs]
    weighted = False
    for r0 in range(0, rows, chunk):
        if r0 + chunk <= plain_rows:
            for h in range(n_heads):
                m_acc[h] = jnp.maximum(m_acc[h], _fold_tree(s_ref[h, r0:r0 + chunk, :], jnp.maximum))
            continue
        mask, weight = mask_of(r0, chunk)
        if weight is not None:
            weighted = True
            w_ref[r0:r0 + chunk, :] = weight
        for h in range(n_heads):
            s = jnp.where(mask, s_ref[h, r0:r0 + chunk, :], NEG)
            s_ref[h, r0:r0 + chunk, :] = s
            m_acc[h] = jnp.maximum(m_acc[h], _fold_tree(s, jnp.maximum))
    outs = []
    for h in range(n_heads):
        mx = jnp.max(m_acc[h], axis=0, keepdims=True)
        for r0 in range(0, rows, chunk):
            p = jnp.exp2(s_ref[h, r0:r0 + chunk, :] - mx)
            if weighted:
                p = p * w_ref[r0:r0 + chunk, :]
            p_ref[h, r0:r0 + chunk, :] = p.astype(p_ref.dtype)
        pv = pv_of(h, p_ref[h, 0:rows, :])
        outs.append(pv[0:HEAD_DIM] / pv[HEAD_DIM:HEAD_DIM + 1])
    return outs


def _count_rows(key_ref, rows, pred):
    chunk = _row_chunk(rows)
    acc = None
    for r0 in range(0, rows, chunk):
        part = _fold(pred(key_ref[r0:r0 + chunk, :], r0), jnp.add)
        acc = part if acc is None else acc + part
    return jnp.sum(acc.astype(jnp.int32), axis=0, keepdims=True)


def _dsa_prefix(rows, i, iq_ref, iw_t, ik_ref, q_ref, k_ref, vt_ref, o_ref, key_ref, cut_ref, s_ref, p_ref,
                hi_ref, lo_ref, topk):
    tq = q_ref.shape[0]
    chunk = min(SCORE_ROWS * LANE // tq, rows)
    lane = lax.broadcasted_iota(jnp.int32, (chunk, LANE), 1)
    crow = lax.broadcasted_iota(jnp.int32, (chunk, tq), 0)
    ctcol = i * tq + lax.broadcasted_iota(jnp.int32, (chunk, tq), 1)

    for r0 in range(0, rows, chunk):
        ik = ik_ref[r0:r0 + chunk, :]
        halves = (jnp.where(lane < IDX_DIM, ik, jnp.zeros_like(ik)),
                  jnp.where(lane >= IDX_DIM, ik, jnp.zeros_like(ik)))
        score = jnp.zeros((chunk, tq), F32)
        for p in range(IDX_HEADS // 2):
            blk = iq_ref[:, p * LANE:(p + 1) * LANE]
            for half in range(2):
                j = 2 * p + half
                lg = lax.dot_general(halves[half], blk, (((1,), (1,)), ((), ())), preferred_element_type=F32)
                score = score + jnp.maximum(lg, 0.0) * iw_t[j:j + 1, :]
        bits = pltpu.bitcast(score, jnp.int32)
        key = bits ^ ((bits >> 31) & 0x7FFFFFFF)
        key = jnp.where(r0 + crow <= ctcol, key, INT_MIN)
        key_ref[r0:r0 + chunk, :] = key
        hi_ref[r0:r0 + chunk, :] = (key >> 16).astype(jnp.int16)
        lo_ref[r0:r0 + chunk, :] = ((key & 0xFFFF) - HALF16).astype(jnp.int16)

    def search16(ref, target):
        def bit_step(b, off):
            cand_off = off | jnp.left_shift(jnp.int32(1), 15 - b)
            cand = (cand_off - HALF16).astype(jnp.int16)
            cnt = _count_rows(ref, rows, lambda c, r0: jnp.where(c >= cand, jnp.int16(1), jnp.int16(0)))
            return jnp.where(cnt >= target, cand_off, off)

        return lax.fori_loop(0, 16, bit_step, jnp.zeros((1, tq), jnp.int32)) - HALF16

    thr_hi = search16(hi_ref, topk)
    thr_hi16 = thr_hi.astype(jnp.int16)
    above = _count_rows(hi_ref, rows, lambda c, r0: jnp.where(c > thr_hi16, jnp.int16(1), jnp.int16(0)))
    chunk16 = _row_chunk(rows)
    for r0 in range(0, rows, chunk16):
        same = hi_ref[r0:r0 + chunk16, :] == thr_hi16
        hi_ref[r0:r0 + chunk16, :] = jnp.where(same, lo_ref[r0:r0 + chunk16, :], jnp.int16(-HALF16))
    thr_lo = search16(hi_ref, topk - above)
    thr = jnp.left_shift(thr_hi, 16) | (thr_lo + HALF16)
    short = thr == INT_MIN
    need = topk - _count_rows(key_ref, rows, lambda kc, r0: jnp.where(kc > thr, 1, 0))
    n_eq = _count_rows(key_ref, rows, lambda kc, r0: jnp.where(kc == thr, 1, 0))
    excess = jnp.where(n_eq > need, jnp.where(short, 0, 1), 0)
    cut_ref[...] = jnp.where(short, -1, rows)

    def row_ids(r0, n):
        return r0 + lax.broadcasted_iota(jnp.int32, (n, tq), 0)

    @pl.when(jnp.max(excess) > 0)
    def _():
        n_bits = max(1, (rows - 1).bit_length())

        def idx_step(b, cut):
            cand = cut | jnp.left_shift(jnp.int32(1), n_bits - 1 - b)
            below = _count_rows(key_ref, rows, lambda kc, r0: jnp.where(
                kc == thr, jnp.where(row_ids(r0, kc.shape[0]) < cand, 1, 0), 0))
            return jnp.where(below < need, cand, cut)

        cut = lax.fori_loop(0, n_bits, idx_step, jnp.zeros((1, tq), jnp.int32))
        cut_ref[...] = jnp.where(excess > 0, cut, cut_ref[...])

    cut = cut_ref[...]

    def mask_of(r0, n):
        kc = key_ref[r0:r0 + n, :]
        chosen = jnp.where(kc > thr, 1, jnp.where(kc == thr, jnp.where(row_ids(r0, n) <= cut, 1, 0), 0))
        return chosen > 0, None

    vt = vt_ref[:, 0:rows]
    cols = [slice(h * HEAD_DIM, (h + 1) * HEAD_DIM) for h in range(A_HEADS)]
    outs = _attend_heads([q_ref[:, c] for c in cols], lambda h, r0, n: k_ref[r0:r0 + n, :],
                         lambda h, p: jnp.dot(vt, p, preferred_element_type=F32),
                         rows, mask_of, s_ref, p_ref)
    for h, c in enumerate(cols):
        o_ref[:, c] = outs[h].T.astype(o_ref.dtype)


def _dsa_kernel(iq_ref, iw_ref, ik_ref, q_ref, k_ref, v_ref, o_ref, vt_ref, key_ref, cut_ref, s_ref, p_ref,
                hi_ref, lo_ref, *, topk, step):
    i = pl.program_id(1)
    tq = q_ref.shape[0]
    s_len = k_ref.shape[0]

    @pl.when(i == 0)
    def _():
        _transpose_v(vt_ref, v_ref)

    iw_t = iw_ref[...].astype(F32).T[:IDX_HEADS, :]
    _for_causal_prefix(i, tq, s_len, step, functools.partial(
        _dsa_prefix, i=i, iq_ref=iq_ref, iw_t=iw_t, ik_ref=ik_ref, q_ref=q_ref, k_ref=k_ref, vt_ref=vt_ref,
        o_ref=o_ref, key_ref=key_ref, cut_ref=cut_ref, s_ref=s_ref, p_ref=p_ref,
        hi_ref=hi_ref, lo_ref=lo_ref, topk=topk))


def _dsa_call(p3, tq, step):
    b, s, _ = p3.shape
    topk = min(DSA_TOPK_MAX, s // 4)
    blk = lambda name, width: _OFF[name] // width
    return pl.pallas_call(
        functools.partial(_dsa_kernel, topk=topk, step=step),
        grid=(b, s // tq),
        in_specs=[
            pl.BlockSpec((None, tq, 8 * LANE), lambda bi, i: (bi, i, blk("i_q", 8))),
            pl.BlockSpec((None, tq, LANE), lambda bi, i: (bi, i, blk("i_w", 1))),
            pl.BlockSpec((None, s, LANE), lambda bi, i: (bi, 0, blk("i_k", 1))),
            pl.BlockSpec((None, tq, 4 * LANE), lambda bi, i: (bi, i, blk("a_q", 4))),
            pl.BlockSpec((None, s, LANE), lambda bi, i: (bi, 0, blk("a_k", 1))),
            pl.BlockSpec((None, s, LANE), lambda bi, i: (bi, 0, blk("a_v", 1))),
        ],
        out_specs=pl.BlockSpec((None, tq, A_HEADS * HEAD_DIM), lambda bi, i: (bi, i, 0)),
        out_shape=jax.ShapeDtypeStruct((b, s, A_HEADS * HEAD_DIM), BF16),
        scratch_shapes=[
            pltpu.VMEM((HEAD_DIM + ONES_ROWS, s), BF16),
            pltpu.VMEM((s, tq), jnp.int32),
            pltpu.VMEM((1, tq), jnp.int32),
            pltpu.VMEM((A_HEADS, s, tq), F32),
            pltpu.VMEM((A_HEADS, s, tq), BF16),
            pltpu.VMEM((s, tq), jnp.int16),
            pltpu.VMEM((s, tq), jnp.int16),
        ],
        compiler_params=_params(("parallel", "arbitrary"), 48),
        name="dsa",
    )(p3, p3, p3, p3, p3, p3)


def _dil_prefix(rows, q_ref, k_ref, vt_ref, o_ref, s_ref, p_ref, w_ref):
    tq = q_ref.shape[0]
    first_q = rows - tq
    period = max(d for _, d in DILATED_PATTERNS)
    shared = {}

    def mask_of(r0, n):
        base = first_q - r0
        d_min, d_max = base - (n - 1), base + (tq - 1)
        live = tuple((w, d) for w, d in DILATED_PATTERNS if d_min <= w)
        on_edge = d_min < 0 or any(d_max > w for w, _ in live)
        key = None if on_edge else (live, base % period)
        if key in shared:
            return shared[key]
        delta = (base + lax.broadcasted_iota(jnp.int32, (n, tq), 1)
                 - lax.broadcasted_iota(jnp.int32, (n, tq), 0))
        mult = jnp.zeros((n, tq), jnp.int32)
        for window, dilation in live:
            hit = jnp.where(delta <= window, 1, 0) if d_max > window else 1
            mult = mult + jnp.where((delta & (dilation - 1)) == 0, hit, 0)
        if d_min < 0:
            mult = jnp.where(delta >= 0, mult, 0)
        out = (mult > 0, mult.astype(F32))
        if key is not None:
            shared[key] = out
        return out

    cols = [slice(h * HEAD_DIM, (h + 1) * HEAD_DIM) for h in range(B_HEADS)]
    outs = _attend_heads([q_ref[:, c] for c in cols], lambda h, r0, n: k_ref[r0:r0 + n, cols[h]],
                         lambda h, p: jnp.dot(vt_ref[h, :, 0:rows], p, preferred_element_type=F32),
                         rows, mask_of, s_ref, p_ref, w_ref)
    for h, c in enumerate(cols):
        o_ref[:, c] = outs[h].T.astype(o_ref.dtype)


def _dil_kernel(q_ref, k_ref, v_ref, o_ref, vt_ref, s_ref, p_ref, w_ref, *, step):
    i = pl.program_id(1)

    @pl.when(i == 0)
    def _():
        for h in range(B_HEADS):
            _transpose_v(vt_ref.at[h], v_ref, h * HEAD_DIM)

    assert step == q_ref.shape[0]
    _for_causal_prefix(i, q_ref.shape[0], k_ref.shape[0], step, functools.partial(
        _dil_prefix, q_ref=q_ref, k_ref=k_ref, vt_ref=vt_ref, o_ref=o_ref,
        s_ref=s_ref, p_ref=p_ref, w_ref=w_ref))


def _dil_call(p3, tq, step):
    b, s, _ = p3.shape
    width = B_HEADS * HEAD_DIM
    return pl.pallas_call(
        functools.partial(_dil_kernel, step=step),
        grid=(b, s // tq),
        in_specs=[
            pl.BlockSpec((None, tq, width), lambda bi, i: (bi, i, _OFF["b_q"] // 4)),
            pl.BlockSpec((None, s, width), lambda bi, i: (bi, 0, _OFF["b_k"] // 4)),
            pl.BlockSpec((None, s, width), lambda bi, i: (bi, 0, _OFF["b_v"] // 4)),
        ],
        out_specs=pl.BlockSpec((None, tq, width), lambda bi, i: (bi, i, 0)),
        out_shape=jax.ShapeDtypeStruct((b, s, width), BF16),
        scratch_shapes=[
            pltpu.VMEM((B_HEADS, HEAD_DIM + ONES_ROWS, s), BF16),
            pltpu.VMEM((B_HEADS, s, tq), F32),
            pltpu.VMEM((B_HEADS, s, tq), BF16),
            pltpu.VMEM((s, tq), F32),
        ],
        compiler_params=_params(("parallel", "arbitrary"), 48),
        name="dilated",
    )(p3, p3, p3)


def _gelu_tanh(x):
    return 0.5 * x * (1.0 + jnp.tanh(np.float32(np.sqrt(2.0 / np.pi)) * (x + 0.044715 * (x * x * x))))


def _cmp_kernel(*refs):
    n = CMP_STRIDE
    xk, xv = refs[:n], refs[n:2 * n]
    pe_k, w1_k, w2_k, pe_v, w1_v, w2_v, kc_ref, vct_ref = refs[2 * n:]

    def branch(x_refs, pe_ref, w1_ref, w2_ref):
        lo = hi = None
        for j in range(n):
            xj = x_refs[j][...].astype(F32)
            a = (xj + pe_ref[j:j + 1, :]).astype(BF16)
            b = (xj + pe_ref[n + j:n + j + 1, :]).astype(BF16)
            dl = jnp.dot(a, w1_ref[j * HEAD_DIM:(j + 1) * HEAD_DIM, :], preferred_element_type=F32)
            dh = jnp.dot(b, w1_ref[(n + j) * HEAD_DIM:(n + j + 1) * HEAD_DIM, :], preferred_element_type=F32)
            lo = dl if lo is None else lo + dl
            hi = dh if hi is None else hi + dh
        nc = lo.shape[0]
        hid = lo + pltpu.roll(hi, nc - 1, 0)
        return jnp.dot(_gelu_tanh(hid).astype(BF16), w2_ref[...], preferred_element_type=F32)

    kc_ref[...] = branch(xk, pe_k, w1_k, w2_k).astype(kc_ref.dtype)
    vct_ref[...] = branch(xv, pe_v, w1_v, w2_v).T.astype(vct_ref.dtype)


def _cmp_call(pc, b, layer, pe_k, w1_k, w2_k, pe_v, w1_v, w2_v):
    nc = pc.shape[0] // b
    pc3 = pc.reshape(b, nc, CMP_STRIDE * CMP_COLS)
    x_specs = []
    for first in (0, C_KV_GROUPS):
        for j in range(CMP_STRIDE):
            x_specs.append(pl.BlockSpec((None, nc, HEAD_DIM),
                                        lambda bi, g, j=j, first=first: (bi, 0, j * CMP_BLOCKS + first + g)))
    flat = CMP_BLOCK * HEAD_DIM
    w_specs = [
        _resident((None, CMP_BLOCK, HEAD_DIM), lambda bi, g: (layer, 0, 0)),
        _resident((None, flat, CMP_HIDDEN), lambda bi, g: (layer, 0, 0)),
        _resident((None, CMP_HIDDEN, HEAD_DIM), lambda bi, g: (layer, 0, 0)),
    ]
    return pl.pallas_call(
        _cmp_kernel,
        grid=(b, C_KV_GROUPS),
        in_specs=x_specs + w_specs + w_specs,
        out_specs=[
            pl.BlockSpec((None, None, nc, HEAD_DIM), lambda bi, g: (bi, g, 0, 0)),
            pl.BlockSpec((None, None, HEAD_DIM, nc), lambda bi, g: (bi, g, 0, 0)),
        ],
        out_shape=[
            jax.ShapeDtypeStruct((b, C_KV_GROUPS, nc, HEAD_DIM), BF16),
            jax.ShapeDtypeStruct((b, C_KV_GROUPS, HEAD_DIM, nc), BF16),
        ],
        compiler_params=_params(("parallel", "parallel"), 32),
        name="nsa_compress",
    )(*([pc3] * (2 * CMP_STRIDE)), pe_k, w1_k, w2_k, pe_v, w1_v, w2_v)


def _nsa_selected(rows, t_row, q_sel, ke_ref, vst_ref, acc_ref, s_ref, p_ref, step):
    tq = q_sel[0].shape[0]

    def mask_of(r0, n):
        srow = r0 + lax.broadcasted_iota(jnp.int32, (n, tq), 0)
        return t_row - srow >= 0, None

    vt = vst_ref[:, 0:rows]
    outs = _attend_heads(q_sel, lambda h, r0, n: ke_ref[r0:r0 + n, :],
                         lambda h, p: jnp.dot(vt, p, preferred_element_type=F32),
                         rows, mask_of, s_ref, p_ref, plain_rows=rows - max(step, tq))
    for r in range(C_GROUP_SIZE):
        acc_ref[r] = outs[r]


def _nsa_kernel(q_ref, kc_ref, vct_ref, ks_ref, vs_ref, kw_ref, vw_ref, g_ref, cos_ref, sin_ref,
                isect_ref, expand_ref, o_ref, vst_ref, vwt_ref, acc_ref, s_ref, p_ref, ke_ref, *, n_sel, step):
    i = pl.program_id(2)
    tq = q_ref.shape[0]
    s_len = ks_ref.shape[0]
    nc = kc_ref.shape[0]
    n_slc = isect_ref.shape[0]

    @pl.when(i == 0)
    def _():
        _transpose_v(vst_ref, vs_ref)
        _transpose_v_tiles(vwt_ref, vw_ref)
        ke_ref[:, 0:HEAD_DIM] = ks_ref[...]
        ke_ref[:, HEAD_DIM:] = expand_ref[...]

    t_row = i * tq + lax.broadcasted_iota(jnp.int32, (1, tq), 1)

    kc = kc_ref[...]
    vct = vct_ref[...]
    cend = lax.broadcasted_iota(jnp.int32, (nc, tq), 0) * CMP_STRIDE + (CMP_BLOCK - 1)
    cmask = cend <= t_row
    o_cmp = []
    p_sum = jnp.zeros((nc, tq), F32)
    for r in range(C_GROUP_SIZE):
        q_r = q_ref[:, r * HEAD_DIM:(r + 1) * HEAD_DIM]
        sc = lax.dot_general(kc, q_r, (((1,), (1,)), ((), ())), preferred_element_type=F32)
        sc = jnp.where(cmask, sc, NEG)
        mx = jnp.max(sc, axis=0, keepdims=True)
        e = jnp.where(cmask, jnp.exp2(sc - mx), 0.0)
        den = jnp.sum(e, axis=0, keepdims=True)
        p = e / jnp.where(den > 0, den, 1.0)
        p_sum = p_sum + p
        o_cmp.append(jnp.dot(vct, p.astype(BF16), preferred_element_type=F32))

    isect = isect_ref[...]
    p_hi = p_sum.astype(BF16)
    p_lo = (p_sum - p_hi.astype(F32)).astype(BF16)
    imp = (jnp.dot(isect, p_hi, preferred_element_type=F32)
           + jnp.dot(isect, p_lo, preferred_element_type=F32))
    blk = lax.broadcasted_iota(jnp.int32, (n_slc, tq), 0)
    cur = t_row // SEL_BLOCK
    val = jnp.where(blk == 0, jnp.inf, jnp.where(blk == cur, jnp.inf, jnp.where(blk == cur - 1, jnp.inf, imp)))
    val = jnp.where(blk <= cur, val, -jnp.inf)
    rank = jnp.zeros((n_slc, tq), jnp.int32)
    for m in range(n_slc):
        vm = val[m:m + 1, :]
        before = jnp.where(vm > val, 1, jnp.where(vm == val, jnp.where(blk > m, 1, 0), 0))
        rank = rank + before
    bias = jnp.where(rank < n_sel, 0.0, NEG)
    if n_slc < LANE:
        bias = jnp.concatenate([bias, jnp.zeros((LANE - n_slc, tq), F32)], axis=0)
    bias_t = bias.T.astype(BF16)

    cos, sin = cos_ref[...], sin_ref[...]
    cols = [slice(r * HEAD_DIM, (r + 1) * HEAD_DIM) for r in range(C_GROUP_SIZE)]
    q_rot = [_rope_full(q_ref[:, c].astype(F32), cos, sin).astype(BF16) for c in cols]
    q_sel = [jnp.concatenate([q, bias_t], axis=1) for q in q_rot]

    _for_causal_prefix(i, tq, s_len, step, functools.partial(
        _nsa_selected, t_row=t_row, q_sel=q_sel, ke_ref=ke_ref, vst_ref=vst_ref,
        acc_ref=acc_ref, s_ref=s_ref, p_ref=p_ref, step=step))

    w_tiles = min((WIN_SIZE - 1 + tq - 1) // LANE + 1, s_len // LANE)
    w_rows = w_tiles * LANE
    w0 = pl.multiple_of(jnp.maximum(i * tq + tq - w_rows, 0), LANE)
    t0 = w0 // LANE

    def win_mask(r0, n):
        delta = t_row - (w0 + r0 + lax.broadcasted_iota(jnp.int32, (n, tq), 0))
        return jnp.where(delta >= 0, jnp.where(delta <= WIN_SIZE - 1, 1, 0), 0) > 0, None

    def pv_win(h, p):
        out = None
        for w in range(w_tiles):
            part = jnp.dot(vwt_ref[t0 + w], p[w * LANE:(w + 1) * LANE, :], preferred_element_type=F32)
            out = part if out is None else out + part
        return out

    o_win = _attend_heads(q_rot, lambda h, r0, n: kw_ref[pl.ds(w0 + r0, n), :], pv_win,
                          w_rows, win_mask, s_ref, p_ref)

    gates = jax.nn.sigmoid(g_ref[...].astype(F32).T)
    for r, c in enumerate(cols):
        o_t = (gates[3 * r:3 * r + 1, :] * o_cmp[r] + gates[3 * r + 1:3 * r + 2, :] * acc_ref[r]
               + gates[3 * r + 2:3 * r + 3, :] * o_win[r])
        o_ref[:, c] = o_t.T.astype(o_ref.dtype)


def _nsa_call(p3, kc, vct, cos, sin, isect, expand, tq, step):
    b, s, _ = p3.shape
    nc = kc.shape[2]
    n_slc = s // SEL_BLOCK
    width = C_GROUP_SIZE * HEAD_DIM
    kv = lambda name: pl.BlockSpec((None, s, HEAD_DIM), lambda bi, g, i, name=name: (bi, 0, _OFF[name] + g))
    return pl.pallas_call(
        functools.partial(_nsa_kernel, n_sel=min(SEL_COUNT, n_slc), step=step),
        grid=(b, C_KV_GROUPS, s // tq),
        in_specs=[
            pl.BlockSpec((None, tq, width), lambda bi, g, i: (bi, i, _OFF["c_q"] // 4 + g)),
            pl.BlockSpec((None, None, nc, HEAD_DIM), lambda bi, g, i: (bi, g, 0, 0)),
            pl.BlockSpec((None, None, HEAD_DIM, nc), lambda bi, g, i: (bi, g, 0, 0)),
            kv("c_ks"), kv("c_vs"), kv("c_kw"), kv("c_vw"),
            pl.BlockSpec((None, tq, LANE), lambda bi, g, i: (bi, i, _OFF["c_g0"] + g)),
            pl.BlockSpec((tq, LANE), lambda bi, g, i: (i, 0)),
            pl.BlockSpec((tq, LANE), lambda bi, g, i: (i, 0)),
            pl.BlockSpec(isect.shape, lambda bi, g, i: (0, 0)),
            pl.BlockSpec(expand.shape, lambda bi, g, i: (0, 0)),
        ],
        out_specs=pl.BlockSpec((None, tq, width), lambda bi, g, i: (bi, i, g)),
        out_shape=jax.ShapeDtypeStruct((b, s, C_HEADS * HEAD_DIM), BF16),
        scratch_shapes=[
            pltpu.VMEM((HEAD_DIM + ONES_ROWS, s), BF16),
            pltpu.VMEM((s // LANE, HEAD_DIM + ONES_ROWS, LANE), BF16),
            pltpu.VMEM((C_GROUP_SIZE, HEAD_DIM, tq), F32),
            pltpu.VMEM((C_GROUP_SIZE, s, tq), F32),
            pltpu.VMEM((C_GROUP_SIZE, s, tq), BF16),
            pltpu.VMEM((s, HEAD_DIM + expand.shape[1]), BF16),
        ],
        compiler_params=_params(("parallel", "parallel", "arbitrary"), 48),
        name="nsa",
    )(p3, kc, vct, p3, p3, p3, p3, p3, cos, sin, isect, expand)


def _out_kernel(oa_ref, ob_ref, oc_ref, w_ref, x_ref, gpost_ref, gnext_ref, xo_ref, h_ref):
    na, nb = oa_ref.shape[1], ob_ref.shape[1]
    y = jnp.dot(oa_ref[...], w_ref[0:na, :], preferred_element_type=F32)
    y = y + jnp.dot(ob_ref[...], w_ref[na:na + nb, :], preferred_element_type=F32)
    y = y + jnp.dot(oc_ref[...], w_ref[na + nb:, :], preferred_element_type=F32)
    x = x_ref[...] + _rms(y, gpost_ref[...])
    xo_ref[...] = x
    h_ref[...] = _rms(x, gnext_ref[...]).astype(h_ref.dtype)


def _out_call(o_a, o_b, o_c, w_out_p, layer, x, g_post, g_next, tm):
    m, d = x.shape
    row = lambda width: pl.BlockSpec((tm, width), lambda i: (i, 0))
    vec = pl.BlockSpec((1, d), lambda i: (0, 0))
    return pl.pallas_call(
        _out_kernel,
        grid=(m // tm,),
        in_specs=[
            row(o_a.shape[1]), row(o_b.shape[1]), row(o_c.shape[1]),
            _resident((None, w_out_p.shape[1], d), lambda i: (layer, 0, 0)),
            row(d), vec, vec,
        ],
        out_specs=[row(d), row(d)],
        out_shape=[jax.ShapeDtypeStruct((m, d), F32), jax.ShapeDtypeStruct((m, d), BF16)],
        compiler_params=_params(("parallel",), 48),
        name="out_proj",
    )(o_a, o_b, o_c, w_out_p, x, g_post, g_next)


def _ffn_kernel(h_ref, wg_ref, wu_ref, wd_ref, y_ref):
    @pl.when(pl.program_id(1) == 0)
    def _():
        y_ref[...] = jnp.zeros(y_ref.shape, F32)

    h = h_ref[...]
    g = jnp.dot(h, wg_ref[...], preferred_element_type=F32)
    u = jnp.dot(h, wu_ref[...], preferred_element_type=F32)
    a = (g * jax.nn.sigmoid(g) * u).astype(BF16)
    y_ref[...] += jnp.dot(a, wd_ref[...], preferred_element_type=F32)


def _ffn_call(h, wg, wu, wd, layer, tm, tf):
    m, d = h.shape
    d_ff = wg.shape[2]
    row = pl.BlockSpec((tm, d), lambda i, f: (i, 0))
    return pl.pallas_call(
        _ffn_kernel,
        grid=(m // tm, d_ff // tf),
        in_specs=[
            row,
            pl.BlockSpec((None, d, tf), lambda i, f: (layer, 0, f)),
            pl.BlockSpec((None, d, tf), lambda i, f: (layer, 0, f)),
            pl.BlockSpec((None, tf, d), lambda i, f: (layer, f, 0)),
        ],
        out_specs=row,
        out_shape=jax.ShapeDtypeStruct((m, d), F32),
        compiler_params=_params(("parallel", "arbitrary"), 52),
        name="ffn",
    )(h, wg, wu, wd)


def _residual_kernel(y_ref, x_ref, g_ref, o_ref):
    o_ref[...] = x_ref[...] + _rms(y_ref[...], g_ref[...])


def _residual_call(y, x, g, tm):
    m, d = x.shape
    row = pl.BlockSpec((tm, d), lambda i: (i, 0))
    return pl.pallas_call(
        _residual_kernel,
        grid=(m // tm,),
        in_specs=[row, row, pl.BlockSpec((1, d), lambda i: (0, 0))],
        out_specs=row,
        out_shape=jax.ShapeDtypeStruct((m, d), F32),
        compiler_params=_params(("parallel",), 32),
        name="ffn_residual",
    )(y, x, g)


def _pad_cols(w, width):
    return jnp.pad(w, ((0, 0), (0, 0), (0, width - w.shape[-1])))


def _prep_w_in(w_in):
    widths = (A_HEADS * HEAD_DIM, HEAD_DIM, HEAD_DIM, IDX_HEADS * IDX_DIM, IDX_DIM, IDX_HEADS,
              B_HEADS * HEAD_DIM, B_HEADS * HEAD_DIM, B_HEADS * HEAD_DIM, C_HEADS * HEAD_DIM,
              C_KV_GROUPS * HEAD_DIM, C_KV_GROUPS * HEAD_DIM, C_KV_GROUPS * HEAD_DIM, C_KV_GROUPS * HEAD_DIM,
              C_KV_GROUPS * HEAD_DIM, C_KV_GROUPS * HEAD_DIM, 3 * C_HEADS)
    names = ("a_q", "a_k", "a_v", "i_q", "i_k", "i_w", "b_q", "b_k", "b_v", "c_q",
             "c_kc", "c_vc", "c_ks", "c_vs", "c_kw", "c_vw", "c_g")
    offs = np.concatenate([[0], np.cumsum(widths)])
    src = {n: w_in[:, :, int(offs[k]):int(offs[k + 1])] for k, n in enumerate(names)}
    per_group = 3 * C_GROUP_SIZE
    src["c_g0"] = _pad_cols(src["c_g"][:, :, :per_group], LANE)
    src["c_g1"] = _pad_cols(src["c_g"][:, :, per_group:], LANE)
    src["i_k"] = jnp.concatenate([src["i_k"], src["i_k"]], axis=-1)
    src["i_w"] = _pad_cols(src["i_w"], LANE)
    order = [n for n, _, _, _ in _SEGMENTS] + [n for n, _ in _CMP_SEGMENTS]
    return jnp.concatenate([src[n] for n in order], axis=-1).astype(BF16)


def _rope_tables(seq):
    def tables(dim):
        inv = 1.0 / (ROPE_THETA ** (jnp.arange(0, dim, 2, dtype=F32) / dim))
        ang = jnp.arange(seq, dtype=F32)[:, None] * inv[None, :]
        cos, sin = jnp.cos(ang), jnp.sin(ang)
        reps = LANE // dim
        return (jnp.tile(jnp.concatenate([cos, cos], axis=-1), (1, reps)),
                jnp.tile(jnp.concatenate([-sin, sin], axis=-1), (1, reps)))
    return tables(HEAD_DIM) + tables(IDX_DIM)


def _selection_constants(seq):
    nc = seq // CMP_STRIDE
    n_slc = seq // SEL_BLOCK
    c_start = np.arange(nc) * CMP_STRIDE
    n_start = np.arange(n_slc) * SEL_BLOCK
    isect = ((c_start[None, :] < n_start[:, None] + SEL_BLOCK)
             & (c_start[None, :] + CMP_BLOCK > n_start[:, None])
             & (np.arange(nc)[None, :] < nc - CMP_BLOCK // CMP_STRIDE + 1)).astype(np.float32)
    expand = np.zeros((seq, max(n_slc, LANE)), np.float32)
    expand[np.arange(seq), np.arange(seq) // SEL_BLOCK] = 1.0
    return jnp.asarray(isect, BF16), jnp.asarray(expand, BF16)


def _tiles(seq):
    return dict(tq=min(2 * LANE, seq), step_dsa=min(4 * LANE, seq), step=min(2 * LANE, seq),
                tm_proj=min(256, seq), tm_out=min(512, seq), tm_ffn=min(1024, seq), tf=512)


def kernel(x, w_in, w_out, cmp_pe_k, cmp_w1_k, cmp_w2_k, cmp_pe_v, cmp_w1_v, cmp_w2_v,
           w_gate, w_up, w_down, g_pre_mix, g_post_mix, g_pre_ffn, g_post_ffn):
    b, s, d = x.shape
    depth = w_in.shape[0]
    m = b * s
    t = _tiles(s)
    tq, step = t["tq"], t["step"]

    w_in_p = _prep_w_in(w_in)
    w_out_p = w_out.astype(BF16)
    wg, wu, wd = w_gate.astype(BF16), w_up.astype(BF16), w_down.astype(BF16)
    w1_k, w2_k = cmp_w1_k.astype(BF16), cmp_w2_k.astype(BF16)
    w1_v, w2_v = cmp_w1_v.astype(BF16), cmp_w2_v.astype(BF16)
    col_scale = jnp.asarray(_COL_SCALE)
    tabs = _rope_tables(s)
    isect, expand = _selection_constants(s)
    vec = lambda g, layer: g[layer][None, :]

    xf = x.reshape(m, d)
    y = None
    for layer in range(depth):
        if layer == 0:
            p, pc = _proj_call((xf, vec(g_pre_mix, 0)), w_in_p, layer, col_scale, tabs, s, t["tm_proj"])
        else:
            rows_in = (y, xf, vec(g_post_ffn, layer - 1), vec(g_pre_mix, layer))
            xf, p, pc = _proj_call(rows_in, w_in_p, layer, col_scale, tabs, s, t["tm_proj"])
        p3 = p.reshape(b, s, NP)
        o_a = _dsa_call(p3, tq, t["step_dsa"])
        o_b = _dil_call(p3, tq, step)
        kc, vct = _cmp_call(pc, b, layer, cmp_pe_k, w1_k, w2_k, cmp_pe_v, w1_v, w2_v)
        o_c = _nsa_call(p3, kc, vct, tabs[0], tabs[1], isect, expand, tq, step)
        xf, h = _out_call(o_a.reshape(m, -1), o_b.reshape(m, -1), o_c.reshape(m, -1), w_out_p, layer,
                          xf, vec(g_post_mix, layer), vec(g_pre_ffn, layer), t["tm_out"])
        y = _ffn_call(h, wg, wu, wd, layer, t["tm_ffn"], t["tf"])
    return _residual_call(y, xf, vec(g_post_ffn, depth - 1), t["tm_out"]).reshape(b, s, d)
```

```python
import functools

import numpy as np
import jax
import jax.numpy as jnp
from jax import lax
from jax.experimental import pallas as pl
from jax.experimental.pallas import tpu as pltpu

F32 = jnp.float32
BF16 = jnp.bfloat16

LANE = 128
SUBLANE = 8
HEAD_DIM = 128
A_HEADS = 4
B_HEADS = 4
C_HEADS = 8
C_KV_GROUPS = 2
C_GROUP_SIZE = C_HEADS // C_KV_GROUPS
IDX_HEADS = 16
IDX_DIM = 64
DSA_TOPK_MAX = 256
DILATED_PATTERNS = ((128, 1), (512, 4), (2048, 16))
CMP_BLOCK = 32
CMP_STRIDE = 16
CMP_HIDDEN = 256
SEL_BLOCK = 64
SEL_COUNT = 16
WIN_SIZE = 512
ROPE_THETA = 10000.0
RMS_EPS = 1e-6
ATTN_SCALE = HEAD_DIM ** -0.5 * float(np.log2(np.e))
IDX_SCALE = IDX_DIM ** -0.5 * IDX_HEADS ** -0.5

NEG = -1e30
INT_MIN = -(2 ** 31)
HALF16 = 2 ** 15
COUNT_ROWS = 8 * SUBLANE
SCORE_ROWS = 256
ATT_ROWS = 128
ONES_ROWS = 16
OUT_SUB_ROWS = 256

MODE_NONE, MODE_ROPE, MODE_IROPE = 0, 1, 2

_SEGMENTS = (
    ("i_q", 8, MODE_IROPE, IDX_SCALE),
    ("c_q", 8, MODE_NONE, ATTN_SCALE),
    ("a_q", 4, MODE_ROPE, ATTN_SCALE),
    ("b_q", 4, MODE_ROPE, ATTN_SCALE),
    ("b_k", 4, MODE_ROPE, 1.0),
    ("b_v", 4, MODE_NONE, 1.0),
    ("c_ks", 2, MODE_ROPE, 1.0),
    ("c_vs", 2, MODE_NONE, 1.0),
    ("c_kw", 2, MODE_ROPE, 1.0),
    ("c_vw", 2, MODE_NONE, 1.0),
    ("a_k", 1, MODE_ROPE, 1.0),
    ("a_v", 1, MODE_NONE, 1.0),
    ("i_k", 1, MODE_IROPE, 1.0),
    ("i_w", 1, MODE_NONE, 1.0),
    ("c_g0", 1, MODE_NONE, 1.0),
    ("c_g1", 1, MODE_NONE, 1.0),
)
_CMP_SEGMENTS = (("c_kc", 2), ("c_vc", 2))
_OFF = {}
_o = 0
for _n, _w, _m, _s in _SEGMENTS:
    _OFF[_n] = _o
    _o += _w
NP_BLOCKS = _o
NP = NP_BLOCKS * LANE
CMP_BLOCKS = sum(w for _, w in _CMP_SEGMENTS)
CMP_COLS = CMP_BLOCKS * LANE
NW = NP + CMP_COLS
_BLOCK_MODE = tuple(m for _, w, m, _ in _SEGMENTS for _ in range(w))
_COL_SCALE = np.repeat(np.array([s for _, w, _, s in _SEGMENTS for _ in range(w)], np.float32), LANE)[None, :]
_PROJ_CHUNK = 4


def _params(sem, vmem_mb):
    return pltpu.CompilerParams(dimension_semantics=sem, vmem_limit_bytes=vmem_mb * 1024 * 1024)


def _resident(block_shape, index_map):
    return pl.BlockSpec(block_shape, index_map, pipeline_mode=pl.Buffered(1))


def _rms(x, g):
    return x * lax.rsqrt(jnp.mean(x * x, axis=-1, keepdims=True) + RMS_EPS) * g


def _rope_full(a, cos, sin):
    return a * cos + pltpu.roll(a, HEAD_DIM // 2, 1) * sin


def _rope_idx(a, cos, sin, first_half):
    partner = jnp.where(first_half, pltpu.roll(a, LANE - IDX_DIM // 2, 1), pltpu.roll(a, IDX_DIM // 2, 1))
    return a * cos + partner * sin


def _proj_kernel(*refs, after_ffn):
    if after_ffn:
        (y_ref, x_ref, gpost_ref, gpre_ref, w_ref, cs_ref, cos_ref, sin_ref, icos_ref, isin_ref,
         xo_ref, o_ref, oc_ref, cmp_ref) = refs
        x = x_ref[...] + _rms(y_ref[...], gpost_ref[...])
        xo_ref[...] = x
        h = _rms(x, gpre_ref[...]).astype(BF16)
    else:
        x_ref, gpre_ref, w_ref, cs_ref, cos_ref, sin_ref, icos_ref, isin_ref, o_ref, oc_ref, cmp_ref = refs
        h = _rms(x_ref[...], gpre_ref[...]).astype(BF16)
    tm = h.shape[0]
    lane = lax.broadcasted_iota(jnp.int32, (tm, LANE), 1)
    first_half = (lane & (IDX_DIM - 1)) < IDX_DIM // 2
    for c0 in range(0, NP_BLOCKS, _PROJ_CHUNK):
        nb = min(_PROJ_CHUNK, NP_BLOCKS - c0)
        acc = jnp.dot(h, w_ref[:, c0 * LANE:(c0 + nb) * LANE], preferred_element_type=F32)
        for b in range(nb):
            col = slice((c0 + b) * LANE, (c0 + b + 1) * LANE)
            a = acc[:, b * LANE:(b + 1) * LANE] * cs_ref[:, col]
            mode = _BLOCK_MODE[c0 + b]
            if mode == MODE_ROPE:
                a = _rope_full(a, cos_ref[...], sin_ref[...])
            elif mode == MODE_IROPE:
                a = _rope_idx(a, icos_ref[...], isin_ref[...], first_half)
            o_ref[:, col] = a.astype(o_ref.dtype)
    acc = jnp.dot(h, w_ref[:, NP:NW], preferred_element_type=F32)
    for b in range(CMP_BLOCKS):
        cmp_ref[b] = acc[:, b * LANE:(b + 1) * LANE]
    for j in range(CMP_STRIDE):
        for b in range(CMP_BLOCKS):
            rows = cmp_ref[b, pl.ds(j, tm // CMP_STRIDE, stride=CMP_STRIDE), :]
            oc_ref[:, j * CMP_COLS + b * LANE:j * CMP_COLS + (b + 1) * LANE] = rows.astype(oc_ref.dtype)


def _proj_call(rows_in, w_in_p, layer, col_scale, tabs, seq, tm):
    after_ffn = len(rows_in) > 2
    m, d = rows_in[0].shape
    tpb = seq // tm
    tab_spec = pl.BlockSpec((tm, LANE), lambda i: (i % tpb, 0))
    row = pl.BlockSpec((tm, d), lambda i: (i, 0))
    vec = pl.BlockSpec((1, d), lambda i: (0, 0))
    out_specs = [
        pl.BlockSpec((tm, NP), lambda i: (i, 0)),
        pl.BlockSpec((tm // CMP_STRIDE, CMP_STRIDE * CMP_COLS), lambda i: (i, 0)),
    ]
    out_shape = [
        jax.ShapeDtypeStruct((m, NP), BF16),
        jax.ShapeDtypeStruct((m // CMP_STRIDE, CMP_STRIDE * CMP_COLS), BF16),
    ]
    if after_ffn:
        out_specs = [row] + out_specs
        out_shape = [jax.ShapeDtypeStruct((m, d), F32)] + out_shape
    return pl.pallas_call(
        functools.partial(_proj_kernel, after_ffn=after_ffn),
        grid=(m // tm,),
        in_specs=([row, row, vec, vec] if after_ffn else [row, vec]) + [
            _resident((None, d, NW), lambda i: (layer, 0, 0)),
            _resident((1, NP), lambda i: (0, 0)),
            tab_spec, tab_spec, tab_spec, tab_spec,
        ],
        out_specs=out_specs,
        out_shape=out_shape,
        scratch_shapes=[pltpu.VMEM((CMP_BLOCKS, tm, LANE), F32)],
        compiler_params=_params(("parallel",), 54),
        name="in_proj",
    )(*rows_in, w_in_p, col_scale, *tabs)


def _transpose_v(dst_ref, src_ref, col0=0):
    rows = src_ref.shape[0]
    for r0 in range(0, rows, LANE):
        tile = src_ref[r0:r0 + LANE, col0:col0 + HEAD_DIM].astype(F32)
        dst_ref[0:HEAD_DIM, r0:r0 + LANE] = tile.T.astype(dst_ref.dtype)
    dst_ref[HEAD_DIM:, :] = jnp.ones((ONES_ROWS, rows), dst_ref.dtype)


def _transpose_v_tiles(dst_ref, src_ref):
    rows, _ = src_ref.shape
    for r in range(rows // LANE):
        tile = src_ref[r * LANE:(r + 1) * LANE, :].astype(F32)
        dst_ref[r, 0:HEAD_DIM, :] = tile.T.astype(dst_ref.dtype)
        dst_ref[r, HEAD_DIM:, :] = jnp.ones((ONES_ROWS, LANE), dst_ref.dtype)


def _for_causal_prefix(i, tq, s_len, step, body):
    n = ((i + 1) * tq + step - 1) // step
    for v in range(1, s_len // step + 1):
        pl.when(n == v)(functools.partial(body, v * step))


def _fold(x, op):
    out = x[0:COUNT_ROWS]
    for r0 in range(COUNT_ROWS, x.shape[0], COUNT_ROWS):
        out = op(out, x[r0:r0 + COUNT_ROWS])
    return out


def _fold_tree(x, op):
    while x.shape[0] > SUBLANE:
        half = x.shape[0] // 2
        x = op(x[:half], x[half:])
    return x


def _row_chunk(rows):
    return SCORE_ROWS if rows % SCORE_ROWS == 0 else LANE


def _attend_heads(qs, k_of, pv_of, rows, mask_of, s_ref, p_ref, w_ref=None, plain_rows=0):
    n_heads = len(qs)
    tq = qs[0].shape[0]
    chunk = ATT_ROWS * LANE // tq
    for h in range(n_heads):
        s_ref[h, 0:rows, :] = lax.dot_general(k_of(h, 0, rows), qs[h], (((1,), (1,)), ((), ())),
                                              preferred_element_type=F32)
    m_acc = [jnp.full((SUBLANE, tq), NEG, F32) for _ in qs]
    weighted = False
    for r0 in range(0, rows, chunk):
        if r0 + chunk <= plain_rows:
            for h in range(n_heads):
                m_acc[h] = jnp.maximum(m_acc[h], _fold_tree(s_ref[h, r0:r0 + chunk, :], jnp.maximum))
            continue
        mask, weight = mask_of(r0, chunk)
        if weight is not None:
            weighted = True
            w_ref[r0:r0 + chunk, :] = weight
        for h in range(n_heads):
            s = jnp.where(mask, s_ref[h, r0:r0 + chunk, :], NEG)
            s_ref[h, r0:r0 + chunk, :] = s
            m_acc[h] = jnp.maximum(m_acc[h], _fold_tree(s, jnp.maximum))
    outs = []
    for h in range(n_heads):
        mx = jnp.max(m_acc[h], axis=0, keepdims=True)
        for r0 in range(0, rows, chunk):
            p = jnp.exp2(s_ref[h, r0:r0 + chunk, :] - mx)
            if weighted:
                p = p * w_ref[r0:r0 + chunk, :]
            p_ref[h, r0:r0 + chunk, :] = p.astype(p_ref.dtype)
        pv = pv_of(h, p_ref[h, 0:rows, :])
        outs.append(pv[0:HEAD_DIM] / pv[HEAD_DIM:HEAD_DIM + 1])
    return outs


def _count_rows(key_ref, rows, pred):
    chunk = _row_chunk(rows)
    acc = None
    for r0 in range(0, rows, chunk):
        part = _fold(pred(key_ref[r0:r0 + chunk, :], r0), jnp.add)
        acc = part if acc is None else acc + part
    return jnp.sum(acc.astype(jnp.int32), axis=0, keepdims=True)


def _dsa_prefix(rows, i, iq_ref, iw_t, ik_ref, q_ref, k_ref, vt_ref, o_ref, key_ref, cut_ref, s_ref, p_ref,
                hi_ref, lo_ref, topk):
    tq = q_ref.shape[0]
    chunk = min(SCORE_ROWS * LANE // tq, rows)
    lane = lax.broadcasted_iota(jnp.int32, (chunk, LANE), 1)
    crow = lax.broadcasted_iota(jnp.int32, (chunk, tq), 0)
    ctcol = i * tq + lax.broadcasted_iota(jnp.int32, (chunk, tq), 1)

    for r0 in range(0, rows, chunk):
        ik = ik_ref[r0:r0 + chunk, :]
        halves = (jnp.where(lane < IDX_DIM, ik, jnp.zeros_like(ik)),
                  jnp.where(lane >= IDX_DIM, ik, jnp.zeros_like(ik)))
        score = jnp.zeros((chunk, tq), F32)
        for p in range(IDX_HEADS // 2):
            blk = iq_ref[:, p * LANE:(p + 1) * LANE]
            for half in range(2):
                j = 2 * p + half
                lg = lax.dot_general(halves[half], blk, (((1,), (1,)), ((), ())), preferred_element_type=F32)
                score = score + jnp.maximum(lg, 0.0) * iw_t[j:j + 1, :]
        bits = pltpu.bitcast(score, jnp.int32)
        key = bits ^ ((bits >> 31) & 0x7FFFFFFF)
        key = jnp.where(r0 + crow <= ctcol, key, INT_MIN)
        key_ref[r0:r0 + chunk, :] = key
        hi_ref[r0:r0 + chunk, :] = (key >> 16).astype(jnp.int16)
        lo_ref[r0:r0 + chunk, :] = ((key & 0xFFFF) - HALF16).astype(jnp.int16)

    def search16(ref, target):
        def bit_step(b, off):
            cand_off = off | jnp.left_shift(jnp.int32(1), 15 - b)
            cand = (cand_off - HALF16).astype(jnp.int16)
            cnt = _count_rows(ref, rows, lambda c, r0: jnp.where(c >= cand, jnp.int16(1), jnp.int16(0)))
            return jnp.where(cnt >= target, cand_off, off)

        return lax.fori_loop(0, 16, bit_step, jnp.zeros((1, tq), jnp.int32)) - HALF16

    thr_hi = search16(hi_ref, topk)
    thr_hi16 = thr_hi.astype(jnp.int16)
    above = _count_rows(hi_ref, rows, lambda c, r0: jnp.where(c > thr_hi16, jnp.int16(1), jnp.int16(0)))
    chunk16 = _row_chunk(rows)
    for r0 in range(0, rows, chunk16):
        same = hi_ref[r0:r0 + chunk16, :] == thr_hi16
        hi_ref[r0:r0 + chunk16, :] = jnp.where(same, lo_ref[r0:r0 + chunk16, :], jnp.int16(-HALF16))
    thr_lo = search16(hi_ref, topk - above)
    thr = jnp.left_shift(thr_hi, 16) | (thr_lo + HALF16)
    short = thr == INT_MIN
    need = topk - _count_rows(key_ref, rows, lambda kc, r0: jnp.where(kc > thr, 1, 0))
    n_eq = _count_rows(key_ref, rows, lambda kc, r0: jnp.where(kc == thr, 1, 0))
    excess = jnp.where(n_eq > need, jnp.where(short, 0, 1), 0)
    cut_ref[...] = jnp.where(short, -1, rows)

    def row_ids(r0, n):
        return r0 + lax.broadcasted_iota(jnp.int32, (n, tq), 0)

    @pl.when(jnp.max(excess) > 0)
    def _():
        n_bits = max(1, (rows - 1).bit_length())

        def idx_step(b, cut):
            cand = cut | jnp.left_shift(jnp.int32(1), n_bits - 1 - b)
            below = _count_rows(key_ref, rows, lambda kc, r0: jnp.where(
                kc == thr, jnp.where(row_ids(r0, kc.shape[0]) < cand, 1, 0), 0))
            return jnp.where(below < need, cand, cut)

        cut = lax.fori_loop(0, n_bits, idx_step, jnp.zeros((1, tq), jnp.int32))
        cut_ref[...] = jnp.where(excess > 0, cut, cut_ref[...])

    cut = cut_ref[...]

    def mask_of(r0, n):
        kc = key_ref[r0:r0 + n, :]
        chosen = jnp.where(kc > thr, 1, jnp.where(kc == thr, jnp.where(row_ids(r0, n) <= cut, 1, 0), 0))
        return chosen > 0, None

    vt = vt_ref[:, 0:rows]
    cols = [slice(h * HEAD_DIM, (h + 1) * HEAD_DIM) for h in range(A_HEADS)]
    outs = _attend_heads([q_ref[:, c] for c in cols], lambda h, r0, n: k_ref[r0:r0 + n, :],
                         lambda h, p: jnp.dot(vt, p, preferred_element_type=F32),
                         rows, mask_of, s_ref, p_ref)
    for h, c in enumerate(cols):
        o_ref[:, c] = outs[h].T.astype(o_ref.dtype)


def _dsa_kernel(iq_ref, iw_ref, ik_ref, q_ref, k_ref, v_ref, o_ref, vt_ref, key_ref, cut_ref, s_ref, p_ref,
                hi_ref, lo_ref, *, topk, step):
    i = pl.program_id(1)
    tq = q_ref.shape[0]
    s_len = k_ref.shape[0]

    @pl.when(i == 0)
    def _():
        _transpose_v(vt_ref, v_ref)

    iw_t = iw_ref[...].astype(F32).T[:IDX_HEADS, :]
    _for_causal_prefix(i, tq, s_len, step, functools.partial(
        _dsa_prefix, i=i, iq_ref=iq_ref, iw_t=iw_t, ik_ref=ik_ref, q_ref=q_ref, k_ref=k_ref, vt_ref=vt_ref,
        o_ref=o_ref, key_ref=key_ref, cut_ref=cut_ref, s_ref=s_ref, p_ref=p_ref,
        hi_ref=hi_ref, lo_ref=lo_ref, topk=topk))


def _dsa_call(p3, tq, step):
    b, s, _ = p3.shape
    topk = min(DSA_TOPK_MAX, s // 4)
    blk = lambda name, width: _OFF[name] // width
    return pl.pallas_call(
        functools.partial(_dsa_kernel, topk=topk, step=step),
        grid=(b, s // tq),
        in_specs=[
            pl.BlockSpec((None, tq, 8 * LANE), lambda bi, i: (bi, i, blk("i_q", 8))),
            pl.BlockSpec((None, tq, LANE), lambda bi, i: (bi, i, blk("i_w", 1))),
            pl.BlockSpec((None, s, LANE), lambda bi, i: (bi, 0, blk("i_k", 1))),
            pl.BlockSpec((None, tq, 4 * LANE), lambda bi, i: (bi, i, blk("a_q", 4))),
            pl.BlockSpec((None, s, LANE), lambda bi, i: (bi, 0, blk("a_k", 1))),
            pl.BlockSpec((None, s, LANE), lambda bi, i: (bi, 0, blk("a_v", 1))),
        ],
        out_specs=pl.BlockSpec((None, tq, A_HEADS * HEAD_DIM), lambda bi, i: (bi, i, 0)),
        out_shape=jax.ShapeDtypeStruct((b, s, A_HEADS * HEAD_DIM), BF16),
        scratch_shapes=[
            pltpu.VMEM((HEAD_DIM + ONES_ROWS, s), BF16),
            pltpu.VMEM((s, tq), jnp.int32),
            pltpu.VMEM((1, tq), jnp.int32),
            pltpu.VMEM((A_HEADS, s, tq), F32),
            pltpu.VMEM((A_HEADS, s, tq), BF16),
            pltpu.VMEM((s, tq), jnp.int16),
            pltpu.VMEM((s, tq), jnp.int16),
        ],
        compiler_params=_params(("parallel", "arbitrary"), 48),
        name="dsa",
    )(p3, p3, p3, p3, p3, p3)


def _dil_prefix(rows, q_ref, k_ref, vt_ref, o_ref, s_ref, p_ref, w_ref):
    tq = q_ref.shape[0]
    first_q = rows - tq
    period = max(d for _, d in DILATED_PATTERNS)
    shared = {}

    def mask_of(r0, n):
        base = first_q - r0
        d_min, d_max = base - (n - 1), base + (tq - 1)
        live = tuple((w, d) for w, d in DILATED_PATTERNS if d_min <= w)
        on_edge = d_min < 0 or any(d_max > w for w, _ in live)
        key = None if on_edge else (live, base % period)
        if key in shared:
            return shared[key]
        delta = (base + lax.broadcasted_iota(jnp.int32, (n, tq), 1)
                 - lax.broadcasted_iota(jnp.int32, (n, tq), 0))
        mult = jnp.zeros((n, tq), jnp.int32)
        for window, dilation in live:
            hit = jnp.where(delta <= window, 1, 0) if d_max > window else 1
            mult = mult + jnp.where((delta & (dilation - 1)) == 0, hit, 0)
        if d_min < 0:
            mult = jnp.where(delta >= 0, mult, 0)
        out = (mult > 0, mult.astype(F32))
        if key is not None:
            shared[key] = out
        return out

    cols = [slice(h * HEAD_DIM, (h + 1) * HEAD_DIM) for h in range(B_HEADS)]
    outs = _attend_heads([q_ref[:, c] for c in cols], lambda h, r0, n: k_ref[r0:r0 + n, cols[h]],
                         lambda h, p: jnp.dot(vt_ref[h, :, 0:rows], p, preferred_element_type=F32),
                         rows, mask_of, s_ref, p_ref, w_ref)
    for h, c in enumerate(cols):
        o_ref[:, c] = outs[h].T.astype(o_ref.dtype)


def _dil_kernel(q_ref, k_ref, v_ref, o_ref, vt_ref, s_ref, p_ref, w_ref, *, step):
    i = pl.program_id(1)

    @pl.when(i == 0)
    def _():
        for h in range(B_HEADS):
            _transpose_v(vt_ref.at[h], v_ref, h * HEAD_DIM)

    assert step == q_ref.shape[0]
    _for_causal_prefix(i, q_ref.shape[0], k_ref.shape[0], step, functools.partial(
        _dil_prefix, q_ref=q_ref, k_ref=k_ref, vt_ref=vt_ref, o_ref=o_ref,
        s_ref=s_ref, p_ref=p_ref, w_ref=w_ref))


def _dil_call(p3, tq, step):
    b, s, _ = p3.shape
    width = B_HEADS * HEAD_DIM
    return pl.pallas_call(
        functools.partial(_dil_kernel, step=step),
        grid=(b, s // tq),
        in_specs=[
            pl.BlockSpec((None, tq, width), lambda bi, i: (bi, i, _OFF["b_q"] // 4)),
            pl.BlockSpec((None, s, width), lambda bi, i: (bi, 0, _OFF["b_k"] // 4)),
            pl.BlockSpec((None, s, width), lambda bi, i: (bi, 0, _OFF["b_v"] // 4)),
        ],
        out_specs=pl.BlockSpec((None, tq, width), lambda bi, i: (bi, i, 0)),
        out_shape=jax.ShapeDtypeStruct((b, s, width), BF16),
        scratch_shapes=[
            pltpu.VMEM((B_HEADS, HEAD_DIM + ONES_ROWS, s), BF16),
            pltpu.VMEM((B_HEADS, s, tq), F32),
            pltpu.VMEM((B_HEADS, s, tq), BF16),
            pltpu.VMEM((s, tq), F32),
        ],
        compiler_params=_params(("parallel", "arbitrary"), 48),
        name="dilated",
    )(p3, p3, p3)


def _gelu_tanh(x):
    return 0.5 * x * (1.0 + jnp.tanh(np.float32(np.sqrt(2.0 / np.pi)) * (x + 0.044715 * (x * x * x))))


def _cmp_kernel(*refs):
    n = CMP_STRIDE
    xk, xv = refs[:n], refs[n:2 * n]
    pe_k, w1_k, w2_k, pe_v, w1_v, w2_v, kc_ref, vct_ref = refs[2 * n:]

    def branch(x_refs, pe_ref, w1_ref, w2_ref):
        lo = hi = None
        for j in range(n):
            xj = x_refs[j][...].astype(F32)
            a = (xj + pe_ref[j:j + 1, :]).astype(BF16)
            b = (xj + pe_ref[n + j:n + j + 1, :]).astype(BF16)
            dl = jnp.dot(a, w1_ref[j * HEAD_DIM:(j + 1) * HEAD_DIM, :], preferred_element_type=F32)
            dh = jnp.dot(b, w1_ref[(n + j) * HEAD_DIM:(n + j + 1) * HEAD_DIM, :], preferred_element_type=F32)
            lo = dl if lo is None else lo + dl
            hi = dh if hi is None else hi + dh
        nc = lo.shape[0]
        hid = lo + pltpu.roll(hi, nc - 1, 0)
        return jnp.dot(_gelu_tanh(hid).astype(BF16), w2_ref[...], preferred_element_type=F32)

    kc_ref[...] = branch(xk, pe_k, w1_k, w2_k).astype(kc_ref.dtype)
    vct_ref[...] = branch(xv, pe_v, w1_v, w2_v).T.astype(vct_ref.dtype)


def _cmp_call(pc, b, layer, pe_k, w1_k, w2_k, pe_v, w1_v, w2_v):
    nc = pc.shape[0] // b
    pc3 = pc.reshape(b, nc, CMP_STRIDE * CMP_COLS)
    x_specs = []
    for first in (0, C_KV_GROUPS):
        for j in range(CMP_STRIDE):
            x_specs.append(pl.BlockSpec((None, nc, HEAD_DIM),
                                        lambda bi, g, j=j, first=first: (bi, 0, j * CMP_BLOCKS + first + g)))
    flat = CMP_BLOCK * HEAD_DIM
    w_specs = [
        _resident((None, CMP_BLOCK, HEAD_DIM), lambda bi, g: (layer, 0, 0)),
        _resident((None, flat, CMP_HIDDEN), lambda bi, g: (layer, 0, 0)),
        _resident((None, CMP_HIDDEN, HEAD_DIM), lambda bi, g: (layer, 0, 0)),
    ]
    return pl.pallas_call(
        _cmp_kernel,
        grid=(b, C_KV_GROUPS),
        in_specs=x_specs + w_specs + w_specs,
        out_specs=[
            pl.BlockSpec((None, None, nc, HEAD_DIM), lambda bi, g: (bi, g, 0, 0)),
            pl.BlockSpec((None, None, HEAD_DIM, nc), lambda bi, g: (bi, g, 0, 0)),
        ],
        out_shape=[
            jax.ShapeDtypeStruct((b, C_KV_GROUPS, nc, HEAD_DIM), BF16),
            jax.ShapeDtypeStruct((b, C_KV_GROUPS, HEAD_DIM, nc), BF16),
        ],
        compiler_params=_params(("parallel", "parallel"), 32),
        name="nsa_compress",
    )(*([pc3] * (2 * CMP_STRIDE)), pe_k, w1_k, w2_k, pe_v, w1_v, w2_v)


def _nsa_selected(rows, t_row, q_sel, ke_ref, vst_ref, acc_ref, s_ref, p_ref, step):
    tq = q_sel[0].shape[0]

    def mask_of(r0, n):
        srow = r0 + lax.broadcasted_iota(jnp.int32, (n, tq), 0)
        return t_row - srow >= 0, None

    vt = vst_ref[:, 0:rows]
    outs = _attend_heads(q_sel, lambda h, r0, n: ke_ref[r0:r0 + n, :],
                         lambda h, p: jnp.dot(vt, p, preferred_element_type=F32),
                         rows, mask_of, s_ref, p_ref, plain_rows=rows - max(step, tq))
    for r in range(C_GROUP_SIZE):
        acc_ref[r] = outs[r]


def _nsa_kernel(q_ref, kc_ref, vct_ref, ks_ref, vs_ref, kw_ref, vw_ref, g_ref, cos_ref, sin_ref,
                isect_ref, expand_ref, o_ref, vst_ref, vwt_ref, acc_ref, s_ref, p_ref, ke_ref, *, n_sel, step):
    i = pl.program_id(2)
    tq = q_ref.shape[0]
    s_len = ks_ref.shape[0]
    nc = kc_ref.shape[0]
    n_slc = isect_ref.shape[0]

    @pl.when(i == 0)
    def _():
        _transpose_v(vst_ref, vs_ref)
        _transpose_v_tiles(vwt_ref, vw_ref)
        ke_ref[:, 0:HEAD_DIM] = ks_ref[...]
        ke_ref[:, HEAD_DIM:] = expand_ref[...]

    t_row = i * tq + lax.broadcasted_iota(jnp.int32, (1, tq), 1)

    kc = kc_ref[...]
    vct = vct_ref[...]
    cend = lax.broadcasted_iota(jnp.int32, (nc, tq), 0) * CMP_STRIDE + (CMP_BLOCK - 1)
    cmask = cend <= t_row
    o_cmp = []
    p_sum = jnp.zeros((nc, tq), F32)
    for r in range(C_GROUP_SIZE):
        q_r = q_ref[:, r * HEAD_DIM:(r + 1) * HEAD_DIM]
        sc = lax.dot_general(kc, q_r, (((1,), (1,)), ((), ())), preferred_element_type=F32)
        sc = jnp.where(cmask, sc, NEG)
        mx = jnp.max(sc, axis=0, keepdims=True)
        e = jnp.where(cmask, jnp.exp2(sc - mx), 0.0)
        den = jnp.sum(e, axis=0, keepdims=True)
        p = e / jnp.where(den > 0, den, 1.0)
        p_sum = p_sum + p
        o_cmp.append(jnp.dot(vct, p.astype(BF16), preferred_element_type=F32))

    isect = isect_ref[...]
    p_hi = p_sum.astype(BF16)
    p_lo = (p_sum - p_hi.astype(F32)).astype(BF16)
    imp = (jnp.dot(isect, p_hi, preferred_element_type=F32)
           + jnp.dot(isect, p_lo, preferred_element_type=F32))
    blk = lax.broadcasted_iota(jnp.int32, (n_slc, tq), 0)
    cur = t_row // SEL_BLOCK
    val = jnp.where(blk == 0, jnp.inf, jnp.where(blk == cur, jnp.inf, jnp.where(blk == cur - 1, jnp.inf, imp)))
    val = jnp.where(blk <= cur, val, -jnp.inf)
    rank = jnp.zeros((n_slc, tq), jnp.int32)
    for m in range(n_slc):
        vm = val[m:m + 1, :]
        before = jnp.where(vm > val, 1, jnp.where(vm == val, jnp.where(blk > m, 1, 0), 0))
        rank = rank + before
    bias = jnp.where(rank < n_sel, 0.0, NEG)
    if n_slc < LANE:
        bias = jnp.concatenate([bias, jnp.zeros((LANE - n_slc, tq), F32)], axis=0)
    bias_t = bias.T.astype(BF16)

    cos, sin = cos_ref[...], sin_ref[...]
    cols = [slice(r * HEAD_DIM, (r + 1) * HEAD_DIM) for r in range(C_GROUP_SIZE)]
    q_rot = [_rope_full(q_ref[:, c].astype(F32), cos, sin).astype(BF16) for c in cols]
    q_sel = [jnp.concatenate([q, bias_t], axis=1) for q in q_rot]

    _for_causal_prefix(i, tq, s_len, step, functools.partial(
        _nsa_selected, t_row=t_row, q_sel=q_sel, ke_ref=ke_ref, vst_ref=vst_ref,
        acc_ref=acc_ref, s_ref=s_ref, p_ref=p_ref, step=step))

    w_tiles = min((WIN_SIZE - 1 + tq - 1) // LANE + 1, s_len // LANE)
    w_rows = w_tiles * LANE
    w0 = pl.multiple_of(jnp.maximum(i * tq + tq - w_rows, 0), LANE)
    t0 = w0 // LANE

    def win_mask(r0, n):
        delta = t_row - (w0 + r0 + lax.broadcasted_iota(jnp.int32, (n, tq), 0))
        return jnp.where(delta >= 0, jnp.where(delta <= WIN_SIZE - 1, 1, 0), 0) > 0, None

    def pv_win(h, p):
        out = None
        for w in range(w_tiles):
            part = jnp.dot(vwt_ref[t0 + w], p[w * LANE:(w + 1) * LANE, :], preferred_element_type=F32)
            out = part if out is None else out + part
        return out

    o_win = _attend_heads(q_rot, lambda h, r0, n: kw_ref[pl.ds(w0 + r0, n), :], pv_win,
                          w_rows, win_mask, s_ref, p_ref)

    gates = jax.nn.sigmoid(g_ref[...].astype(F32).T)
    for r, c in enumerate(cols):
        o_t = (gates[3 * r:3 * r + 1, :] * o_cmp[r] + gates[3 * r + 1:3 * r + 2, :] * acc_ref[r]
               + gates[3 * r + 2:3 * r + 3, :] * o_win[r])
        o_ref[:, c] = o_t.T.astype(o_ref.dtype)


def _nsa_call(p3, kc, vct, cos, sin, isect, expand, tq, step):
    b, s, _ = p3.shape
    nc = kc.shape[2]
    n_slc = s // SEL_BLOCK
    width = C_GROUP_SIZE * HEAD_DIM
    kv = lambda name: pl.BlockSpec((None, s, HEAD_DIM), lambda bi, g, i, name=name: (bi, 0, _OFF[name] + g))
    return pl.pallas_call(
        functools.partial(_nsa_kernel, n_sel=min(SEL_COUNT, n_slc), step=step),
        grid=(b, C_KV_GROUPS, s // tq),
        in_specs=[
            pl.BlockSpec((None, tq, width), lambda bi, g, i: (bi, i, _OFF["c_q"] // 4 + g)),
            pl.BlockSpec((None, None, nc, HEAD_DIM), lambda bi, g, i: (bi, g, 0, 0)),
            pl.BlockSpec((None, None, HEAD_DIM, nc), lambda bi, g, i: (bi, g, 0, 0)),
            kv("c_ks"), kv("c_vs"), kv("c_kw"), kv("c_vw"),
            pl.BlockSpec((None, tq, LANE), lambda bi, g, i: (bi, i, _OFF["c_g0"] + g)),
            pl.BlockSpec((tq, LANE), lambda bi, g, i: (i, 0)),
            pl.BlockSpec((tq, LANE), lambda bi, g, i: (i, 0)),
            pl.BlockSpec(isect.shape, lambda bi, g, i: (0, 0)),
            pl.BlockSpec(expand.shape, lambda bi, g, i: (0, 0)),
        ],
        out_specs=pl.BlockSpec((None, tq, width), lambda bi, g, i: (bi, i, g)),
        out_shape=jax.ShapeDtypeStruct((b, s, C_HEADS * HEAD_DIM), BF16),
        scratch_shapes=[
            pltpu.VMEM((HEAD_DIM + ONES_ROWS, s), BF16),
            pltpu.VMEM((s // LANE, HEAD_DIM + ONES_ROWS, LANE), BF16),
            pltpu.VMEM((C_GROUP_SIZE, HEAD_DIM, tq), F32),
            pltpu.VMEM((C_GROUP_SIZE, s, tq), F32),
            pltpu.VMEM((C_GROUP_SIZE, s, tq), BF16),
            pltpu.VMEM((s, HEAD_DIM + expand.shape[1]), BF16),
        ],
        compiler_params=_params(("parallel", "parallel", "arbitrary"), 48),
        name="nsa",
    )(p3, kc, vct, p3, p3, p3, p3, p3, cos, sin, isect, expand)


def _out_kernel(oa_ref, ob_ref, oc_ref, w_ref, x_ref, gpost_ref, gnext_ref, xo_ref, h_ref):
    na, nb = oa_ref.shape[1], ob_ref.shape[1]
    tm = x_ref.shape[0]
    sub = min(OUT_SUB_ROWS, tm)
    for r0 in range(0, tm, sub):
        r = slice(r0, r0 + sub)
        y = jnp.dot(oa_ref[r, :], w_ref[0:na, :], preferred_element_type=F32)
        y = y + jnp.dot(ob_ref[r, :], w_ref[na:na + nb, :], preferred_element_type=F32)
        y = y + jnp.dot(oc_ref[r, :], w_ref[na + nb:, :], preferred_element_type=F32)
        x = x_ref[r, :] + _rms(y, gpost_ref[...])
        xo_ref[r, :] = x
        h_ref[r, :] = _rms(x, gnext_ref[...]).astype(h_ref.dtype)


def _out_call(o_a, o_b, o_c, w_out_p, layer, x, g_post, g_next, tm):
    m, d = x.shape
    row = lambda width: pl.BlockSpec((tm, width), lambda i: (i, 0))
    vec = pl.BlockSpec((1, d), lambda i: (0, 0))
    return pl.pallas_call(
        _out_kernel,
        grid=(m // tm,),
        in_specs=[
            row(o_a.shape[1]), row(o_b.shape[1]), row(o_c.shape[1]),
            _resident((None, w_out_p.shape[1], d), lambda i: (layer, 0, 0)),
            row(d), vec, vec,
        ],
        out_specs=[row(d), row(d)],
        out_shape=[jax.ShapeDtypeStruct((m, d), F32), jax.ShapeDtypeStruct((m, d), BF16)],
        compiler_params=_params(("parallel",), 48),
        name="out_proj",
    )(o_a, o_b, o_c, w_out_p, x, g_post, g_next)


def _ffn_kernel(h_ref, wg_ref, wu_ref, wd_ref, y_ref):
    @pl.when(pl.program_id(1) == 0)
    def _():
        y_ref[...] = jnp.zeros(y_ref.shape, F32)

    h = h_ref[...]
    g = jnp.dot(h, wg_ref[...], preferred_element_type=F32)
    u = jnp.dot(h, wu_ref[...], preferred_element_type=F32)
    a = (g * jax.nn.sigmoid(g) * u).astype(BF16)
    y_ref[...] += jnp.dot(a, wd_ref[...], preferred_element_type=F32)


def _ffn_call(h, wg, wu, wd, layer, tm, tf):
    m, d = h.shape
    d_ff = wg.shape[2]
    row = pl.BlockSpec((tm, d), lambda i, f: (i, 0))
    return pl.pallas_call(
        _ffn_kernel,
        grid=(m // tm, d_ff // tf),
        in_specs=[
            row,
            pl.BlockSpec((None, d, tf), lambda i, f: (layer, 0, f)),
            pl.BlockSpec((None, d, tf), lambda i, f: (layer, 0, f)),
            pl.BlockSpec((None, tf, d), lambda i, f: (layer, f, 0)),
        ],
        out_specs=row,
        out_shape=jax.ShapeDtypeStruct((m, d), F32),
        compiler_params=_params(("parallel", "arbitrary"), 52),
        name="ffn",
    )(h, wg, wu, wd)


def _residual_kernel(y_ref, x_ref, g_ref, o_ref):
    o_ref[...] = x_ref[...] + _rms(y_ref[...], g_ref[...])


def _residual_call(y, x, g, tm):
    m, d = x.shape
    row = pl.BlockSpec((tm, d), lambda i: (i, 0))
    return pl.pallas_call(
        _residual_kernel,
        grid=(m // tm,),
        in_specs=[row, row, pl.BlockSpec((1, d), lambda i: (0, 0))],
        out_specs=row,
        out_shape=jax.ShapeDtypeStruct((m, d), F32),
        compiler_params=_params(("parallel",), 32),
        name="ffn_residual",
    )(y, x, g)


def _w_in_layout():
    widths = (A_HEADS * HEAD_DIM, HEAD_DIM, HEAD_DIM, IDX_HEADS * IDX_DIM, IDX_DIM, IDX_HEADS,
              B_HEADS * HEAD_DIM, B_HEADS * HEAD_DIM, B_HEADS * HEAD_DIM, C_HEADS * HEAD_DIM,
              C_KV_GROUPS * HEAD_DIM, C_KV_GROUPS * HEAD_DIM, C_KV_GROUPS * HEAD_DIM, C_KV_GROUPS * HEAD_DIM,
              C_KV_GROUPS * HEAD_DIM, C_KV_GROUPS * HEAD_DIM, 3 * C_HEADS)
    names = ("a_q", "a_k", "a_v", "i_q", "i_k", "i_w", "b_q", "b_k", "b_v", "c_q",
             "c_kc", "c_vc", "c_ks", "c_vs", "c_kw", "c_vw", "c_g")
    offs = np.concatenate([[0], np.cumsum(widths)])
    src = {n: (int(offs[k]), int(widths[k])) for k, n in enumerate(names)}
    per_group = 3 * C_GROUP_SIZE
    src["c_g0"] = (src["c_g"][0], per_group)
    src["c_g1"] = (src["c_g"][0] + per_group, per_group)
    copies, zeros = [], []
    dst = 0
    for name, blocks in [(n, w) for n, w, _, _ in _SEGMENTS] + list(_CMP_SEGMENTS):
        start, width = src[name]
        copies.append((dst, start, width))
        if name == "i_k":
            copies.append((dst + width, start, width))
            width *= 2
        if width < blocks * LANE:
            zeros.append((dst + width, blocks * LANE - width))
        dst += blocks * LANE
    return copies, zeros, int(offs[-1])


def _w_prep_kernel(w_ref, o_ref, *, copies, zeros):
    rows = w_ref.shape[0]
    for dst, start, width in copies:
        for c0 in range(0, width, 4 * LANE):
            n = min(4 * LANE, width - c0)
            o_ref[:, dst + c0:dst + c0 + n] = w_ref[:, start + c0:start + c0 + n].astype(o_ref.dtype)
    for dst, width in zeros:
        o_ref[:, dst:dst + width] = jnp.zeros((rows, width), o_ref.dtype)


def _prep_w_in(w_in, tk=256):
    depth, d, width = w_in.shape
    copies, zeros, src_width = _w_in_layout()
    assert width == src_width
    return pl.pallas_call(
        functools.partial(_w_prep_kernel, copies=copies, zeros=zeros),
        grid=(depth, d // tk),
        in_specs=[pl.BlockSpec((None, tk, width), lambda l, i: (l, i, 0))],
        out_specs=pl.BlockSpec((None, tk, NW), lambda l, i: (l, i, 0)),
        out_shape=jax.ShapeDtypeStruct((depth, d, NW), BF16),
        compiler_params=_params(("parallel", "parallel"), 32),
        name="w_in_prep",
    )(w_in)


def _rope_tables(seq):
    def tables(dim):
        inv = 1.0 / (ROPE_THETA ** (jnp.arange(0, dim, 2, dtype=F32) / dim))
        ang = jnp.arange(seq, dtype=F32)[:, None] * inv[None, :]
        cos, sin = jnp.cos(ang), jnp.sin(ang)
        reps = LANE // dim
        return (jnp.tile(jnp.concatenate([cos, cos], axis=-1), (1, reps)),
                jnp.tile(jnp.concatenate([-sin, sin], axis=-1), (1, reps)))
    return tables(HEAD_DIM) + tables(IDX_DIM)


def _selection_constants(seq):
    nc = seq // CMP_STRIDE
    n_slc = seq // SEL_BLOCK
    c_start = np.arange(nc) * CMP_STRIDE
    n_start = np.arange(n_slc) * SEL_BLOCK
    isect = ((c_start[None, :] < n_start[:, None] + SEL_BLOCK)
             & (c_start[None, :] + CMP_BLOCK > n_start[:, None])
             & (np.arange(nc)[None, :] < nc - CMP_BLOCK // CMP_STRIDE + 1)).astype(np.float32)
    expand = np.zeros((seq, max(n_slc, LANE)), np.float32)
    expand[np.arange(seq), np.arange(seq) // SEL_BLOCK] = 1.0
    return jnp.asarray(isect, BF16), jnp.asarray(expand, BF16)


def _tiles(seq):
    return dict(tq=min(2 * LANE, seq), step_dsa=min(4 * LANE, seq), step=min(2 * LANE, seq),
                tm_proj=min(256, seq), tm_out=min(512, seq), tm_ffn=min(1024, seq), tf=512)


def kernel(x, w_in, w_out, cmp_pe_k, cmp_w1_k, cmp_w2_k, cmp_pe_v, cmp_w1_v, cmp_w2_v,
           w_gate, w_up, w_down, g_pre_mix, g_post_mix, g_pre_ffn, g_post_ffn):
    b, s, d = x.shape
    depth = w_in.shape[0]
    m = b * s
    t = _tiles(s)
    tq, step = t["tq"], t["step"]

    w_in_p = _prep_w_in(w_in)
    w_out_p = w_out.astype(BF16)
    wg, wu, wd = w_gate.astype(BF16), w_up.astype(BF16), w_down.astype(BF16)
    w1_k, w2_k = cmp_w1_k.astype(BF16), cmp_w2_k.astype(BF16)
    w1_v, w2_v = cmp_w1_v.astype(BF16), cmp_w2_v.astype(BF16)
    col_scale = jnp.asarray(_COL_SCALE)
    tabs = _rope_tables(s)
    isect, expand = _selection_constants(s)
    vec = lambda g, layer: g[layer][None, :]

    xf = x.reshape(m, d)
    y = None
    for layer in range(depth):
        if layer == 0:
            p, pc = _proj_call((xf, vec(g_pre_mix, 0)), w_in_p, layer, col_scale, tabs, s, t["tm_proj"])
        else:
            rows_in = (y, xf, vec(g_post_ffn, layer - 1), vec(g_pre_mix, layer))
            xf, p, pc = _proj_call(rows_in, w_in_p, layer, col_scale, tabs, s, t["tm_proj"])
        p3 = p.reshape(b, s, NP)
        o_a = _dsa_call(p3, tq, t["step_dsa"])
        o_b = _dil_call(p3, tq, step)
        kc, vct = _cmp_call(pc, b, layer, cmp_pe_k, w1_k, w2_k, cmp_pe_v, w1_v, w2_v)
        o_c = _nsa_call(p3, kc, vct, tabs[0], tabs[1], isect, expand, tq, step)
        xf, h = _out_call(o_a.reshape(m, -1), o_b.reshape(m, -1), o_c.reshape(m, -1), w_out_p, layer,
                          xf, vec(g_post_mix, layer), vec(g_pre_ffn, layer), t["tm_out"])
        y = _ffn_call(h, wg, wu, wd, layer, t["tm_ffn"], t["tf"])
    return _residual_call(y, xf, vec(g_post_ffn, depth - 1), t["tm_out"]).reshape(b, s, d)
```

```python
import functools

import numpy as np
import jax
import jax.numpy as jnp
from jax import lax
from jax.experimental import pallas as pl
from jax.experimental.pallas import tpu as pltpu

F32 = jnp.float32
BF16 = jnp.bfloat16

LANE = 128
SUBLANE = 8
HEAD_DIM = 128
A_HEADS = 4
B_HEADS = 4
C_HEADS = 8
C_KV_GROUPS = 2
C_GROUP_SIZE = C_HEADS // C_KV_GROUPS
IDX_HEADS = 16
IDX_DIM = 64
DSA_TOPK_MAX = 256
DILATED_PATTERNS = ((128, 1), (512, 4), (2048, 16))
CMP_BLOCK = 32
CMP_STRIDE = 16
CMP_HIDDEN = 256
SEL_BLOCK = 64
SEL_COUNT = 16
WIN_SIZE = 512
ROPE_THETA = 10000.0
RMS_EPS = 1e-6
ATTN_SCALE = HEAD_DIM ** -0.5 * float(np.log2(np.e))
IDX_SCALE = IDX_DIM ** -0.5 * IDX_HEADS ** -0.5

NEG = -1e30
INT_MIN = -(2 ** 31)
HALF16 = 2 ** 15
COUNT_ROWS = 8 * SUBLANE
SCORE_ROWS = 256
ATT_ROWS = 128
ONES_ROWS = 16
OUT_SUB_ROWS = 256

MODE_NONE, MODE_ROPE, MODE_IROPE = 0, 1, 2

_SEGMENTS = (
    ("i_q", 8, MODE_IROPE, IDX_SCALE),
    ("c_q", 8, MODE_NONE, ATTN_SCALE),
    ("a_q", 4, MODE_ROPE, ATTN_SCALE),
    ("b_q", 4, MODE_ROPE, ATTN_SCALE),
    ("b_k", 4, MODE_ROPE, 1.0),
    ("b_v", 4, MODE_NONE, 1.0),
    ("c_ks", 2, MODE_ROPE, 1.0),
    ("c_vs", 2, MODE_NONE, 1.0),
    ("c_kw", 2, MODE_ROPE, 1.0),
    ("c_vw", 2, MODE_NONE, 1.0),
    ("a_k", 1, MODE_ROPE, 1.0),
    ("a_v", 1, MODE_NONE, 1.0),
    ("i_k", 1, MODE_IROPE, 1.0),
    ("i_w", 1, MODE_NONE, 1.0),
    ("c_g0", 1, MODE_NONE, 1.0),
    ("c_g1", 1, MODE_NONE, 1.0),
)
_CMP_SEGMENTS = (("c_kc", 2), ("c_vc", 2))
_OFF = {}
_o = 0
for _n, _w, _m, _s in _SEGMENTS:
    _OFF[_n] = _o
    _o += _w
NP_BLOCKS = _o
NP = NP_BLOCKS * LANE
CMP_BLOCKS = sum(w for _, w in _CMP_SEGMENTS)
CMP_COLS = CMP_BLOCKS * LANE
NW = NP + CMP_COLS
_BLOCK_MODE = tuple(m for _, w, m, _ in _SEGMENTS for _ in range(w))
_COL_SCALE = np.repeat(np.array([s for _, w, _, s in _SEGMENTS for _ in range(w)], np.float32), LANE)[None, :]
_PROJ_CHUNK = 4


def _params(sem, vmem_mb):
    return pltpu.CompilerParams(dimension_semantics=sem, vmem_limit_bytes=vmem_mb * 1024 * 1024)


def _resident(block_shape, index_map):
    return pl.BlockSpec(block_shape, index_map, pipeline_mode=pl.Buffered(1))


def _rms(x, g):
    return x * lax.rsqrt(jnp.mean(x * x, axis=-1, keepdims=True) + RMS_EPS) * g


def _rope_full(a, cos, sin):
    return a * cos + pltpu.roll(a, HEAD_DIM // 2, 1) * sin


def _rope_idx(a, cos, sin, first_half):
    partner = jnp.where(first_half, pltpu.roll(a, LANE - IDX_DIM // 2, 1), pltpu.roll(a, IDX_DIM // 2, 1))
    return a * cos + partner * sin


def _proj_kernel(*refs, after_ffn):
    if after_ffn:
        (y_ref, x_ref, gpost_ref, gpre_ref, w_ref, cs_ref, cos_ref, sin_ref, icos_ref, isin_ref,
         xo_ref, o_ref, oc_ref, cmp_ref) = refs
        x = x_ref[...] + _rms(y_ref[...], gpost_ref[...])
        xo_ref[...] = x
        h = _rms(x, gpre_ref[...]).astype(BF16)
    else:
        x_ref, gpre_ref, w_ref, cs_ref, cos_ref, sin_ref, icos_ref, isin_ref, o_ref, oc_ref, cmp_ref = refs
        h = _rms(x_ref[...], gpre_ref[...]).astype(BF16)
    tm = h.shape[0]
    lane = lax.broadcasted_iota(jnp.int32, (tm, LANE), 1)
    first_half = (lane & (IDX_DIM - 1)) < IDX_DIM // 2
    for c0 in range(0, NP_BLOCKS, _PROJ_CHUNK):
        nb = min(_PROJ_CHUNK, NP_BLOCKS - c0)
        acc = lax.dot_general(h, w_ref[c0 * LANE:(c0 + nb) * LANE, :], (((1,), (1,)), ((), ())),
                              preferred_element_type=F32)
        for b in range(nb):
            col = slice((c0 + b) * LANE, (c0 + b + 1) * LANE)
            a = acc[:, b * LANE:(b + 1) * LANE] * cs_ref[:, col]
            mode = _BLOCK_MODE[c0 + b]
            if mode == MODE_ROPE:
                a = _rope_full(a, cos_ref[...], sin_ref[...])
            elif mode == MODE_IROPE:
                a = _rope_idx(a, icos_ref[...], isin_ref[...], first_half)
            o_ref[:, col] = a.astype(o_ref.dtype)
    acc = lax.dot_general(h, w_ref[NP:NW, :], (((1,), (1,)), ((), ())), preferred_element_type=F32)
    for b in range(CMP_BLOCKS):
        cmp_ref[b] = acc[:, b * LANE:(b + 1) * LANE]
    for j in range(CMP_STRIDE):
        for b in range(CMP_BLOCKS):
            rows = cmp_ref[b, pl.ds(j, tm // CMP_STRIDE, stride=CMP_STRIDE), :]
            oc_ref[:, j * CMP_COLS + b * LANE:j * CMP_COLS + (b + 1) * LANE] = rows.astype(oc_ref.dtype)


def _proj_call(rows_in, w_in_p, layer, col_scale, tabs, seq, tm):
    after_ffn = len(rows_in) > 2
    m, d = rows_in[0].shape
    tpb = seq // tm
    tab_spec = pl.BlockSpec((tm, LANE), lambda i: (i % tpb, 0))
    row = pl.BlockSpec((tm, d), lambda i: (i, 0))
    vec = pl.BlockSpec((1, d), lambda i: (0, 0))
    out_specs = [
        pl.BlockSpec((tm, NP), lambda i: (i, 0)),
        pl.BlockSpec((tm // CMP_STRIDE, CMP_STRIDE * CMP_COLS), lambda i: (i, 0)),
    ]
    out_shape = [
        jax.ShapeDtypeStruct((m, NP), BF16),
        jax.ShapeDtypeStruct((m // CMP_STRIDE, CMP_STRIDE * CMP_COLS), BF16),
    ]
    if after_ffn:
        out_specs = [row] + out_specs
        out_shape = [jax.ShapeDtypeStruct((m, d), F32)] + out_shape
    return pl.pallas_call(
        functools.partial(_proj_kernel, after_ffn=after_ffn),
        grid=(m // tm,),
        in_specs=([row, row, vec, vec] if after_ffn else [row, vec]) + [
            _resident((None, NW, d), lambda i: (layer, 0, 0)),
            _resident((1, NP), lambda i: (0, 0)),
            tab_spec, tab_spec, tab_spec, tab_spec,
        ],
        out_specs=out_specs,
        out_shape=out_shape,
        scratch_shapes=[pltpu.VMEM((CMP_BLOCKS, tm, LANE), F32)],
        compiler_params=_params(("parallel",), 54),
        name="in_proj",
    )(*rows_in, w_in_p, col_scale, *tabs)


def _transpose_v(dst_ref, src_ref, col0=0):
    rows = src_ref.shape[0]
    for r0 in range(0, rows, LANE):
        tile = src_ref[r0:r0 + LANE, col0:col0 + HEAD_DIM].astype(F32)
        dst_ref[0:HEAD_DIM, r0:r0 + LANE] = tile.T.astype(dst_ref.dtype)
    dst_ref[HEAD_DIM:, :] = jnp.ones((ONES_ROWS, rows), dst_ref.dtype)


def _transpose_v_tiles(dst_ref, src_ref):
    rows, _ = src_ref.shape
    for r in range(rows // LANE):
        tile = src_ref[r * LANE:(r + 1) * LANE, :].astype(F32)
        dst_ref[r, 0:HEAD_DIM, :] = tile.T.astype(dst_ref.dtype)
        dst_ref[r, HEAD_DIM:, :] = jnp.ones((ONES_ROWS, LANE), dst_ref.dtype)


def _for_causal_prefix(i, tq, s_len, step, body):
    n = ((i + 1) * tq + step - 1) // step
    for v in range(1, s_len // step + 1):
        pl.when(n == v)(functools.partial(body, v * step))


def _fold(x, op):
    out = x[0:COUNT_ROWS]
    for r0 in range(COUNT_ROWS, x.shape[0], COUNT_ROWS):
        out = op(out, x[r0:r0 + COUNT_ROWS])
    return out


def _fold_tree(x, op):
    while x.shape[0] > SUBLANE:
        half = x.shape[0] // 2
        x = op(x[:half], x[half:])
    return x


def _row_chunk(rows):
    return SCORE_ROWS if rows % SCORE_ROWS == 0 else LANE


def _attend_heads(qs, k_of, pv_of, rows, mask_of, s_ref, p_ref, w_ref=None, plain_rows=0):
    n_heads = len(qs)
    tq = qs[0].shape[0]
    chunk = ATT_ROWS * LANE // tq
    for h in range(n_heads):
        s_ref[h, 0:rows, :] = lax.dot_general(k_of(h, 0, rows), qs[h], (((1,), (1,)), ((), ())),
                                              preferred_element_type=F32)
    m_acc = [jnp.full((SUBLANE, tq), NEG, F32) for _ in qs]
    weighted = False
    for r0 in range(0, rows, chunk):
        if r0 + chunk <= plain_rows:
            for h in range(n_heads):
                m_acc[h] = jnp.maximum(m_acc[h], _fold_tree(s_ref[h, r0:r0 + chunk, :], jnp.maximum))
            continue
        mask, weight = mask_of(r0, chunk)
        if weight is not None:
            weighted = True
            w_ref[r0:r0 + chunk, :] = weight
        for h in range(n_heads):
            s = jnp.where(mask, s_ref[h, r0:r0 + chunk, :], NEG)
            s_ref[h, r0:r0 + chunk, :] = s
            m_acc[h] = jnp.maximum(m_acc[h], _fold_tree(s, jnp.maximum))
    outs = []
    for h in range(n_heads):
        mx = jnp.max(m_acc[h], axis=0, keepdims=True)
        for r0 in range(0, rows, chunk):
            p = jnp.exp2(s_ref[h, r0:r0 + chunk, :] - mx)
            if weighted:
                p = p * w_ref[r0:r0 + chunk, :]
            p_ref[h, r0:r0 + chunk, :] = p.astype(p_ref.dtype)
        pv = pv_of(h, p_ref[h, 0:rows, :])
        outs.append(pv[0:HEAD_DIM] / pv[HEAD_DIM:HEAD_DIM + 1])
    return outs


def _count_rows(key_ref, rows, pred):
    chunk = _row_chunk(rows)
    acc = None
    for r0 in range(0, rows, chunk):
        part = _fold(pred(key_ref[r0:r0 + chunk, :], r0), jnp.add)
        acc = part if acc is None else acc + part
    return jnp.sum(acc.astype(jnp.int32), axis=0, keepdims=True)


def _dsa_prefix(rows, i, iq_ref, iw_t, ik_ref, q_ref, k_ref, vt_ref, o_ref, key_ref, cut_ref, s_ref, p_ref,
                hi_ref, lo_ref, topk):
    tq = q_ref.shape[0]
    chunk = min(SCORE_ROWS * LANE // tq, rows)
    lane = lax.broadcasted_iota(jnp.int32, (chunk, LANE), 1)
    crow = lax.broadcasted_iota(jnp.int32, (chunk, tq), 0)
    ctcol = i * tq + lax.broadcasted_iota(jnp.int32, (chunk, tq), 1)

    for r0 in range(0, rows, chunk):
        ik = ik_ref[r0:r0 + chunk, :]
        halves = (jnp.where(lane < IDX_DIM, ik, jnp.zeros_like(ik)),
                  jnp.where(lane >= IDX_DIM, ik, jnp.zeros_like(ik)))
        score = jnp.zeros((chunk, tq), F32)
        for p in range(IDX_HEADS // 2):
            blk = iq_ref[:, p * LANE:(p + 1) * LANE]
            for half in range(2):
                j = 2 * p + half
                lg = lax.dot_general(halves[half], blk, (((1,), (1,)), ((), ())), preferred_element_type=F32)
                score = score + jnp.maximum(lg, 0.0) * iw_t[j:j + 1, :]
        bits = pltpu.bitcast(score, jnp.int32)
        key = bits ^ ((bits >> 31) & 0x7FFFFFFF)
        key = jnp.where(r0 + crow <= ctcol, key, INT_MIN)
        key_ref[r0:r0 + chunk, :] = key
        hi_ref[r0:r0 + chunk, :] = (key >> 16).astype(jnp.int16)
        lo_ref[r0:r0 + chunk, :] = ((key & 0xFFFF) - HALF16).astype(jnp.int16)

    def search16(ref, target):
        def bit_step(b, off):
            cand_off = off | jnp.left_shift(jnp.int32(1), 15 - b)
            cand = (cand_off - HALF16).astype(jnp.int16)
            cnt = _count_rows(ref, rows, lambda c, r0: jnp.where(c >= cand, jnp.int16(1), jnp.int16(0)))
            return jnp.where(cnt >= target, cand_off, off)

        return lax.fori_loop(0, 16, bit_step, jnp.zeros((1, tq), jnp.int32)) - HALF16

    thr_hi = search16(hi_ref, topk)
    thr_hi16 = thr_hi.astype(jnp.int16)
    above = _count_rows(hi_ref, rows, lambda c, r0: jnp.where(c > thr_hi16, jnp.int16(1), jnp.int16(0)))
    chunk16 = _row_chunk(rows)
    for r0 in range(0, rows, chunk16):
        same = hi_ref[r0:r0 + chunk16, :] == thr_hi16
        hi_ref[r0:r0 + chunk16, :] = jnp.where(same, lo_ref[r0:r0 + chunk16, :], jnp.int16(-HALF16))
    thr_lo = search16(hi_ref, topk - above)
    thr = jnp.left_shift(thr_hi, 16) | (thr_lo + HALF16)
    short = thr == INT_MIN
    need = topk - _count_rows(key_ref, rows, lambda kc, r0: jnp.where(kc > thr, 1, 0))
    n_eq = _count_rows(key_ref, rows, lambda kc, r0: jnp.where(kc == thr, 1, 0))
    excess = jnp.where(n_eq > need, jnp.where(short, 0, 1), 0)
    cut_ref[...] = jnp.where(short, -1, rows)

    def row_ids(r0, n):
        return r0 + lax.broadcasted_iota(jnp.int32, (n, tq), 0)

    @pl.when(jnp.max(excess) > 0)
    def _():
        n_bits = max(1, (rows - 1).bit_length())

        def idx_step(b, cut):
            cand = cut | jnp.left_shift(jnp.int32(1), n_bits - 1 - b)
            below = _count_rows(key_ref, rows, lambda kc, r0: jnp.where(
                kc == thr, jnp.where(row_ids(r0, kc.shape[0]) < cand, 1, 0), 0))
            return jnp.where(below < need, cand, cut)

        cut = lax.fori_loop(0, n_bits, idx_step, jnp.zeros((1, tq), jnp.int32))
        cut_ref[...] = jnp.where(excess > 0, cut, cut_ref[...])

    cut = cut_ref[...]

    def mask_of(r0, n):
        kc = key_ref[r0:r0 + n, :]
        chosen = jnp.where(kc > thr, 1, jnp.where(kc == thr, jnp.where(row_ids(r0, n) <= cut, 1, 0), 0))
        return chosen > 0, None

    vt = vt_ref[:, 0:rows]
    cols = [slice(h * HEAD_DIM, (h + 1) * HEAD_DIM) for h in range(A_HEADS)]
    outs = _attend_heads([q_ref[:, c] for c in cols], lambda h, r0, n: k_ref[r0:r0 + n, :],
                         lambda h, p: jnp.dot(vt, p, preferred_element_type=F32),
                         rows, mask_of, s_ref, p_ref)
    for h, c in enumerate(cols):
        o_ref[:, c] = outs[h].T.astype(o_ref.dtype)


def _dsa_kernel(iq_ref, iw_ref, ik_ref, q_ref, k_ref, v_ref, o_ref, vt_ref, key_ref, cut_ref, s_ref, p_ref,
                hi_ref, lo_ref, *, topk, step):
    i = pl.program_id(1)
    tq = q_ref.shape[0]
    s_len = k_ref.shape[0]

    @pl.when(i == 0)
    def _():
        _transpose_v(vt_ref, v_ref)

    iw_t = iw_ref[...].astype(F32).T[:IDX_HEADS, :]
    _for_causal_prefix(i, tq, s_len, step, functools.partial(
        _dsa_prefix, i=i, iq_ref=iq_ref, iw_t=iw_t, ik_ref=ik_ref, q_ref=q_ref, k_ref=k_ref, vt_ref=vt_ref,
        o_ref=o_ref, key_ref=key_ref, cut_ref=cut_ref, s_ref=s_ref, p_ref=p_ref,
        hi_ref=hi_ref, lo_ref=lo_ref, topk=topk))


def _dsa_call(p3, tq, step):
    b, s, _ = p3.shape
    topk = min(DSA_TOPK_MAX, s // 4)
    blk = lambda name, width: _OFF[name] // width
    return pl.pallas_call(
        functools.partial(_dsa_kernel, topk=topk, step=step),
        grid=(b, s // tq),
        in_specs=[
            pl.BlockSpec((None, tq, 8 * LANE), lambda bi, i: (bi, i, blk("i_q", 8))),
            pl.BlockSpec((None, tq, LANE), lambda bi, i: (bi, i, blk("i_w", 1))),
            pl.BlockSpec((None, s, LANE), lambda bi, i: (bi, 0, blk("i_k", 1))),
            pl.BlockSpec((None, tq, 4 * LANE), lambda bi, i: (bi, i, blk("a_q", 4))),
            pl.BlockSpec((None, s, LANE), lambda bi, i: (bi, 0, blk("a_k", 1))),
            pl.BlockSpec((None, s, LANE), lambda bi, i: (bi, 0, blk("a_v", 1))),
        ],
        out_specs=pl.BlockSpec((None, tq, A_HEADS * HEAD_DIM), lambda bi, i: (bi, i, 0)),
        out_shape=jax.ShapeDtypeStruct((b, s, A_HEADS * HEAD_DIM), BF16),
        scratch_shapes=[
            pltpu.VMEM((HEAD_DIM + ONES_ROWS, s), BF16),
            pltpu.VMEM((s, tq), jnp.int32),
            pltpu.VMEM((1, tq), jnp.int32),
            pltpu.VMEM((A_HEADS, s, tq), F32),
            pltpu.VMEM((A_HEADS, s, tq), BF16),
            pltpu.VMEM((s, tq), jnp.int16),
            pltpu.VMEM((s, tq), jnp.int16),
        ],
        compiler_params=_params(("parallel", "arbitrary"), 48),
        name="dsa",
    )(p3, p3, p3, p3, p3, p3)


def _dil_prefix(rows, q_ref, k_ref, vt_ref, o_ref, s_ref, p_ref, w_ref):
    tq = q_ref.shape[0]
    first_q = rows - tq
    period = max(d for _, d in DILATED_PATTERNS)
    shared = {}

    def mask_of(r0, n):
        base = first_q - r0
        d_min, d_max = base - (n - 1), base + (tq - 1)
        live = tuple((w, d) for w, d in DILATED_PATTERNS if d_min <= w)
        on_edge = d_min < 0 or any(d_max > w for w, _ in live)
        key = None if on_edge else (live, base % period)
        if key in shared:
            return shared[key]
        delta = (base + lax.broadcasted_iota(jnp.int32, (n, tq), 1)
                 - lax.broadcasted_iota(jnp.int32, (n, tq), 0))
        mult = jnp.zeros((n, tq), jnp.int32)
        for window, dilation in live:
            hit = jnp.where(delta <= window, 1, 0) if d_max > window else 1
            mult = mult + jnp.where((delta & (dilation - 1)) == 0, hit, 0)
        if d_min < 0:
            mult = jnp.where(delta >= 0, mult, 0)
        out = (mult > 0, mult.astype(F32))
        if key is not None:
            shared[key] = out
        return out

    cols = [slice(h * HEAD_DIM, (h + 1) * HEAD_DIM) for h in range(B_HEADS)]
    outs = _attend_heads([q_ref[:, c] for c in cols], lambda h, r0, n: k_ref[r0:r0 + n, cols[h]],
                         lambda h, p: jnp.dot(vt_ref[h, :, 0:rows], p, preferred_element_type=F32),
                         rows, mask_of, s_ref, p_ref, w_ref)
    for h, c in enumerate(cols):
        o_ref[:, c] = outs[h].T.astype(o_ref.dtype)


def _dil_kernel(q_ref, k_ref, v_ref, o_ref, vt_ref, s_ref, p_ref, w_ref, *, step):
    i = pl.program_id(1)

    @pl.when(i == 0)
    def _():
        for h in range(B_HEADS):
            _transpose_v(vt_ref.at[h], v_ref, h * HEAD_DIM)

    assert step == q_ref.shape[0]
    _for_causal_prefix(i, q_ref.shape[0], k_ref.shape[0], step, functools.partial(
        _dil_prefix, q_ref=q_ref, k_ref=k_ref, vt_ref=vt_ref, o_ref=o_ref,
        s_ref=s_ref, p_ref=p_ref, w_ref=w_ref))


def _dil_call(p3, tq, step):
    b, s, _ = p3.shape
    width = B_HEADS * HEAD_DIM
    return pl.pallas_call(
        functools.partial(_dil_kernel, step=step),
        grid=(b, s // tq),
        in_specs=[
            pl.BlockSpec((None, tq, width), lambda bi, i: (bi, i, _OFF["b_q"] // 4)),
            pl.BlockSpec((None, s, width), lambda bi, i: (bi, 0, _OFF["b_k"] // 4)),
            pl.BlockSpec((None, s, width), lambda bi, i: (bi, 0, _OFF["b_v"] // 4)),
        ],
        out_specs=pl.BlockSpec((None, tq, width), lambda bi, i: (bi, i, 0)),
        out_shape=jax.ShapeDtypeStruct((b, s, width), BF16),
        scratch_shapes=[
            pltpu.VMEM((B_HEADS, HEAD_DIM + ONES_ROWS, s), BF16),
            pltpu.VMEM((B_HEADS, s, tq), F32),
            pltpu.VMEM((B_HEADS, s, tq), BF16),
            pltpu.VMEM((s, tq), F32),
        ],
        compiler_params=_params(("parallel", "arbitrary"), 48),
        name="dilated",
    )(p3, p3, p3)


def _gelu_tanh(x):
    return 0.5 * x * (1.0 + jnp.tanh(np.float32(np.sqrt(2.0 / np.pi)) * (x + 0.044715 * (x * x * x))))


def _cmp_kernel(*refs):
    n = CMP_STRIDE
    xk, xv = refs[:n], refs[n:2 * n]
    pe_k, w1_k, w2_k, pe_v, w1_v, w2_v, kc_ref, vct_ref = refs[2 * n:]

    def branch(x_refs, pe_ref, w1_ref, w2_ref):
        lo = hi = None
        for j in range(n):
            xj = x_refs[j][...].astype(F32)
            a = (xj + pe_ref[j:j + 1, :]).astype(BF16)
            b = (xj + pe_ref[n + j:n + j + 1, :]).astype(BF16)
            dl = jnp.dot(a, w1_ref[j * HEAD_DIM:(j + 1) * HEAD_DIM, :], preferred_element_type=F32)
            dh = jnp.dot(b, w1_ref[(n + j) * HEAD_DIM:(n + j + 1) * HEAD_DIM, :], preferred_element_type=F32)
            lo = dl if lo is None else lo + dl
            hi = dh if hi is None else hi + dh
        nc = lo.shape[0]
        hid = lo + pltpu.roll(hi, nc - 1, 0)
        return jnp.dot(_gelu_tanh(hid).astype(BF16), w2_ref[...], preferred_element_type=F32)

    kc_ref[...] = branch(xk, pe_k, w1_k, w2_k).astype(kc_ref.dtype)
    vct_ref[...] = branch(xv, pe_v, w1_v, w2_v).T.astype(vct_ref.dtype)


def _cmp_call(pc, b, layer, pe_k, w1_k, w2_k, pe_v, w1_v, w2_v):
    nc = pc.shape[0] // b
    pc3 = pc.reshape(b, nc, CMP_STRIDE * CMP_COLS)
    x_specs = []
    for first in (0, C_KV_GROUPS):
        for j in range(CMP_STRIDE):
            x_specs.append(pl.BlockSpec((None, nc, HEAD_DIM),
                                        lambda bi, g, j=j, first=first: (bi, 0, j * CMP_BLOCKS + first + g)))
    flat = CMP_BLOCK * HEAD_DIM
    w_specs = [
        _resident((None, CMP_BLOCK, HEAD_DIM), lambda bi, g: (layer, 0, 0)),
        _resident((None, flat, CMP_HIDDEN), lambda bi, g: (layer, 0, 0)),
        _resident((None, CMP_HIDDEN, HEAD_DIM), lambda bi, g: (layer, 0, 0)),
    ]
    return pl.pallas_call(
        _cmp_kernel,
        grid=(b, C_KV_GROUPS),
        in_specs=x_specs + w_specs + w_specs,
        out_specs=[
            pl.BlockSpec((None, None, nc, HEAD_DIM), lambda bi, g: (bi, g, 0, 0)),
            pl.BlockSpec((None, None, HEAD_DIM, nc), lambda bi, g: (bi, g, 0, 0)),
        ],
        out_shape=[
            jax.ShapeDtypeStruct((b, C_KV_GROUPS, nc, HEAD_DIM), BF16),
            jax.ShapeDtypeStruct((b, C_KV_GROUPS, HEAD_DIM, nc), BF16),
        ],
        compiler_params=_params(("parallel", "parallel"), 32),
        name="nsa_compress",
    )(*([pc3] * (2 * CMP_STRIDE)), pe_k, w1_k, w2_k, pe_v, w1_v, w2_v)


def _nsa_selected(rows, t_row, q_sel, ke_ref, vst_ref, acc_ref, s_ref, p_ref, step):
    tq = q_sel[0].shape[0]

    def mask_of(r0, n):
        srow = r0 + lax.broadcasted_iota(jnp.int32, (n, tq), 0)
        return t_row - srow >= 0, None

    vt = vst_ref[:, 0:rows]
    outs = _attend_heads(q_sel, lambda h, r0, n: ke_ref[r0:r0 + n, :],
                         lambda h, p: jnp.dot(vt, p, preferred_element_type=F32),
                         rows, mask_of, s_ref, p_ref, plain_rows=rows - max(step, tq))
    for r in range(C_GROUP_SIZE):
        acc_ref[r] = outs[r]


def _nsa_kernel(q_ref, kc_ref, vct_ref, ks_ref, vs_ref, kw_ref, vw_ref, g_ref, cos_ref, sin_ref,
                isect_ref, expand_ref, o_ref, vst_ref, vwt_ref, acc_ref, s_ref, p_ref, ke_ref, *, n_sel, step):
    i = pl.program_id(2)
    tq = q_ref.shape[0]
    s_len = ks_ref.shape[0]
    nc = kc_ref.shape[0]
    n_slc = isect_ref.shape[0]

    @pl.when(i == 0)
    def _():
        _transpose_v(vst_ref, vs_ref)
        _transpose_v_tiles(vwt_ref, vw_ref)
        ke_ref[:, 0:HEAD_DIM] = ks_ref[...]
        ke_ref[:, HEAD_DIM:] = expand_ref[...]

    t_row = i * tq + lax.broadcasted_iota(jnp.int32, (1, tq), 1)

    kc = kc_ref[...]
    vct = vct_ref[...]
    cend = lax.broadcasted_iota(jnp.int32, (nc, tq), 0) * CMP_STRIDE + (CMP_BLOCK - 1)
    cmask = cend <= t_row
    o_cmp = []
    p_sum = jnp.zeros((nc, tq), F32)
    for r in range(C_GROUP_SIZE):
        q_r = q_ref[:, r * HEAD_DIM:(r + 1) * HEAD_DIM]
        sc = lax.dot_general(kc, q_r, (((1,), (1,)), ((), ())), preferred_element_type=F32)
        sc = jnp.where(cmask, sc, NEG)
        mx = jnp.max(sc, axis=0, keepdims=True)
        e = jnp.where(cmask, jnp.exp2(sc - mx), 0.0)
        den = jnp.sum(e, axis=0, keepdims=True)
        p = e / jnp.where(den > 0, den, 1.0)
        p_sum = p_sum + p
        o_cmp.append(jnp.dot(vct, p.astype(BF16), preferred_element_type=F32))

    isect = isect_ref[...]
    p_hi = p_sum.astype(BF16)
    p_lo = (p_sum - p_hi.astype(F32)).astype(BF16)
    imp = (jnp.dot(isect, p_hi, preferred_element_type=F32)
           + jnp.dot(isect, p_lo, preferred_element_type=F32))
    blk = lax.broadcasted_iota(jnp.int32, (n_slc, tq), 0)
    cur = t_row // SEL_BLOCK
    val = jnp.where(blk == 0, jnp.inf, jnp.where(blk == cur, jnp.inf, jnp.where(blk == cur - 1, jnp.inf, imp)))
    val = jnp.where(blk <= cur, val, -jnp.inf)
    rank = jnp.zeros((n_slc, tq), jnp.int32)
    for m in range(n_slc):
        vm = val[m:m + 1, :]
        before = jnp.where(vm > val, 1, jnp.where(vm == val, jnp.where(blk > m, 1, 0), 0))
        rank = rank + before
    bias = jnp.where(rank < n_sel, 0.0, NEG)
    if n_slc < LANE:
        bias = jnp.concatenate([bias, jnp.zeros((LANE - n_slc, tq), F32)], axis=0)
    bias_t = bias.T.astype(BF16)

    cos, sin = cos_ref[...], sin_ref[...]
    cols = [slice(r * HEAD_DIM, (r + 1) * HEAD_DIM) for r in range(C_GROUP_SIZE)]
    q_rot = [_rope_full(q_ref[:, c].astype(F32), cos, sin).astype(BF16) for c in cols]
    q_sel = [jnp.concatenate([q, bias_t], axis=1) for q in q_rot]

    _for_causal_prefix(i, tq, s_len, step, functools.partial(
        _nsa_selected, t_row=t_row, q_sel=q_sel, ke_ref=ke_ref, vst_ref=vst_ref,
        acc_ref=acc_ref, s_ref=s_ref, p_ref=p_ref, step=step))

    w_tiles = min((WIN_SIZE - 1 + tq - 1) // LANE + 1, s_len // LANE)
    w_rows = w_tiles * LANE
    w0 = pl.multiple_of(jnp.maximum(i * tq + tq - w_rows, 0), LANE)
    t0 = w0 // LANE

    def win_mask(r0, n):
        delta = t_row - (w0 + r0 + lax.broadcasted_iota(jnp.int32, (n, tq), 0))
        return jnp.where(delta >= 0, jnp.where(delta <= WIN_SIZE - 1, 1, 0), 0) > 0, None

    def pv_win(h, p):
        out = None
        for w in range(w_tiles):
            part = jnp.dot(vwt_ref[t0 + w], p[w * LANE:(w + 1) * LANE, :], preferred_element_type=F32)
            out = part if out is None else out + part
        return out

    o_win = _attend_heads(q_rot, lambda h, r0, n: kw_ref[pl.ds(w0 + r0, n), :], pv_win,
                          w_rows, win_mask, s_ref, p_ref)

    gates = jax.nn.sigmoid(g_ref[...].astype(F32).T)
    for r, c in enumerate(cols):
        o_t = (gates[3 * r:3 * r + 1, :] * o_cmp[r] + gates[3 * r + 1:3 * r + 2, :] * acc_ref[r]
               + gates[3 * r + 2:3 * r + 3, :] * o_win[r])
        o_ref[:, c] = o_t.T.astype(o_ref.dtype)


def _nsa_call(p3, kc, vct, cos, sin, isect, expand, tq, step):
    b, s, _ = p3.shape
    nc = kc.shape[2]
    n_slc = s // SEL_BLOCK
    width = C_GROUP_SIZE * HEAD_DIM
    kv = lambda name: pl.BlockSpec((None, s, HEAD_DIM), lambda bi, g, i, name=name: (bi, 0, _OFF[name] + g))
    return pl.pallas_call(
        functools.partial(_nsa_kernel, n_sel=min(SEL_COUNT, n_slc), step=step),
        grid=(b, C_KV_GROUPS, s // tq),
        in_specs=[
            pl.BlockSpec((None, tq, width), lambda bi, g, i: (bi, i, _OFF["c_q"] // 4 + g)),
            pl.BlockSpec((None, None, nc, HEAD_DIM), lambda bi, g, i: (bi, g, 0, 0)),
            pl.BlockSpec((None, None, HEAD_DIM, nc), lambda bi, g, i: (bi, g, 0, 0)),
            kv("c_ks"), kv("c_vs"), kv("c_kw"), kv("c_vw"),
            pl.BlockSpec((None, tq, LANE), lambda bi, g, i: (bi, i, _OFF["c_g0"] + g)),
            pl.BlockSpec((tq, LANE), lambda bi, g, i: (i, 0)),
            pl.BlockSpec((tq, LANE), lambda bi, g, i: (i, 0)),
            pl.BlockSpec(isect.shape, lambda bi, g, i: (0, 0)),
            pl.BlockSpec(expand.shape, lambda bi, g, i: (0, 0)),
        ],
        out_specs=pl.BlockSpec((None, tq, width), lambda bi, g, i: (bi, i, g)),
        out_shape=jax.ShapeDtypeStruct((b, s, C_HEADS * HEAD_DIM), BF16),
        scratch_shapes=[
            pltpu.VMEM((HEAD_DIM + ONES_ROWS, s), BF16),
            pltpu.VMEM((s // LANE, HEAD_DIM + ONES_ROWS, LANE), BF16),
            pltpu.VMEM((C_GROUP_SIZE, HEAD_DIM, tq), F32),
            pltpu.VMEM((C_GROUP_SIZE, s, tq), F32),
            pltpu.VMEM((C_GROUP_SIZE, s, tq), BF16),
            pltpu.VMEM((s, HEAD_DIM + expand.shape[1]), BF16),
        ],
        compiler_params=_params(("parallel", "parallel", "arbitrary"), 48),
        name="nsa",
    )(p3, kc, vct, p3, p3, p3, p3, p3, cos, sin, isect, expand)


def _out_kernel(oa_ref, ob_ref, oc_ref, w_ref, x_ref, gpost_ref, gnext_ref, xo_ref, h_ref):
    na, nb = oa_ref.shape[1], ob_ref.shape[1]
    tm = x_ref.shape[0]
    sub = min(OUT_SUB_ROWS, tm)
    for r0 in range(0, tm, sub):
        r = slice(r0, r0 + sub)
        y = jnp.dot(oa_ref[r, :], w_ref[0:na, :], preferred_element_type=F32)
        y = y + jnp.dot(ob_ref[r, :], w_ref[na:na + nb, :], preferred_element_type=F32)
        y = y + jnp.dot(oc_ref[r, :], w_ref[na + nb:, :], preferred_element_type=F32)
        x = x_ref[r, :] + _rms(y, gpost_ref[...])
        xo_ref[r, :] = x
        h_ref[r, :] = _rms(x, gnext_ref[...]).astype(h_ref.dtype)


def _out_call(o_a, o_b, o_c, w_out_p, layer, x, g_post, g_next, tm):
    m, d = x.shape
    row = lambda width: pl.BlockSpec((tm, width), lambda i: (i, 0))
    vec = pl.BlockSpec((1, d), lambda i: (0, 0))
    return pl.pallas_call(
        _out_kernel,
        grid=(m // tm,),
        in_specs=[
            row(o_a.shape[1]), row(o_b.shape[1]), row(o_c.shape[1]),
            _resident((None, w_out_p.shape[1], d), lambda i: (layer, 0, 0)),
            row(d), vec, vec,
        ],
        out_specs=[row(d), row(d)],
        out_shape=[jax.ShapeDtypeStruct((m, d), F32), jax.ShapeDtypeStruct((m, d), BF16)],
        compiler_params=_params(("parallel",), 48),
        name="out_proj",
    )(o_a, o_b, o_c, w_out_p, x, g_post, g_next)


def _ffn_kernel(h_ref, wg_ref, wu_ref, wd_ref, y_ref):
    @pl.when(pl.program_id(1) == 0)
    def _():
        y_ref[...] = jnp.zeros(y_ref.shape, F32)

    h = h_ref[...]
    g = jnp.dot(h, wg_ref[...], preferred_element_type=F32)
    u = jnp.dot(h, wu_ref[...], preferred_element_type=F32)
    a = (g * jax.nn.sigmoid(g) * u).astype(BF16)
    y_ref[...] += jnp.dot(a, wd_ref[...], preferred_element_type=F32)


def _ffn_call(h, wg, wu, wd, layer, tm, tf):
    m, d = h.shape
    d_ff = wg.shape[2]
    row = pl.BlockSpec((tm, d), lambda i, f: (i, 0))
    return pl.pallas_call(
        _ffn_kernel,
        grid=(m // tm, d_ff // tf),
        in_specs=[
            row,
            pl.BlockSpec((None, d, tf), lambda i, f: (layer, 0, f)),
            pl.BlockSpec((None, d, tf), lambda i, f: (layer, 0, f)),
            pl.BlockSpec((None, tf, d), lambda i, f: (layer, f, 0)),
        ],
        out_specs=row,
        out_shape=jax.ShapeDtypeStruct((m, d), F32),
        compiler_params=_params(("parallel", "arbitrary"), 52),
        name="ffn",
    )(h, wg, wu, wd)


def _residual_kernel(y_ref, x_ref, g_ref, o_ref):
    o_ref[...] = x_ref[...] + _rms(y_ref[...], g_ref[...])


def _residual_call(y, x, g, tm):
    m, d = x.shape
    row = pl.BlockSpec((tm, d), lambda i: (i, 0))
    return pl.pallas_call(
        _residual_kernel,
        grid=(m // tm,),
        in_specs=[row, row, pl.BlockSpec((1, d), lambda i: (0, 0))],
        out_specs=row,
        out_shape=jax.ShapeDtypeStruct((m, d), F32),
        compiler_params=_params(("parallel",), 32),
        name="ffn_residual",
    )(y, x, g)


def _w_in_layout():
    widths = (A_HEADS * HEAD_DIM, HEAD_DIM, HEAD_DIM, IDX_HEADS * IDX_DIM, IDX_DIM, IDX_HEADS,
              B_HEADS * HEAD_DIM, B_HEADS * HEAD_DIM, B_HEADS * HEAD_DIM, C_HEADS * HEAD_DIM,
              C_KV_GROUPS * HEAD_DIM, C_KV_GROUPS * HEAD_DIM, C_KV_GROUPS * HEAD_DIM, C_KV_GROUPS * HEAD_DIM,
              C_KV_GROUPS * HEAD_DIM, C_KV_GROUPS * HEAD_DIM, 3 * C_HEADS)
    names = ("a_q", "a_k", "a_v", "i_q", "i_k", "i_w", "b_q", "b_k", "b_v", "c_q",
             "c_kc", "c_vc", "c_ks", "c_vs", "c_kw", "c_vw", "c_g")
    offs = np.concatenate([[0], np.cumsum(widths)])
    src = {n: (int(offs[k]), int(widths[k])) for k, n in enumerate(names)}
    per_group = 3 * C_GROUP_SIZE
    src["c_g0"] = (src["c_g"][0], per_group)
    src["c_g1"] = (src["c_g"][0] + per_group, per_group)
    pieces = []
    dst = 0
    for name, blocks in [(n, w) for n, w, _, _ in _SEGMENTS] + list(_CMP_SEGMENTS):
        start, width = src[name]
        rows = blocks * LANE
        if name == "i_k":
            pieces.append((dst, [(start, width), (start, width)], 0))
        elif width < rows:
            pieces.append((dst, [(start, width)], rows - width))
        else:
            for r0 in range(0, rows, _PROJ_CHUNK * LANE):
                pieces.append((dst + r0, [(start + r0, min(_PROJ_CHUNK * LANE, rows - r0))], 0))
        dst += rows
    return pieces, int(offs[-1])


def _w_prep_kernel(w_ref, o_ref, *, pieces):
    cols = w_ref.shape[1]
    for dst, parts, n_zero in pieces:
        vals = [w_ref[r:r + n, :] for r, n in parts]
        if n_zero:
            vals.append(jnp.zeros((n_zero, cols), F32))
        blk = vals[0] if len(vals) == 1 else jnp.concatenate(vals, axis=0)
        o_ref[dst:dst + blk.shape[0], :] = blk.astype(o_ref.dtype)


def _prep_w_in(w_in, tk=256):
    w_t = jnp.swapaxes(w_in, 1, 2)
    depth, width, d = w_t.shape
    pieces, src_width = _w_in_layout()
    assert width == src_width
    return pl.pallas_call(
        functools.partial(_w_prep_kernel, pieces=pieces),
        grid=(depth, d // tk),
        in_specs=[pl.BlockSpec((None, width, tk), lambda l, i: (l, 0, i))],
        out_specs=pl.BlockSpec((None, NW, tk), lambda l, i: (l, 0, i)),
        out_shape=jax.ShapeDtypeStruct((depth, NW, d), BF16),
        compiler_params=_params(("parallel", "parallel"), 40),
        name="w_in_prep",
    )(w_t)


def _rope_tables(seq):
    def tables(dim):
        inv = 1.0 / (ROPE_THETA ** (jnp.arange(0, dim, 2, dtype=F32) / dim))
        ang = jnp.arange(seq, dtype=F32)[:, None] * inv[None, :]
        cos, sin = jnp.cos(ang), jnp.sin(ang)
        reps = LANE // dim
        return (jnp.tile(jnp.concatenate([cos, cos], axis=-1), (1, reps)),
                jnp.tile(jnp.concatenate([-sin, sin], axis=-1), (1, reps)))
    return tables(HEAD_DIM) + tables(IDX_DIM)


def _selection_constants(seq):
    nc = seq // CMP_STRIDE
    n_slc = seq // SEL_BLOCK
    c_start = np.arange(nc) * CMP_STRIDE
    n_start = np.arange(n_slc) * SEL_BLOCK
    isect = ((c_start[None, :] < n_start[:, None] + SEL_BLOCK)
             & (c_start[None, :] + CMP_BLOCK > n_start[:, None])
             & (np.arange(nc)[None, :] < nc - CMP_BLOCK // CMP_STRIDE + 1)).astype(np.float32)
    expand = np.zeros((seq, max(n_slc, LANE)), np.float32)
    expand[np.arange(seq), np.arange(seq) // SEL_BLOCK] = 1.0
    return jnp.asarray(isect, BF16), jnp.asarray(expand, BF16)


def _tiles(seq):
    return dict(tq=min(2 * LANE, seq), step_dsa=min(4 * LANE, seq), step=min(2 * LANE, seq),
                tm_proj=min(256, seq), tm_out=min(512, seq), tm_ffn=min(1024, seq), tf=512)


def kernel(x, w_in, w_out, cmp_pe_k, cmp_w1_k, cmp_w2_k, cmp_pe_v, cmp_w1_v, cmp_w2_v,
           w_gate, w_up, w_down, g_pre_mix, g_post_mix, g_pre_ffn, g_post_ffn):
    b, s, d = x.shape
    depth = w_in.shape[0]
    m = b * s
    t = _tiles(s)
    tq, step = t["tq"], t["step"]

    w_in_p = _prep_w_in(w_in)
    w_out_p = w_out.astype(BF16)
    wg, wu, wd = w_gate.astype(BF16), w_up.astype(BF16), w_down.astype(BF16)
    w1_k, w2_k = cmp_w1_k.astype(BF16), cmp_w2_k.astype(BF16)
    w1_v, w2_v = cmp_w1_v.astype(BF16), cmp_w2_v.astype(BF16)
    col_scale = jnp.asarray(_COL_SCALE)
    tabs = _rope_tables(s)
    isect, expand = _selection_constants(s)
    vec = lambda g, layer: g[layer][None, :]

    xf = x.reshape(m, d)
    y = None
    for layer in range(depth):
        if layer == 0:
            p, pc = _proj_call((xf, vec(g_pre_mix, 0)), w_in_p, layer, col_scale, tabs, s, t["tm_proj"])
        else:
            rows_in = (y, xf, vec(g_post_ffn, layer - 1), vec(g_pre_mix, layer))
            xf, p, pc = _proj_call(rows_in, w_in_p, layer, col_scale, tabs, s, t["tm_proj"])
        p3 = p.reshape(b, s, NP)
        o_a = _dsa_call(p3, tq, t["step_dsa"])
        o_b = _dil_call(p3, tq, step)
        kc, vct = _cmp_call(pc, b, layer, cmp_pe_k, w1_k, w2_k, cmp_pe_v, w1_v, w2_v)
        o_c = _nsa_call(p3, kc, vct, tabs[0], tabs[1], isect, expand, tq, step)
        xf, h = _out_call(o_a.reshape(m, -1), o_b.reshape(m, -1), o_c.reshape(m, -1), w_out_p, layer,
                          xf, vec(g_post_mix, layer), vec(g_pre_ffn, layer), t["tm_out"])
        y = _ffn_call(h, wg, wu, wd, layer, t["tm_ffn"], t["tf"])
    return _residual_call(y, xf, vec(g_post_ffn, depth - 1), t["tm_out"]).reshape(b, s, d)
```

```python
import functools

import numpy as np
import jax
import jax.numpy as jnp
from jax import lax
from jax.experimental import pallas as pl
from jax.experimental.pallas import tpu as pltpu

F32 = jnp.float32
BF16 = jnp.bfloat16

LANE = 128
SUBLANE = 8
HEAD_DIM = 128
A_HEADS = 4
B_HEADS = 4
C_HEADS = 8
C_KV_GROUPS = 2
C_GROUP_SIZE = C_HEADS // C_KV_GROUPS
IDX_HEADS = 16
IDX_DIM = 64
DSA_TOPK_MAX = 256
DILATED_PATTERNS = ((128, 1), (512, 4), (2048, 16))
CMP_BLOCK = 32
CMP_STRIDE = 16
CMP_HIDDEN = 256
SEL_BLOCK = 64
SEL_COUNT = 16
WIN_SIZE = 512
ROPE_THETA = 10000.0
RMS_EPS = 1e-6
ATTN_SCALE = HEAD_DIM ** -0.5 * float(np.log2(np.e))
IDX_SCALE = IDX_DIM ** -0.5 * IDX_HEADS ** -0.5

NEG = -1e30
INT_MIN = -(2 ** 31)
HALF16 = 2 ** 15
COUNT_ROWS = 8 * SUBLANE
SCORE_ROWS = 256
ATT_ROWS = 128
ONES_ROWS = 16
OUT_SUB_ROWS = 256

MODE_NONE, MODE_ROPE, MODE_IROPE = 0, 1, 2

_SEGMENTS = (
    ("i_q", 8, MODE_IROPE, IDX_SCALE),
    ("c_q", 8, MODE_NONE, ATTN_SCALE),
    ("a_q", 4, MODE_ROPE, ATTN_SCALE),
    ("b_q", 4, MODE_ROPE, ATTN_SCALE),
    ("b_k", 4, MODE_ROPE, 1.0),
    ("b_v", 4, MODE_NONE, 1.0),
    ("c_ks", 2, MODE_ROPE, 1.0),
    ("c_vs", 2, MODE_NONE, 1.0),
    ("c_kw", 2, MODE_ROPE, 1.0),
    ("c_vw", 2, MODE_NONE, 1.0),
    ("a_k", 1, MODE_ROPE, 1.0),
    ("a_v", 1, MODE_NONE, 1.0),
    ("i_k", 1, MODE_IROPE, 1.0),
    ("i_w", 1, MODE_NONE, 1.0),
    ("c_g0", 1, MODE_NONE, 1.0),
    ("c_g1", 1, MODE_NONE, 1.0),
)
_CMP_SEGMENTS = (("c_kc", 2), ("c_vc", 2))
_OFF = {}
_o = 0
for _n, _w, _m, _s in _SEGMENTS:
    _OFF[_n] = _o
    _o += _w
NP_BLOCKS = _o
NP = NP_BLOCKS * LANE
CMP_BLOCKS = sum(w for _, w in _CMP_SEGMENTS)
CMP_COLS = CMP_BLOCKS * LANE
NW = NP + CMP_COLS
_BLOCK_MODE = tuple(m for _, w, m, _ in _SEGMENTS for _ in range(w))
_COL_SCALE = np.repeat(np.array([s for _, w, _, s in _SEGMENTS for _ in range(w)], np.float32), LANE)[None, :]
_PROJ_CHUNK = 4


def _params(sem, vmem_mb):
    return pltpu.CompilerParams(dimension_semantics=sem, vmem_limit_bytes=vmem_mb * 1024 * 1024)


def _resident(block_shape, index_map):
    return pl.BlockSpec(block_shape, index_map, pipeline_mode=pl.Buffered(1))


def _rms(x, g):
    return x * lax.rsqrt(jnp.mean(x * x, axis=-1, keepdims=True) + RMS_EPS) * g


def _rope_full(a, cos, sin):
    return a * cos + pltpu.roll(a, HEAD_DIM // 2, 1) * sin


def _rope_idx(a, cos, sin, first_half):
    partner = jnp.where(first_half, pltpu.roll(a, LANE - IDX_DIM // 2, 1), pltpu.roll(a, IDX_DIM // 2, 1))
    return a * cos + partner * sin


def _proj_kernel(*refs, after_ffn):
    if after_ffn:
        (y_ref, x_ref, gpost_ref, gpre_ref, w_ref, cs_ref, cos_ref, sin_ref, icos_ref, isin_ref,
         xo_ref, o_ref, oc_ref, cmp_ref) = refs
        x = x_ref[...] + _rms(y_ref[...], gpost_ref[...])
        xo_ref[...] = x
        h = _rms(x, gpre_ref[...]).astype(BF16)
    else:
        x_ref, gpre_ref, w_ref, cs_ref, cos_ref, sin_ref, icos_ref, isin_ref, o_ref, oc_ref, cmp_ref = refs
        h = _rms(x_ref[...], gpre_ref[...]).astype(BF16)
    tm = h.shape[0]
    lane = lax.broadcasted_iota(jnp.int32, (tm, LANE), 1)
    first_half = (lane & (IDX_DIM - 1)) < IDX_DIM // 2
    for c0 in range(0, NP_BLOCKS, _PROJ_CHUNK):
        nb = min(_PROJ_CHUNK, NP_BLOCKS - c0)
        acc = lax.dot_general(h, w_ref[c0 * LANE:(c0 + nb) * LANE, :], (((1,), (1,)), ((), ())),
                              preferred_element_type=F32)
        for b in range(nb):
            col = slice((c0 + b) * LANE, (c0 + b + 1) * LANE)
            a = acc[:, b * LANE:(b + 1) * LANE] * cs_ref[:, col]
            mode = _BLOCK_MODE[c0 + b]
            if mode == MODE_ROPE:
                a = _rope_full(a, cos_ref[...], sin_ref[...])
            elif mode == MODE_IROPE:
                a = _rope_idx(a, icos_ref[...], isin_ref[...], first_half)
            o_ref[:, col] = a.astype(o_ref.dtype)
    acc = lax.dot_general(h, w_ref[NP:NW, :], (((1,), (1,)), ((), ())), preferred_element_type=F32)
    for b in range(CMP_BLOCKS):
        cmp_ref[b] = acc[:, b * LANE:(b + 1) * LANE]
    for j in range(CMP_STRIDE):
        for b in range(CMP_BLOCKS):
            rows = cmp_ref[b, pl.ds(j, tm // CMP_STRIDE, stride=CMP_STRIDE), :]
            oc_ref[:, j * CMP_COLS + b * LANE:j * CMP_COLS + (b + 1) * LANE] = rows.astype(oc_ref.dtype)


def _proj_call(rows_in, w_in_p, layer, col_scale, tabs, seq, tm):
    after_ffn = len(rows_in) > 2
    m, d = rows_in[0].shape
    tpb = seq // tm
    tab_spec = pl.BlockSpec((tm, LANE), lambda i: (i % tpb, 0))
    row = pl.BlockSpec((tm, d), lambda i: (i, 0))
    vec = pl.BlockSpec((1, d), lambda i: (0, 0))
    out_specs = [
        pl.BlockSpec((tm, NP), lambda i: (i, 0)),
        pl.BlockSpec((tm // CMP_STRIDE, CMP_STRIDE * CMP_COLS), lambda i: (i, 0)),
    ]
    out_shape = [
        jax.ShapeDtypeStruct((m, NP), BF16),
        jax.ShapeDtypeStruct((m // CMP_STRIDE, CMP_STRIDE * CMP_COLS), BF16),
    ]
    if after_ffn:
        out_specs = [row] + out_specs
        out_shape = [jax.ShapeDtypeStruct((m, d), F32)] + out_shape
    return pl.pallas_call(
        functools.partial(_proj_kernel, after_ffn=after_ffn),
        grid=(m // tm,),
        in_specs=([row, row, vec, vec] if after_ffn else [row, vec]) + [
            _resident((None, NW, d), lambda i: (layer, 0, 0)),
            _resident((1, NP), lambda i: (0, 0)),
            tab_spec, tab_spec, tab_spec, tab_spec,
        ],
        out_specs=out_specs,
        out_shape=out_shape,
        scratch_shapes=[pltpu.VMEM((CMP_BLOCKS, tm, LANE), F32)],
        compiler_params=_params(("parallel",), 54),
        name="in_proj",
    )(*rows_in, w_in_p, col_scale, *tabs)


def _transpose_v(dst_ref, src_ref, col0=0):
    rows = src_ref.shape[0]
    for r0 in range(0, rows, LANE):
        tile = src_ref[r0:r0 + LANE, col0:col0 + HEAD_DIM].astype(F32)
        dst_ref[0:HEAD_DIM, r0:r0 + LANE] = tile.T.astype(dst_ref.dtype)
    dst_ref[HEAD_DIM:, :] = jnp.ones((ONES_ROWS, rows), dst_ref.dtype)


def _transpose_v_tiles(dst_ref, src_ref):
    rows, _ = src_ref.shape
    for r in range(rows // LANE):
        tile = src_ref[r * LANE:(r + 1) * LANE, :].astype(F32)
        dst_ref[r, 0:HEAD_DIM, :] = tile.T.astype(dst_ref.dtype)
        dst_ref[r, HEAD_DIM:, :] = jnp.ones((ONES_ROWS, LANE), dst_ref.dtype)


def _for_causal_prefix(i, tq, s_len, step, body):
    n = ((i + 1) * tq + step - 1) // step
    for v in range(1, s_len // step + 1):
        pl.when(n == v)(functools.partial(body, v * step))


def _fold(x, op):
    out = x[0:COUNT_ROWS]
    for r0 in range(COUNT_ROWS, x.shape[0], COUNT_ROWS):
        out = op(out, x[r0:r0 + COUNT_ROWS])
    return out


def _fold_tree(x, op):
    while x.shape[0] > SUBLANE:
        half = x.shape[0] // 2
        x = op(x[:half], x[half:])
    return x


def _row_chunk(rows):
    return SCORE_ROWS if rows % SCORE_ROWS == 0 else LANE


def _attend_heads(qs, k_of, pv_of, rows, mask_of, s_ref, p_ref, w_ref=None, plain_rows=0):
    n_heads = len(qs)
    tq = qs[0].shape[0]
    chunk = ATT_ROWS * LANE // tq
    for h in range(n_heads):
        s_ref[h, 0:rows, :] = lax.dot_general(k_of(h, 0, rows), qs[h], (((1,), (1,)), ((), ())),
                                              preferred_element_type=F32)
    m_acc = [jnp.full((SUBLANE, tq), NEG, F32) for _ in qs]
    weighted = False
    for r0 in range(0, rows, chunk):
        if r0 + chunk <= plain_rows:
            for h in range(n_heads):
                m_acc[h] = jnp.maximum(m_acc[h], _fold_tree(s_ref[h, r0:r0 + chunk, :], jnp.maximum))
            continue
        mask, weight = mask_of(r0, chunk)
        if weight is not None:
            weighted = True
            w_ref[r0:r0 + chunk, :] = weight
        for h in range(n_heads):
            s = jnp.where(mask, s_ref[h, r0:r0 + chunk, :], NEG)
            s_ref[h, r0:r0 + chunk, :] = s
            m_acc[h] = jnp.maximum(m_acc[h], _fold_tree(s, jnp.maximum))
    outs = []
    for h in range(n_heads):
        mx = jnp.max(m_acc[h], axis=0, keepdims=True)
        for r0 in range(0, rows, chunk):
            p = jnp.exp2(s_ref[h, r0:r0 + chunk, :] - mx)
            if weighted:
                p = p * w_ref[r0:r0 + chunk, :]
            p_ref[h, r0:r0 + chunk, :] = p.astype(p_ref.dtype)
        pv = pv_of(h, p_ref[h, 0:rows, :])
        outs.append(pv[0:HEAD_DIM] / pv[HEAD_DIM:HEAD_DIM + 1])
    return outs


def _count_rows(key_ref, rows, pred):
    chunk = _row_chunk(rows)
    acc = None
    for r0 in range(0, rows, chunk):
        part = _fold(pred(key_ref[r0:r0 + chunk, :], r0), jnp.add)
        acc = part if acc is None else acc + part
    return jnp.sum(acc.astype(jnp.int32), axis=0, keepdims=True)


def _dsa_prefix(rows, i, iq_ref, iw_t, ik_ref, q_ref, k_ref, vt_ref, o_ref, key_ref, cut_ref, s_ref, p_ref,
                hi_ref, lo_ref, topk):
    tq = q_ref.shape[0]
    chunk = min(SCORE_ROWS * LANE // tq, rows)
    lane = lax.broadcasted_iota(jnp.int32, (chunk, LANE), 1)
    crow = lax.broadcasted_iota(jnp.int32, (chunk, tq), 0)
    ctcol = i * tq + lax.broadcasted_iota(jnp.int32, (chunk, tq), 1)

    for r0 in range(0, rows, chunk):
        ik = ik_ref[r0:r0 + chunk, :]
        halves = (jnp.where(lane < IDX_DIM, ik, jnp.zeros_like(ik)),
                  jnp.where(lane >= IDX_DIM, ik, jnp.zeros_like(ik)))
        score = jnp.zeros((chunk, tq), F32)
        for p in range(IDX_HEADS // 2):
            blk = iq_ref[:, p * LANE:(p + 1) * LANE]
            for half in range(2):
                j = 2 * p + half
                lg = lax.dot_general(halves[half], blk, (((1,), (1,)), ((), ())), preferred_element_type=F32)
                score = score + jnp.maximum(lg, 0.0) * iw_t[j:j + 1, :]
        bits = pltpu.bitcast(score, jnp.int32)
        key = bits ^ ((bits >> 31) & 0x7FFFFFFF)
        key = jnp.where(r0 + crow <= ctcol, key, INT_MIN)
        key_ref[r0:r0 + chunk, :] = key
        hi_ref[r0:r0 + chunk, :] = (key >> 16).astype(jnp.int16)
        lo_ref[r0:r0 + chunk, :] = ((key & 0xFFFF) - HALF16).astype(jnp.int16)

    def search16(ref, target):
        def bit_step(b, off):
            cand_off = off | jnp.left_shift(jnp.int32(1), 15 - b)
            cand = (cand_off - HALF16).astype(jnp.int16)
            cnt = _count_rows(ref, rows, lambda c, r0: jnp.where(c >= cand, jnp.int16(1), jnp.int16(0)))
            return jnp.where(cnt >= target, cand_off, off)

        return lax.fori_loop(0, 16, bit_step, jnp.zeros((1, tq), jnp.int32)) - HALF16

    thr_hi = search16(hi_ref, topk)
    thr_hi16 = thr_hi.astype(jnp.int16)
    above = _count_rows(hi_ref, rows, lambda c, r0: jnp.where(c > thr_hi16, jnp.int16(1), jnp.int16(0)))
    chunk16 = _row_chunk(rows)
    for r0 in range(0, rows, chunk16):
        same = hi_ref[r0:r0 + chunk16, :] == thr_hi16
        hi_ref[r0:r0 + chunk16, :] = jnp.where(same, lo_ref[r0:r0 + chunk16, :], jnp.int16(-HALF16))
    thr_lo = search16(hi_ref, topk - above)
    thr = jnp.left_shift(thr_hi, 16) | (thr_lo + HALF16)
    short = thr == INT_MIN
    need = topk - _count_rows(key_ref, rows, lambda kc, r0: jnp.where(kc > thr, 1, 0))
    n_eq = _count_rows(key_ref, rows, lambda kc, r0: jnp.where(kc == thr, 1, 0))
    excess = jnp.where(n_eq > need, jnp.where(short, 0, 1), 0)
    cut_ref[...] = jnp.where(short, -1, rows)

    def row_ids(r0, n):
        return r0 + lax.broadcasted_iota(jnp.int32, (n, tq), 0)

    @pl.when(jnp.max(excess) > 0)
    def _():
        n_bits = max(1, (rows - 1).bit_length())

        def idx_step(b, cut):
            cand = cut | jnp.left_shift(jnp.int32(1), n_bits - 1 - b)
            below = _count_rows(key_ref, rows, lambda kc, r0: jnp.where(
                kc == thr, jnp.where(row_ids(r0, kc.shape[0]) < cand, 1, 0), 0))
            return jnp.where(below < need, cand, cut)

        cut = lax.fori_loop(0, n_bits, idx_step, jnp.zeros((1, tq), jnp.int32))
        cut_ref[...] = jnp.where(excess > 0, cut, cut_ref[...])

    cut = cut_ref[...]

    def mask_of(r0, n):
        kc = key_ref[r0:r0 + n, :]
        chosen = jnp.where(kc > thr, 1, jnp.where(kc == thr, jnp.where(row_ids(r0, n) <= cut, 1, 0), 0))
        return chosen > 0, None

    vt = vt_ref[:, 0:rows]
    cols = [slice(h * HEAD_DIM, (h + 1) * HEAD_DIM) for h in range(A_HEADS)]
    outs = _attend_heads([q_ref[:, c] for c in cols], lambda h, r0, n: k_ref[r0:r0 + n, :],
                         lambda h, p: jnp.dot(vt, p, preferred_element_type=F32),
                         rows, mask_of, s_ref, p_ref)
    for h, c in enumerate(cols):
        o_ref[:, c] = outs[h].T.astype(o_ref.dtype)


def _dsa_kernel(iq_ref, iw_ref, ik_ref, q_ref, k_ref, v_ref, o_ref, vt_ref, key_ref, cut_ref, s_ref, p_ref,
                hi_ref, lo_ref, *, topk, step):
    i = pl.program_id(1)
    tq = q_ref.shape[0]
    s_len = k_ref.shape[0]

    @pl.when(i == 0)
    def _():
        _transpose_v(vt_ref, v_ref)

    iw_t = iw_ref[...].astype(F32).T[:IDX_HEADS, :]
    _for_causal_prefix(i, tq, s_len, step, functools.partial(
        _dsa_prefix, i=i, iq_ref=iq_ref, iw_t=iw_t, ik_ref=ik_ref, q_ref=q_ref, k_ref=k_ref, vt_ref=vt_ref,
        o_ref=o_ref, key_ref=key_ref, cut_ref=cut_ref, s_ref=s_ref, p_ref=p_ref,
        hi_ref=hi_ref, lo_ref=lo_ref, topk=topk))


def _dsa_call(p3, tq, step):
    b, s, _ = p3.shape
    topk = min(DSA_TOPK_MAX, s // 4)
    blk = lambda name, width: _OFF[name] // width
    return pl.pallas_call(
        functools.partial(_dsa_kernel, topk=topk, step=step),
        grid=(b, s // tq),
        in_specs=[
            pl.BlockSpec((None, tq, 8 * LANE), lambda bi, i: (bi, i, blk("i_q", 8))),
            pl.BlockSpec((None, tq, LANE), lambda bi, i: (bi, i, blk("i_w", 1))),
            pl.BlockSpec((None, s, LANE), lambda bi, i: (bi, 0, blk("i_k", 1))),
            pl.BlockSpec((None, tq, 4 * LANE), lambda bi, i: (bi, i, blk("a_q", 4))),
            pl.BlockSpec((None, s, LANE), lambda bi, i: (bi, 0, blk("a_k", 1))),
            pl.BlockSpec((None, s, LANE), lambda bi, i: (bi, 0, blk("a_v", 1))),
        ],
        out_specs=pl.BlockSpec((None, tq, A_HEADS * HEAD_DIM), lambda bi, i: (bi, i, 0)),
        out_shape=jax.ShapeDtypeStruct((b, s, A_HEADS * HEAD_DIM), BF16),
        scratch_shapes=[
            pltpu.VMEM((HEAD_DIM + ONES_ROWS, s), BF16),
            pltpu.VMEM((s, tq), jnp.int32),
            pltpu.VMEM((1, tq), jnp.int32),
            pltpu.VMEM((A_HEADS, s, tq), F32),
            pltpu.VMEM((A_HEADS, s, tq), BF16),
            pltpu.VMEM((s, tq), jnp.int16),
            pltpu.VMEM((s, tq), jnp.int16),
        ],
        compiler_params=_params(("parallel", "arbitrary"), 48),
        name="dsa",
    )(p3, p3, p3, p3, p3, p3)


def _dil_prefix(rows, q_ref, k_ref, vt_ref, o_ref, s_ref, p_ref, w_ref):
    tq = q_ref.shape[0]
    first_q = rows - tq
    period = max(d for _, d in DILATED_PATTERNS)
    shared = {}

    def mask_of(r0, n):
        base = first_q - r0
        d_min, d_max = base - (n - 1), base + (tq - 1)
        live = tuple((w, d) for w, d in DILATED_PATTERNS if d_min <= w)
        on_edge = d_min < 0 or any(d_max > w for w, _ in live)
        key = None if on_edge else (live, base % period)
        if key in shared:
            return shared[key]
        delta = (base + lax.broadcasted_iota(jnp.int32, (n, tq), 1)
                 - lax.broadcasted_iota(jnp.int32, (n, tq), 0))
        mult = jnp.zeros((n, tq), jnp.int32)
        for window, dilation in live:
            hit = jnp.where(delta <= window, 1, 0) if d_max > window else 1
            mult = mult + jnp.where((delta & (dilation - 1)) == 0, hit, 0)
        if d_min < 0:
            mult = jnp.where(delta >= 0, mult, 0)
        out = (mult > 0, mult.astype(F32))
        if key is not None:
            shared[key] = out
        return out

    cols = [slice(h * HEAD_DIM, (h + 1) * HEAD_DIM) for h in range(B_HEADS)]
    outs = _attend_heads([q_ref[:, c] for c in cols], lambda h, r0, n: k_ref[r0:r0 + n, cols[h]],
                         lambda h, p: jnp.dot(vt_ref[h, :, 0:rows], p, preferred_element_type=F32),
                         rows, mask_of, s_ref, p_ref, w_ref)
    for h, c in enumerate(cols):
        o_ref[:, c] = outs[h].T.astype(o_ref.dtype)


def _dil_kernel(q_ref, k_ref, v_ref, o_ref, vt_ref, s_ref, p_ref, w_ref, *, step):
    i = pl.program_id(1)

    @pl.when(i == 0)
    def _():
        for h in range(B_HEADS):
            _transpose_v(vt_ref.at[h], v_ref, h * HEAD_DIM)

    assert step == q_ref.shape[0]
    _for_causal_prefix(i, q_ref.shape[0], k_ref.shape[0], step, functools.partial(
        _dil_prefix, q_ref=q_ref, k_ref=k_ref, vt_ref=vt_ref, o_ref=o_ref,
        s_ref=s_ref, p_ref=p_ref, w_ref=w_ref))


def _dil_call(p3, tq, step):
    b, s, _ = p3.shape
    width = B_HEADS * HEAD_DIM
    return pl.pallas_call(
        functools.partial(_dil_kernel, step=step),
        grid=(b, s // tq),
        in_specs=[
            pl.BlockSpec((None, tq, width), lambda bi, i: (bi, i, _OFF["b_q"] // 4)),
            pl.BlockSpec((None, s, width), lambda bi, i: (bi, 0, _OFF["b_k"] // 4)),
            pl.BlockSpec((None, s, width), lambda bi, i: (bi, 0, _OFF["b_v"] // 4)),
        ],
        out_specs=pl.BlockSpec((None, tq, width), lambda bi, i: (bi, i, 0)),
        out_shape=jax.ShapeDtypeStruct((b, s, width), BF16),
        scratch_shapes=[
            pltpu.VMEM((B_HEADS, HEAD_DIM + ONES_ROWS, s), BF16),
            pltpu.VMEM((B_HEADS, s, tq), F32),
            pltpu.VMEM((B_HEADS, s, tq), BF16),
            pltpu.VMEM((s, tq), F32),
        ],
        compiler_params=_params(("parallel", "arbitrary"), 48),
        name="dilated",
    )(p3, p3, p3)


def _gelu_tanh(x):
    return 0.5 * x * (1.0 + jnp.tanh(np.float32(np.sqrt(2.0 / np.pi)) * (x + 0.044715 * (x * x * x))))


def _cmp_kernel(*refs):
    n = CMP_STRIDE
    xk, xv = refs[:n], refs[n:2 * n]
    pe_k, w1_k, w2_k, pe_v, w1_v, w2_v, kc_ref, vct_ref = refs[2 * n:]

    def branch(x_refs, pe_ref, w1_ref, w2_ref):
        lo = hi = None
        for j in range(n):
            xj = x_refs[j][...].astype(F32)
            a = (xj + pe_ref[j:j + 1, :]).astype(BF16)
            b = (xj + pe_ref[n + j:n + j + 1, :]).astype(BF16)
            dl = jnp.dot(a, w1_ref[j * HEAD_DIM:(j + 1) * HEAD_DIM, :], preferred_element_type=F32)
            dh = jnp.dot(b, w1_ref[(n + j) * HEAD_DIM:(n + j + 1) * HEAD_DIM, :], preferred_element_type=F32)
            lo = dl if lo is None else lo + dl
            hi = dh if hi is None else hi + dh
        nc = lo.shape[0]
        hid = lo + pltpu.roll(hi, nc - 1, 0)
        return jnp.dot(_gelu_tanh(hid).astype(BF16), w2_ref[...], preferred_element_type=F32)

    kc_ref[...] = branch(xk, pe_k, w1_k, w2_k).astype(kc_ref.dtype)
    vct_ref[...] = branch(xv, pe_v, w1_v, w2_v).T.astype(vct_ref.dtype)


def _cmp_call(pc, b, layer, pe_k, w1_k, w2_k, pe_v, w1_v, w2_v):
    nc = pc.shape[0] // b
    pc3 = pc.reshape(b, nc, CMP_STRIDE * CMP_COLS)
    x_specs = []
    for first in (0, C_KV_GROUPS):
        for j in range(CMP_STRIDE):
            x_specs.append(pl.BlockSpec((None, nc, HEAD_DIM),
                                        lambda bi, g, j=j, first=first: (bi, 0, j * CMP_BLOCKS + first + g)))
    flat = CMP_BLOCK * HEAD_DIM
    w_specs = [
        _resident((None, CMP_BLOCK, HEAD_DIM), lambda bi, g: (layer, 0, 0)),
        _resident((None, flat, CMP_HIDDEN), lambda bi, g: (layer, 0, 0)),
        _resident((None, CMP_HIDDEN, HEAD_DIM), lambda bi, g: (layer, 0, 0)),
    ]
    return pl.pallas_call(
        _cmp_kernel,
        grid=(b, C_KV_GROUPS),
        in_specs=x_specs + w_specs + w_specs,
        out_specs=[
            pl.BlockSpec((None, None, nc, HEAD_DIM), lambda bi, g: (bi, g, 0, 0)),
            pl.BlockSpec((None, None, HEAD_DIM, nc), lambda bi, g: (bi, g, 0, 0)),
        ],
        out_shape=[
            jax.ShapeDtypeStruct((b, C_KV_GROUPS, nc, HEAD_DIM), BF16),
            jax.ShapeDtypeStruct((b, C_KV_GROUPS, HEAD_DIM, nc), BF16),
        ],
        compiler_params=_params(("parallel", "parallel"), 32),
        name="nsa_compress",
    )(*([pc3] * (2 * CMP_STRIDE)), pe_k, w1_k, w2_k, pe_v, w1_v, w2_v)


def _nsa_selected(rows, t_row, q_sel, ke_ref, vst_ref, acc_ref, s_ref, p_ref, step):
    tq = q_sel[0].shape[0]

    def mask_of(r0, n):
        srow = r0 + lax.broadcasted_iota(jnp.int32, (n, tq), 0)
        return t_row - srow >= 0, None

    vt = vst_ref[:, 0:rows]
    outs = _attend_heads(q_sel, lambda h, r0, n: ke_ref[r0:r0 + n, :],
                         lambda h, p: jnp.dot(vt, p, preferred_element_type=F32),
                         rows, mask_of, s_ref, p_ref, plain_rows=rows - max(step, tq))
    for r in range(C_GROUP_SIZE):
        acc_ref[r] = outs[r]


def _nsa_kernel(q_ref, kc_ref, vct_ref, ks_ref, vs_ref, kw_ref, vw_ref, g_ref, cos_ref, sin_ref,
                isect_ref, expand_ref, o_ref, vst_ref, vwt_ref, acc_ref, s_ref, p_ref, ke_ref, *, n_sel, step):
    i = pl.program_id(2)
    tq = q_ref.shape[0]
    s_len = ks_ref.shape[0]
    nc = kc_ref.shape[0]
    n_slc = isect_ref.shape[0]

    @pl.when(i == 0)
    def _():
        _transpose_v(vst_ref, vs_ref)
        _transpose_v_tiles(vwt_ref, vw_ref)
        ke_ref[:, 0:HEAD_DIM] = ks_ref[...]
        ke_ref[:, HEAD_DIM:] = expand_ref[...]

    t_row = i * tq + lax.broadcasted_iota(jnp.int32, (1, tq), 1)

    kc = kc_ref[...]
    vct = vct_ref[...]
    cend = lax.broadcasted_iota(jnp.int32, (nc, tq), 0) * CMP_STRIDE + (CMP_BLOCK - 1)
    cmask = cend <= t_row
    o_cmp = []
    p_sum = jnp.zeros((nc, tq), F32)
    for r in range(C_GROUP_SIZE):
        q_r = q_ref[:, r * HEAD_DIM:(r + 1) * HEAD_DIM]
        sc = lax.dot_general(kc, q_r, (((1,), (1,)), ((), ())), preferred_element_type=F32)
        sc = jnp.where(cmask, sc, NEG)
        mx = jnp.max(sc, axis=0, keepdims=True)
        e = jnp.where(cmask, jnp.exp2(sc - mx), 0.0)
        den = jnp.sum(e, axis=0, keepdims=True)
        p = e / jnp.where(den > 0, den, 1.0)
        p_sum = p_sum + p
        o_cmp.append(jnp.dot(vct, p.astype(BF16), preferred_element_type=F32))

    isect = isect_ref[...]
    p_hi = p_sum.astype(BF16)
    p_lo = (p_sum - p_hi.astype(F32)).astype(BF16)
    imp = (jnp.dot(isect, p_hi, preferred_element_type=F32)
           + jnp.dot(isect, p_lo, preferred_element_type=F32))
    blk = lax.broadcasted_iota(jnp.int32, (n_slc, tq), 0)
    cur = t_row // SEL_BLOCK
    val = jnp.where(blk == 0, jnp.inf, jnp.where(blk == cur, jnp.inf, jnp.where(blk == cur - 1, jnp.inf, imp)))
    val = jnp.where(blk <= cur, val, -jnp.inf)
    rank = jnp.zeros((n_slc, tq), jnp.int32)
    for m in range(n_slc):
        vm = val[m:m + 1, :]
        before = jnp.where(vm > val, 1, jnp.where(vm == val, jnp.where(blk > m, 1, 0), 0))
        rank = rank + before
    bias = jnp.where(rank < n_sel, 0.0, NEG)
    if n_slc < LANE:
        bias = jnp.concatenate([bias, jnp.zeros((LANE - n_slc, tq), F32)], axis=0)
    bias_t = bias.T.astype(BF16)

    cos, sin = cos_ref[...], sin_ref[...]
    cols = [slice(r * HEAD_DIM, (r + 1) * HEAD_DIM) for r in range(C_GROUP_SIZE)]
    q_rot = [_rope_full(q_ref[:, c].astype(F32), cos, sin).astype(BF16) for c in cols]
    q_sel = [jnp.concatenate([q, bias_t], axis=1) for q in q_rot]

    _for_causal_prefix(i, tq, s_len, step, functools.partial(
        _nsa_selected, t_row=t_row, q_sel=q_sel, ke_ref=ke_ref, vst_ref=vst_ref,
        acc_ref=acc_ref, s_ref=s_ref, p_ref=p_ref, step=step))

    w_tiles = min((WIN_SIZE - 1 + tq - 1) // LANE + 1, s_len // LANE)
    w_rows = w_tiles * LANE
    w0 = pl.multiple_of(jnp.maximum(i * tq + tq - w_rows, 0), LANE)
    t0 = w0 // LANE

    def win_mask(r0, n):
        delta = t_row - (w0 + r0 + lax.broadcasted_iota(jnp.int32, (n, tq), 0))
        return jnp.where(delta >= 0, jnp.where(delta <= WIN_SIZE - 1, 1, 0), 0) > 0, None

    def pv_win(h, p):
        out = None
        for w in range(w_tiles):
            part = jnp.dot(vwt_ref[t0 + w], p[w * LANE:(w + 1) * LANE, :], preferred_element_type=F32)
            out = part if out is None else out + part
        return out

    o_win = _attend_heads(q_rot, lambda h, r0, n: kw_ref[pl.ds(w0 + r0, n), :], pv_win,
                          w_rows, win_mask, s_ref, p_ref)

    gates = jax.nn.sigmoid(g_ref[...].astype(F32).T)
    for r, c in enumerate(cols):
        o_t = (gates[3 * r:3 * r + 1, :] * o_cmp[r] + gates[3 * r + 1:3 * r + 2, :] * acc_ref[r]
               + gates[3 * r + 2:3 * r + 3, :] * o_win[r])
        o_ref[:, c] = o_t.T.astype(o_ref.dtype)


def _nsa_call(p3, kc, vct, cos, sin, isect, expand, tq, step):
    b, s, _ = p3.shape
    nc = kc.shape[2]
    n_slc = s // SEL_BLOCK
    width = C_GROUP_SIZE * HEAD_DIM
    kv = lambda name: pl.BlockSpec((None, s, HEAD_DIM), lambda bi, g, i, name=name: (bi, 0, _OFF[name] + g))
    return pl.pallas_call(
        functools.partial(_nsa_kernel, n_sel=min(SEL_COUNT, n_slc), step=step),
        grid=(b, C_KV_GROUPS, s // tq),
        in_specs=[
            pl.BlockSpec((None, tq, width), lambda bi, g, i: (bi, i, _OFF["c_q"] // 4 + g)),
            pl.BlockSpec((None, None, nc, HEAD_DIM), lambda bi, g, i: (bi, g, 0, 0)),
            pl.BlockSpec((None, None, HEAD_DIM, nc), lambda bi, g, i: (bi, g, 0, 0)),
            kv("c_ks"), kv("c_vs"), kv("c_kw"), kv("c_vw"),
            pl.BlockSpec((None, tq, LANE), lambda bi, g, i: (bi, i, _OFF["c_g0"] + g)),
            pl.BlockSpec((tq, LANE), lambda bi, g, i: (i, 0)),
            pl.BlockSpec((tq, LANE), lambda bi, g, i: (i, 0)),
            pl.BlockSpec(isect.shape, lambda bi, g, i: (0, 0)),
            pl.BlockSpec(expand.shape, lambda bi, g, i: (0, 0)),
        ],
        out_specs=pl.BlockSpec((None, tq, width), lambda bi, g, i: (bi, i, g)),
        out_shape=jax.ShapeDtypeStruct((b, s, C_HEADS * HEAD_DIM), BF16),
        scratch_shapes=[
            pltpu.VMEM((HEAD_DIM + ONES_ROWS, s), BF16),
            pltpu.VMEM((s // LANE, HEAD_DIM + ONES_ROWS, LANE), BF16),
            pltpu.VMEM((C_GROUP_SIZE, HEAD_DIM, tq), F32),
            pltpu.VMEM((C_GROUP_SIZE, s, tq), F32),
            pltpu.VMEM((C_GROUP_SIZE, s, tq), BF16),
            pltpu.VMEM((s, HEAD_DIM + expand.shape[1]), BF16),
        ],
        compiler_params=_params(("parallel", "parallel", "arbitrary"), 48),
        name="nsa",
    )(p3, kc, vct, p3, p3, p3, p3, p3, cos, sin, isect, expand)


def _out_kernel(oa_ref, ob_ref, oc_ref, w_ref, x_ref, gpost_ref, gnext_ref, xo_ref, h_ref):
    na, nb = oa_ref.shape[1], ob_ref.shape[1]
    tm = x_ref.shape[0]
    sub = min(OUT_SUB_ROWS, tm)
    for r0 in range(0, tm, sub):
        r = slice(r0, r0 + sub)
        y = jnp.dot(oa_ref[r, :], w_ref[0:na, :], preferred_element_type=F32)
        y = y + jnp.dot(ob_ref[r, :], w_ref[na:na + nb, :], preferred_element_type=F32)
        y = y + jnp.dot(oc_ref[r, :], w_ref[na + nb:, :], preferred_element_type=F32)
        x = x_ref[r, :] + _rms(y, gpost_ref[...])
        xo_ref[r, :] = x
        h_ref[r, :] = _rms(x, gnext_ref[...]).astype(h_ref.dtype)


def _out_call(o_a, o_b, o_c, w_out_p, layer, x, g_post, g_next, tm):
    m, d = x.shape
    row = lambda width: pl.BlockSpec((tm, width), lambda i: (i, 0))
    vec = pl.BlockSpec((1, d), lambda i: (0, 0))
    return pl.pallas_call(
        _out_kernel,
        grid=(m // tm,),
        in_specs=[
            row(o_a.shape[1]), row(o_b.shape[1]), row(o_c.shape[1]),
            _resident((None, w_out_p.shape[1], d), lambda i: (layer, 0, 0)),
            row(d), vec, vec,
        ],
        out_specs=[row(d), row(d)],
        out_shape=[jax.ShapeDtypeStruct((m, d), F32), jax.ShapeDtypeStruct((m, d), BF16)],
        compiler_params=_params(("parallel",), 48),
        name="out_proj",
    )(o_a, o_b, o_c, w_out_p, x, g_post, g_next)


def _ffn_kernel(h_ref, wg_ref, wu_ref, wd_ref, *rest):
    y_ref = rest[-1]

    @pl.when(pl.program_id(1) == 0)
    def _():
        y_ref[...] = jnp.zeros(y_ref.shape, F32)

    h = h_ref[...]
    g = jnp.dot(h, wg_ref[...], preferred_element_type=F32)
    u = jnp.dot(h, wu_ref[...], preferred_element_type=F32)
    a = (g * jax.nn.sigmoid(g) * u).astype(BF16)
    y_ref[...] += jnp.dot(a, wd_ref[...], preferred_element_type=F32)

    if len(rest) > 1:
        x_ref, gpost_ref, _ = rest

        @pl.when(pl.program_id(1) == pl.num_programs(1) - 1)
        def _():
            sub = min(OUT_SUB_ROWS, y_ref.shape[0])
            for r0 in range(0, y_ref.shape[0], sub):
                r = slice(r0, r0 + sub)
                y_ref[r, :] = x_ref[r, :] + _rms(y_ref[r, :], gpost_ref[...])


def _ffn_call(h, wg, wu, wd, layer, tm, tf, residual=None):
    m, d = h.shape
    d_ff = wg.shape[2]
    row = pl.BlockSpec((tm, d), lambda i, f: (i, 0))
    extra_specs, extra = [], ()
    if residual is not None:
        extra_specs = [pl.BlockSpec((tm, d), lambda i, f: (i, 0), pipeline_mode=pl.Buffered(1)),
                       pl.BlockSpec((1, d), lambda i, f: (0, 0))]
        extra = residual
    return pl.pallas_call(
        _ffn_kernel,
        grid=(m // tm, d_ff // tf),
        in_specs=[
            row,
            pl.BlockSpec((None, d, tf), lambda i, f: (layer, 0, f)),
            pl.BlockSpec((None, d, tf), lambda i, f: (layer, 0, f)),
            pl.BlockSpec((None, tf, d), lambda i, f: (layer, f, 0)),
        ] + extra_specs,
        out_specs=row,
        out_shape=jax.ShapeDtypeStruct((m, d), F32),
        compiler_params=_params(("parallel", "arbitrary"), 54),
        name="ffn",
    )(h, wg, wu, wd, *extra)


def _w_in_layout():
    widths = (A_HEADS * HEAD_DIM, HEAD_DIM, HEAD_DIM, IDX_HEADS * IDX_DIM, IDX_DIM, IDX_HEADS,
              B_HEADS * HEAD_DIM, B_HEADS * HEAD_DIM, B_HEADS * HEAD_DIM, C_HEADS * HEAD_DIM,
              C_KV_GROUPS * HEAD_DIM, C_KV_GROUPS * HEAD_DIM, C_KV_GROUPS * HEAD_DIM, C_KV_GROUPS * HEAD_DIM,
              C_KV_GROUPS * HEAD_DIM, C_KV_GROUPS * HEAD_DIM, 3 * C_HEADS)
    names = ("a_q", "a_k", "a_v", "i_q", "i_k", "i_w", "b_q", "b_k", "b_v", "c_q",
             "c_kc", "c_vc", "c_ks", "c_vs", "c_kw", "c_vw", "c_g")
    offs = np.concatenate([[0], np.cumsum(widths)])
    src = {n: (int(offs[k]), int(widths[k])) for k, n in enumerate(names)}
    per_group = 3 * C_GROUP_SIZE
    src["c_g0"] = (src["c_g"][0], per_group)
    src["c_g1"] = (src["c_g"][0] + per_group, per_group)
    pieces = []
    dst = 0
    for name, blocks in [(n, w) for n, w, _, _ in _SEGMENTS] + list(_CMP_SEGMENTS):
        start, width = src[name]
        rows = blocks * LANE
        if name == "i_k":
            pieces.append((dst, [(start, width), (start, width)], 0))
        elif width < rows:
            pieces.append((dst, [(start, width)], rows - width))
        else:
            for r0 in range(0, rows, _PROJ_CHUNK * LANE):
                pieces.append((dst + r0, [(start + r0, min(_PROJ_CHUNK * LANE, rows - r0))], 0))
        dst += rows
    return pieces, int(offs[-1])


def _w_prep_kernel(w_ref, o_ref, *, pieces):
    cols = w_ref.shape[1]
    for dst, parts, n_zero in pieces:
        vals = [w_ref[r:r + n, :] for r, n in parts]
        if n_zero:
            vals.append(jnp.zeros((n_zero, cols), F32))
        blk = vals[0] if len(vals) == 1 else jnp.concatenate(vals, axis=0)
        o_ref[dst:dst + blk.shape[0], :] = blk.astype(o_ref.dtype)


def _prep_w_in(w_in, tk=256):
    w_t = jnp.swapaxes(w_in, 1, 2)
    depth, width, d = w_t.shape
    pieces, src_width = _w_in_layout()
    assert width == src_width
    return pl.pallas_call(
        functools.partial(_w_prep_kernel, pieces=pieces),
        grid=(depth, d // tk),
        in_specs=[pl.BlockSpec((None, width, tk), lambda l, i: (l, 0, i))],
        out_specs=pl.BlockSpec((None, NW, tk), lambda l, i: (l, 0, i)),
        out_shape=jax.ShapeDtypeStruct((depth, NW, d), BF16),
        compiler_params=_params(("parallel", "parallel"), 40),
        name="w_in_prep",
    )(w_t)


def _rope_tables(seq):
    def tables(dim):
        inv = 1.0 / (ROPE_THETA ** (jnp.arange(0, dim, 2, dtype=F32) / dim))
        ang = jnp.arange(seq, dtype=F32)[:, None] * inv[None, :]
        cos, sin = jnp.cos(ang), jnp.sin(ang)
        reps = LANE // dim
        return (jnp.tile(jnp.concatenate([cos, cos], axis=-1), (1, reps)),
                jnp.tile(jnp.concatenate([-sin, sin], axis=-1), (1, reps)))
    return tables(HEAD_DIM) + tables(IDX_DIM)


def _selection_constants(seq):
    nc = seq // CMP_STRIDE
    n_slc = seq // SEL_BLOCK
    c_start = np.arange(nc) * CMP_STRIDE
    n_start = np.arange(n_slc) * SEL_BLOCK
    isect = ((c_start[None, :] < n_start[:, None] + SEL_BLOCK)
             & (c_start[None, :] + CMP_BLOCK > n_start[:, None])
             & (np.arange(nc)[None, :] < nc - CMP_BLOCK // CMP_STRIDE + 1)).astype(np.float32)
    expand = np.zeros((seq, max(n_slc, LANE)), np.float32)
    expand[np.arange(seq), np.arange(seq) // SEL_BLOCK] = 1.0
    return jnp.asarray(isect, BF16), jnp.asarray(expand, BF16)


def _tiles(seq):
    return dict(tq=min(2 * LANE, seq), step_dsa=min(4 * LANE, seq), step=min(2 * LANE, seq),
                tm_proj=min(256, seq), tm_out=min(512, seq), tm_ffn=min(1024, seq), tf=512)


def kernel(x, w_in, w_out, cmp_pe_k, cmp_w1_k, cmp_w2_k, cmp_pe_v, cmp_w1_v, cmp_w2_v,
           w_gate, w_up, w_down, g_pre_mix, g_post_mix, g_pre_ffn, g_post_ffn):
    b, s, d = x.shape
    depth = w_in.shape[0]
    m = b * s
    t = _tiles(s)
    tq, step = t["tq"], t["step"]

    w_in_p = _prep_w_in(w_in)
    w_out_p = w_out.astype(BF16)
    wg, wu, wd = w_gate.astype(BF16), w_up.astype(BF16), w_down.astype(BF16)
    w1_k, w2_k = cmp_w1_k.astype(BF16), cmp_w2_k.astype(BF16)
    w1_v, w2_v = cmp_w1_v.astype(BF16), cmp_w2_v.astype(BF16)
    col_scale = jnp.asarray(_COL_SCALE)
    tabs = _rope_tables(s)
    isect, expand = _selection_constants(s)
    vec = lambda g, layer: g[layer][None, :]

    xf = x.reshape(m, d)
    y = None
    for layer in range(depth):
        if layer == 0:
            p, pc = _proj_call((xf, vec(g_pre_mix, 0)), w_in_p, layer, col_scale, tabs, s, t["tm_proj"])
        else:
            rows_in = (y, xf, vec(g_post_ffn, layer - 1), vec(g_pre_mix, layer))
            xf, p, pc = _proj_call(rows_in, w_in_p, layer, col_scale, tabs, s, t["tm_proj"])
        p3 = p.reshape(b, s, NP)
        o_a = _dsa_call(p3, tq, t["step_dsa"])
        o_b = _dil_call(p3, tq, step)
        kc, vct = _cmp_call(pc, b, layer, cmp_pe_k, w1_k, w2_k, cmp_pe_v, w1_v, w2_v)
        o_c = _nsa_call(p3, kc, vct, tabs[0], tabs[1], isect, expand, tq, step)
        xf, h = _out_call(o_a.reshape(m, -1), o_b.reshape(m, -1), o_c.reshape(m, -1), w_out_p, layer,
                          xf, vec(g_post_mix, layer), vec(g_pre_ffn, layer), t["tm_out"])
        last = (xf, vec(g_post_ffn, layer)) if layer == depth - 1 else None
        y = _ffn_call(h, wg, wu, wd, layer, t["tm_ffn"], t["tf"], residual=last)
    return y.reshape(b, s, d)
```

```python
import functools

import numpy as np
import jax
import jax.numpy as jnp
from jax import lax
from jax.experimental import pallas as pl
from jax.experimental.pallas import tpu as pltpu

F32 = jnp.float32
BF16 = jnp.bfloat16

LANE = 128
SUBLANE = 8
HEAD_DIM = 128
A_HEADS = 4
B_HEADS = 4
C_HEADS = 8
C_KV_GROUPS = 2
C_GROUP_SIZE = C_HEADS // C_KV_GROUPS
IDX_HEADS = 16
IDX_DIM = 64
DSA_TOPK_MAX = 256
DILATED_PATTERNS = ((128, 1), (512, 4), (2048, 16))
CMP_BLOCK = 32
CMP_STRIDE = 16
CMP_HIDDEN = 256
SEL_BLOCK = 64
SEL_COUNT = 16
WIN_SIZE = 512
ROPE_THETA = 10000.0
RMS_EPS = 1e-6
ATTN_SCALE = HEAD_DIM ** -0.5 * float(np.log2(np.e))
IDX_SCALE = IDX_DIM ** -0.5 * IDX_HEADS ** -0.5

NEG = -1e30
INT_MIN = -(2 ** 31)
HALF16 = 2 ** 15
COUNT_ROWS = 8 * SUBLANE
SCORE_ROWS = 256
ATT_ROWS = 128
ONES_ROWS = 16
OUT_SUB_ROWS = 256

MODE_NONE, MODE_ROPE, MODE_IROPE, MODE_MISC = 0, 1, 2, 3

_SEGMENTS = (
    ("i_q", 8, MODE_IROPE, IDX_SCALE),
    ("c_q", 8, MODE_NONE, ATTN_SCALE),
    ("a_q", 4, MODE_ROPE, ATTN_SCALE),
    ("b_q", 4, MODE_ROPE, ATTN_SCALE),
    ("b_k", 4, MODE_ROPE, 1.0),
    ("b_v", 4, MODE_NONE, 1.0),
    ("c_ks", 2, MODE_ROPE, 1.0),
    ("c_vs", 2, MODE_NONE, 1.0),
    ("c_kw", 2, MODE_ROPE, 1.0),
    ("c_vw", 2, MODE_NONE, 1.0),
    ("a_k", 1, MODE_ROPE, 1.0),
    ("a_v", 1, MODE_NONE, 1.0),
    ("misc", 1, MODE_MISC, 1.0),
)
_MISC_LANES = {"i_k": (0, IDX_DIM), "i_w": (IDX_DIM, IDX_HEADS), "c_g": (IDX_DIM + IDX_HEADS, 3 * C_HEADS)}
_CMP_SEGMENTS = (("c_kc", 2), ("c_vc", 2))
_OFF = {}
_o = 0
for _n, _w, _m, _s in _SEGMENTS:
    _OFF[_n] = _o
    _o += _w
NP_BLOCKS = _o
NP = NP_BLOCKS * LANE
CMP_BLOCKS = sum(w for _, w in _CMP_SEGMENTS)
CMP_COLS = CMP_BLOCKS * LANE
NW = NP + CMP_COLS
_BLOCK_MODE = tuple(m for _, w, m, _ in _SEGMENTS for _ in range(w))
_COL_SCALE = np.repeat(np.array([s for _, w, _, s in _SEGMENTS for _ in range(w)], np.float32), LANE)[None, :]
_PROJ_CHUNK = 4


def _params(sem, vmem_mb):
    return pltpu.CompilerParams(dimension_semantics=sem, vmem_limit_bytes=vmem_mb * 1024 * 1024)


def _resident(block_shape, index_map):
    return pl.BlockSpec(block_shape, index_map, pipeline_mode=pl.Buffered(1))


def _rms(x, g):
    return x * lax.rsqrt(jnp.mean(x * x, axis=-1, keepdims=True) + RMS_EPS) * g


def _rope_full(a, cos, sin):
    return a * cos + pltpu.roll(a, HEAD_DIM // 2, 1) * sin


def _rope_idx(a, cos, sin, first_half):
    partner = jnp.where(first_half, pltpu.roll(a, LANE - IDX_DIM // 2, 1), pltpu.roll(a, IDX_DIM // 2, 1))
    return a * cos + partner * sin


def _proj_kernel(*refs, after_ffn):
    if after_ffn:
        (y_ref, x_ref, gpost_ref, gpre_ref, w_ref, cs_ref, cos_ref, sin_ref, icos_ref, isin_ref,
         xo_ref, o_ref, oc_ref, cmp_ref) = refs
        x = x_ref[...] + _rms(y_ref[...], gpost_ref[...])
        xo_ref[...] = x
        h = _rms(x, gpre_ref[...]).astype(BF16)
    else:
        x_ref, gpre_ref, w_ref, cs_ref, cos_ref, sin_ref, icos_ref, isin_ref, o_ref, oc_ref, cmp_ref = refs
        h = _rms(x_ref[...], gpre_ref[...]).astype(BF16)
    tm = h.shape[0]
    lane = lax.broadcasted_iota(jnp.int32, (tm, LANE), 1)
    first_half = (lane & (IDX_DIM - 1)) < IDX_DIM // 2
    for c0 in range(0, NP_BLOCKS, _PROJ_CHUNK):
        nb = min(_PROJ_CHUNK, NP_BLOCKS - c0)
        acc = lax.dot_general(h, w_ref[c0 * LANE:(c0 + nb) * LANE, :], (((1,), (1,)), ((), ())),
                              preferred_element_type=F32)
        for b in range(nb):
            col = slice((c0 + b) * LANE, (c0 + b + 1) * LANE)
            a = acc[:, b * LANE:(b + 1) * LANE] * cs_ref[:, col]
            mode = _BLOCK_MODE[c0 + b]
            if mode == MODE_ROPE:
                a = _rope_full(a, cos_ref[...], sin_ref[...])
            elif mode == MODE_IROPE:
                a = _rope_idx(a, icos_ref[...], isin_ref[...], first_half)
            elif mode == MODE_MISC:
                a = jnp.where(lane < IDX_DIM, _rope_idx(a, icos_ref[...], isin_ref[...], first_half), a)
            o_ref[:, col] = a.astype(o_ref.dtype)
    acc = lax.dot_general(h, w_ref[NP:NW, :], (((1,), (1,)), ((), ())), preferred_element_type=F32)
    for b in range(CMP_BLOCKS):
        cmp_ref[b] = acc[:, b * LANE:(b + 1) * LANE]
    for j in range(CMP_STRIDE):
        for b in range(CMP_BLOCKS):
            rows = cmp_ref[b, pl.ds(j, tm // CMP_STRIDE, stride=CMP_STRIDE), :]
            oc_ref[:, j * CMP_COLS + b * LANE:j * CMP_COLS + (b + 1) * LANE] = rows.astype(oc_ref.dtype)


def _proj_call(rows_in, w_in_p, layer, col_scale, tabs, seq, tm):
    after_ffn = len(rows_in) > 2
    m, d = rows_in[0].shape
    tpb = seq // tm
    tab_spec = pl.BlockSpec((tm, LANE), lambda i: (i % tpb, 0))
    row = pl.BlockSpec((tm, d), lambda i: (i, 0))
    vec = pl.BlockSpec((1, d), lambda i: (0, 0))
    out_specs = [
        pl.BlockSpec((tm, NP), lambda i: (i, 0)),
        pl.BlockSpec((tm // CMP_STRIDE, CMP_STRIDE * CMP_COLS), lambda i: (i, 0)),
    ]
    out_shape = [
        jax.ShapeDtypeStruct((m, NP), BF16),
        jax.ShapeDtypeStruct((m // CMP_STRIDE, CMP_STRIDE * CMP_COLS), BF16),
    ]
    if after_ffn:
        out_specs = [row] + out_specs
        out_shape = [jax.ShapeDtypeStruct((m, d), F32)] + out_shape
    return pl.pallas_call(
        functools.partial(_proj_kernel, after_ffn=after_ffn),
        grid=(m // tm,),
        in_specs=([row, row, vec, vec] if after_ffn else [row, vec]) + [
            _resident((None, NW, d), lambda i: (layer, 0, 0)),
            _resident((1, NP), lambda i: (0, 0)),
            tab_spec, tab_spec, tab_spec, tab_spec,
        ],
        out_specs=out_specs,
        out_shape=out_shape,
        scratch_shapes=[pltpu.VMEM((CMP_BLOCKS, tm, LANE), F32)],
        compiler_params=_params(("parallel",), 54),
        name="in_proj",
    )(*rows_in, w_in_p, col_scale, *tabs)


def _transpose_v(dst_ref, src_ref, col0=0):
    rows = src_ref.shape[0]
    for r0 in range(0, rows, LANE):
        tile = src_ref[r0:r0 + LANE, col0:col0 + HEAD_DIM].astype(F32)
        dst_ref[0:HEAD_DIM, r0:r0 + LANE] = tile.T.astype(dst_ref.dtype)
    dst_ref[HEAD_DIM:, :] = jnp.ones((ONES_ROWS, rows), dst_ref.dtype)


def _transpose_v_tiles(dst_ref, src_ref):
    rows, _ = src_ref.shape
    for r in range(rows // LANE):
        tile = src_ref[r * LANE:(r + 1) * LANE, :].astype(F32)
        dst_ref[r, 0:HEAD_DIM, :] = tile.T.astype(dst_ref.dtype)
        dst_ref[r, HEAD_DIM:, :] = jnp.ones((ONES_ROWS, LANE), dst_ref.dtype)


def _for_causal_prefix(i, tq, s_len, step, body):
    n = ((i + 1) * tq + step - 1) // step
    for v in range(1, s_len // step + 1):
        pl.when(n == v)(functools.partial(body, v * step))


def _fold(x, op):
    out = x[0:COUNT_ROWS]
    for r0 in range(COUNT_ROWS, x.shape[0], COUNT_ROWS):
        out = op(out, x[r0:r0 + COUNT_ROWS])
    return out


def _fold_tree(x, op):
    while x.shape[0] > SUBLANE:
        half = x.shape[0] // 2
        x = op(x[:half], x[half:])
    return x


def _row_chunk(rows):
    return SCORE_ROWS if rows % SCORE_ROWS == 0 else LANE


def _attend_heads(qs, k_of, pv_of, rows, mask_of, s_ref, p_ref, w_ref=None, plain_rows=0):
    n_heads = len(qs)
    tq = qs[0].shape[0]
    chunk = ATT_ROWS * LANE // tq
    for h in range(n_heads):
        s_ref[h, 0:rows, :] = lax.dot_general(k_of(h, 0, rows), qs[h], (((1,), (1,)), ((), ())),
                                              preferred_element_type=F32)
    m_acc = [jnp.full((SUBLANE, tq), NEG, F32) for _ in qs]
    weighted = False
    for r0 in range(0, rows, chunk):
        if r0 + chunk <= plain_rows:
            for h in range(n_heads):
                m_acc[h] = jnp.maximum(m_acc[h], _fold_tree(s_ref[h, r0:r0 + chunk, :], jnp.maximum))
            continue
        mask, weight = mask_of(r0, chunk)
        if weight is not None:
            weighted = True
            w_ref[r0:r0 + chunk, :] = weight
        for h in range(n_heads):
            s = jnp.where(mask, s_ref[h, r0:r0 + chunk, :], NEG)
            s_ref[h, r0:r0 + chunk, :] = s
            m_acc[h] = jnp.maximum(m_acc[h], _fold_tree(s, jnp.maximum))
    outs = []
    for h in range(n_heads):
        mx = jnp.max(m_acc[h], axis=0, keepdims=True)
        for r0 in range(0, rows, chunk):
            p = jnp.exp2(s_ref[h, r0:r0 + chunk, :] - mx)
            if weighted:
                p = p * w_ref[r0:r0 + chunk, :]
            p_ref[h, r0:r0 + chunk, :] = p.astype(p_ref.dtype)
        pv = pv_of(h, p_ref[h, 0:rows, :])
        outs.append(pv[0:HEAD_DIM] / pv[HEAD_DIM:HEAD_DIM + 1])
    return outs


def _count_rows(key_ref, rows, pred):
    chunk = _row_chunk(rows)
    acc = None
    for r0 in range(0, rows, chunk):
        part = _fold(pred(key_ref[r0:r0 + chunk, :], r0), jnp.add)
        acc = part if acc is None else acc + part
    return jnp.sum(acc.astype(jnp.int32), axis=0, keepdims=True)


def _dsa_prefix(rows, i, iq_ref, iw_t, ik_ref, q_ref, k_ref, vt_ref, o_ref, key_ref, cut_ref, s_ref, p_ref,
                hi_ref, lo_ref, topk):
    tq = q_ref.shape[0]
    chunk = min(SCORE_ROWS * LANE // tq, rows)
    crow = lax.broadcasted_iota(jnp.int32, (chunk, tq), 0)
    ctcol = i * tq + lax.broadcasted_iota(jnp.int32, (chunk, tq), 1)

    for r0 in range(0, rows, chunk):
        halves = (ik_ref[0, r0:r0 + chunk, :], ik_ref[1, r0:r0 + chunk, :])
        score = jnp.zeros((chunk, tq), F32)
        for p in range(IDX_HEADS // 2):
            blk = iq_ref[:, p * LANE:(p + 1) * LANE]
            for half in range(2):
                j = 2 * p + half
                lg = lax.dot_general(halves[half], blk, (((1,), (1,)), ((), ())), preferred_element_type=F32)
                score = score + jnp.maximum(lg, 0.0) * iw_t[j:j + 1, :]
        bits = pltpu.bitcast(score, jnp.int32)
        key = bits ^ ((bits >> 31) & 0x7FFFFFFF)
        key = jnp.where(r0 + crow <= ctcol, key, INT_MIN)
        key_ref[r0:r0 + chunk, :] = key
        hi_ref[r0:r0 + chunk, :] = (key >> 16).astype(jnp.int16)
        lo_ref[r0:r0 + chunk, :] = ((key & 0xFFFF) - HALF16).astype(jnp.int16)

    def search16(ref, target):
        def bit_step(b, off):
            cand_off = off | jnp.left_shift(jnp.int32(1), 15 - b)
            cand = (cand_off - HALF16).astype(jnp.int16)
            cnt = _count_rows(ref, rows, lambda c, r0: jnp.where(c >= cand, jnp.int16(1), jnp.int16(0)))
            return jnp.where(cnt >= target, cand_off, off)

        return lax.fori_loop(0, 16, bit_step, jnp.zeros((1, tq), jnp.int32)) - HALF16

    thr_hi = search16(hi_ref, topk)
    thr_hi16 = thr_hi.astype(jnp.int16)
    above = _count_rows(hi_ref, rows, lambda c, r0: jnp.where(c > thr_hi16, jnp.int16(1), jnp.int16(0)))
    chunk16 = _row_chunk(rows)
    for r0 in range(0, rows, chunk16):
        same = hi_ref[r0:r0 + chunk16, :] == thr_hi16
        hi_ref[r0:r0 + chunk16, :] = jnp.where(same, lo_ref[r0:r0 + chunk16, :], jnp.int16(-HALF16))
    thr_lo = search16(hi_ref, topk - above)
    thr = jnp.left_shift(thr_hi, 16) | (thr_lo + HALF16)
    short = thr == INT_MIN
    need = topk - _count_rows(key_ref, rows, lambda kc, r0: jnp.where(kc > thr, 1, 0))
    n_eq = _count_rows(key_ref, rows, lambda kc, r0: jnp.where(kc == thr, 1, 0))
    excess = jnp.where(n_eq > need, jnp.where(short, 0, 1), 0)
    cut_ref[...] = jnp.where(short, -1, rows)

    def row_ids(r0, n):
        return r0 + lax.broadcasted_iota(jnp.int32, (n, tq), 0)

    @pl.when(jnp.max(excess) > 0)
    def _():
        n_bits = max(1, (rows - 1).bit_length())

        def idx_step(b, cut):
            cand = cut | jnp.left_shift(jnp.int32(1), n_bits - 1 - b)
            below = _count_rows(key_ref, rows, lambda kc, r0: jnp.where(
                kc == thr, jnp.where(row_ids(r0, kc.shape[0]) < cand, 1, 0), 0))
            return jnp.where(below < need, cand, cut)

        cut = lax.fori_loop(0, n_bits, idx_step, jnp.zeros((1, tq), jnp.int32))
        cut_ref[...] = jnp.where(excess > 0, cut, cut_ref[...])

    cut = cut_ref[...]

    def mask_of(r0, n):
        kc = key_ref[r0:r0 + n, :]
        chosen = jnp.where(kc > thr, 1, jnp.where(kc == thr, jnp.where(row_ids(r0, n) <= cut, 1, 0), 0))
        return chosen > 0, None

    vt = vt_ref[:, 0:rows]
    cols = [slice(h * HEAD_DIM, (h + 1) * HEAD_DIM) for h in range(A_HEADS)]
    outs = _attend_heads([q_ref[:, c] for c in cols], lambda h, r0, n: k_ref[r0:r0 + n, :],
                         lambda h, p: jnp.dot(vt, p, preferred_element_type=F32),
                         rows, mask_of, s_ref, p_ref)
    for h, c in enumerate(cols):
        o_ref[:, c] = outs[h].T.astype(o_ref.dtype)


def _dsa_kernel(iq_ref, mq_ref, mk_ref, q_ref, k_ref, v_ref, o_ref, vt_ref, key_ref, cut_ref, s_ref, p_ref,
                hi_ref, lo_ref, ikh_ref, *, topk, step):
    i = pl.program_id(1)
    tq = q_ref.shape[0]
    s_len = k_ref.shape[0]

    k_lane0, w_lane0 = _MISC_LANES["i_k"][0], _MISC_LANES["i_w"][0]

    @pl.when(i == 0)
    def _():
        _transpose_v(vt_ref, v_ref)
        lane = lax.broadcasted_iota(jnp.int32, (LANE, LANE), 1)
        for r0 in range(0, s_len, LANE):
            blk = mk_ref[r0:r0 + LANE, :].astype(F32)
            assert k_lane0 == 0
            ikh_ref[0, r0:r0 + LANE, :] = jnp.where(lane < IDX_DIM, blk, 0.0).astype(ikh_ref.dtype)
            ikh_ref[1, r0:r0 + LANE, :] = jnp.where(lane >= IDX_DIM, pltpu.roll(blk, IDX_DIM, 1),
                                                    0.0).astype(ikh_ref.dtype)

    iw_t = mq_ref[...].astype(F32).T[w_lane0:w_lane0 + IDX_HEADS, :]
    _for_causal_prefix(i, tq, s_len, step, functools.partial(
        _dsa_prefix, i=i, iq_ref=iq_ref, iw_t=iw_t, ik_ref=ikh_ref, q_ref=q_ref, k_ref=k_ref, vt_ref=vt_ref,
        o_ref=o_ref, key_ref=key_ref, cut_ref=cut_ref, s_ref=s_ref, p_ref=p_ref,
        hi_ref=hi_ref, lo_ref=lo_ref, topk=topk))


def _dsa_call(p3, tq, step):
    b, s, _ = p3.shape
    topk = min(DSA_TOPK_MAX, s // 4)
    blk = lambda name, width: _OFF[name] // width
    return pl.pallas_call(
        functools.partial(_dsa_kernel, topk=topk, step=step),
        grid=(b, s // tq),
        in_specs=[
            pl.BlockSpec((None, tq, 8 * LANE), lambda bi, i: (bi, i, blk("i_q", 8))),
            pl.BlockSpec((None, tq, LANE), lambda bi, i: (bi, i, blk("misc", 1))),
            pl.BlockSpec((None, s, LANE), lambda bi, i: (bi, 0, blk("misc", 1))),
            pl.BlockSpec((None, tq, 4 * LANE), lambda bi, i: (bi, i, blk("a_q", 4))),
            pl.BlockSpec((None, s, LANE), lambda bi, i: (bi, 0, blk("a_k", 1))),
            pl.BlockSpec((None, s, LANE), lambda bi, i: (bi, 0, blk("a_v", 1))),
        ],
        out_specs=pl.BlockSpec((None, tq, A_HEADS * HEAD_DIM), lambda bi, i: (bi, i, 0)),
        out_shape=jax.ShapeDtypeStruct((b, s, A_HEADS * HEAD_DIM), BF16),
        scratch_shapes=[
            pltpu.VMEM((HEAD_DIM + ONES_ROWS, s), BF16),
            pltpu.VMEM((s, tq), jnp.int32),
            pltpu.VMEM((1, tq), jnp.int32),
            pltpu.VMEM((A_HEADS, s, tq), F32),
            pltpu.VMEM((A_HEADS, s, tq), BF16),
            pltpu.VMEM((s, tq), jnp.int16),
            pltpu.VMEM((s, tq), jnp.int16),
            pltpu.VMEM((2, s, LANE), BF16),
        ],
        compiler_params=_params(("parallel", "arbitrary"), 48),
        name="dsa",
    )(p3, p3, p3, p3, p3, p3)


def _dil_prefix(rows, q_ref, k_ref, vt_ref, o_ref, s_ref, p_ref, w_ref):
    tq = q_ref.shape[0]
    first_q = rows - tq
    period = max(d for _, d in DILATED_PATTERNS)
    shared = {}

    def mask_of(r0, n):
        base = first_q - r0
        d_min, d_max = base - (n - 1), base + (tq - 1)
        live = tuple((w, d) for w, d in DILATED_PATTERNS if d_min <= w)
        on_edge = d_min < 0 or any(d_max > w for w, _ in live)
        key = None if on_edge else (live, base % period)
        if key in shared:
            return shared[key]
        delta = (base + lax.broadcasted_iota(jnp.int32, (n, tq), 1)
                 - lax.broadcasted_iota(jnp.int32, (n, tq), 0))
        mult = jnp.zeros((n, tq), jnp.int32)
        for window, dilation in live:
            hit = jnp.where(delta <= window, 1, 0) if d_max > window else 1
            mult = mult + jnp.where((delta & (dilation - 1)) == 0, hit, 0)
        if d_min < 0:
            mult = jnp.where(delta >= 0, mult, 0)
        out = (mult > 0, mult.astype(F32))
        if key is not None:
            shared[key] = out
        return out

    cols = [slice(h * HEAD_DIM, (h + 1) * HEAD_DIM) for h in range(B_HEADS)]
    outs = _attend_heads([q_ref[:, c] for c in cols], lambda h, r0, n: k_ref[r0:r0 + n, cols[h]],
                         lambda h, p: jnp.dot(vt_ref[h, :, 0:rows], p, preferred_element_type=F32),
                         rows, mask_of, s_ref, p_ref, w_ref)
    for h, c in enumerate(cols):
        o_ref[:, c] = outs[h].T.astype(o_ref.dtype)


def _dil_kernel(q_ref, k_ref, v_ref, o_ref, vt_ref, s_ref, p_ref, w_ref, *, step):
    i = pl.program_id(1)

    @pl.when(i == 0)
    def _():
        for h in range(B_HEADS):
            _transpose_v(vt_ref.at[h], v_ref, h * HEAD_DIM)

    assert step == q_ref.shape[0]
    _for_causal_prefix(i, q_ref.shape[0], k_ref.shape[0], step, functools.partial(
        _dil_prefix, q_ref=q_ref, k_ref=k_ref, vt_ref=vt_ref, o_ref=o_ref,
        s_ref=s_ref, p_ref=p_ref, w_ref=w_ref))


def _dil_call(p3, tq, step):
    b, s, _ = p3.shape
    width = B_HEADS * HEAD_DIM
    return pl.pallas_call(
        functools.partial(_dil_kernel, step=step),
        grid=(b, s // tq),
        in_specs=[
            pl.BlockSpec((None, tq, width), lambda bi, i: (bi, i, _OFF["b_q"] // 4)),
            pl.BlockSpec((None, s, width), lambda bi, i: (bi, 0, _OFF["b_k"] // 4)),
            pl.BlockSpec((None, s, width), lambda bi, i: (bi, 0, _OFF["b_v"] // 4)),
        ],
        out_specs=pl.BlockSpec((None, tq, width), lambda bi, i: (bi, i, 0)),
        out_shape=jax.ShapeDtypeStruct((b, s, width), BF16),
        scratch_shapes=[
            pltpu.VMEM((B_HEADS, HEAD_DIM + ONES_ROWS, s), BF16),
            pltpu.VMEM((B_HEADS, s, tq), F32),
            pltpu.VMEM((B_HEADS, s, tq), BF16),
            pltpu.VMEM((s, tq), F32),
        ],
        compiler_params=_params(("parallel", "arbitrary"), 48),
        name="dilated",
    )(p3, p3, p3)


def _gelu_tanh(x):
    return 0.5 * x * (1.0 + jnp.tanh(np.float32(np.sqrt(2.0 / np.pi)) * (x + 0.044715 * (x * x * x))))


def _cmp_kernel(*refs):
    n = CMP_STRIDE
    xk, xv = refs[:n], refs[n:2 * n]
    pe_k, w1_k, w2_k, pe_v, w1_v, w2_v, kc_ref, vct_ref = refs[2 * n:]

    def branch(x_refs, pe_ref, w1_ref, w2_ref):
        lo = hi = None
        for j in range(n):
            xj = x_refs[j][...].astype(F32)
            a = (xj + pe_ref[j:j + 1, :]).astype(BF16)
            b = (xj + pe_ref[n + j:n + j + 1, :]).astype(BF16)
            dl = jnp.dot(a, w1_ref[j * HEAD_DIM:(j + 1) * HEAD_DIM, :], preferred_element_type=F32)
            dh = jnp.dot(b, w1_ref[(n + j) * HEAD_DIM:(n + j + 1) * HEAD_DIM, :], preferred_element_type=F32)
            lo = dl if lo is None else lo + dl
            hi = dh if hi is None else hi + dh
        nc = lo.shape[0]
        hid = lo + pltpu.roll(hi, nc - 1, 0)
        return jnp.dot(_gelu_tanh(hid).astype(BF16), w2_ref[...], preferred_element_type=F32)

    kc_ref[...] = branch(xk, pe_k, w1_k, w2_k).astype(kc_ref.dtype)
    vct_ref[...] = branch(xv, pe_v, w1_v, w2_v).T.astype(vct_ref.dtype)


def _cmp_call(pc, b, layer, pe_k, w1_k, w2_k, pe_v, w1_v, w2_v):
    nc = pc.shape[0] // b
    pc3 = pc.reshape(b, nc, CMP_STRIDE * CMP_COLS)
    x_specs = []
    for first in (0, C_KV_GROUPS):
        for j in range(CMP_STRIDE):
            x_specs.append(pl.BlockSpec((None, nc, HEAD_DIM),
                                        lambda bi, g, j=j, first=first: (bi, 0, j * CMP_BLOCKS + first + g)))
    flat = CMP_BLOCK * HEAD_DIM
    w_specs = [
        _resident((None, CMP_BLOCK, HEAD_DIM), lambda bi, g: (layer, 0, 0)),
        _resident((None, flat, CMP_HIDDEN), lambda bi, g: (layer, 0, 0)),
        _resident((None, CMP_HIDDEN, HEAD_DIM), lambda bi, g: (layer, 0, 0)),
    ]
    return pl.pallas_call(
        _cmp_kernel,
        grid=(b, C_KV_GROUPS),
        in_specs=x_specs + w_specs + w_specs,
        out_specs=[
            pl.BlockSpec((None, None, nc, HEAD_DIM), lambda bi, g: (bi, g, 0, 0)),
            pl.BlockSpec((None, None, HEAD_DIM, nc), lambda bi, g: (bi, g, 0, 0)),
        ],
        out_shape=[
            jax.ShapeDtypeStruct((b, C_KV_GROUPS, nc, HEAD_DIM), BF16),
            jax.ShapeDtypeStruct((b, C_KV_GROUPS, HEAD_DIM, nc), BF16),
        ],
        compiler_params=_params(("parallel", "parallel"), 32),
        name="nsa_compress",
    )(*([pc3] * (2 * CMP_STRIDE)), pe_k, w1_k, w2_k, pe_v, w1_v, w2_v)


def _nsa_selected(rows, t_row, q_sel, ke_ref, vst_ref, acc_ref, s_ref, p_ref, step):
    tq = q_sel[0].shape[0]

    def mask_of(r0, n):
        srow = r0 + lax.broadcasted_iota(jnp.int32, (n, tq), 0)
        return t_row - srow >= 0, None

    vt = vst_ref[:, 0:rows]
    outs = _attend_heads(q_sel, lambda h, r0, n: ke_ref[r0:r0 + n, :],
                         lambda h, p: jnp.dot(vt, p, preferred_element_type=F32),
                         rows, mask_of, s_ref, p_ref, plain_rows=rows - max(step, tq))
    for r in range(C_GROUP_SIZE):
        acc_ref[r] = outs[r]


def _nsa_kernel(q_ref, kc_ref, vct_ref, ks_ref, vs_ref, kw_ref, vw_ref, g_ref, cos_ref, sin_ref,
                isect_ref, expand_ref, o_ref, vst_ref, vwt_ref, acc_ref, s_ref, p_ref, ke_ref, *, n_sel, step):
    i = pl.program_id(2)
    tq = q_ref.shape[0]
    s_len = ks_ref.shape[0]
    nc = kc_ref.shape[0]
    n_slc = isect_ref.shape[0]

    @pl.when(i == 0)
    def _():
        _transpose_v(vst_ref, vs_ref)
        _transpose_v_tiles(vwt_ref, vw_ref)
        ke_ref[:, 0:HEAD_DIM] = ks_ref[...]
        ke_ref[:, HEAD_DIM:] = expand_ref[...]

    t_row = i * tq + lax.broadcasted_iota(jnp.int32, (1, tq), 1)

    kc = kc_ref[...]
    vct = vct_ref[...]
    cend = lax.broadcasted_iota(jnp.int32, (nc, tq), 0) * CMP_STRIDE + (CMP_BLOCK - 1)
    cmask = cend <= t_row
    o_cmp = []
    p_sum = jnp.zeros((nc, tq), F32)
    for r in range(C_GROUP_SIZE):
        q_r = q_ref[:, r * HEAD_DIM:(r + 1) * HEAD_DIM]
        sc = lax.dot_general(kc, q_r, (((1,), (1,)), ((), ())), preferred_element_type=F32)
        sc = jnp.where(cmask, sc, NEG)
        mx = jnp.max(sc, axis=0, keepdims=True)
        e = jnp.where(cmask, jnp.exp2(sc - mx), 0.0)
        den = jnp.sum(e, axis=0, keepdims=True)
        p = e / jnp.where(den > 0, den, 1.0)
        p_sum = p_sum + p
        o_cmp.append(jnp.dot(vct, p.astype(BF16), preferred_element_type=F32))

    isect = isect_ref[...]
    p_hi = p_sum.astype(BF16)
    p_lo = (p_sum - p_hi.astype(F32)).astype(BF16)
    imp = (jnp.dot(isect, p_hi, preferred_element_type=F32)
           + jnp.dot(isect, p_lo, preferred_element_type=F32))
    blk = lax.broadcasted_iota(jnp.int32, (n_slc, tq), 0)
    cur = t_row // SEL_BLOCK
    val = jnp.where(blk == 0, jnp.inf, jnp.where(blk == cur, jnp.inf, jnp.where(blk == cur - 1, jnp.inf, imp)))
    val = jnp.where(blk <= cur, val, -jnp.inf)
    rank = jnp.zeros((n_slc, tq), jnp.int32)
    for m in range(n_slc):
        vm = val[m:m + 1, :]
        before = jnp.where(vm > val, 1, jnp.where(vm == val, jnp.where(blk > m, 1, 0), 0))
        rank = rank + before
    bias = jnp.where(rank < n_sel, 0.0, NEG)
    if n_slc < LANE:
        bias = jnp.concatenate([bias, jnp.zeros((LANE - n_slc, tq), F32)], axis=0)
    bias_t = bias.T.astype(BF16)

    cos, sin = cos_ref[...], sin_ref[...]
    cols = [slice(r * HEAD_DIM, (r + 1) * HEAD_DIM) for r in range(C_GROUP_SIZE)]
    q_rot = [_rope_full(q_ref[:, c].astype(F32), cos, sin).astype(BF16) for c in cols]
    q_sel = [jnp.concatenate([q, bias_t], axis=1) for q in q_rot]

    _for_causal_prefix(i, tq, s_len, step, functools.partial(
        _nsa_selected, t_row=t_row, q_sel=q_sel, ke_ref=ke_ref, vst_ref=vst_ref,
        acc_ref=acc_ref, s_ref=s_ref, p_ref=p_ref, step=step))

    w_tiles = min((WIN_SIZE - 1 + tq - 1) // LANE + 1, s_len // LANE)
    w_rows = w_tiles * LANE
    w0 = pl.multiple_of(jnp.maximum(i * tq + tq - w_rows, 0), LANE)
    t0 = w0 // LANE

    def win_mask(r0, n):
        delta = t_row - (w0 + r0 + lax.broadcasted_iota(jnp.int32, (n, tq), 0))
        return jnp.where(delta >= 0, jnp.where(delta <= WIN_SIZE - 1, 1, 0), 0) > 0, None

    def pv_win(h, p):
        out = None
        for w in range(w_tiles):
            part = jnp.dot(vwt_ref[t0 + w], p[w * LANE:(w + 1) * LANE, :], preferred_element_type=F32)
            out = part if out is None else out + part
        return out

    o_win = _attend_heads(q_rot, lambda h, r0, n: kw_ref[pl.ds(w0 + r0, n), :], pv_win,
                          w_rows, win_mask, s_ref, p_ref)

    g_lane0 = _MISC_LANES["c_g"][0]
    per_group = 3 * C_GROUP_SIZE
    logits_t = g_ref[...].astype(F32).T
    mine = [logits_t[g_lane0 + k * per_group:g_lane0 + (k + 1) * per_group, :] for k in range(C_KV_GROUPS)]
    group = pl.program_id(1)
    logits = mine[0]
    for k in range(1, C_KV_GROUPS):
        logits = jnp.where(group == k, mine[k], logits)
    gates = jax.nn.sigmoid(logits)
    for r, c in enumerate(cols):
        o_t = (gates[3 * r:3 * r + 1, :] * o_cmp[r] + gates[3 * r + 1:3 * r + 2, :] * acc_ref[r]
               + gates[3 * r + 2:3 * r + 3, :] * o_win[r])
        o_ref[:, c] = o_t.T.astype(o_ref.dtype)


def _nsa_call(p3, kc, vct, cos, sin, isect, expand, tq, step):
    b, s, _ = p3.shape
    nc = kc.shape[2]
    n_slc = s // SEL_BLOCK
    width = C_GROUP_SIZE * HEAD_DIM
    kv = lambda name: pl.BlockSpec((None, s, HEAD_DIM), lambda bi, g, i, name=name: (bi, 0, _OFF[name] + g))
    return pl.pallas_call(
        functools.partial(_nsa_kernel, n_sel=min(SEL_COUNT, n_slc), step=step),
        grid=(b, C_KV_GROUPS, s // tq),
        in_specs=[
            pl.BlockSpec((None, tq, width), lambda bi, g, i: (bi, i, _OFF["c_q"] // 4 + g)),
            pl.BlockSpec((None, None, nc, HEAD_DIM), lambda bi, g, i: (bi, g, 0, 0)),
            pl.BlockSpec((None, None, HEAD_DIM, nc), lambda bi, g, i: (bi, g, 0, 0)),
            kv("c_ks"), kv("c_vs"), kv("c_kw"), kv("c_vw"),
            pl.BlockSpec((None, tq, LANE), lambda bi, g, i: (bi, i, _OFF["misc"])),
            pl.BlockSpec((tq, LANE), lambda bi, g, i: (i, 0)),
            pl.BlockSpec((tq, LANE), lambda bi, g, i: (i, 0)),
            pl.BlockSpec(isect.shape, lambda bi, g, i: (0, 0)),
            pl.BlockSpec(expand.shape, lambda bi, g, i: (0, 0)),
        ],
        out_specs=pl.BlockSpec((None, tq, width), lambda bi, g, i: (bi, i, g)),
        out_shape=jax.ShapeDtypeStruct((b, s, C_HEADS * HEAD_DIM), BF16),
        scratch_shapes=[
            pltpu.VMEM((HEAD_DIM + ONES_ROWS, s), BF16),
            pltpu.VMEM((s // LANE, HEAD_DIM + ONES_ROWS, LANE), BF16),
            pltpu.VMEM((C_GROUP_SIZE, HEAD_DIM, tq), F32),
            pltpu.VMEM((C_GROUP_SIZE, s, tq), F32),
            pltpu.VMEM((C_GROUP_SIZE, s, tq), BF16),
            pltpu.VMEM((s, HEAD_DIM + expand.shape[1]), BF16),
        ],
        compiler_params=_params(("parallel", "parallel", "arbitrary"), 48),
        name="nsa",
    )(p3, kc, vct, p3, p3, p3, p3, p3, cos, sin, isect, expand)


def _out_kernel(oa_ref, ob_ref, oc_ref, w_ref, x_ref, gpost_ref, gnext_ref, xo_ref, h_ref):
    na, nb = oa_ref.shape[1], ob_ref.shape[1]
    tm = x_ref.shape[0]
    sub = min(OUT_SUB_ROWS, tm)
    for r0 in range(0, tm, sub):
        r = slice(r0, r0 + sub)
        y = jnp.dot(oa_ref[r, :], w_ref[0:na, :], preferred_element_type=F32)
        y = y + jnp.dot(ob_ref[r, :], w_ref[na:na + nb, :], preferred_element_type=F32)
        y = y + jnp.dot(oc_ref[r, :], w_ref[na + nb:, :], preferred_element_type=F32)
        x = x_ref[r, :] + _rms(y, gpost_ref[...])
        xo_ref[r, :] = x
        h_ref[r, :] = _rms(x, gnext_ref[...]).astype(h_ref.dtype)


def _out_call(o_a, o_b, o_c, w_out_p, layer, x, g_post, g_next, tm):
    m, d = x.shape
    row = lambda width: pl.BlockSpec((tm, width), lambda i: (i, 0))
    vec = pl.BlockSpec((1, d), lambda i: (0, 0))
    return pl.pallas_call(
        _out_kernel,
        grid=(m // tm,),
        in_specs=[
            row(o_a.shape[1]), row(o_b.shape[1]), row(o_c.shape[1]),
            _resident((None, w_out_p.shape[1], d), lambda i: (layer, 0, 0)),
            row(d), vec, vec,
        ],
        out_specs=[row(d), row(d)],
        out_shape=[jax.ShapeDtypeStruct((m, d), F32), jax.ShapeDtypeStruct((m, d), BF16)],
        compiler_params=_params(("parallel",), 48),
        name="out_proj",
    )(o_a, o_b, o_c, w_out_p, x, g_post, g_next)


def _ffn_kernel(h_ref, wg_ref, wu_ref, wd_ref, y_ref):
    @pl.when(pl.program_id(1) == 0)
    def _():
        y_ref[...] = jnp.zeros(y_ref.shape, F32)

    h = h_ref[...]
    g = jnp.dot(h, wg_ref[...], preferred_element_type=F32)
    u = jnp.dot(h, wu_ref[...], preferred_element_type=F32)
    a = (g * jax.nn.sigmoid(g) * u).astype(BF16)
    y_ref[...] += jnp.dot(a, wd_ref[...], preferred_element_type=F32)


def _ffn_call(h, wg, wu, wd, layer, tm, tf):
    m, d = h.shape
    d_ff = wg.shape[2]
    row = pl.BlockSpec((tm, d), lambda i, f: (i, 0))
    return pl.pallas_call(
        _ffn_kernel,
        grid=(m // tm, d_ff // tf),
        in_specs=[
            row,
            pl.BlockSpec((None, d, tf), lambda i, f: (layer, 0, f)),
            pl.BlockSpec((None, d, tf), lambda i, f: (layer, 0, f)),
            pl.BlockSpec((None, tf, d), lambda i, f: (layer, f, 0)),
        ],
        out_specs=row,
        out_shape=jax.ShapeDtypeStruct((m, d), F32),
        compiler_params=_params(("parallel", "arbitrary"), 52),
        name="ffn",
    )(h, wg, wu, wd)


def _residual_kernel(y_ref, x_ref, g_ref, o_ref):
    o_ref[...] = x_ref[...] + _rms(y_ref[...], g_ref[...])


def _residual_call(y, x, g, tm):
    m, d = x.shape
    row = pl.BlockSpec((tm, d), lambda i: (i, 0))
    return pl.pallas_call(
        _residual_kernel,
        grid=(m // tm,),
        in_specs=[row, row, pl.BlockSpec((1, d), lambda i: (0, 0))],
        out_specs=row,
        out_shape=jax.ShapeDtypeStruct((m, d), F32),
        compiler_params=_params(("parallel",), 32),
        name="ffn_residual",
    )(y, x, g)


def _w_in_layout():
    widths = (A_HEADS * HEAD_DIM, HEAD_DIM, HEAD_DIM, IDX_HEADS * IDX_DIM, IDX_DIM, IDX_HEADS,
              B_HEADS * HEAD_DIM, B_HEADS * HEAD_DIM, B_HEADS * HEAD_DIM, C_HEADS * HEAD_DIM,
              C_KV_GROUPS * HEAD_DIM, C_KV_GROUPS * HEAD_DIM, C_KV_GROUPS * HEAD_DIM, C_KV_GROUPS * HEAD_DIM,
              C_KV_GROUPS * HEAD_DIM, C_KV_GROUPS * HEAD_DIM, 3 * C_HEADS)
    names = ("a_q", "a_k", "a_v", "i_q", "i_k", "i_w", "b_q", "b_k", "b_v", "c_q",
             "c_kc", "c_vc", "c_ks", "c_vs", "c_kw", "c_vw", "c_g")
    offs = np.concatenate([[0], np.cumsum(widths)])
    src = {n: (int(offs[k]), int(widths[k])) for k, n in enumerate(names)}
    pieces = []
    dst = 0
    for name, blocks in [(n, w) for n, w, _, _ in _SEGMENTS] + list(_CMP_SEGMENTS):
        rows = blocks * LANE
        if name == "misc":
            parts, used = [], 0
            for part, (lane0, width) in _MISC_LANES.items():
                assert lane0 == used and src[part][1] == width
                parts.append(src[part])
                used += width
            pieces.append((dst, parts, rows - used))
        else:
            start, width = src[name]
            assert width == rows
            for r0 in range(0, rows, _PROJ_CHUNK * LANE):
                pieces.append((dst + r0, [(start + r0, min(_PROJ_CHUNK * LANE, rows - r0))], 0))
        dst += rows
    return pieces, int(offs[-1])


def _w_prep_kernel(w_ref, o_ref, *, pieces):
    cols = w_ref.shape[1]
    for dst, parts, n_zero in pieces:
        vals = [w_ref[r:r + n, :] for r, n in parts]
        if n_zero:
            vals.append(jnp.zeros((n_zero, cols), F32))
        blk = vals[0] if len(vals) == 1 else jnp.concatenate(vals, axis=0)
        o_ref[dst:dst + blk.shape[0], :] = blk.astype(o_ref.dtype)


def _prep_w_in(w_in, tk=256):
    w_t = jnp.swapaxes(w_in, 1, 2)
    depth, width, d = w_t.shape
    pieces, src_width = _w_in_layout()
    assert width == src_width
    return pl.pallas_call(
        functools.partial(_w_prep_kernel, pieces=pieces),
        grid=(depth, d // tk),
        in_specs=[pl.BlockSpec((None, width, tk), lambda l, i: (l, 0, i))],
        out_specs=pl.BlockSpec((None, NW, tk), lambda l, i: (l, 0, i)),
        out_shape=jax.ShapeDtypeStruct((depth, NW, d), BF16),
        compiler_params=_params(("parallel", "parallel"), 40),
        name="w_in_prep",
    )(w_t)


def _rope_tables(seq):
    def tables(dim):
        inv = 1.0 / (ROPE_THETA ** (jnp.arange(0, dim, 2, dtype=F32) / dim))
        ang = jnp.arange(seq, dtype=F32)[:, None] * inv[None, :]
        cos, sin = jnp.cos(ang), jnp.sin(ang)
        reps = LANE // dim
        return (jnp.tile(jnp.concatenate([cos, cos], axis=-1), (1, reps)),
                jnp.tile(jnp.concatenate([-sin, sin], axis=-1), (1, reps)))
    return tables(HEAD_DIM) + tables(IDX_DIM)


def _selection_constants(seq):
    nc = seq // CMP_STRIDE
    n_slc = seq // SEL_BLOCK
    c_start = np.arange(nc) * CMP_STRIDE
    n_start = np.arange(n_slc) * SEL_BLOCK
    isect = ((c_start[None, :] < n_start[:, None] + SEL_BLOCK)
             & (c_start[None, :] + CMP_BLOCK > n_start[:, None])
             & (np.arange(nc)[None, :] < nc - CMP_BLOCK // CMP_STRIDE + 1)).astype(np.float32)
    expand = np.zeros((seq, max(n_slc, LANE)), np.float32)
    expand[np.arange(seq), np.arange(seq) // SEL_BLOCK] = 1.0
    return jnp.asarray(isect, BF16), jnp.asarray(expand, BF16)


def _tiles(seq):
    return dict(tq=min(2 * LANE, seq), step_dsa=min(4 * LANE, seq), step=min(2 * LANE, seq),
                tm_proj=min(256, seq), tm_out=min(512, seq), tm_ffn=min(1024, seq), tf=512)


def kernel(x, w_in, w_out, cmp_pe_k, cmp_w1_k, cmp_w2_k, cmp_pe_v, cmp_w1_v, cmp_w2_v,
           w_gate, w_up, w_down, g_pre_mix, g_post_mix, g_pre_ffn, g_post_ffn):
    b, s, d = x.shape
    depth = w_in.shape[0]
    m = b * s
    t = _tiles(s)
    tq, step = t["tq"], t["step"]

    w_in_p = _prep_w_in(w_in)
    w_out_p = w_out.astype(BF16)
    wg, wu, wd = w_gate.astype(BF16), w_up.astype(BF16), w_down.astype(BF16)
    w1_k, w2_k = cmp_w1_k.astype(BF16), cmp_w2_k.astype(BF16)
    w1_v, w2_v = cmp_w1_v.astype(BF16), cmp_w2_v.astype(BF16)
    col_scale = jnp.asarray(_COL_SCALE)
    tabs = _rope_tables(s)
    isect, expand = _selection_constants(s)
    vec = lambda g, layer: g[layer][None, :]

    xf = x.reshape(m, d)
    y = None
    for layer in range(depth):
        if layer == 0:
            p, pc = _proj_call((xf, vec(g_pre_mix, 0)), w_in_p, layer, col_scale, tabs, s, t["tm_proj"])
        else:
            rows_in = (y, xf, vec(g_post_ffn, layer - 1), vec(g_pre_mix, layer))
            xf, p, pc = _proj_call(rows_in, w_in_p, layer, col_scale, tabs, s, t["tm_proj"])
        p3 = p.reshape(b, s, NP)
        o_a = _dsa_call(p3, tq, t["step_dsa"])
        o_b = _dil_call(p3, tq, step)
        kc, vct = _cmp_call(pc, b, layer, cmp_pe_k, w1_k, w2_k, cmp_pe_v, w1_v, w2_v)
        o_c = _nsa_call(p3, kc, vct, tabs[0], tabs[1], isect, expand, tq, step)
        xf, h = _out_call(o_a.reshape(m, -1), o_b.reshape(m, -1), o_c.reshape(m, -1), w_out_p, layer,
                          xf, vec(g_post_mix, layer), vec(g_pre_ffn, layer), t["tm_out"])
        y = _ffn_call(h, wg, wu, wd, layer, t["tm_ffn"], t["tf"])
    return _residual_call(y, xf, vec(g_post_ffn, depth - 1), t["tm_out"]).reshape(b, s, d)
```

```python
import functools

import numpy as np
import jax
import jax.numpy as jnp
from jax import lax
from jax.experimental import pallas as pl
from jax.experimental.pallas import tpu as pltpu

F32 = jnp.float32
BF16 = jnp.bfloat16

LANE = 128
SUBLANE = 8
HEAD_DIM = 128
A_HEADS = 4
B_HEADS = 4
C_HEADS = 8
C_KV_GROUPS = 2
C_GROUP_SIZE = C_HEADS // C_KV_GROUPS
IDX_HEADS = 16
IDX_DIM = 64
DSA_TOPK_MAX = 256
DILATED_PATTERNS = ((128, 1), (512, 4), (2048, 16))
CMP_BLOCK = 32
CMP_STRIDE = 16
CMP_HIDDEN = 256
SEL_BLOCK = 64
SEL_COUNT = 16
WIN_SIZE = 512
ROPE_THETA = 10000.0
RMS_EPS = 1e-6
ATTN_SCALE = HEAD_DIM ** -0.5 * float(np.log2(np.e))
IDX_SCALE = IDX_DIM ** -0.5 * IDX_HEADS ** -0.5

NEG = -1e30
INT_MIN = -(2 ** 31)
HALF16 = 2 ** 15
COUNT_ROWS = 8 * SUBLANE
SCORE_ROWS = 256
ATT_ROWS = 128
ONES_ROWS = 16
OUT_SUB_ROWS = 256

MODE_NONE, MODE_ROPE, MODE_IROPE, MODE_MISC = 0, 1, 2, 3

_SEGMENTS = (
    ("i_q", 8, MODE_IROPE, IDX_SCALE),
    ("c_q", 8, MODE_NONE, ATTN_SCALE),
    ("a_q", 4, MODE_ROPE, ATTN_SCALE),
    ("b_q", 4, MODE_ROPE, ATTN_SCALE),
    ("b_k", 4, MODE_ROPE, 1.0),
    ("b_v", 4, MODE_NONE, 1.0),
    ("c_ks", 2, MODE_ROPE, 1.0),
    ("c_vs", 2, MODE_NONE, 1.0),
    ("c_kw", 2, MODE_ROPE, 1.0),
    ("c_vw", 2, MODE_NONE, 1.0),
    ("a_k", 1, MODE_ROPE, 1.0),
    ("a_v", 1, MODE_NONE, 1.0),
    ("misc", 1, MODE_MISC, 1.0),
)
_MISC_LANES = {"i_k": (0, IDX_DIM), "i_w": (IDX_DIM, IDX_HEADS), "c_g": (IDX_DIM + IDX_HEADS, 3 * C_HEADS)}
_CMP_SEGMENTS = (("c_kc", 2), ("c_vc", 2))
_OFF = {}
_o = 0
for _n, _w, _m, _s in _SEGMENTS:
    _OFF[_n] = _o
    _o += _w
NP_BLOCKS = _o
NP = NP_BLOCKS * LANE
CMP_BLOCKS = sum(w for _, w in _CMP_SEGMENTS)
CMP_COLS = CMP_BLOCKS * LANE
NW = NP + CMP_COLS
_BLOCK_MODE = tuple(m for _, w, m, _ in _SEGMENTS for _ in range(w))
_COL_SCALE = np.repeat(np.array([s for _, w, _, s in _SEGMENTS for _ in range(w)], np.float32), LANE)[None, :]
_PROJ_CHUNK = 4


_VMEM_MB = dict(in_proj=54, dsa=48, dilated=48, nsa_compress=32, nsa=48, out_proj=48, ffn=52,
                ffn_residual=32, w_in_prep=40)


def _params(sem, name):
    return pltpu.CompilerParams(dimension_semantics=sem, vmem_limit_bytes=_VMEM_MB[name] * 1024 * 1024)


def _resident(block_shape, index_map):
    return pl.BlockSpec(block_shape, index_map, pipeline_mode=pl.Buffered(1))


def _rms(x, g):
    return x * lax.rsqrt(jnp.mean(x * x, axis=-1, keepdims=True) + RMS_EPS) * g


def _rope_full(a, cos, sin):
    return a * cos + pltpu.roll(a, HEAD_DIM // 2, 1) * sin


def _rope_idx(a, cos, sin, first_half):
    partner = jnp.where(first_half, pltpu.roll(a, LANE - IDX_DIM // 2, 1), pltpu.roll(a, IDX_DIM // 2, 1))
    return a * cos + partner * sin


def _proj_kernel(*refs, after_ffn):
    if after_ffn:
        (y_ref, x_ref, gpost_ref, gpre_ref, w_ref, cs_ref, cos_ref, sin_ref, icos_ref, isin_ref,
         xo_ref, o_ref, oc_ref, cmp_ref) = refs
        x = x_ref[...] + _rms(y_ref[...], gpost_ref[...])
        xo_ref[...] = x
        h = _rms(x, gpre_ref[...]).astype(BF16)
    else:
        x_ref, gpre_ref, w_ref, cs_ref, cos_ref, sin_ref, icos_ref, isin_ref, o_ref, oc_ref, cmp_ref = refs
        h = _rms(x_ref[...], gpre_ref[...]).astype(BF16)
    tm = h.shape[0]
    lane = lax.broadcasted_iota(jnp.int32, (tm, LANE), 1)
    first_half = (lane & (IDX_DIM - 1)) < IDX_DIM // 2
    for c0 in range(0, NP_BLOCKS, _PROJ_CHUNK):
        nb = min(_PROJ_CHUNK, NP_BLOCKS - c0)
        acc = lax.dot_general(h, w_ref[c0 * LANE:(c0 + nb) * LANE, :], (((1,), (1,)), ((), ())),
                              preferred_element_type=F32)
        for b in range(nb):
            col = slice((c0 + b) * LANE, (c0 + b + 1) * LANE)
            a = acc[:, b * LANE:(b + 1) * LANE] * cs_ref[:, col]
            mode = _BLOCK_MODE[c0 + b]
            if mode == MODE_ROPE:
                a = _rope_full(a, cos_ref[...], sin_ref[...])
            elif mode == MODE_IROPE:
                a = _rope_idx(a, icos_ref[...], isin_ref[...], first_half)
            elif mode == MODE_MISC:
                a = jnp.where(lane < IDX_DIM, _rope_idx(a, icos_ref[...], isin_ref[...], first_half), a)
            o_ref[:, col] = a.astype(o_ref.dtype)
    acc = lax.dot_general(h, w_ref[NP:NW, :], (((1,), (1,)), ((), ())), preferred_element_type=F32)
    for b in range(CMP_BLOCKS):
        cmp_ref[b] = acc[:, b * LANE:(b + 1) * LANE]
    for j in range(CMP_STRIDE):
        for b in range(CMP_BLOCKS):
            rows = cmp_ref[b, pl.ds(j, tm // CMP_STRIDE, stride=CMP_STRIDE), :]
            oc_ref[:, j * CMP_COLS + b * LANE:j * CMP_COLS + (b + 1) * LANE] = rows.astype(oc_ref.dtype)


def _proj_call(rows_in, w_in_p, layer, col_scale, tabs, seq, tm):
    after_ffn = len(rows_in) > 2
    m, d = rows_in[0].shape
    tpb = seq // tm
    tab_spec = pl.BlockSpec((tm, LANE), lambda i: (i % tpb, 0))
    row = pl.BlockSpec((tm, d), lambda i: (i, 0))
    vec = pl.BlockSpec((1, d), lambda i: (0, 0))
    out_specs = [
        pl.BlockSpec((tm, NP), lambda i: (i, 0)),
        pl.BlockSpec((tm // CMP_STRIDE, CMP_STRIDE * CMP_COLS), lambda i: (i, 0)),
    ]
    out_shape = [
        jax.ShapeDtypeStruct((m, NP), BF16),
        jax.ShapeDtypeStruct((m // CMP_STRIDE, CMP_STRIDE * CMP_COLS), BF16),
    ]
    if after_ffn:
        out_specs = [row] + out_specs
        out_shape = [jax.ShapeDtypeStruct((m, d), F32)] + out_shape
    return pl.pallas_call(
        functools.partial(_proj_kernel, after_ffn=after_ffn),
        grid=(m // tm,),
        in_specs=([row, row, vec, vec] if after_ffn else [row, vec]) + [
            _resident((None, NW, d), lambda i: (layer, 0, 0)),
            _resident((1, NP), lambda i: (0, 0)),
            tab_spec, tab_spec, tab_spec, tab_spec,
        ],
        out_specs=out_specs,
        out_shape=out_shape,
        scratch_shapes=[pltpu.VMEM((CMP_BLOCKS, tm, LANE), F32)],
        compiler_params=_params(("parallel",), "in_proj"),
        name="in_proj",
    )(*rows_in, w_in_p, col_scale, *tabs)


def _transpose_v(dst_ref, src_ref, col0=0):
    rows = src_ref.shape[0]
    for r0 in range(0, rows, LANE):
        tile = src_ref[r0:r0 + LANE, col0:col0 + HEAD_DIM].astype(F32)
        dst_ref[0:HEAD_DIM, r0:r0 + LANE] = tile.T.astype(dst_ref.dtype)
    dst_ref[HEAD_DIM:, :] = jnp.ones((ONES_ROWS, rows), dst_ref.dtype)


def _transpose_v_tiles(dst_ref, src_ref):
    rows, _ = src_ref.shape
    for r in range(rows // LANE):
        tile = src_ref[r * LANE:(r + 1) * LANE, :].astype(F32)
        dst_ref[r, 0:HEAD_DIM, :] = tile.T.astype(dst_ref.dtype)
        dst_ref[r, HEAD_DIM:, :] = jnp.ones((ONES_ROWS, LANE), dst_ref.dtype)


def _for_causal_prefix(i, tq, s_len, step, body):
    n = ((i + 1) * tq + step - 1) // step
    for v in range(1, s_len // step + 1):
        pl.when(n == v)(functools.partial(body, v * step))


def _fold(x, op):
    out = x[0:COUNT_ROWS]
    for r0 in range(COUNT_ROWS, x.shape[0], COUNT_ROWS):
        out = op(out, x[r0:r0 + COUNT_ROWS])
    return out


def _fold_tree(x, op):
    while x.shape[0] > SUBLANE:
        half = x.shape[0] // 2
        x = op(x[:half], x[half:])
    return x


def _row_chunk(rows):
    return SCORE_ROWS if rows % SCORE_ROWS == 0 else LANE


def _attend_heads(qs, k_of, pv_of, rows, mask_of, s_ref, p_ref, w_ref=None, plain_rows=0):
    n_heads = len(qs)
    tq = qs[0].shape[0]
    chunk = ATT_ROWS * LANE // tq
    for h in range(n_heads):
        s_ref[h, 0:rows, :] = lax.dot_general(k_of(h, 0, rows), qs[h], (((1,), (1,)), ((), ())),
                                              preferred_element_type=F32)
    m_acc = [jnp.full((SUBLANE, tq), NEG, F32) for _ in qs]
    weighted = False
    for r0 in range(0, rows, chunk):
        if r0 + chunk <= plain_rows:
            for h in range(n_heads):
                m_acc[h] = jnp.maximum(m_acc[h], _fold_tree(s_ref[h, r0:r0 + chunk, :], jnp.maximum))
            continue
        mask, weight = mask_of(r0, chunk)
        if weight is not None:
            weighted = True
            w_ref[r0:r0 + chunk, :] = weight
        for h in range(n_heads):
            s = jnp.where(mask, s_ref[h, r0:r0 + chunk, :], NEG)
            s_ref[h, r0:r0 + chunk, :] = s
            m_acc[h] = jnp.maximum(m_acc[h], _fold_tree(s, jnp.maximum))
    outs = []
    for h in range(n_heads):
        mx = jnp.max(m_acc[h], axis=0, keepdims=True)
        for r0 in range(0, rows, chunk):
            p = jnp.exp2(s_ref[h, r0:r0 + chunk, :] - mx)
            if weighted:
                p = p * w_ref[r0:r0 + chunk, :]
            p_ref[h, r0:r0 + chunk, :] = p.astype(p_ref.dtype)
        pv = pv_of(h, p_ref[h, 0:rows, :])
        outs.append(pv[0:HEAD_DIM] / pv[HEAD_DIM:HEAD_DIM + 1])
    return outs


def _count_rows(key_ref, rows, pred):
    chunk = _row_chunk(rows)
    acc = None
    for r0 in range(0, rows, chunk):
        part = _fold(pred(key_ref[r0:r0 + chunk, :], r0), jnp.add)
        acc = part if acc is None else acc + part
    return jnp.sum(acc.astype(jnp.int32), axis=0, keepdims=True)


def _dsa_prefix(rows, i, iq_ref, iw_t, ik_ref, q_ref, k_ref, vt_ref, o_ref, key_ref, cut_ref, s_ref, p_ref,
                hi_ref, lo_ref, topk):
    tq = q_ref.shape[0]
    chunk = min(SCORE_ROWS * LANE // tq, rows)
    crow = lax.broadcasted_iota(jnp.int32, (chunk, tq), 0)
    ctcol = i * tq + lax.broadcasted_iota(jnp.int32, (chunk, tq), 1)

    for r0 in range(0, rows, chunk):
        halves = (ik_ref[0, r0:r0 + chunk, :], ik_ref[1, r0:r0 + chunk, :])
        score = jnp.zeros((chunk, tq), F32)
        for p in range(IDX_HEADS // 2):
            blk = iq_ref[:, p * LANE:(p + 1) * LANE]
            for half in range(2):
                j = 2 * p + half
                lg = lax.dot_general(halves[half], blk, (((1,), (1,)), ((), ())), preferred_element_type=F32)
                score = score + jnp.maximum(lg, 0.0) * iw_t[j:j + 1, :]
        bits = pltpu.bitcast(score, jnp.int32)
        key = bits ^ ((bits >> 31) & 0x7FFFFFFF)
        key = jnp.where(r0 + crow <= ctcol, key, INT_MIN)
        key_ref[r0:r0 + chunk, :] = key
        hi_ref[r0:r0 + chunk, :] = (key >> 16).astype(jnp.int16)
        lo_ref[r0:r0 + chunk, :] = ((key & 0xFFFF) - HALF16).astype(jnp.int16)

    def search16(ref, target):
        def bit_step(b, off):
            cand_off = off | jnp.left_shift(jnp.int32(1), 15 - b)
            cand = (cand_off - HALF16).astype(jnp.int16)
            cnt = _count_rows(ref, rows, lambda c, r0: jnp.where(c >= cand, jnp.int16(1), jnp.int16(0)))
            return jnp.where(cnt >= target, cand_off, off)

        return lax.fori_loop(0, 16, bit_step, jnp.zeros((1, tq), jnp.int32)) - HALF16

    thr_hi = search16(hi_ref, topk)
    thr_hi16 = thr_hi.astype(jnp.int16)
    above = _count_rows(hi_ref, rows, lambda c, r0: jnp.where(c > thr_hi16, jnp.int16(1), jnp.int16(0)))
    chunk16 = _row_chunk(rows)
    for r0 in range(0, rows, chunk16):
        same = hi_ref[r0:r0 + chunk16, :] == thr_hi16
        hi_ref[r0:r0 + chunk16, :] = jnp.where(same, lo_ref[r0:r0 + chunk16, :], jnp.int16(-HALF16))
    thr_lo = search16(hi_ref, topk - above)
    thr = jnp.left_shift(thr_hi, 16) | (thr_lo + HALF16)
    short = thr == INT_MIN
    need = topk - _count_rows(key_ref, rows, lambda kc, r0: jnp.where(kc > thr, 1, 0))
    n_eq = _count_rows(key_ref, rows, lambda kc, r0: jnp.where(kc == thr, 1, 0))
    excess = jnp.where(n_eq > need, jnp.where(short, 0, 1), 0)
    cut_ref[...] = jnp.where(short, -1, rows)

    def row_ids(r0, n):
        return r0 + lax.broadcasted_iota(jnp.int32, (n, tq), 0)

    @pl.when(jnp.max(excess) > 0)
    def _():
        n_bits = max(1, (rows - 1).bit_length())

        def idx_step(b, cut):
            cand = cut | jnp.left_shift(jnp.int32(1), n_bits - 1 - b)
            below = _count_rows(key_ref, rows, lambda kc, r0: jnp.where(
                kc == thr, jnp.where(row_ids(r0, kc.shape[0]) < cand, 1, 0), 0))
            return jnp.where(below < need, cand, cut)

        cut = lax.fori_loop(0, n_bits, idx_step, jnp.zeros((1, tq), jnp.int32))
        cut_ref[...] = jnp.where(excess > 0, cut, cut_ref[...])

    cut = cut_ref[...]

    def mask_of(r0, n):
        kc = key_ref[r0:r0 + n, :]
        chosen = jnp.where(kc > thr, 1, jnp.where(kc == thr, jnp.where(row_ids(r0, n) <= cut, 1, 0), 0))
        return chosen > 0, None

    vt = vt_ref[:, 0:rows]
    cols = [slice(h * HEAD_DIM, (h + 1) * HEAD_DIM) for h in range(A_HEADS)]
    outs = _attend_heads([q_ref[:, c] for c in cols], lambda h, r0, n: k_ref[r0:r0 + n, :],
                         lambda h, p: jnp.dot(vt, p, preferred_element_type=F32),
                         rows, mask_of, s_ref, p_ref)
    for h, c in enumerate(cols):
        o_ref[:, c] = outs[h].T.astype(o_ref.dtype)


def _dsa_kernel(iq_ref, mq_ref, mk_ref, q_ref, k_ref, v_ref, o_ref, vt_ref, key_ref, cut_ref, s_ref, p_ref,
                hi_ref, lo_ref, ikh_ref, *, topk, step):
    i = pl.program_id(1)
    tq = q_ref.shape[0]
    s_len = k_ref.shape[0]

    w_lane0 = _MISC_LANES["i_w"][0]
    assert _MISC_LANES["i_k"] == (0, IDX_DIM)

    @pl.when(i == 0)
    def _():
        _transpose_v(vt_ref, v_ref)
        lane = lax.broadcasted_iota(jnp.int32, (LANE, LANE), 1)
        for r0 in range(0, s_len, LANE):
            blk = mk_ref[r0:r0 + LANE, :].astype(F32)
            ikh_ref[0, r0:r0 + LANE, :] = jnp.where(lane < IDX_DIM, blk, 0.0).astype(ikh_ref.dtype)
            ikh_ref[1, r0:r0 + LANE, :] = jnp.where(lane >= IDX_DIM, pltpu.roll(blk, IDX_DIM, 1),
                                                    0.0).astype(ikh_ref.dtype)

    iw_t = mq_ref[...].astype(F32).T[w_lane0:w_lane0 + IDX_HEADS, :]
    _for_causal_prefix(i, tq, s_len, step, functools.partial(
        _dsa_prefix, i=i, iq_ref=iq_ref, iw_t=iw_t, ik_ref=ikh_ref, q_ref=q_ref, k_ref=k_ref, vt_ref=vt_ref,
        o_ref=o_ref, key_ref=key_ref, cut_ref=cut_ref, s_ref=s_ref, p_ref=p_ref,
        hi_ref=hi_ref, lo_ref=lo_ref, topk=topk))


def _dsa_call(p3, tq, step):
    b, s, _ = p3.shape
    topk = min(DSA_TOPK_MAX, s // 4)
    blk = lambda name, width: _OFF[name] // width
    return pl.pallas_call(
        functools.partial(_dsa_kernel, topk=topk, step=step),
        grid=(b, s // tq),
        in_specs=[
            pl.BlockSpec((None, tq, 8 * LANE), lambda bi, i: (bi, i, blk("i_q", 8))),
            pl.BlockSpec((None, tq, LANE), lambda bi, i: (bi, i, blk("misc", 1))),
            pl.BlockSpec((None, s, LANE), lambda bi, i: (bi, 0, blk("misc", 1))),
            pl.BlockSpec((None, tq, 4 * LANE), lambda bi, i: (bi, i, blk("a_q", 4))),
            pl.BlockSpec((None, s, LANE), lambda bi, i: (bi, 0, blk("a_k", 1))),
            pl.BlockSpec((None, s, LANE), lambda bi, i: (bi, 0, blk("a_v", 1))),
        ],
        out_specs=pl.BlockSpec((None, tq, A_HEADS * HEAD_DIM), lambda bi, i: (bi, i, 0)),
        out_shape=jax.ShapeDtypeStruct((b, s, A_HEADS * HEAD_DIM), BF16),
        scratch_shapes=[
            pltpu.VMEM((HEAD_DIM + ONES_ROWS, s), BF16),
            pltpu.VMEM((s, tq), jnp.int32),
            pltpu.VMEM((1, tq), jnp.int32),
            pltpu.VMEM((A_HEADS, s, tq), F32),
            pltpu.VMEM((A_HEADS, s, tq), BF16),
            pltpu.VMEM((s, tq), jnp.int16),
            pltpu.VMEM((s, tq), jnp.int16),
            pltpu.VMEM((2, s, LANE), BF16),
        ],
        compiler_params=_params(("parallel", "arbitrary"), "dsa"),
        name="dsa",
    )(p3, p3, p3, p3, p3, p3)


def _dil_prefix(rows, q_ref, k_ref, vt_ref, o_ref, s_ref, p_ref, w_ref):
    tq = q_ref.shape[0]
    first_q = rows - tq
    period = max(d for _, d in DILATED_PATTERNS)
    shared = {}

    def mask_of(r0, n):
        base = first_q - r0
        d_min, d_max = base - (n - 1), base + (tq - 1)
        live = tuple((w, d) for w, d in DILATED_PATTERNS if d_min <= w)
        on_edge = d_min < 0 or any(d_max > w for w, _ in live)
        key = None if on_edge else (live, base % period)
        if key in shared:
            return shared[key]
        delta = (base + lax.broadcasted_iota(jnp.int32, (n, tq), 1)
                 - lax.broadcasted_iota(jnp.int32, (n, tq), 0))
        mult = jnp.zeros((n, tq), jnp.int32)
        for window, dilation in live:
            hit = jnp.where(delta <= window, 1, 0) if d_max > window else 1
            mult = mult + jnp.where((delta & (dilation - 1)) == 0, hit, 0)
        if d_min < 0:
            mult = jnp.where(delta >= 0, mult, 0)
        out = (mult > 0, mult.astype(F32))
        if key is not None:
            shared[key] = out
        return out

    cols = [slice(h * HEAD_DIM, (h + 1) * HEAD_DIM) for h in range(B_HEADS)]
    outs = _attend_heads([q_ref[:, c] for c in cols], lambda h, r0, n: k_ref[r0:r0 + n, cols[h]],
                         lambda h, p: jnp.dot(vt_ref[h, :, 0:rows], p, preferred_element_type=F32),
                         rows, mask_of, s_ref, p_ref, w_ref)
    for h, c in enumerate(cols):
        o_ref[:, c] = outs[h].T.astype(o_ref.dtype)


def _dil_kernel(q_ref, k_ref, v_ref, o_ref, vt_ref, s_ref, p_ref, w_ref, *, step):
    i = pl.program_id(1)

    @pl.when(i == 0)
    def _():
        for h in range(B_HEADS):
            _transpose_v(vt_ref.at[h], v_ref, h * HEAD_DIM)

    assert step == q_ref.shape[0]
    _for_causal_prefix(i, q_ref.shape[0], k_ref.shape[0], step, functools.partial(
        _dil_prefix, q_ref=q_ref, k_ref=k_ref, vt_ref=vt_ref, o_ref=o_ref,
        s_ref=s_ref, p_ref=p_ref, w_ref=w_ref))


def _dil_call(p3, tq, step):
    b, s, _ = p3.shape
    width = B_HEADS * HEAD_DIM
    return pl.pallas_call(
        functools.partial(_dil_kernel, step=step),
        grid=(b, s // tq),
        in_specs=[
            pl.BlockSpec((None, tq, width), lambda bi, i: (bi, i, _OFF["b_q"] // 4)),
            pl.BlockSpec((None, s, width), lambda bi, i: (bi, 0, _OFF["b_k"] // 4)),
            pl.BlockSpec((None, s, width), lambda bi, i: (bi, 0, _OFF["b_v"] // 4)),
        ],
        out_specs=pl.BlockSpec((None, tq, width), lambda bi, i: (bi, i, 0)),
        out_shape=jax.ShapeDtypeStruct((b, s, width), BF16),
        scratch_shapes=[
            pltpu.VMEM((B_HEADS, HEAD_DIM + ONES_ROWS, s), BF16),
            pltpu.VMEM((B_HEADS, s, tq), F32),
            pltpu.VMEM((B_HEADS, s, tq), BF16),
            pltpu.VMEM((s, tq), F32),
        ],
        compiler_params=_params(("parallel", "arbitrary"), "dilated"),
        name="dilated",
    )(p3, p3, p3)


def _gelu_tanh(x):
    return 0.5 * x * (1.0 + jnp.tanh(np.float32(np.sqrt(2.0 / np.pi)) * (x + 0.044715 * (x * x * x))))


def _cmp_kernel(*refs):
    n = CMP_STRIDE
    xk, xv = refs[:n], refs[n:2 * n]
    pe_k, w1_k, w2_k, pe_v, w1_v, w2_v, kc_ref, vct_ref = refs[2 * n:]

    def branch(x_refs, pe_ref, w1_ref, w2_ref):
        lo = hi = None
        for j in range(n):
            xj = x_refs[j][...].astype(F32)
            a = (xj + pe_ref[j:j + 1, :]).astype(BF16)
            b = (xj + pe_ref[n + j:n + j + 1, :]).astype(BF16)
            dl = jnp.dot(a, w1_ref[j * HEAD_DIM:(j + 1) * HEAD_DIM, :], preferred_element_type=F32)
            dh = jnp.dot(b, w1_ref[(n + j) * HEAD_DIM:(n + j + 1) * HEAD_DIM, :], preferred_element_type=F32)
            lo = dl if lo is None else lo + dl
            hi = dh if hi is None else hi + dh
        nc = lo.shape[0]
        hid = lo + pltpu.roll(hi, nc - 1, 0)
        return jnp.dot(_gelu_tanh(hid).astype(BF16), w2_ref[...], preferred_element_type=F32)

    kc_ref[...] = branch(xk, pe_k, w1_k, w2_k).astype(kc_ref.dtype)
    vct_ref[...] = branch(xv, pe_v, w1_v, w2_v).T.astype(vct_ref.dtype)


def _cmp_call(pc, b, layer, pe_k, w1_k, w2_k, pe_v, w1_v, w2_v):
    nc = pc.shape[0] // b
    pc3 = pc.reshape(b, nc, CMP_STRIDE * CMP_COLS)
    x_specs = []
    for first in (0, C_KV_GROUPS):
        for j in range(CMP_STRIDE):
            x_specs.append(pl.BlockSpec((None, nc, HEAD_DIM),
                                        lambda bi, g, j=j, first=first: (bi, 0, j * CMP_BLOCKS + first + g)))
    flat = CMP_BLOCK * HEAD_DIM
    w_specs = [
        _resident((None, CMP_BLOCK, HEAD_DIM), lambda bi, g: (layer, 0, 0)),
        _resident((None, flat, CMP_HIDDEN), lambda bi, g: (layer, 0, 0)),
        _resident((None, CMP_HIDDEN, HEAD_DIM), lambda bi, g: (layer, 0, 0)),
    ]
    return pl.pallas_call(
        _cmp_kernel,
        grid=(b, C_KV_GROUPS),
        in_specs=x_specs + w_specs + w_specs,
        out_specs=[
            pl.BlockSpec((None, None, nc, HEAD_DIM), lambda bi, g: (bi, g, 0, 0)),
            pl.BlockSpec((None, None, HEAD_DIM, nc), lambda bi, g: (bi, g, 0, 0)),
        ],
        out_shape=[
            jax.ShapeDtypeStruct((b, C_KV_GROUPS, nc, HEAD_DIM), BF16),
            jax.ShapeDtypeStruct((b, C_KV_GROUPS, HEAD_DIM, nc), BF16),
        ],
        compiler_params=_params(("parallel", "parallel"), "nsa_compress"),
        name="nsa_compress",
    )(*([pc3] * (2 * CMP_STRIDE)), pe_k, w1_k, w2_k, pe_v, w1_v, w2_v)


def _nsa_selected(rows, t_row, q_sel, ke_ref, vst_ref, acc_ref, s_ref, p_ref, step):
    tq = q_sel[0].shape[0]

    def mask_of(r0, n):
        srow = r0 + lax.broadcasted_iota(jnp.int32, (n, tq), 0)
        return t_row - srow >= 0, None

    vt = vst_ref[:, 0:rows]
    outs = _attend_heads(q_sel, lambda h, r0, n: ke_ref[r0:r0 + n, :],
                         lambda h, p: jnp.dot(vt, p, preferred_element_type=F32),
                         rows, mask_of, s_ref, p_ref, plain_rows=rows - max(step, tq))
    for r in range(C_GROUP_SIZE):
        acc_ref[r] = outs[r]


def _nsa_kernel(q_ref, kc_ref, vct_ref, ks_ref, vs_ref, kw_ref, vw_ref, g_ref, cos_ref, sin_ref,
                isect_ref, expand_ref, o_ref, vst_ref, vwt_ref, acc_ref, s_ref, p_ref, ke_ref, *, n_sel, step):
    i = pl.program_id(2)
    tq = q_ref.shape[0]
    s_len = ks_ref.shape[0]
    nc = kc_ref.shape[0]
    n_slc = isect_ref.shape[0]

    @pl.when(i == 0)
    def _():
        _transpose_v(vst_ref, vs_ref)
        _transpose_v_tiles(vwt_ref, vw_ref)
        ke_ref[:, 0:HEAD_DIM] = ks_ref[...]
        ke_ref[:, HEAD_DIM:] = expand_ref[...]

    t_row = i * tq + lax.broadcasted_iota(jnp.int32, (1, tq), 1)

    kc = kc_ref[...]
    vct = vct_ref[...]
    cend = lax.broadcasted_iota(jnp.int32, (nc, tq), 0) * CMP_STRIDE + (CMP_BLOCK - 1)
    cmask = cend <= t_row
    o_cmp = []
    p_sum = jnp.zeros((nc, tq), F32)
    for r in range(C_GROUP_SIZE):
        q_r = q_ref[:, r * HEAD_DIM:(r + 1) * HEAD_DIM]
        sc = lax.dot_general(kc, q_r, (((1,), (1,)), ((), ())), preferred_element_type=F32)
        sc = jnp.where(cmask, sc, NEG)
        mx = jnp.max(sc, axis=0, keepdims=True)
        e = jnp.where(cmask, jnp.exp2(sc - mx), 0.0)
        den = jnp.sum(e, axis=0, keepdims=True)
        p = e / jnp.where(den > 0, den, 1.0)
        p_sum = p_sum + p
        o_cmp.append(jnp.dot(vct, p.astype(BF16), preferred_element_type=F32))

    isect = isect_ref[...]
    p_hi = p_sum.astype(BF16)
    p_lo = (p_sum - p_hi.astype(F32)).astype(BF16)
    imp = (jnp.dot(isect, p_hi, preferred_element_type=F32)
           + jnp.dot(isect, p_lo, preferred_element_type=F32))
    blk = lax.broadcasted_iota(jnp.int32, (n_slc, tq), 0)
    cur = t_row // SEL_BLOCK
    val = jnp.where(blk == 0, jnp.inf, jnp.where(blk == cur, jnp.inf, jnp.where(blk == cur - 1, jnp.inf, imp)))
    val = jnp.where(blk <= cur, val, -jnp.inf)
    rank = jnp.zeros((n_slc, tq), jnp.int32)
    for m in range(n_slc):
        vm = val[m:m + 1, :]
        before = jnp.where(vm > val, 1, jnp.where(vm == val, jnp.where(blk > m, 1, 0), 0))
        rank = rank + before
    bias = jnp.where(rank < n_sel, 0.0, NEG)
    if n_slc < LANE:
        bias = jnp.concatenate([bias, jnp.zeros((LANE - n_slc, tq), F32)], axis=0)
    bias_t = bias.T.astype(BF16)

    cos, sin = cos_ref[...], sin_ref[...]
    cols = [slice(r * HEAD_DIM, (r + 1) * HEAD_DIM) for r in range(C_GROUP_SIZE)]
    q_rot = [_rope_full(q_ref[:, c].astype(F32), cos, sin).astype(BF16) for c in cols]
    q_sel = [jnp.concatenate([q, bias_t], axis=1) for q in q_rot]

    _for_causal_prefix(i, tq, s_len, step, functools.partial(
        _nsa_selected, t_row=t_row, q_sel=q_sel, ke_ref=ke_ref, vst_ref=vst_ref,
        acc_ref=acc_ref, s_ref=s_ref, p_ref=p_ref, step=step))

    w_tiles = min((WIN_SIZE - 1 + tq - 1) // LANE + 1, s_len // LANE)
    w_rows = w_tiles * LANE
    w0 = pl.multiple_of(jnp.maximum(i * tq + tq - w_rows, 0), LANE)
    t0 = w0 // LANE

    def win_mask(r0, n):
        delta = t_row - (w0 + r0 + lax.broadcasted_iota(jnp.int32, (n, tq), 0))
        return jnp.where(delta >= 0, jnp.where(delta <= WIN_SIZE - 1, 1, 0), 0) > 0, None

    def pv_win(h, p):
        out = None
        for w in range(w_tiles):
            part = jnp.dot(vwt_ref[t0 + w], p[w * LANE:(w + 1) * LANE, :], preferred_element_type=F32)
            out = part if out is None else out + part
        return out

    o_win = _attend_heads(q_rot, lambda h, r0, n: kw_ref[pl.ds(w0 + r0, n), :], pv_win,
                          w_rows, win_mask, s_ref, p_ref)

    g_lane0 = _MISC_LANES["c_g"][0]
    per_group = 3 * C_GROUP_SIZE
    logits_t = g_ref[...].astype(F32).T
    mine = [logits_t[g_lane0 + k * per_group:g_lane0 + (k + 1) * per_group, :] for k in range(C_KV_GROUPS)]
    group = pl.program_id(1)
    logits = mine[0]
    for k in range(1, C_KV_GROUPS):
        logits = jnp.where(group == k, mine[k], logits)
    gates = jax.nn.sigmoid(logits)
    for r, c in enumerate(cols):
        o_t = (gates[3 * r:3 * r + 1, :] * o_cmp[r] + gates[3 * r + 1:3 * r + 2, :] * acc_ref[r]
               + gates[3 * r + 2:3 * r + 3, :] * o_win[r])
        o_ref[:, c] = o_t.T.astype(o_ref.dtype)


def _nsa_call(p3, kc, vct, cos, sin, isect, expand, tq, step):
    b, s, _ = p3.shape
    nc = kc.shape[2]
    n_slc = s // SEL_BLOCK
    width = C_GROUP_SIZE * HEAD_DIM
    kv = lambda name: pl.BlockSpec((None, s, HEAD_DIM), lambda bi, g, i, name=name: (bi, 0, _OFF[name] + g))
    return pl.pallas_call(
        functools.partial(_nsa_kernel, n_sel=min(SEL_COUNT, n_slc), step=step),
        grid=(b, C_KV_GROUPS, s // tq),
        in_specs=[
            pl.BlockSpec((None, tq, width), lambda bi, g, i: (bi, i, _OFF["c_q"] // 4 + g)),
            pl.BlockSpec((None, None, nc, HEAD_DIM), lambda bi, g, i: (bi, g, 0, 0)),
            pl.BlockSpec((None, None, HEAD_DIM, nc), lambda bi, g, i: (bi, g, 0, 0)),
            kv("c_ks"), kv("c_vs"), kv("c_kw"), kv("c_vw"),
            pl.BlockSpec((None, tq, LANE), lambda bi, g, i: (bi, i, _OFF["misc"])),
            pl.BlockSpec((tq, LANE), lambda bi, g, i: (i, 0)),
            pl.BlockSpec((tq, LANE), lambda bi, g, i: (i, 0)),
            pl.BlockSpec(isect.shape, lambda bi, g, i: (0, 0)),
            pl.BlockSpec(expand.shape, lambda bi, g, i: (0, 0)),
        ],
        out_specs=pl.BlockSpec((None, tq, width), lambda bi, g, i: (bi, i, g)),
        out_shape=jax.ShapeDtypeStruct((b, s, C_HEADS * HEAD_DIM), BF16),
        scratch_shapes=[
            pltpu.VMEM((HEAD_DIM + ONES_ROWS, s), BF16),
            pltpu.VMEM((s // LANE, HEAD_DIM + ONES_ROWS, LANE), BF16),
            pltpu.VMEM((C_GROUP_SIZE, HEAD_DIM, tq), F32),
            pltpu.VMEM((C_GROUP_SIZE, s, tq), F32),
            pltpu.VMEM((C_GROUP_SIZE, s, tq), BF16),
            pltpu.VMEM((s, HEAD_DIM + expand.shape[1]), BF16),
        ],
        compiler_params=_params(("parallel", "parallel", "arbitrary"), "nsa"),
        name="nsa",
    )(p3, kc, vct, p3, p3, p3, p3, p3, cos, sin, isect, expand)


def _out_kernel(oa_ref, ob_ref, oc_ref, w_ref, x_ref, gpost_ref, gnext_ref, xo_ref, h_ref):
    na, nb = oa_ref.shape[1], ob_ref.shape[1]
    tm = x_ref.shape[0]
    sub = min(OUT_SUB_ROWS, tm)
    for r0 in range(0, tm, sub):
        r = slice(r0, r0 + sub)
        y = jnp.dot(oa_ref[r, :], w_ref[0:na, :], preferred_element_type=F32)
        y = y + jnp.dot(ob_ref[r, :], w_ref[na:na + nb, :], preferred_element_type=F32)
        y = y + jnp.dot(oc_ref[r, :], w_ref[na + nb:, :], preferred_element_type=F32)
        x = x_ref[r, :] + _rms(y, gpost_ref[...])
        xo_ref[r, :] = x
        h_ref[r, :] = _rms(x, gnext_ref[...]).astype(h_ref.dtype)


def _out_call(o_a, o_b, o_c, w_out_p, layer, x, g_post, g_next, tm):
    m, d = x.shape
    row = lambda width: pl.BlockSpec((tm, width), lambda i: (i, 0))
    vec = pl.BlockSpec((1, d), lambda i: (0, 0))
    return pl.pallas_call(
        _out_kernel,
        grid=(m // tm,),
        in_specs=[
            row(o_a.shape[1]), row(o_b.shape[1]), row(o_c.shape[1]),
            _resident((None, w_out_p.shape[1], d), lambda i: (layer, 0, 0)),
            row(d), vec, vec,
        ],
        out_specs=[row(d), row(d)],
        out_shape=[jax.ShapeDtypeStruct((m, d), F32), jax.ShapeDtypeStruct((m, d), BF16)],
        compiler_params=_params(("parallel",), "out_proj"),
        name="out_proj",
    )(o_a, o_b, o_c, w_out_p, x, g_post, g_next)


def _ffn_kernel(h_ref, wg_ref, wu_ref, wd_ref, y_ref):
    @pl.when(pl.program_id(1) == 0)
    def _():
        y_ref[...] = jnp.zeros(y_ref.shape, F32)

    h = h_ref[...]
    g = jnp.dot(h, wg_ref[...], preferred_element_type=F32)
    u = jnp.dot(h, wu_ref[...], preferred_element_type=F32)
    a = (g * jax.nn.sigmoid(g) * u).astype(BF16)
    y_ref[...] += jnp.dot(a, wd_ref[...], preferred_element_type=F32)


def _ffn_call(h, wg, wu, wd, layer, tm, tf):
    m, d = h.shape
    d_ff = wg.shape[2]
    row = pl.BlockSpec((tm, d), lambda i, f: (i, 0))
    return pl.pallas_call(
        _ffn_kernel,
        grid=(m // tm, d_ff // tf),
        in_specs=[
            row,
            pl.BlockSpec((None, d, tf), lambda i, f: (layer, 0, f)),
            pl.BlockSpec((None, d, tf), lambda i, f: (layer, 0, f)),
            pl.BlockSpec((None, tf, d), lambda i, f: (layer, f, 0)),
        ],
        out_specs=row,
        out_shape=jax.ShapeDtypeStruct((m, d), F32),
        compiler_params=_params(("parallel", "arbitrary"), "ffn"),
        name="ffn",
    )(h, wg, wu, wd)


def _residual_kernel(y_ref, x_ref, g_ref, o_ref):
    o_ref[...] = x_ref[...] + _rms(y_ref[...], g_ref[...])


def _residual_call(y, x, g, tm):
    m, d = x.shape
    row = pl.BlockSpec((tm, d), lambda i: (i, 0))
    return pl.pallas_call(
        _residual_kernel,
        grid=(m // tm,),
        in_specs=[row, row, pl.BlockSpec((1, d), lambda i: (0, 0))],
        out_specs=row,
        out_shape=jax.ShapeDtypeStruct((m, d), F32),
        compiler_params=_params(("parallel",), "ffn_residual"),
        name="ffn_residual",
    )(y, x, g)


def _w_in_layout():
    widths = (A_HEADS * HEAD_DIM, HEAD_DIM, HEAD_DIM, IDX_HEADS * IDX_DIM, IDX_DIM, IDX_HEADS,
              B_HEADS * HEAD_DIM, B_HEADS * HEAD_DIM, B_HEADS * HEAD_DIM, C_HEADS * HEAD_DIM,
              C_KV_GROUPS * HEAD_DIM, C_KV_GROUPS * HEAD_DIM, C_KV_GROUPS * HEAD_DIM, C_KV_GROUPS * HEAD_DIM,
              C_KV_GROUPS * HEAD_DIM, C_KV_GROUPS * HEAD_DIM, 3 * C_HEADS)
    names = ("a_q", "a_k", "a_v", "i_q", "i_k", "i_w", "b_q", "b_k", "b_v", "c_q",
             "c_kc", "c_vc", "c_ks", "c_vs", "c_kw", "c_vw", "c_g")
    offs = np.concatenate([[0], np.cumsum(widths)])
    src = {n: (int(offs[k]), int(widths[k])) for k, n in enumerate(names)}
    pieces = []
    dst = 0
    for name, blocks in [(n, w) for n, w, _, _ in _SEGMENTS] + list(_CMP_SEGMENTS):
        rows = blocks * LANE
        if name == "misc":
            parts, used = [], 0
            for part, (lane0, width) in _MISC_LANES.items():
                assert lane0 == used and src[part][1] == width
                parts.append(src[part])
                used += width
            pieces.append((dst, parts, rows - used))
        else:
            start, width = src[name]
            assert width == rows
            for r0 in range(0, rows, _PROJ_CHUNK * LANE):
                pieces.append((dst + r0, [(start + r0, min(_PROJ_CHUNK * LANE, rows - r0))], 0))
        dst += rows
    return pieces, int(offs[-1])


def _w_prep_kernel(w_ref, o_ref, *, pieces):
    cols = w_ref.shape[1]
    for dst, parts, n_zero in pieces:
        vals = [w_ref[r:r + n, :] for r, n in parts]
        if n_zero:
            vals.append(jnp.zeros((n_zero, cols), F32))
        blk = vals[0] if len(vals) == 1 else jnp.concatenate(vals, axis=0)
        o_ref[dst:dst + blk.shape[0], :] = blk.astype(o_ref.dtype)


def _prep_w_in(w_in, tk=256):
    w_t = jnp.swapaxes(w_in, 1, 2)
    depth, width, d = w_t.shape
    pieces, src_width = _w_in_layout()
    assert width == src_width
    return pl.pallas_call(
        functools.partial(_w_prep_kernel, pieces=pieces),
        grid=(depth, d // tk),
        in_specs=[pl.BlockSpec((None, width, tk), lambda l, i: (l, 0, i))],
        out_specs=pl.BlockSpec((None, NW, tk), lambda l, i: (l, 0, i)),
        out_shape=jax.ShapeDtypeStruct((depth, NW, d), BF16),
        compiler_params=_params(("parallel", "parallel"), "w_in_prep"),
        name="w_in_prep",
    )(w_t)


def _rope_tables(seq):
    def tables(dim):
        inv = 1.0 / (ROPE_THETA ** (jnp.arange(0, dim, 2, dtype=F32) / dim))
        ang = jnp.arange(seq, dtype=F32)[:, None] * inv[None, :]
        cos, sin = jnp.cos(ang), jnp.sin(ang)
        reps = LANE // dim
        return (jnp.tile(jnp.concatenate([cos, cos], axis=-1), (1, reps)),
                jnp.tile(jnp.concatenate([-sin, sin], axis=-1), (1, reps)))
    return tables(HEAD_DIM) + tables(IDX_DIM)


def _selection_constants(seq):
    nc = seq // CMP_STRIDE
    n_slc = seq // SEL_BLOCK
    c_start = np.arange(nc) * CMP_STRIDE
    n_start = np.arange(n_slc) * SEL_BLOCK
    isect = ((c_start[None, :] < n_start[:, None] + SEL_BLOCK)
             & (c_start[None, :] + CMP_BLOCK > n_start[:, None])
             & (np.arange(nc)[None, :] < nc - CMP_BLOCK // CMP_STRIDE + 1)).astype(np.float32)
    expand = np.zeros((seq, max(n_slc, LANE)), np.float32)
    expand[np.arange(seq), np.arange(seq) // SEL_BLOCK] = 1.0
    return jnp.asarray(isect, BF16), jnp.asarray(expand, BF16)


def _tiles(seq):
    return dict(tq=min(2 * LANE, seq), step_dsa=min(4 * LANE, seq), step=min(2 * LANE, seq),
                tm_proj=min(256, seq), tm_out=min(512, seq), tm_ffn=min(1024, seq), tf=512)


def kernel(x, w_in, w_out, cmp_pe_k, cmp_w1_k, cmp_w2_k, cmp_pe_v, cmp_w1_v, cmp_w2_v,
           w_gate, w_up, w_down, g_pre_mix, g_post_mix, g_pre_ffn, g_post_ffn):
    b, s, d = x.shape
    depth = w_in.shape[0]
    m = b * s
    t = _tiles(s)
    tq, step = t["tq"], t["step"]

    w_in_p = _prep_w_in(w_in)
    w_out_p = w_out.astype(BF16)
    wg, wu, wd = w_gate.astype(BF16), w_up.astype(BF16), w_down.astype(BF16)
    w1_k, w2_k = cmp_w1_k.astype(BF16), cmp_w2_k.astype(BF16)
    w1_v, w2_v = cmp_w1_v.astype(BF16), cmp_w2_v.astype(BF16)
    col_scale = jnp.asarray(_COL_SCALE)
    tabs = _rope_tables(s)
    isect, expand = _selection_constants(s)
    vec = lambda g, layer: g[layer][None, :]

    xf = x.reshape(m, d)
    y = None
    for layer in range(depth):
        if layer == 0:
            p, pc = _proj_call((xf, vec(g_pre_mix, 0)), w_in_p, layer, col_scale, tabs, s, t["tm_proj"])
        else:
            rows_in = (y, xf, vec(g_post_ffn, layer - 1), vec(g_pre_mix, layer))
            xf, p, pc = _proj_call(rows_in, w_in_p, layer, col_scale, tabs, s, t["tm_proj"])
        p3 = p.reshape(b, s, NP)
        o_a = _dsa_call(p3, tq, t["step_dsa"])
        o_b = _dil_call(p3, tq, step)
        kc, vct = _cmp_call(pc, b, layer, cmp_pe_k, w1_k, w2_k, cmp_pe_v, w1_v, w2_v)
        o_c = _nsa_call(p3, kc, vct, tabs[0], tabs[1], isect, expand, tq, step)
        xf, h = _out_call(o_a.reshape(m, -1), o_b.reshape(m, -1), o_c.reshape(m, -1), w_out_p, layer,
                          xf, vec(g_post_mix, layer), vec(g_pre_ffn, layer), t["tm_out"])
        y = _ffn_call(h, wg, wu, wd, layer, t["tm_ffn"], t["tf"])
    return _residual_call(y, xf, vec(g_post_ffn, depth - 1), t["tm_out"]).reshape(b, s, d)
```

```python
import functools

import numpy as np
import jax
import jax.numpy as jnp
from jax import lax
from jax.experimental import pallas as pl
from jax.experimental.pallas import tpu as pltpu

F32 = jnp.float32
BF16 = jnp.bfloat16

LANE = 128
SUBLANE = 8
HEAD_DIM = 128
A_HEADS = 4
B_HEADS = 4
C_HEADS = 8
C_KV_GROUPS = 2
C_GROUP_SIZE = C_HEADS // C_KV_GROUPS
IDX_HEADS = 16
IDX_DIM = 64
DSA_TOPK_MAX = 256
DILATED_PATTERNS = ((128, 1), (512, 4), (2048, 16))
CMP_BLOCK = 32
CMP_STRIDE = 16
CMP_HIDDEN = 256
SEL_BLOCK = 64
SEL_COUNT = 16
WIN_SIZE = 512
ROPE_THETA = 10000.0
RMS_EPS = 1e-6
ATTN_SCALE = HEAD_DIM ** -0.5 * float(np.log2(np.e))
IDX_SCALE = IDX_DIM ** -0.5 * IDX_HEADS ** -0.5

NEG = -1e30
INT_MIN = -(2 ** 31)
HALF16 = 2 ** 15
COUNT_ROWS = 8 * SUBLANE
SCORE_ROWS = 256
ATT_ROWS = 128
ONES_ROWS = 16
OUT_SUB_ROWS = 256

MODE_NONE, MODE_ROPE, MODE_IROPE, MODE_MISC = 0, 1, 2, 3

_SEGMENTS = (
    ("i_q", 8, MODE_IROPE, IDX_SCALE),
    ("c_q", 8, MODE_NONE, ATTN_SCALE),
    ("a_q", 4, MODE_ROPE, ATTN_SCALE),
    ("b_q", 4, MODE_ROPE, ATTN_SCALE),
    ("b_k", 4, MODE_ROPE, 1.0),
    ("b_v", 4, MODE_NONE, 1.0),
    ("c_ks", 2, MODE_ROPE, 1.0),
    ("c_vs", 2, MODE_NONE, 1.0),
    ("c_kw", 2, MODE_ROPE, 1.0),
    ("c_vw", 2, MODE_NONE, 1.0),
    ("a_k", 1, MODE_ROPE, 1.0),
    ("a_v", 1, MODE_NONE, 1.0),
    ("misc", 1, MODE_MISC, 1.0),
)
_MISC_LANES = {"i_k": (0, IDX_DIM), "i_w": (IDX_DIM, IDX_HEADS), "c_g": (IDX_DIM + IDX_HEADS, 3 * C_HEADS)}
_CMP_SEGMENTS = (("c_kc", 2), ("c_vc", 2))
_OFF = {}
_o = 0
for _n, _w, _m, _s in _SEGMENTS:
    _OFF[_n] = _o
    _o += _w
NP_BLOCKS = _o
NP = NP_BLOCKS * LANE
CMP_BLOCKS = sum(w for _, w in _CMP_SEGMENTS)
CMP_COLS = CMP_BLOCKS * LANE
NW = NP + CMP_COLS
_BLOCK_MODE = tuple(m for _, w, m, _ in _SEGMENTS for _ in range(w))
_COL_SCALE = np.repeat(np.array([s for _, w, _, s in _SEGMENTS for _ in range(w)], np.float32), LANE)[None, :]
_PROJ_CHUNK = 4


_VMEM_MB = dict(in_proj=54, dsa=48, dilated=48, nsa_compress=32, nsa=48, out_proj=48, ffn=52,
                ffn_residual=32, w_in_prep=40)


def _params(sem, name):
    return pltpu.CompilerParams(dimension_semantics=sem, vmem_limit_bytes=_VMEM_MB[name] * 1024 * 1024)


def _resident(block_shape, index_map):
    return pl.BlockSpec(block_shape, index_map, pipeline_mode=pl.Buffered(1))


def _rms(x, g):
    return x * lax.rsqrt(jnp.mean(x * x, axis=-1, keepdims=True) + RMS_EPS) * g


def _rope_full(a, cos, sin):
    return a * cos + pltpu.roll(a, HEAD_DIM // 2, 1) * sin


def _rope_idx(a, cos, sin, first_half):
    partner = jnp.where(first_half, pltpu.roll(a, LANE - IDX_DIM // 2, 1), pltpu.roll(a, IDX_DIM // 2, 1))
    return a * cos + partner * sin


def _proj_kernel(*refs, after_ffn):
    if after_ffn:
        (y_ref, x_ref, gpost_ref, gpre_ref, w_ref, cs_ref, cos_ref, sin_ref, icos_ref, isin_ref,
         xo_ref, o_ref, oc_ref, cmp_ref) = refs
        x = x_ref[...] + _rms(y_ref[...], gpost_ref[...])
        xo_ref[...] = x
        h = _rms(x, gpre_ref[...]).astype(BF16)
    else:
        x_ref, gpre_ref, w_ref, cs_ref, cos_ref, sin_ref, icos_ref, isin_ref, o_ref, oc_ref, cmp_ref = refs
        h = _rms(x_ref[...], gpre_ref[...]).astype(BF16)
    tm = h.shape[0]
    lane = lax.broadcasted_iota(jnp.int32, (tm, LANE), 1)
    first_half = (lane & (IDX_DIM - 1)) < IDX_DIM // 2
    for c0 in range(0, NP_BLOCKS, _PROJ_CHUNK):
        nb = min(_PROJ_CHUNK, NP_BLOCKS - c0)
        acc = lax.dot_general(h, w_ref[c0 * LANE:(c0 + nb) * LANE, :], (((1,), (1,)), ((), ())),
                              preferred_element_type=F32)
        for b in range(nb):
            col = slice((c0 + b) * LANE, (c0 + b + 1) * LANE)
            a = acc[:, b * LANE:(b + 1) * LANE] * cs_ref[:, col]
            mode = _BLOCK_MODE[c0 + b]
            if mode == MODE_ROPE:
                a = _rope_full(a, cos_ref[...], sin_ref[...])
            elif mode == MODE_IROPE:
                a = _rope_idx(a, icos_ref[...], isin_ref[...], first_half)
            elif mode == MODE_MISC:
                a = jnp.where(lane < IDX_DIM, _rope_idx(a, icos_ref[...], isin_ref[...], first_half), a)
            o_ref[:, col] = a.astype(o_ref.dtype)
    acc = lax.dot_general(h, w_ref[NP:NW, :], (((1,), (1,)), ((), ())), preferred_element_type=F32)
    for b in range(CMP_BLOCKS):
        cmp_ref[b] = acc[:, b * LANE:(b + 1) * LANE]
    for j in range(CMP_STRIDE):
        for b in range(CMP_BLOCKS):
            rows = cmp_ref[b, pl.ds(j, tm // CMP_STRIDE, stride=CMP_STRIDE), :]
            oc_ref[:, j * CMP_COLS + b * LANE:j * CMP_COLS + (b + 1) * LANE] = rows.astype(oc_ref.dtype)


def _proj_call(rows_in, w_in_p, layer, col_scale, tabs, seq, tm):
    after_ffn = len(rows_in) > 2
    m, d = rows_in[0].shape
    tpb = seq // tm
    tab_spec = pl.BlockSpec((tm, LANE), lambda i: (i % tpb, 0))
    row = pl.BlockSpec((tm, d), lambda i: (i, 0))
    vec = pl.BlockSpec((1, d), lambda i: (0, 0))
    out_specs = [
        pl.BlockSpec((tm, NP), lambda i: (i, 0)),
        pl.BlockSpec((tm // CMP_STRIDE, CMP_STRIDE * CMP_COLS), lambda i: (i, 0)),
    ]
    out_shape = [
        jax.ShapeDtypeStruct((m, NP), BF16),
        jax.ShapeDtypeStruct((m // CMP_STRIDE, CMP_STRIDE * CMP_COLS), BF16),
    ]
    if after_ffn:
        out_specs = [row] + out_specs
        out_shape = [jax.ShapeDtypeStruct((m, d), F32)] + out_shape
    return pl.pallas_call(
        functools.partial(_proj_kernel, after_ffn=after_ffn),
        grid=(m // tm,),
        in_specs=([row, row, vec, vec] if after_ffn else [row, vec]) + [
            _resident((None, NW, d), lambda i: (layer, 0, 0)),
            _resident((1, NP), lambda i: (0, 0)),
            tab_spec, tab_spec, tab_spec, tab_spec,
        ],
        out_specs=out_specs,
        out_shape=out_shape,
        scratch_shapes=[pltpu.VMEM((CMP_BLOCKS, tm, LANE), F32)],
        compiler_params=_params(("parallel",), "in_proj"),
        name="in_proj",
    )(*rows_in, w_in_p, col_scale, *tabs)


def _transpose_v(dst_ref, src_ref, col0=0):
    rows = src_ref.shape[0]
    for r0 in range(0, rows, LANE):
        tile = src_ref[r0:r0 + LANE, col0:col0 + HEAD_DIM].astype(F32)
        dst_ref[0:HEAD_DIM, r0:r0 + LANE] = tile.T.astype(dst_ref.dtype)
    dst_ref[HEAD_DIM:, :] = jnp.ones((ONES_ROWS, rows), dst_ref.dtype)


def _transpose_v_tiles(dst_ref, src_ref):
    rows, _ = src_ref.shape
    for r in range(rows // LANE):
        tile = src_ref[r * LANE:(r + 1) * LANE, :].astype(F32)
        dst_ref[r, 0:HEAD_DIM, :] = tile.T.astype(dst_ref.dtype)
        dst_ref[r, HEAD_DIM:, :] = jnp.ones((ONES_ROWS, LANE), dst_ref.dtype)


def _for_causal_prefix(i, tq, s_len, step, body):
    n = ((i + 1) * tq + step - 1) // step
    for v in range(1, s_len // step + 1):
        pl.when(n == v)(functools.partial(body, v * step))


def _fold(x, op):
    out = x[0:COUNT_ROWS]
    for r0 in range(COUNT_ROWS, x.shape[0], COUNT_ROWS):
        out = op(out, x[r0:r0 + COUNT_ROWS])
    return out


def _fold_tree(x, op):
    while x.shape[0] > SUBLANE:
        half = x.shape[0] // 2
        x = op(x[:half], x[half:])
    return x


def _row_chunk(rows):
    return SCORE_ROWS if rows % SCORE_ROWS == 0 else LANE


def _attend_heads(qs, k_of, pv_of, rows, mask_of, s_ref, p_ref, w_ref=None, plain_rows=0):
    n_heads = len(qs)
    tq = qs[0].shape[0]
    chunk = ATT_ROWS * LANE // tq
    for h in range(n_heads):
        s_ref[h, 0:rows, :] = lax.dot_general(k_of(h, 0, rows), qs[h], (((1,), (1,)), ((), ())),
                                              preferred_element_type=F32)
    m_acc = [jnp.full((SUBLANE, tq), NEG, F32) for _ in qs]
    weighted = False
    for r0 in range(0, rows, chunk):
        if r0 + chunk <= plain_rows:
            for h in range(n_heads):
                m_acc[h] = jnp.maximum(m_acc[h], _fold_tree(s_ref[h, r0:r0 + chunk, :], jnp.maximum))
            continue
        mask, weight = mask_of(r0, chunk)
        if weight is not None:
            weighted = True
            w_ref[r0:r0 + chunk, :] = weight
        for h in range(n_heads):
            s = jnp.where(mask, s_ref[h, r0:r0 + chunk, :], NEG)
            s_ref[h, r0:r0 + chunk, :] = s
            m_acc[h] = jnp.maximum(m_acc[h], _fold_tree(s, jnp.maximum))
    outs = []
    for h in range(n_heads):
        mx = jnp.max(m_acc[h], axis=0, keepdims=True)
        for r0 in range(0, rows, chunk):
            p = jnp.exp2(s_ref[h, r0:r0 + chunk, :] - mx)
            if weighted:
                p = p * w_ref[r0:r0 + chunk, :]
            p_ref[h, r0:r0 + chunk, :] = p.astype(p_ref.dtype)
        pv = pv_of(h, p_ref[h, 0:rows, :])
        outs.append(pv[0:HEAD_DIM] / pv[HEAD_DIM:HEAD_DIM + 1])
    return outs


def _count_rows(key_ref, rows, pred):
    chunk = _row_chunk(rows)
    acc = None
    for r0 in range(0, rows, chunk):
        part = _fold(pred(key_ref[r0:r0 + chunk, :], r0), jnp.add)
        acc = part if acc is None else acc + part
    return jnp.sum(acc.astype(jnp.int32), axis=0, keepdims=True)


def _dsa_prefix(rows, i, iq_ref, iw_t, ik_ref, q_ref, k_ref, vt_ref, o_ref, key_ref, cut_ref, s_ref, p_ref,
                hi_ref, lo_ref, topk, step):
    tq = q_ref.shape[0]
    chunk = min(SCORE_ROWS * LANE // tq, rows)
    crow = lax.broadcasted_iota(jnp.int32, (chunk, tq), 0)
    ctcol = i * tq + lax.broadcasted_iota(jnp.int32, (chunk, tq), 1)

    def score_chunk(r0):
        halves = (ik_ref[0, r0:r0 + chunk, :], ik_ref[1, r0:r0 + chunk, :])
        score = jnp.zeros((chunk, tq), F32)
        for p in range(IDX_HEADS // 2):
            blk = iq_ref[:, p * LANE:(p + 1) * LANE]
            for half in range(2):
                j = 2 * p + half
                lg = lax.dot_general(halves[half], blk, (((1,), (1,)), ((), ())), preferred_element_type=F32)
                score = score + jnp.maximum(lg, 0.0) * iw_t[j:j + 1, :]
        bits = pltpu.bitcast(score, jnp.int32)
        key = bits ^ ((bits >> 31) & 0x7FFFFFFF)
        key = jnp.where(r0 + crow <= ctcol, key, INT_MIN)
        key_ref[r0:r0 + chunk, :] = key
        hi_ref[r0:r0 + chunk, :] = (key >> 16).astype(jnp.int16)
        lo_ref[r0:r0 + chunk, :] = ((key & 0xFFFF) - HALF16).astype(jnp.int16)

    def dead_chunk(r0):
        key_ref[r0:r0 + chunk, :] = jnp.full((chunk, tq), INT_MIN, jnp.int32)
        hi_ref[r0:r0 + chunk, :] = jnp.full((chunk, tq), -HALF16, jnp.int16)
        lo_ref[r0:r0 + chunk, :] = jnp.full((chunk, tq), -HALF16, jnp.int16)

    tile_end = (i + 1) * tq
    for r0 in range(0, rows, chunk):
        if r0 < rows - step + tq:
            score_chunk(r0)
        else:
            pl.when(r0 < tile_end)(functools.partial(score_chunk, r0))
            pl.when(r0 >= tile_end)(functools.partial(dead_chunk, r0))

    def search16(ref, target):
        def bit_step(b, off):
            cand_off = off | jnp.left_shift(jnp.int32(1), 15 - b)
            cand = (cand_off - HALF16).astype(jnp.int16)
            cnt = _count_rows(ref, rows, lambda c, r0: jnp.where(c >= cand, jnp.int16(1), jnp.int16(0)))
            return jnp.where(cnt >= target, cand_off, off)

        return lax.fori_loop(0, 16, bit_step, jnp.zeros((1, tq), jnp.int32)) - HALF16

    thr_hi = search16(hi_ref, topk)
    thr_hi16 = thr_hi.astype(jnp.int16)
    above = _count_rows(hi_ref, rows, lambda c, r0: jnp.where(c > thr_hi16, jnp.int16(1), jnp.int16(0)))
    chunk16 = _row_chunk(rows)
    for r0 in range(0, rows, chunk16):
        same = hi_ref[r0:r0 + chunk16, :] == thr_hi16
        hi_ref[r0:r0 + chunk16, :] = jnp.where(same, lo_ref[r0:r0 + chunk16, :], jnp.int16(-HALF16))
    thr_lo = search16(hi_ref, topk - above)
    thr = jnp.left_shift(thr_hi, 16) | (thr_lo + HALF16)
    short = thr == INT_MIN
    need = topk - _count_rows(key_ref, rows, lambda kc, r0: jnp.where(kc > thr, 1, 0))
    n_eq = _count_rows(key_ref, rows, lambda kc, r0: jnp.where(kc == thr, 1, 0))
    excess = jnp.where(n_eq > need, jnp.where(short, 0, 1), 0)
    cut_ref[...] = jnp.where(short, -1, rows)

    def row_ids(r0, n):
        return r0 + lax.broadcasted_iota(jnp.int32, (n, tq), 0)

    @pl.when(jnp.max(excess) > 0)
    def _():
        n_bits = max(1, (rows - 1).bit_length())

        def idx_step(b, cut):
            cand = cut | jnp.left_shift(jnp.int32(1), n_bits - 1 - b)
            below = _count_rows(key_ref, rows, lambda kc, r0: jnp.where(
                kc == thr, jnp.where(row_ids(r0, kc.shape[0]) < cand, 1, 0), 0))
            return jnp.where(below < need, cand, cut)

        cut = lax.fori_loop(0, n_bits, idx_step, jnp.zeros((1, tq), jnp.int32))
        cut_ref[...] = jnp.where(excess > 0, cut, cut_ref[...])

    cut = cut_ref[...]

    def mask_of(r0, n):
        kc = key_ref[r0:r0 + n, :]
        chosen = jnp.where(kc > thr, 1, jnp.where(kc == thr, jnp.where(row_ids(r0, n) <= cut, 1, 0), 0))
        return chosen > 0, None

    vt = vt_ref[:, 0:rows]
    cols = [slice(h * HEAD_DIM, (h + 1) * HEAD_DIM) for h in range(A_HEADS)]
    outs = _attend_heads([q_ref[:, c] for c in cols], lambda h, r0, n: k_ref[r0:r0 + n, :],
                         lambda h, p: jnp.dot(vt, p, preferred_element_type=F32),
                         rows, mask_of, s_ref, p_ref)
    for h, c in enumerate(cols):
        o_ref[:, c] = outs[h].T.astype(o_ref.dtype)


def _dsa_kernel(iq_ref, mq_ref, mk_ref, q_ref, k_ref, v_ref, o_ref, vt_ref, key_ref, cut_ref, s_ref, p_ref,
                hi_ref, lo_ref, ikh_ref, *, topk, step):
    i = pl.program_id(1)
    tq = q_ref.shape[0]
    s_len = k_ref.shape[0]

    w_lane0 = _MISC_LANES["i_w"][0]
    assert _MISC_LANES["i_k"] == (0, IDX_DIM)

    @pl.when(i == 0)
    def _():
        _transpose_v(vt_ref, v_ref)
        lane = lax.broadcasted_iota(jnp.int32, (LANE, LANE), 1)
        for r0 in range(0, s_len, LANE):
            blk = mk_ref[r0:r0 + LANE, :].astype(F32)
            ikh_ref[0, r0:r0 + LANE, :] = jnp.where(lane < IDX_DIM, blk, 0.0).astype(ikh_ref.dtype)
            ikh_ref[1, r0:r0 + LANE, :] = jnp.where(lane >= IDX_DIM, pltpu.roll(blk, IDX_DIM, 1),
                                                    0.0).astype(ikh_ref.dtype)

    iw_t = mq_ref[...].astype(F32).T[w_lane0:w_lane0 + IDX_HEADS, :]
    _for_causal_prefix(i, tq, s_len, step, functools.partial(
        _dsa_prefix, i=i, iq_ref=iq_ref, iw_t=iw_t, ik_ref=ikh_ref, q_ref=q_ref, k_ref=k_ref, vt_ref=vt_ref,
        o_ref=o_ref, key_ref=key_ref, cut_ref=cut_ref, s_ref=s_ref, p_ref=p_ref,
        hi_ref=hi_ref, lo_ref=lo_ref, topk=topk, step=step))


def _dsa_call(p3, tq, step):
    b, s, _ = p3.shape
    topk = min(DSA_TOPK_MAX, s // 4)
    blk = lambda name, width: _OFF[name] // width
    return pl.pallas_call(
        functools.partial(_dsa_kernel, topk=topk, step=step),
        grid=(b, s // tq),
        in_specs=[
            pl.BlockSpec((None, tq, 8 * LANE), lambda bi, i: (bi, i, blk("i_q", 8))),
            pl.BlockSpec((None, tq, LANE), lambda bi, i: (bi, i, blk("misc", 1))),
            pl.BlockSpec((None, s, LANE), lambda bi, i: (bi, 0, blk("misc", 1))),
            pl.BlockSpec((None, tq, 4 * LANE), lambda bi, i: (bi, i, blk("a_q", 4))),
            pl.BlockSpec((None, s, LANE), lambda bi, i: (bi, 0, blk("a_k", 1))),
            pl.BlockSpec((None, s, LANE), lambda bi, i: (bi, 0, blk("a_v", 1))),
        ],
        out_specs=pl.BlockSpec((None, tq, A_HEADS * HEAD_DIM), lambda bi, i: (bi, i, 0)),
        out_shape=jax.ShapeDtypeStruct((b, s, A_HEADS * HEAD_DIM), BF16),
        scratch_shapes=[
            pltpu.VMEM((HEAD_DIM + ONES_ROWS, s), BF16),
            pltpu.VMEM((s, tq), jnp.int32),
            pltpu.VMEM((1, tq), jnp.int32),
            pltpu.VMEM((A_HEADS, s, tq), F32),
            pltpu.VMEM((A_HEADS, s, tq), BF16),
            pltpu.VMEM((s, tq), jnp.int16),
            pltpu.VMEM((s, tq), jnp.int16),
            pltpu.VMEM((2, s, LANE), BF16),
        ],
        compiler_params=_params(("parallel", "arbitrary"), "dsa"),
        name="dsa",
    )(p3, p3, p3, p3, p3, p3)


def _dil_prefix(rows, q_ref, k_ref, vt_ref, o_ref, s_ref, p_ref, w_ref):
    tq = q_ref.shape[0]
    first_q = rows - tq
    period = max(d for _, d in DILATED_PATTERNS)
    shared = {}

    def mask_of(r0, n):
        base = first_q - r0
        d_min, d_max = base - (n - 1), base + (tq - 1)
        live = tuple((w, d) for w, d in DILATED_PATTERNS if d_min <= w)
        on_edge = d_min < 0 or any(d_max > w for w, _ in live)
        key = None if on_edge else (live, base % period)
        if key in shared:
            return shared[key]
        delta = (base + lax.broadcasted_iota(jnp.int32, (n, tq), 1)
                 - lax.broadcasted_iota(jnp.int32, (n, tq), 0))
        mult = jnp.zeros((n, tq), jnp.int32)
        for window, dilation in live:
            hit = jnp.where(delta <= window, 1, 0) if d_max > window else 1
            mult = mult + jnp.where((delta & (dilation - 1)) == 0, hit, 0)
        if d_min < 0:
            mult = jnp.where(delta >= 0, mult, 0)
        out = (mult > 0, mult.astype(F32))
        if key is not None:
            shared[key] = out
        return out

    cols = [slice(h * HEAD_DIM, (h + 1) * HEAD_DIM) for h in range(B_HEADS)]
    outs = _attend_heads([q_ref[:, c] for c in cols], lambda h, r0, n: k_ref[r0:r0 + n, cols[h]],
                         lambda h, p: jnp.dot(vt_ref[h, :, 0:rows], p, preferred_element_type=F32),
                         rows, mask_of, s_ref, p_ref, w_ref)
    for h, c in enumerate(cols):
        o_ref[:, c] = outs[h].T.astype(o_ref.dtype)


def _dil_kernel(q_ref, k_ref, v_ref, o_ref, vt_ref, s_ref, p_ref, w_ref, *, step):
    i = pl.program_id(1)

    @pl.when(i == 0)
    def _():
        for h in range(B_HEADS):
            _transpose_v(vt_ref.at[h], v_ref, h * HEAD_DIM)

    assert step == q_ref.shape[0]
    _for_causal_prefix(i, q_ref.shape[0], k_ref.shape[0], step, functools.partial(
        _dil_prefix, q_ref=q_ref, k_ref=k_ref, vt_ref=vt_ref, o_ref=o_ref,
        s_ref=s_ref, p_ref=p_ref, w_ref=w_ref))


def _dil_call(p3, tq, step):
    b, s, _ = p3.shape
    width = B_HEADS * HEAD_DIM
    return pl.pallas_call(
        functools.partial(_dil_kernel, step=step),
        grid=(b, s // tq),
        in_specs=[
            pl.BlockSpec((None, tq, width), lambda bi, i: (bi, i, _OFF["b_q"] // 4)),
            pl.BlockSpec((None, s, width), lambda bi, i: (bi, 0, _OFF["b_k"] // 4)),
            pl.BlockSpec((None, s, width), lambda bi, i: (bi, 0, _OFF["b_v"] // 4)),
        ],
        out_specs=pl.BlockSpec((None, tq, width), lambda bi, i: (bi, i, 0)),
        out_shape=jax.ShapeDtypeStruct((b, s, width), BF16),
        scratch_shapes=[
            pltpu.VMEM((B_HEADS, HEAD_DIM + ONES_ROWS, s), BF16),
            pltpu.VMEM((B_HEADS, s, tq), F32),
            pltpu.VMEM((B_HEADS, s, tq), BF16),
            pltpu.VMEM((s, tq), F32),
        ],
        compiler_params=_params(("parallel", "arbitrary"), "dilated"),
        name="dilated",
    )(p3, p3, p3)


def _gelu_tanh(x):
    return 0.5 * x * (1.0 + jnp.tanh(np.float32(np.sqrt(2.0 / np.pi)) * (x + 0.044715 * (x * x * x))))


def _cmp_kernel(*refs):
    n = CMP_STRIDE
    xk, xv = refs[:n], refs[n:2 * n]
    pe_k, w1_k, w2_k, pe_v, w1_v, w2_v, kc_ref, vct_ref = refs[2 * n:]

    def branch(x_refs, pe_ref, w1_ref, w2_ref):
        lo = hi = None
        for j in range(n):
            xj = x_refs[j][...].astype(F32)
            a = (xj + pe_ref[j:j + 1, :]).astype(BF16)
            b = (xj + pe_ref[n + j:n + j + 1, :]).astype(BF16)
            dl = jnp.dot(a, w1_ref[j * HEAD_DIM:(j + 1) * HEAD_DIM, :], preferred_element_type=F32)
            dh = jnp.dot(b, w1_ref[(n + j) * HEAD_DIM:(n + j + 1) * HEAD_DIM, :], preferred_element_type=F32)
            lo = dl if lo is None else lo + dl
            hi = dh if hi is None else hi + dh
        nc = lo.shape[0]
        hid = lo + pltpu.roll(hi, nc - 1, 0)
        return jnp.dot(_gelu_tanh(hid).astype(BF16), w2_ref[...], preferred_element_type=F32)

    kc_ref[...] = branch(xk, pe_k, w1_k, w2_k).astype(kc_ref.dtype)
    vct_ref[...] = branch(xv, pe_v, w1_v, w2_v).T.astype(vct_ref.dtype)


def _cmp_call(pc, b, layer, pe_k, w1_k, w2_k, pe_v, w1_v, w2_v):
    nc = pc.shape[0] // b
    pc3 = pc.reshape(b, nc, CMP_STRIDE * CMP_COLS)
    x_specs = []
    for first in (0, C_KV_GROUPS):
        for j in range(CMP_STRIDE):
            x_specs.append(pl.BlockSpec((None, nc, HEAD_DIM),
                                        lambda bi, g, j=j, first=first: (bi, 0, j * CMP_BLOCKS + first + g)))
    flat = CMP_BLOCK * HEAD_DIM
    w_specs = [
        _resident((None, CMP_BLOCK, HEAD_DIM), lambda bi, g: (layer, 0, 0)),
        _resident((None, flat, CMP_HIDDEN), lambda bi, g: (layer, 0, 0)),
        _resident((None, CMP_HIDDEN, HEAD_DIM), lambda bi, g: (layer, 0, 0)),
    ]
    return pl.pallas_call(
        _cmp_kernel,
        grid=(b, C_KV_GROUPS),
        in_specs=x_specs + w_specs + w_specs,
        out_specs=[
            pl.BlockSpec((None, None, nc, HEAD_DIM), lambda bi, g: (bi, g, 0, 0)),
            pl.BlockSpec((None, None, HEAD_DIM, nc), lambda bi, g: (bi, g, 0, 0)),
        ],
        out_shape=[
            jax.ShapeDtypeStruct((b, C_KV_GROUPS, nc, HEAD_DIM), BF16),
            jax.ShapeDtypeStruct((b, C_KV_GROUPS, HEAD_DIM, nc), BF16),
        ],
        compiler_params=_params(("parallel", "parallel"), "nsa_compress"),
        name="nsa_compress",
    )(*([pc3] * (2 * CMP_STRIDE)), pe_k, w1_k, w2_k, pe_v, w1_v, w2_v)


def _nsa_selected(rows, t_row, q_sel, ke_ref, vst_ref, acc_ref, s_ref, p_ref, step):
    tq = q_sel[0].shape[0]

    def mask_of(r0, n):
        srow = r0 + lax.broadcasted_iota(jnp.int32, (n, tq), 0)
        return t_row - srow >= 0, None

    vt = vst_ref[:, 0:rows]
    outs = _attend_heads(q_sel, lambda h, r0, n: ke_ref[r0:r0 + n, :],
                         lambda h, p: jnp.dot(vt, p, preferred_element_type=F32),
                         rows, mask_of, s_ref, p_ref, plain_rows=rows - max(step, tq))
    for r in range(C_GROUP_SIZE):
        acc_ref[r] = outs[r]


def _nsa_kernel(q_ref, kc_ref, vct_ref, ks_ref, vs_ref, kw_ref, vw_ref, g_ref, cos_ref, sin_ref,
                isect_ref, expand_ref, o_ref, vst_ref, vwt_ref, acc_ref, s_ref, p_ref, ke_ref, *, n_sel, step):
    i = pl.program_id(2)
    tq = q_ref.shape[0]
    s_len = ks_ref.shape[0]
    nc = kc_ref.shape[0]
    n_slc = isect_ref.shape[0]

    @pl.when(i == 0)
    def _():
        _transpose_v(vst_ref, vs_ref)
        _transpose_v_tiles(vwt_ref, vw_ref)
        ke_ref[:, 0:HEAD_DIM] = ks_ref[...]
        ke_ref[:, HEAD_DIM:] = expand_ref[...]

    t_row = i * tq + lax.broadcasted_iota(jnp.int32, (1, tq), 1)

    kc = kc_ref[...]
    vct = vct_ref[...]
    cend = lax.broadcasted_iota(jnp.int32, (nc, tq), 0) * CMP_STRIDE + (CMP_BLOCK - 1)
    cmask = cend <= t_row
    o_cmp = []
    p_sum = jnp.zeros((nc, tq), F32)
    for r in range(C_GROUP_SIZE):
        q_r = q_ref[:, r * HEAD_DIM:(r + 1) * HEAD_DIM]
        sc = lax.dot_general(kc, q_r, (((1,), (1,)), ((), ())), preferred_element_type=F32)
        sc = jnp.where(cmask, sc, NEG)
        mx = jnp.max(sc, axis=0, keepdims=True)
        e = jnp.where(cmask, jnp.exp2(sc - mx), 0.0)
        den = jnp.sum(e, axis=0, keepdims=True)
        p = e / jnp.where(den > 0, den, 1.0)
        p_sum = p_sum + p
        o_cmp.append(jnp.dot(vct, p.astype(BF16), preferred_element_type=F32))

    isect = isect_ref[...]
    p_hi = p_sum.astype(BF16)
    p_lo = (p_sum - p_hi.astype(F32)).astype(BF16)
    imp = (jnp.dot(isect, p_hi, preferred_element_type=F32)
           + jnp.dot(isect, p_lo, preferred_element_type=F32))
    blk = lax.broadcasted_iota(jnp.int32, (n_slc, tq), 0)
    cur = t_row // SEL_BLOCK
    val = jnp.where(blk == 0, jnp.inf, jnp.where(blk == cur, jnp.inf, jnp.where(blk == cur - 1, jnp.inf, imp)))
    val = jnp.where(blk <= cur, val, -jnp.inf)
    rank = jnp.zeros((n_slc, tq), jnp.int32)
    for m in range(n_slc):
        vm = val[m:m + 1, :]
        before = jnp.where(vm > val, 1, jnp.where(vm == val, jnp.where(blk > m, 1, 0), 0))
        rank = rank + before
    bias = jnp.where(rank < n_sel, 0.0, NEG)
    if n_slc < LANE:
        bias = jnp.concatenate([bias, jnp.zeros((LANE - n_slc, tq), F32)], axis=0)
    bias_t = bias.T.astype(BF16)

    cos, sin = cos_ref[...], sin_ref[...]
    cols = [slice(r * HEAD_DIM, (r + 1) * HEAD_DIM) for r in range(C_GROUP_SIZE)]
    q_rot = [_rope_full(q_ref[:, c].astype(F32), cos, sin).astype(BF16) for c in cols]
    q_sel = [jnp.concatenate([q, bias_t], axis=1) for q in q_rot]

    _for_causal_prefix(i, tq, s_len, step, functools.partial(
        _nsa_selected, t_row=t_row, q_sel=q_sel, ke_ref=ke_ref, vst_ref=vst_ref,
        acc_ref=acc_ref, s_ref=s_ref, p_ref=p_ref, step=step))

    w_tiles = min((WIN_SIZE - 1 + tq - 1) // LANE + 1, s_len // LANE)
    w_rows = w_tiles * LANE
    w0 = pl.multiple_of(jnp.maximum(i * tq + tq - w_rows, 0), LANE)
    t0 = w0 // LANE

    def win_mask(r0, n):
        delta = t_row - (w0 + r0 + lax.broadcasted_iota(jnp.int32, (n, tq), 0))
        return jnp.where(delta >= 0, jnp.where(delta <= WIN_SIZE - 1, 1, 0), 0) > 0, None

    def pv_win(h, p):
        out = None
        for w in range(w_tiles):
            part = jnp.dot(vwt_ref[t0 + w], p[w * LANE:(w + 1) * LANE, :], preferred_element_type=F32)
            out = part if out is None else out + part
        return out

    o_win = _attend_heads(q_rot, lambda h, r0, n: kw_ref[pl.ds(w0 + r0, n), :], pv_win,
                          w_rows, win_mask, s_ref, p_ref)

    g_lane0 = _MISC_LANES["c_g"][0]
    per_group = 3 * C_GROUP_SIZE
    logits_t = g_ref[...].astype(F32).T
    mine = [logits_t[g_lane0 + k * per_group:g_lane0 + (k + 1) * per_group, :] for k in range(C_KV_GROUPS)]
    group = pl.program_id(1)
    logits = mine[0]
    for k in range(1, C_KV_GROUPS):
        logits = jnp.where(group == k, mine[k], logits)
    gates = jax.nn.sigmoid(logits)
    for r, c in enumerate(cols):
        o_t = (gates[3 * r:3 * r + 1, :] * o_cmp[r] + gates[3 * r + 1:3 * r + 2, :] * acc_ref[r]
               + gates[3 * r + 2:3 * r + 3, :] * o_win[r])
        o_ref[:, c] = o_t.T.astype(o_ref.dtype)


def _nsa_call(p3, kc, vct, cos, sin, isect, expand, tq, step):
    b, s, _ = p3.shape
    nc = kc.shape[2]
    n_slc = s // SEL_BLOCK
    width = C_GROUP_SIZE * HEAD_DIM
    kv = lambda name: pl.BlockSpec((None, s, HEAD_DIM), lambda bi, g, i, name=name: (bi, 0, _OFF[name] + g))
    return pl.pallas_call(
        functools.partial(_nsa_kernel, n_sel=min(SEL_COUNT, n_slc), step=step),
        grid=(b, C_KV_GROUPS, s // tq),
        in_specs=[
            pl.BlockSpec((None, tq, width), lambda bi, g, i: (bi, i, _OFF["c_q"] // 4 + g)),
            pl.BlockSpec((None, None, nc, HEAD_DIM), lambda bi, g, i: (bi, g, 0, 0)),
            pl.BlockSpec((None, None, HEAD_DIM, nc), lambda bi, g, i: (bi, g, 0, 0)),
            kv("c_ks"), kv("c_vs"), kv("c_kw"), kv("c_vw"),
            pl.BlockSpec((None, tq, LANE), lambda bi, g, i: (bi, i, _OFF["misc"])),
            pl.BlockSpec((tq, LANE), lambda bi, g, i: (i, 0)),
            pl.BlockSpec((tq, LANE), lambda bi, g, i: (i, 0)),
            pl.BlockSpec(isect.shape, lambda bi, g, i: (0, 0)),
            pl.BlockSpec(expand.shape, lambda bi, g, i: (0, 0)),
        ],
        out_specs=pl.BlockSpec((None, tq, width), lambda bi, g, i: (bi, i, g)),
        out_shape=jax.ShapeDtypeStruct((b, s, C_HEADS * HEAD_DIM), BF16),
        scratch_shapes=[
            pltpu.VMEM((HEAD_DIM + ONES_ROWS, s), BF16),
            pltpu.VMEM((s // LANE, HEAD_DIM + ONES_ROWS, LANE), BF16),
            pltpu.VMEM((C_GROUP_SIZE, HEAD_DIM, tq), F32),
            pltpu.VMEM((C_GROUP_SIZE, s, tq), F32),
            pltpu.VMEM((C_GROUP_SIZE, s, tq), BF16),
            pltpu.VMEM((s, HEAD_DIM + expand.shape[1]), BF16),
        ],
        compiler_params=_params(("parallel", "parallel", "arbitrary"), "nsa"),
        name="nsa",
    )(p3, kc, vct, p3, p3, p3, p3, p3, cos, sin, isect, expand)


def _out_kernel(oa_ref, ob_ref, oc_ref, w_ref, x_ref, gpost_ref, gnext_ref, xo_ref, h_ref):
    na, nb = oa_ref.shape[1], ob_ref.shape[1]
    tm = x_ref.shape[0]
    sub = min(OUT_SUB_ROWS, tm)
    for r0 in range(0, tm, sub):
        r = slice(r0, r0 + sub)
        y = jnp.dot(oa_ref[r, :], w_ref[0:na, :], preferred_element_type=F32)
        y = y + jnp.dot(ob_ref[r, :], w_ref[na:na + nb, :], preferred_element_type=F32)
        y = y + jnp.dot(oc_ref[r, :], w_ref[na + nb:, :], preferred_element_type=F32)
        x = x_ref[r, :] + _rms(y, gpost_ref[...])
        xo_ref[r, :] = x
        h_ref[r, :] = _rms(x, gnext_ref[...]).astype(h_ref.dtype)


def _out_call(o_a, o_b, o_c, w_out_p, layer, x, g_post, g_next, tm):
    m, d = x.shape
    row = lambda width: pl.BlockSpec((tm, width), lambda i: (i, 0))
    vec = pl.BlockSpec((1, d), lambda i: (0, 0))
    return pl.pallas_call(
        _out_kernel,
        grid=(m // tm,),
        in_specs=[
            row(o_a.shape[1]), row(o_b.shape[1]), row(o_c.shape[1]),
            _resident((None, w_out_p.shape[1], d), lambda i: (layer, 0, 0)),
            row(d), vec, vec,
        ],
        out_specs=[row(d), row(d)],
        out_shape=[jax.ShapeDtypeStruct((m, d), F32), jax.ShapeDtypeStruct((m, d), BF16)],
        compiler_params=_params(("parallel",), "out_proj"),
        name="out_proj",
    )(o_a, o_b, o_c, w_out_p, x, g_post, g_next)


def _ffn_kernel(h_ref, wg_ref, wu_ref, wd_ref, y_ref):
    @pl.when(pl.program_id(1) == 0)
    def _():
        y_ref[...] = jnp.zeros(y_ref.shape, F32)

    h = h_ref[...]
    g = jnp.dot(h, wg_ref[...], preferred_element_type=F32)
    u = jnp.dot(h, wu_ref[...], preferred_element_type=F32)
    a = (g * jax.nn.sigmoid(g) * u).astype(BF16)
    y_ref[...] += jnp.dot(a, wd_ref[...], preferred_element_type=F32)


def _ffn_call(h, wg, wu, wd, layer, tm, tf):
    m, d = h.shape
    d_ff = wg.shape[2]
    row = pl.BlockSpec((tm, d), lambda i, f: (i, 0))
    return pl.pallas_call(
        _ffn_kernel,
        grid=(m // tm, d_ff // tf),
        in_specs=[
            row,
            pl.BlockSpec((None, d, tf), lambda i, f: (layer, 0, f)),
            pl.BlockSpec((None, d, tf), lambda i, f: (layer, 0, f)),
            pl.BlockSpec((None, tf, d), lambda i, f: (layer, f, 0)),
        ],
        out_specs=row,
        out_shape=jax.ShapeDtypeStruct((m, d), F32),
        compiler_params=_params(("parallel", "arbitrary"), "ffn"),
        name="ffn",
    )(h, wg, wu, wd)


def _residual_kernel(y_ref, x_ref, g_ref, o_ref):
    o_ref[...] = x_ref[...] + _rms(y_ref[...], g_ref[...])


def _residual_call(y, x, g, tm):
    m, d = x.shape
    row = pl.BlockSpec((tm, d), lambda i: (i, 0))
    return pl.pallas_call(
        _residual_kernel,
        grid=(m // tm,),
        in_specs=[row, row, pl.BlockSpec((1, d), lambda i: (0, 0))],
        out_specs=row,
        out_shape=jax.ShapeDtypeStruct((m, d), F32),
        compiler_params=_params(("parallel",), "ffn_residual"),
        name="ffn_residual",
    )(y, x, g)


def _w_in_layout():
    widths = (A_HEADS * HEAD_DIM, HEAD_DIM, HEAD_DIM, IDX_HEADS * IDX_DIM, IDX_DIM, IDX_HEADS,
              B_HEADS * HEAD_DIM, B_HEADS * HEAD_DIM, B_HEADS * HEAD_DIM, C_HEADS * HEAD_DIM,
              C_KV_GROUPS * HEAD_DIM, C_KV_GROUPS * HEAD_DIM, C_KV_GROUPS * HEAD_DIM, C_KV_GROUPS * HEAD_DIM,
              C_KV_GROUPS * HEAD_DIM, C_KV_GROUPS * HEAD_DIM, 3 * C_HEADS)
    names = ("a_q", "a_k", "a_v", "i_q", "i_k", "i_w", "b_q", "b_k", "b_v", "c_q",
             "c_kc", "c_vc", "c_ks", "c_vs", "c_kw", "c_vw", "c_g")
    offs = np.concatenate([[0], np.cumsum(widths)])
    src = {n: (int(offs[k]), int(widths[k])) for k, n in enumerate(names)}
    pieces = []
    dst = 0
    for name, blocks in [(n, w) for n, w, _, _ in _SEGMENTS] + list(_CMP_SEGMENTS):
        rows = blocks * LANE
        if name == "misc":
            parts, used = [], 0
            for part, (lane0, width) in _MISC_LANES.items():
                assert lane0 == used and src[part][1] == width
                parts.append(src[part])
                used += width
            pieces.append((dst, parts, rows - used))
        else:
            start, width = src[name]
            assert width == rows
            for r0 in range(0, rows, _PROJ_CHUNK * LANE):
                pieces.append((dst + r0, [(start + r0, min(_PROJ_CHUNK * LANE, rows - r0))], 0))
        dst += rows
    return pieces, int(offs[-1])


def _w_prep_kernel(w_ref, o_ref, *, pieces):
    cols = w_ref.shape[1]
    for dst, parts, n_zero in pieces:
        vals = [w_ref[r:r + n, :] for r, n in parts]
        if n_zero:
            vals.append(jnp.zeros((n_zero, cols), F32))
        blk = vals[0] if len(vals) == 1 else jnp.concatenate(vals, axis=0)
        o_ref[dst:dst + blk.shape[0], :] = blk.astype(o_ref.dtype)


def _prep_w_in(w_in, tk=256):
    w_t = jnp.swapaxes(w_in, 1, 2)
    depth, width, d = w_t.shape
    pieces, src_width = _w_in_layout()
    assert width == src_width
    return pl.pallas_call(
        functools.partial(_w_prep_kernel, pieces=pieces),
        grid=(depth, d // tk),
        in_specs=[pl.BlockSpec((None, width, tk), lambda l, i: (l, 0, i))],
        out_specs=pl.BlockSpec((None, NW, tk), lambda l, i: (l, 0, i)),
        out_shape=jax.ShapeDtypeStruct((depth, NW, d), BF16),
        compiler_params=_params(("parallel", "parallel"), "w_in_prep"),
        name="w_in_prep",
    )(w_t)


def _rope_tables(seq):
    def tables(dim):
        inv = 1.0 / (ROPE_THETA ** (jnp.arange(0, dim, 2, dtype=F32) / dim))
        ang = jnp.arange(seq, dtype=F32)[:, None] * inv[None, :]
        cos, sin = jnp.cos(ang), jnp.sin(ang)
        reps = LANE // dim
        return (jnp.tile(jnp.concatenate([cos, cos], axis=-1), (1, reps)),
                jnp.tile(jnp.concatenate([-sin, sin], axis=-1), (1, reps)))
    return tables(HEAD_DIM) + tables(IDX_DIM)


def _selection_constants(seq):
    nc = seq // CMP_STRIDE
    n_slc = seq // SEL_BLOCK
    c_start = np.arange(nc) * CMP_STRIDE
    n_start = np.arange(n_slc) * SEL_BLOCK
    isect = ((c_start[None, :] < n_start[:, None] + SEL_BLOCK)
             & (c_start[None, :] + CMP_BLOCK > n_start[:, None])
             & (np.arange(nc)[None, :] < nc - CMP_BLOCK // CMP_STRIDE + 1)).astype(np.float32)
    expand = np.zeros((seq, max(n_slc, LANE)), np.float32)
    expand[np.arange(seq), np.arange(seq) // SEL_BLOCK] = 1.0
    return jnp.asarray(isect, BF16), jnp.asarray(expand, BF16)


def _tiles(seq):
    return dict(tq=min(2 * LANE, seq), step_dsa=min(4 * LANE, seq), step=min(2 * LANE, seq),
                tm_proj=min(256, seq), tm_out=min(512, seq), tm_ffn=min(1024, seq), tf=512)


def kernel(x, w_in, w_out, cmp_pe_k, cmp_w1_k, cmp_w2_k, cmp_pe_v, cmp_w1_v, cmp_w2_v,
           w_gate, w_up, w_down, g_pre_mix, g_post_mix, g_pre_ffn, g_post_ffn):
    b, s, d = x.shape
    depth = w_in.shape[0]
    m = b * s
    t = _tiles(s)
    tq, step = t["tq"], t["step"]

    w_in_p = _prep_w_in(w_in)
    w_out_p = w_out.astype(BF16)
    wg, wu, wd = w_gate.astype(BF16), w_up.astype(BF16), w_down.astype(BF16)
    w1_k, w2_k = cmp_w1_k.astype(BF16), cmp_w2_k.astype(BF16)
    w1_v, w2_v = cmp_w1_v.astype(BF16), cmp_w2_v.astype(BF16)
    col_scale = jnp.asarray(_COL_SCALE)
    tabs = _rope_tables(s)
    isect, expand = _selection_constants(s)
    vec = lambda g, layer: g[layer][None, :]

    xf = x.reshape(m, d)
    y = None
    for layer in range(depth):
        if layer == 0:
            p, pc = _proj_call((xf, vec(g_pre_mix, 0)), w_in_p, layer, col_scale, tabs, s, t["tm_proj"])
        else:
            rows_in = (y, xf, vec(g_post_ffn, layer - 1), vec(g_pre_mix, layer))
            xf, p, pc = _proj_call(rows_in, w_in_p, layer, col_scale, tabs, s, t["tm_proj"])
        p3 = p.reshape(b, s, NP)
        o_a = _dsa_call(p3, tq, t["step_dsa"])
        o_b = _dil_call(p3, tq, step)
        kc, vct = _cmp_call(pc, b, layer, cmp_pe_k, w1_k, w2_k, cmp_pe_v, w1_v, w2_v)
        o_c = _nsa_call(p3, kc, vct, tabs[0], tabs[1], isect, expand, tq, step)
        xf, h = _out_call(o_a.reshape(m, -1), o_b.reshape(m, -1), o_c.reshape(m, -1), w_out_p, layer,
                          xf, vec(g_post_mix, layer), vec(g_pre_ffn, layer), t["tm_out"])
        y = _ffn_call(h, wg, wu, wd, layer, t["tm_ffn"], t["tf"])
    return _residual_call(y, xf, vec(g_post_ffn, depth - 1), t["tm_out"]).reshape(b, s, d)
```

```python
import functools

import numpy as np
import jax
import jax.numpy as jnp
from jax import lax
from jax.experimental import pallas as pl
from jax.experimental.pallas import tpu as pltpu

F32 = jnp.float32
BF16 = jnp.bfloat16

LANE = 128
SUBLANE = 8
HEAD_DIM = 128
A_HEADS = 4
B_HEADS = 4
C_HEADS = 8
C_KV_GROUPS = 2
C_GROUP_SIZE = C_HEADS // C_KV_GROUPS
IDX_HEADS = 16
IDX_DIM = 64
DSA_TOPK_MAX = 256
DILATED_PATTERNS = ((128, 1), (512, 4), (2048, 16))
CMP_BLOCK = 32
CMP_STRIDE = 16
CMP_HIDDEN = 256
SEL_BLOCK = 64
SEL_COUNT = 16
WIN_SIZE = 512
ROPE_THETA = 10000.0
RMS_EPS = 1e-6
ATTN_SCALE = HEAD_DIM ** -0.5 * float(np.log2(np.e))
IDX_SCALE = IDX_DIM ** -0.5 * IDX_HEADS ** -0.5

NEG = -1e30
INT_MIN = -(2 ** 31)
HALF16 = 2 ** 15
COUNT_ROWS = 8 * SUBLANE
SCORE_ROWS = 256
ATT_ROWS = 128
OUT_SUB_ROWS = 256

MODE_NONE, MODE_ROPE, MODE_IROPE, MODE_MISC = 0, 1, 2, 3

_SEGMENTS = (
    ("i_q", 8, MODE_IROPE, IDX_SCALE),
    ("c_q", 8, MODE_NONE, ATTN_SCALE),
    ("a_q", 4, MODE_ROPE, ATTN_SCALE),
    ("b_q", 4, MODE_ROPE, ATTN_SCALE),
    ("b_k", 4, MODE_ROPE, 1.0),
    ("b_v", 4, MODE_NONE, 1.0),
    ("c_ks", 2, MODE_ROPE, 1.0),
    ("c_vs", 2, MODE_NONE, 1.0),
    ("c_kw", 2, MODE_ROPE, 1.0),
    ("c_vw", 2, MODE_NONE, 1.0),
    ("a_k", 1, MODE_ROPE, 1.0),
    ("a_v", 1, MODE_NONE, 1.0),
    ("misc", 1, MODE_MISC, 1.0),
)
_MISC_LANES = {"i_k": (0, IDX_DIM), "i_w": (IDX_DIM, IDX_HEADS), "c_g": (IDX_DIM + IDX_HEADS, 3 * C_HEADS)}
_CMP_SEGMENTS = (("c_kc", 2), ("c_vc", 2))
_OFF = {}
_o = 0
for _n, _w, _m, _s in _SEGMENTS:
    _OFF[_n] = _o
    _o += _w
NP_BLOCKS = _o
NP = NP_BLOCKS * LANE
CMP_BLOCKS = sum(w for _, w in _CMP_SEGMENTS)
CMP_COLS = CMP_BLOCKS * LANE
NW = NP + CMP_COLS
_BLOCK_MODE = tuple(m for _, w, m, _ in _SEGMENTS for _ in range(w))
_COL_SCALE = np.repeat(np.array([s for _, w, _, s in _SEGMENTS for _ in range(w)], np.float32), LANE)[None, :]
_PROJ_CHUNK = 4


_VMEM_MB = dict(in_proj=54, dsa=48, dilated=48, nsa_compress=32, nsa=48, out_proj=48, ffn=52,
                ffn_residual=32, w_in_prep=40)


def _params(sem, name):
    return pltpu.CompilerParams(dimension_semantics=sem, vmem_limit_bytes=_VMEM_MB[name] * 1024 * 1024)


def _resident(block_shape, index_map):
    return pl.BlockSpec(block_shape, index_map, pipeline_mode=pl.Buffered(1))


def _rms(x, g):
    return x * lax.rsqrt(jnp.mean(x * x, axis=-1, keepdims=True) + RMS_EPS) * g


def _rope_full(a, cos, sin):
    return a * cos + pltpu.roll(a, HEAD_DIM // 2, 1) * sin


def _rope_idx(a, cos, sin, first_half):
    partner = jnp.where(first_half, pltpu.roll(a, LANE - IDX_DIM // 2, 1), pltpu.roll(a, IDX_DIM // 2, 1))
    return a * cos + partner * sin


def _proj_kernel(*refs, after_ffn):
    if after_ffn:
        (y_ref, x_ref, gpost_ref, gpre_ref, w_ref, cs_ref, cos_ref, sin_ref, icos_ref, isin_ref,
         xo_ref, o_ref, oc_ref, cmp_ref) = refs
        x = x_ref[...] + _rms(y_ref[...], gpost_ref[...])
        xo_ref[...] = x
        h = _rms(x, gpre_ref[...]).astype(BF16)
    else:
        x_ref, gpre_ref, w_ref, cs_ref, cos_ref, sin_ref, icos_ref, isin_ref, o_ref, oc_ref, cmp_ref = refs
        h = _rms(x_ref[...], gpre_ref[...]).astype(BF16)
    tm = h.shape[0]
    lane = lax.broadcasted_iota(jnp.int32, (tm, LANE), 1)
    first_half = (lane & (IDX_DIM - 1)) < IDX_DIM // 2
    for c0 in range(0, NP_BLOCKS, _PROJ_CHUNK):
        nb = min(_PROJ_CHUNK, NP_BLOCKS - c0)
        acc = lax.dot_general(h, w_ref[c0 * LANE:(c0 + nb) * LANE, :], (((1,), (1,)), ((), ())),
                              preferred_element_type=F32)
        for b in range(nb):
            col = slice((c0 + b) * LANE, (c0 + b + 1) * LANE)
            a = acc[:, b * LANE:(b + 1) * LANE] * cs_ref[:, col]
            mode = _BLOCK_MODE[c0 + b]
            if mode == MODE_ROPE:
                a = _rope_full(a, cos_ref[...], sin_ref[...])
            elif mode == MODE_IROPE:
                a = _rope_idx(a, icos_ref[...], isin_ref[...], first_half)
            elif mode == MODE_MISC:
                a = jnp.where(lane < IDX_DIM, _rope_idx(a, icos_ref[...], isin_ref[...], first_half), a)
            o_ref[:, col] = a.astype(o_ref.dtype)
    acc = lax.dot_general(h, w_ref[NP:NW, :], (((1,), (1,)), ((), ())), preferred_element_type=F32)
    for b in range(CMP_BLOCKS):
        cmp_ref[b] = acc[:, b * LANE:(b + 1) * LANE]
    for j in range(CMP_STRIDE):
        for b in range(CMP_BLOCKS):
            rows = cmp_ref[b, pl.ds(j, tm // CMP_STRIDE, stride=CMP_STRIDE), :]
            oc_ref[:, j * CMP_COLS + b * LANE:j * CMP_COLS + (b + 1) * LANE] = rows.astype(oc_ref.dtype)


def _proj_call(rows_in, w_in_p, layer, col_scale, tabs, seq, tm):
    after_ffn = len(rows_in) > 2
    m, d = rows_in[0].shape
    tpb = seq // tm
    tab_spec = pl.BlockSpec((tm, LANE), lambda i: (i % tpb, 0))
    row = pl.BlockSpec((tm, d), lambda i: (i, 0))
    vec = pl.BlockSpec((1, d), lambda i: (0, 0))
    out_specs = [
        pl.BlockSpec((tm, NP), lambda i: (i, 0)),
        pl.BlockSpec((tm // CMP_STRIDE, CMP_STRIDE * CMP_COLS), lambda i: (i, 0)),
    ]
    out_shape = [
        jax.ShapeDtypeStruct((m, NP), BF16),
        jax.ShapeDtypeStruct((m // CMP_STRIDE, CMP_STRIDE * CMP_COLS), BF16),
    ]
    if after_ffn:
        out_specs = [row] + out_specs
        out_shape = [jax.ShapeDtypeStruct((m, d), F32)] + out_shape
    return pl.pallas_call(
        functools.partial(_proj_kernel, after_ffn=after_ffn),
        grid=(m // tm,),
        in_specs=([row, row, vec, vec] if after_ffn else [row, vec]) + [
            _resident((None, NW, d), lambda i: (layer, 0, 0)),
            _resident((1, NP), lambda i: (0, 0)),
            tab_spec, tab_spec, tab_spec, tab_spec,
        ],
        out_specs=out_specs,
        out_shape=out_shape,
        scratch_shapes=[pltpu.VMEM((CMP_BLOCKS, tm, LANE), F32)],
        compiler_params=_params(("parallel",), "in_proj"),
        name="in_proj",
    )(*rows_in, w_in_p, col_scale, *tabs)


def _transpose_v(dst_ref, src_ref, col0=0):
    rows = src_ref.shape[0]
    for r0 in range(0, rows, LANE):
        tile = src_ref[r0:r0 + LANE, col0:col0 + HEAD_DIM].astype(F32)
        dst_ref[:, r0:r0 + LANE] = tile.T.astype(dst_ref.dtype)


def _transpose_v_tiles(dst_ref, src_ref):
    rows, _ = src_ref.shape
    for r in range(rows // LANE):
        tile = src_ref[r * LANE:(r + 1) * LANE, :].astype(F32)
        dst_ref[r] = tile.T.astype(dst_ref.dtype)


def _for_causal_prefix(i, tq, s_len, step, body):
    n = ((i + 1) * tq + step - 1) // step
    for v in range(1, s_len // step + 1):
        pl.when(n == v)(functools.partial(body, v * step))


def _fold(x, op):
    out = x[0:COUNT_ROWS]
    for r0 in range(COUNT_ROWS, x.shape[0], COUNT_ROWS):
        out = op(out, x[r0:r0 + COUNT_ROWS])
    return out


def _fold_tree(x, op):
    while x.shape[0] > SUBLANE:
        half = x.shape[0] // 2
        x = op(x[:half], x[half:])
    return x


def _row_chunk(rows):
    return SCORE_ROWS if rows % SCORE_ROWS == 0 else LANE


def _attend_heads(qs, k_of, pv_of, rows, mask_of, s_ref, p_ref, w_ref=None, plain_rows=0):
    n_heads = len(qs)
    tq = qs[0].shape[0]
    chunk = ATT_ROWS * LANE // tq
    for h in range(n_heads):
        s_ref[h, 0:rows, :] = lax.dot_general(k_of(h, 0, rows), qs[h], (((1,), (1,)), ((), ())),
                                              preferred_element_type=F32)
    m_acc = [jnp.full((SUBLANE, tq), NEG, F32) for _ in qs]
    weighted = False
    for r0 in range(0, rows, chunk):
        if r0 + chunk <= plain_rows:
            for h in range(n_heads):
                m_acc[h] = jnp.maximum(m_acc[h], _fold_tree(s_ref[h, r0:r0 + chunk, :], jnp.maximum))
            continue
        mask, weight = mask_of(r0, chunk)
        if weight is not None:
            weighted = True
            w_ref[r0:r0 + chunk, :] = weight
        for h in range(n_heads):
            s = jnp.where(mask, s_ref[h, r0:r0 + chunk, :], NEG)
            s_ref[h, r0:r0 + chunk, :] = s
            m_acc[h] = jnp.maximum(m_acc[h], _fold_tree(s, jnp.maximum))
    outs = []
    for h in range(n_heads):
        mx = jnp.max(m_acc[h], axis=0, keepdims=True)
        l_acc = jnp.zeros((SUBLANE, tq), F32)
        for r0 in range(0, rows, chunk):
            p = jnp.exp2(s_ref[h, r0:r0 + chunk, :] - mx)
            if weighted:
                p = p * w_ref[r0:r0 + chunk, :]
            l_acc = l_acc + _fold_tree(p, jnp.add)
            p_ref[h, r0:r0 + chunk, :] = p.astype(p_ref.dtype)
        den = jnp.sum(l_acc, axis=0, keepdims=True)
        outs.append(pv_of(h, p_ref[h, 0:rows, :]) / den)
    return outs


def _count_rows(key_ref, rows, pred):
    chunk = _row_chunk(rows)
    acc = None
    for r0 in range(0, rows, chunk):
        part = _fold(pred(key_ref[r0:r0 + chunk, :], r0), jnp.add)
        acc = part if acc is None else acc + part
    return jnp.sum(acc.astype(jnp.int32), axis=0, keepdims=True)


def _dsa_prefix(rows, i, iq_ref, iw_t, ik_ref, q_ref, k_ref, vt_ref, o_ref, key_ref, cut_ref, s_ref, p_ref,
                hi_ref, lo_ref, topk):
    tq = q_ref.shape[0]
    chunk = min(SCORE_ROWS * LANE // tq, rows)
    crow = lax.broadcasted_iota(jnp.int32, (chunk, tq), 0)
    ctcol = i * tq + lax.broadcasted_iota(jnp.int32, (chunk, tq), 1)

    for r0 in range(0, rows, chunk):
        halves = (ik_ref[0, r0:r0 + chunk, :], ik_ref[1, r0:r0 + chunk, :])
        score = jnp.zeros((chunk, tq), F32)
        for p in range(IDX_HEADS // 2):
            blk = iq_ref[:, p * LANE:(p + 1) * LANE]
            for half in range(2):
                j = 2 * p + half
                lg = lax.dot_general(halves[half], blk, (((1,), (1,)), ((), ())), preferred_element_type=F32)
                score = score + jnp.maximum(lg, 0.0) * iw_t[j:j + 1, :]
        bits = pltpu.bitcast(score, jnp.int32)
        key = bits ^ ((bits >> 31) & 0x7FFFFFFF)
        key = jnp.where(r0 + crow <= ctcol, key, INT_MIN)
        key_ref[r0:r0 + chunk, :] = key
        hi_ref[r0:r0 + chunk, :] = (key >> 16).astype(jnp.int16)
        lo_ref[r0:r0 + chunk, :] = ((key & 0xFFFF) - HALF16).astype(jnp.int16)

    def search16(ref, target):
        def bit_step(b, off):
            cand_off = off | jnp.left_shift(jnp.int32(1), 15 - b)
            cand = (cand_off - HALF16).astype(jnp.int16)
            cnt = _count_rows(ref, rows, lambda c, r0: jnp.where(c >= cand, jnp.int16(1), jnp.int16(0)))
            return jnp.where(cnt >= target, cand_off, off)

        return lax.fori_loop(0, 16, bit_step, jnp.zeros((1, tq), jnp.int32)) - HALF16

    thr_hi = search16(hi_ref, topk)
    thr_hi16 = thr_hi.astype(jnp.int16)
    above = _count_rows(hi_ref, rows, lambda c, r0: jnp.where(c > thr_hi16, jnp.int16(1), jnp.int16(0)))
    chunk16 = _row_chunk(rows)
    for r0 in range(0, rows, chunk16):
        same = hi_ref[r0:r0 + chunk16, :] == thr_hi16
        hi_ref[r0:r0 + chunk16, :] = jnp.where(same, lo_ref[r0:r0 + chunk16, :], jnp.int16(-HALF16))
    thr_lo = search16(hi_ref, topk - above)
    thr = jnp.left_shift(thr_hi, 16) | (thr_lo + HALF16)
    short = thr == INT_MIN
    need = topk - _count_rows(key_ref, rows, lambda kc, r0: jnp.where(kc > thr, 1, 0))
    n_eq = _count_rows(key_ref, rows, lambda kc, r0: jnp.where(kc == thr, 1, 0))
    excess = jnp.where(n_eq > need, jnp.where(short, 0, 1), 0)
    cut_ref[...] = jnp.where(short, -1, rows)

    def row_ids(r0, n):
        return r0 + lax.broadcasted_iota(jnp.int32, (n, tq), 0)

    @pl.when(jnp.max(excess) > 0)
    def _():
        n_bits = max(1, (rows - 1).bit_length())

        def idx_step(b, cut):
            cand = cut | jnp.left_shift(jnp.int32(1), n_bits - 1 - b)
            below = _count_rows(key_ref, rows, lambda kc, r0: jnp.where(
                kc == thr, jnp.where(row_ids(r0, kc.shape[0]) < cand, 1, 0), 0))
            return jnp.where(below < need, cand, cut)

        cut = lax.fori_loop(0, n_bits, idx_step, jnp.zeros((1, tq), jnp.int32))
        cut_ref[...] = jnp.where(excess > 0, cut, cut_ref[...])

    cut = cut_ref[...]

    def mask_of(r0, n):
        kc = key_ref[r0:r0 + n, :]
        chosen = jnp.where(kc > thr, 1, jnp.where(kc == thr, jnp.where(row_ids(r0, n) <= cut, 1, 0), 0))
        return chosen > 0, None

    vt = vt_ref[:, 0:rows]
    cols = [slice(h * HEAD_DIM, (h + 1) * HEAD_DIM) for h in range(A_HEADS)]
    outs = _attend_heads([q_ref[:, c] for c in cols], lambda h, r0, n: k_ref[r0:r0 + n, :],
                         lambda h, p: jnp.dot(vt, p, preferred_element_type=F32),
                         rows, mask_of, s_ref, p_ref)
    for h, c in enumerate(cols):
        o_ref[:, c] = outs[h].T.astype(o_ref.dtype)


def _dsa_kernel(iq_ref, mq_ref, mk_ref, q_ref, k_ref, v_ref, o_ref, vt_ref, key_ref, cut_ref, s_ref, p_ref,
                hi_ref, lo_ref, ikh_ref, *, topk, step):
    i = pl.program_id(1)
    tq = q_ref.shape[0]
    s_len = k_ref.shape[0]

    w_lane0 = _MISC_LANES["i_w"][0]
    assert _MISC_LANES["i_k"] == (0, IDX_DIM)

    @pl.when(i == 0)
    def _():
        _transpose_v(vt_ref, v_ref)
        lane = lax.broadcasted_iota(jnp.int32, (LANE, LANE), 1)
        for r0 in range(0, s_len, LANE):
            blk = mk_ref[r0:r0 + LANE, :].astype(F32)
            ikh_ref[0, r0:r0 + LANE, :] = jnp.where(lane < IDX_DIM, blk, 0.0).astype(ikh_ref.dtype)
            ikh_ref[1, r0:r0 + LANE, :] = jnp.where(lane >= IDX_DIM, pltpu.roll(blk, IDX_DIM, 1),
                                                    0.0).astype(ikh_ref.dtype)

    iw_t = mq_ref[...].astype(F32).T[w_lane0:w_lane0 + IDX_HEADS, :]
    _for_causal_prefix(i, tq, s_len, step, functools.partial(
        _dsa_prefix, i=i, iq_ref=iq_ref, iw_t=iw_t, ik_ref=ikh_ref, q_ref=q_ref, k_ref=k_ref, vt_ref=vt_ref,
        o_ref=o_ref, key_ref=key_ref, cut_ref=cut_ref, s_ref=s_ref, p_ref=p_ref,
        hi_ref=hi_ref, lo_ref=lo_ref, topk=topk))


def _dsa_call(p3, tq, step):
    b, s, _ = p3.shape
    topk = min(DSA_TOPK_MAX, s // 4)
    blk = lambda name, width: _OFF[name] // width
    return pl.pallas_call(
        functools.partial(_dsa_kernel, topk=topk, step=step),
        grid=(b, s // tq),
        in_specs=[
            pl.BlockSpec((None, tq, 8 * LANE), lambda bi, i: (bi, i, blk("i_q", 8))),
            pl.BlockSpec((None, tq, LANE), lambda bi, i: (bi, i, blk("misc", 1))),
            pl.BlockSpec((None, s, LANE), lambda bi, i: (bi, 0, blk("misc", 1))),
            pl.BlockSpec((None, tq, 4 * LANE), lambda bi, i: (bi, i, blk("a_q", 4))),
            pl.BlockSpec((None, s, LANE), lambda bi, i: (bi, 0, blk("a_k", 1))),
            pl.BlockSpec((None, s, LANE), lambda bi, i: (bi, 0, blk("a_v", 1))),
        ],
        out_specs=pl.BlockSpec((None, tq, A_HEADS * HEAD_DIM), lambda bi, i: (bi, i, 0)),
        out_shape=jax.ShapeDtypeStruct((b, s, A_HEADS * HEAD_DIM), BF16),
        scratch_shapes=[
            pltpu.VMEM((HEAD_DIM, s), BF16),
            pltpu.VMEM((s, tq), jnp.int32),
            pltpu.VMEM((1, tq), jnp.int32),
            pltpu.VMEM((A_HEADS, s, tq), F32),
            pltpu.VMEM((A_HEADS, s, tq), BF16),
            pltpu.VMEM((s, tq), jnp.int16),
            pltpu.VMEM((s, tq), jnp.int16),
            pltpu.VMEM((2, s, LANE), BF16),
        ],
        compiler_params=_params(("parallel", "arbitrary"), "dsa"),
        name="dsa",
    )(p3, p3, p3, p3, p3, p3)


def _dil_prefix(rows, q_ref, k_ref, vt_ref, o_ref, s_ref, p_ref, w_ref):
    tq = q_ref.shape[0]
    first_q = rows - tq
    period = max(d for _, d in DILATED_PATTERNS)
    shared = {}

    def mask_of(r0, n):
        base = first_q - r0
        d_min, d_max = base - (n - 1), base + (tq - 1)
        live = tuple((w, d) for w, d in DILATED_PATTERNS if d_min <= w)
        on_edge = d_min < 0 or any(d_max > w for w, _ in live)
        key = None if on_edge else (live, base % period)
        if key in shared:
            return shared[key]
        delta = (base + lax.broadcasted_iota(jnp.int32, (n, tq), 1)
                 - lax.broadcasted_iota(jnp.int32, (n, tq), 0))
        mult = jnp.zeros((n, tq), jnp.int32)
        for window, dilation in live:
            hit = jnp.where(delta <= window, 1, 0) if d_max > window else 1
            mult = mult + jnp.where((delta & (dilation - 1)) == 0, hit, 0)
        if d_min < 0:
            mult = jnp.where(delta >= 0, mult, 0)
        out = (mult > 0, mult.astype(F32))
        if key is not None:
            shared[key] = out
        return out

    cols = [slice(h * HEAD_DIM, (h + 1) * HEAD_DIM) for h in range(B_HEADS)]
    outs = _attend_heads([q_ref[:, c] for c in cols], lambda h, r0, n: k_ref[r0:r0 + n, cols[h]],
                         lambda h, p: jnp.dot(vt_ref[h, :, 0:rows], p, preferred_element_type=F32),
                         rows, mask_of, s_ref, p_ref, w_ref)
    for h, c in enumerate(cols):
        o_ref[:, c] = outs[h].T.astype(o_ref.dtype)


def _dil_kernel(q_ref, k_ref, v_ref, o_ref, vt_ref, s_ref, p_ref, w_ref, *, step):
    i = pl.program_id(1)

    @pl.when(i == 0)
    def _():
        for h in range(B_HEADS):
            _transpose_v(vt_ref.at[h], v_ref, h * HEAD_DIM)

    assert step == q_ref.shape[0]
    _for_causal_prefix(i, q_ref.shape[0], k_ref.shape[0], step, functools.partial(
        _dil_prefix, q_ref=q_ref, k_ref=k_ref, vt_ref=vt_ref, o_ref=o_ref,
        s_ref=s_ref, p_ref=p_ref, w_ref=w_ref))


def _dil_call(p3, tq, step):
    b, s, _ = p3.shape
    width = B_HEADS * HEAD_DIM
    return pl.pallas_call(
        functools.partial(_dil_kernel, step=step),
        grid=(b, s // tq),
        in_specs=[
            pl.BlockSpec((None, tq, width), lambda bi, i: (bi, i, _OFF["b_q"] // 4)),
            pl.BlockSpec((None, s, width), lambda bi, i: (bi, 0, _OFF["b_k"] // 4)),
            pl.BlockSpec((None, s, width), lambda bi, i: (bi, 0, _OFF["b_v"] // 4)),
        ],
        out_specs=pl.BlockSpec((None, tq, width), lambda bi, i: (bi, i, 0)),
        out_shape=jax.ShapeDtypeStruct((b, s, width), BF16),
        scratch_shapes=[
            pltpu.VMEM((B_HEADS, HEAD_DIM, s), BF16),
            pltpu.VMEM((B_HEADS, s, tq), F32),
            pltpu.VMEM((B_HEADS, s, tq), BF16),
            pltpu.VMEM((s, tq), F32),
        ],
        compiler_params=_params(("parallel", "arbitrary"), "dilated"),
        name="dilated",
    )(p3, p3, p3)


def _gelu_tanh(x):
    return 0.5 * x * (1.0 + jnp.tanh(np.float32(np.sqrt(2.0 / np.pi)) * (x + 0.044715 * (x * x * x))))


def _cmp_kernel(*refs):
    n = CMP_STRIDE
    xk, xv = refs[:n], refs[n:2 * n]
    pe_k, w1_k, w2_k, pe_v, w1_v, w2_v, kc_ref, vct_ref = refs[2 * n:]

    def branch(x_refs, pe_ref, w1_ref, w2_ref):
        lo = hi = None
        for j in range(n):
            xj = x_refs[j][...].astype(F32)
            a = (xj + pe_ref[j:j + 1, :]).astype(BF16)
            b = (xj + pe_ref[n + j:n + j + 1, :]).astype(BF16)
            dl = jnp.dot(a, w1_ref[j * HEAD_DIM:(j + 1) * HEAD_DIM, :], preferred_element_type=F32)
            dh = jnp.dot(b, w1_ref[(n + j) * HEAD_DIM:(n + j + 1) * HEAD_DIM, :], preferred_element_type=F32)
            lo = dl if lo is None else lo + dl
            hi = dh if hi is None else hi + dh
        nc = lo.shape[0]
        hid = lo + pltpu.roll(hi, nc - 1, 0)
        return jnp.dot(_gelu_tanh(hid).astype(BF16), w2_ref[...], preferred_element_type=F32)

    kc_ref[...] = branch(xk, pe_k, w1_k, w2_k).astype(kc_ref.dtype)
    vct_ref[...] = branch(xv, pe_v, w1_v, w2_v).T.astype(vct_ref.dtype)


def _cmp_call(pc, b, layer, pe_k, w1_k, w2_k, pe_v, w1_v, w2_v):
    nc = pc.shape[0] // b
    pc3 = pc.reshape(b, nc, CMP_STRIDE * CMP_COLS)
    x_specs = []
    for first in (0, C_KV_GROUPS):
        for j in range(CMP_STRIDE):
            x_specs.append(pl.BlockSpec((None, nc, HEAD_DIM),
                                        lambda bi, g, j=j, first=first: (bi, 0, j * CMP_BLOCKS + first + g)))
    flat = CMP_BLOCK * HEAD_DIM
    w_specs = [
        _resident((None, CMP_BLOCK, HEAD_DIM), lambda bi, g: (layer, 0, 0)),
        _resident((None, flat, CMP_HIDDEN), lambda bi, g: (layer, 0, 0)),
        _resident((None, CMP_HIDDEN, HEAD_DIM), lambda bi, g: (layer, 0, 0)),
    ]
    return pl.pallas_call(
        _cmp_kernel,
        grid=(b, C_KV_GROUPS),
        in_specs=x_specs + w_specs + w_specs,
        out_specs=[
            pl.BlockSpec((None, None, nc, HEAD_DIM), lambda bi, g: (bi, g, 0, 0)),
            pl.BlockSpec((None, None, HEAD_DIM, nc), lambda bi, g: (bi, g, 0, 0)),
        ],
        out_shape=[
            jax.ShapeDtypeStruct((b, C_KV_GROUPS, nc, HEAD_DIM), BF16),
            jax.ShapeDtypeStruct((b, C_KV_GROUPS, HEAD_DIM, nc), BF16),
        ],
        compiler_params=_params(("parallel", "parallel"), "nsa_compress"),
        name="nsa_compress",
    )(*([pc3] * (2 * CMP_STRIDE)), pe_k, w1_k, w2_k, pe_v, w1_v, w2_v)


def _nsa_selected(rows, t_row, q_sel, ke_ref, vst_ref, acc_ref, s_ref, p_ref, step):
    tq = q_sel[0].shape[0]

    def mask_of(r0, n):
        srow = r0 + lax.broadcasted_iota(jnp.int32, (n, tq), 0)
        return t_row - srow >= 0, None

    vt = vst_ref[:, 0:rows]
    outs = _attend_heads(q_sel, lambda h, r0, n: ke_ref[r0:r0 + n, :],
                         lambda h, p: jnp.dot(vt, p, preferred_element_type=F32),
                         rows, mask_of, s_ref, p_ref, plain_rows=rows - max(step, tq))
    for r in range(C_GROUP_SIZE):
        acc_ref[r] = outs[r]


def _nsa_kernel(q_ref, kc_ref, vct_ref, ks_ref, vs_ref, kw_ref, vw_ref, g_ref, cos_ref, sin_ref,
                isect_ref, expand_ref, o_ref, vst_ref, vwt_ref, acc_ref, s_ref, p_ref, ke_ref, *, n_sel, step):
    i = pl.program_id(2)
    tq = q_ref.shape[0]
    s_len = ks_ref.shape[0]
    nc = kc_ref.shape[0]
    n_slc = isect_ref.shape[0]

    @pl.when(i == 0)
    def _():
        _transpose_v(vst_ref, vs_ref)
        _transpose_v_tiles(vwt_ref, vw_ref)
        ke_ref[:, 0:HEAD_DIM] = ks_ref[...]
        ke_ref[:, HEAD_DIM:] = expand_ref[...]

    t_row = i * tq + lax.broadcasted_iota(jnp.int32, (1, tq), 1)

    kc = kc_ref[...]
    vct = vct_ref[...]
    cend = lax.broadcasted_iota(jnp.int32, (nc, tq), 0) * CMP_STRIDE + (CMP_BLOCK - 1)
    cmask = cend <= t_row
    o_cmp = []
    p_sum = jnp.zeros((nc, tq), F32)
    for r in range(C_GROUP_SIZE):
        q_r = q_ref[:, r * HEAD_DIM:(r + 1) * HEAD_DIM]
        sc = lax.dot_general(kc, q_r, (((1,), (1,)), ((), ())), preferred_element_type=F32)
        sc = jnp.where(cmask, sc, NEG)
        mx = jnp.max(sc, axis=0, keepdims=True)
        e = jnp.where(cmask, jnp.exp2(sc - mx), 0.0)
        den = jnp.sum(e, axis=0, keepdims=True)
        p = e / jnp.where(den > 0, den, 1.0)
        p_sum = p_sum + p
        o_cmp.append(jnp.dot(vct, p.astype(BF16), preferred_element_type=F32))

    isect = isect_ref[...]
    p_hi = p_sum.astype(BF16)
    p_lo = (p_sum - p_hi.astype(F32)).astype(BF16)
    imp = (jnp.dot(isect, p_hi, preferred_element_type=F32)
           + jnp.dot(isect, p_lo, preferred_element_type=F32))
    blk = lax.broadcasted_iota(jnp.int32, (n_slc, tq), 0)
    cur = t_row // SEL_BLOCK
    val = jnp.where(blk == 0, jnp.inf, jnp.where(blk == cur, jnp.inf, jnp.where(blk == cur - 1, jnp.inf, imp)))
    val = jnp.where(blk <= cur, val, -jnp.inf)
    rank = jnp.zeros((n_slc, tq), jnp.int32)
    for m in range(n_slc):
        vm = val[m:m + 1, :]
        before = jnp.where(vm > val, 1, jnp.where(vm == val, jnp.where(blk > m, 1, 0), 0))
        rank = rank + before
    bias = jnp.where(rank < n_sel, 0.0, NEG)
    if n_slc < LANE:
        bias = jnp.concatenate([bias, jnp.zeros((LANE - n_slc, tq), F32)], axis=0)
    bias_t = bias.T.astype(BF16)

    cos, sin = cos_ref[...], sin_ref[...]
    cols = [slice(r * HEAD_DIM, (r + 1) * HEAD_DIM) for r in range(C_GROUP_SIZE)]
    q_rot = [_rope_full(q_ref[:, c].astype(F32), cos, sin).astype(BF16) for c in cols]
    q_sel = [jnp.concatenate([q, bias_t], axis=1) for q in q_rot]

    _for_causal_prefix(i, tq, s_len, step, functools.partial(
        _nsa_selected, t_row=t_row, q_sel=q_sel, ke_ref=ke_ref, vst_ref=vst_ref,
        acc_ref=acc_ref, s_ref=s_ref, p_ref=p_ref, step=step))

    w_tiles = min((WIN_SIZE - 1 + tq - 1) // LANE + 1, s_len // LANE)
    w_rows = w_tiles * LANE
    w0 = pl.multiple_of(jnp.maximum(i * tq + tq - w_rows, 0), LANE)
    t0 = w0 // LANE

    def win_mask(r0, n):
        delta = t_row - (w0 + r0 + lax.broadcasted_iota(jnp.int32, (n, tq), 0))
        return jnp.where(delta >= 0, jnp.where(delta <= WIN_SIZE - 1, 1, 0), 0) > 0, None

    def pv_win(h, p):
        out = None
        for w in range(w_tiles):
            part = jnp.dot(vwt_ref[t0 + w], p[w * LANE:(w + 1) * LANE, :], preferred_element_type=F32)
            out = part if out is None else out + part
        return out

    o_win = _attend_heads(q_rot, lambda h, r0, n: kw_ref[pl.ds(w0 + r0, n), :], pv_win,
                          w_rows, win_mask, s_ref, p_ref)

    g_lane0 = _MISC_LANES["c_g"][0]
    per_group = 3 * C_GROUP_SIZE
    logits_t = g_ref[...].astype(F32).T
    mine = [logits_t[g_lane0 + k * per_group:g_lane0 + (k + 1) * per_group, :] for k in range(C_KV_GROUPS)]
    group = pl.program_id(1)
    logits = mine[0]
    for k in range(1, C_KV_GROUPS):
        logits = jnp.where(group == k, mine[k], logits)
    gates = jax.nn.sigmoid(logits)
    for r, c in enumerate(cols):
        o_t = (gates[3 * r:3 * r + 1, :] * o_cmp[r] + gates[3 * r + 1:3 * r + 2, :] * acc_ref[r]
               + gates[3 * r + 2:3 * r + 3, :] * o_win[r])
        o_ref[:, c] = o_t.T.astype(o_ref.dtype)


def _nsa_call(p3, kc, vct, cos, sin, isect, expand, tq, step):
    b, s, _ = p3.shape
    nc = kc.shape[2]
    n_slc = s // SEL_BLOCK
    width = C_GROUP_SIZE * HEAD_DIM
    kv = lambda name: pl.BlockSpec((None, s, HEAD_DIM), lambda bi, g, i, name=name: (bi, 0, _OFF[name] + g))
    return pl.pallas_call(
        functools.partial(_nsa_kernel, n_sel=min(SEL_COUNT, n_slc), step=step),
        grid=(b, C_KV_GROUPS, s // tq),
        in_specs=[
            pl.BlockSpec((None, tq, width), lambda bi, g, i: (bi, i, _OFF["c_q"] // 4 + g)),
            pl.BlockSpec((None, None, nc, HEAD_DIM), lambda bi, g, i: (bi, g, 0, 0)),
            pl.BlockSpec((None, None, HEAD_DIM, nc), lambda bi, g, i: (bi, g, 0, 0)),
            kv("c_ks"), kv("c_vs"), kv("c_kw"), kv("c_vw"),
            pl.BlockSpec((None, tq, LANE), lambda bi, g, i: (bi, i, _OFF["misc"])),
            pl.BlockSpec((tq, LANE), lambda bi, g, i: (i, 0)),
            pl.BlockSpec((tq, LANE), lambda bi, g, i: (i, 0)),
            pl.BlockSpec(isect.shape, lambda bi, g, i: (0, 0)),
            pl.BlockSpec(expand.shape, lambda bi, g, i: (0, 0)),
        ],
        out_specs=pl.BlockSpec((None, tq, width), lambda bi, g, i: (bi, i, g)),
        out_shape=jax.ShapeDtypeStruct((b, s, C_HEADS * HEAD_DIM), BF16),
        scratch_shapes=[
            pltpu.VMEM((HEAD_DIM, s), BF16),
            pltpu.VMEM((s // LANE, HEAD_DIM, LANE), BF16),
            pltpu.VMEM((C_GROUP_SIZE, HEAD_DIM, tq), F32),
            pltpu.VMEM((C_GROUP_SIZE, s, tq), F32),
            pltpu.VMEM((C_GROUP_SIZE, s, tq), BF16),
            pltpu.VMEM((s, HEAD_DIM + expand.shape[1]), BF16),
        ],
        compiler_params=_params(("parallel", "parallel", "arbitrary"), "nsa"),
        name="nsa",
    )(p3, kc, vct, p3, p3, p3, p3, p3, cos, sin, isect, expand)


def _out_kernel(oa_ref, ob_ref, oc_ref, w_ref, x_ref, gpost_ref, gnext_ref, xo_ref, h_ref):
    na, nb = oa_ref.shape[1], ob_ref.shape[1]
    tm = x_ref.shape[0]
    sub = min(OUT_SUB_ROWS, tm)
    for r0 in range(0, tm, sub):
        r = slice(r0, r0 + sub)
        y = jnp.dot(oa_ref[r, :], w_ref[0:na, :], preferred_element_type=F32)
        y = y + jnp.dot(ob_ref[r, :], w_ref[na:na + nb, :], preferred_element_type=F32)
        y = y + jnp.dot(oc_ref[r, :], w_ref[na + nb:, :], preferred_element_type=F32)
        x = x_ref[r, :] + _rms(y, gpost_ref[...])
        xo_ref[r, :] = x
        h_ref[r, :] = _rms(x, gnext_ref[...]).astype(h_ref.dtype)


def _out_call(o_a, o_b, o_c, w_out_p, layer, x, g_post, g_next, tm):
    m, d = x.shape
    row = lambda width: pl.BlockSpec((tm, width), lambda i: (i, 0))
    vec = pl.BlockSpec((1, d), lambda i: (0, 0))
    return pl.pallas_call(
        _out_kernel,
        grid=(m // tm,),
        in_specs=[
            row(o_a.shape[1]), row(o_b.shape[1]), row(o_c.shape[1]),
            _resident((None, w_out_p.shape[1], d), lambda i: (layer, 0, 0)),
            row(d), vec, vec,
        ],
        out_specs=[row(d), row(d)],
        out_shape=[jax.ShapeDtypeStruct((m, d), F32), jax.ShapeDtypeStruct((m, d), BF16)],
        compiler_params=_params(("parallel",), "out_proj"),
        name="out_proj",
    )(o_a, o_b, o_c, w_out_p, x, g_post, g_next)


def _ffn_kernel(h_ref, wg_ref, wu_ref, wd_ref, y_ref):
    @pl.when(pl.program_id(1) == 0)
    def _():
        y_ref[...] = jnp.zeros(y_ref.shape, F32)

    h = h_ref[...]
    g = jnp.dot(h, wg_ref[...], preferred_element_type=F32)
    u = jnp.dot(h, wu_ref[...], preferred_element_type=F32)
    a = (g * jax.nn.sigmoid(g) * u).astype(BF16)
    y_ref[...] += jnp.dot(a, wd_ref[...], preferred_element_type=F32)


def _ffn_call(h, wg, wu, wd, layer, tm, tf):
    m, d = h.shape
    d_ff = wg.shape[2]
    row = pl.BlockSpec((tm, d), lambda i, f: (i, 0))
    return pl.pallas_call(
        _ffn_kernel,
        grid=(m // tm, d_ff // tf),
        in_specs=[
            row,
            pl.BlockSpec((None, d, tf), lambda i, f: (layer, 0, f)),
            pl.BlockSpec((None, d, tf), lambda i, f: (layer, 0, f)),
            pl.BlockSpec((None, tf, d), lambda i, f: (layer, f, 0)),
        ],
        out_specs=row,
        out_shape=jax.ShapeDtypeStruct((m, d), F32),
        compiler_params=_params(("parallel", "arbitrary"), "ffn"),
        name="ffn",
    )(h, wg, wu, wd)


def _residual_kernel(y_ref, x_ref, g_ref, o_ref):
    o_ref[...] = x_ref[...] + _rms(y_ref[...], g_ref[...])


def _residual_call(y, x, g, tm):
    m, d = x.shape
    row = pl.BlockSpec((tm, d), lambda i: (i, 0))
    return pl.pallas_call(
        _residual_kernel,
        grid=(m // tm,),
        in_specs=[row, row, pl.BlockSpec((1, d), lambda i: (0, 0))],
        out_specs=row,
        out_shape=jax.ShapeDtypeStruct((m, d), F32),
        compiler_params=_params(("parallel",), "ffn_residual"),
        name="ffn_residual",
    )(y, x, g)


def _w_in_layout():
    widths = (A_HEADS * HEAD_DIM, HEAD_DIM, HEAD_DIM, IDX_HEADS * IDX_DIM, IDX_DIM, IDX_HEADS,
              B_HEADS * HEAD_DIM, B_HEADS * HEAD_DIM, B_HEADS * HEAD_DIM, C_HEADS * HEAD_DIM,
              C_KV_GROUPS * HEAD_DIM, C_KV_GROUPS * HEAD_DIM, C_KV_GROUPS * HEAD_DIM, C_KV_GROUPS * HEAD_DIM,
              C_KV_GROUPS * HEAD_DIM, C_KV_GROUPS * HEAD_DIM, 3 * C_HEADS)
    names = ("a_q", "a_k", "a_v", "i_q", "i_k", "i_w", "b_q", "b_k", "b_v", "c_q",
             "c_kc", "c_vc", "c_ks", "c_vs", "c_kw", "c_vw", "c_g")
    offs = np.concatenate([[0], np.cumsum(widths)])
    src = {n: (int(offs[k]), int(widths[k])) for k, n in enumerate(names)}
    pieces = []
    dst = 0
    for name, blocks in [(n, w) for n, w, _, _ in _SEGMENTS] + list(_CMP_SEGMENTS):
        rows = blocks * LANE
        if name == "misc":
            parts, used = [], 0
            for part, (lane0, width) in _MISC_LANES.items():
                assert lane0 == used and src[part][1] == width
                parts.append(src[part])
                used += width
            pieces.append((dst, parts, rows - used))
        else:
            start, width = src[name]
            assert width == rows
            for r0 in range(0, rows, _PROJ_CHUNK * LANE):
                pieces.append((dst + r0, [(start + r0, min(_PROJ_CHUNK * LANE, rows - r0))], 0))
        dst += rows
    return pieces, int(offs[-1])


def _w_prep_kernel(w_ref, o_ref, *, pieces):
    cols = w_ref.shape[1]
    for dst, parts, n_zero in pieces:
        vals = [w_ref[r:r + n, :] for r, n in parts]
        if n_zero:
            vals.append(jnp.zeros((n_zero, cols), F32))
        blk = vals[0] if len(vals) == 1 else jnp.concatenate(vals, axis=0)
        o_ref[dst:dst + blk.shape[0], :] = blk.astype(o_ref.dtype)


def _prep_w_in(w_in, tk=256):
    w_t = jnp.swapaxes(w_in, 1, 2)
    depth, width, d = w_t.shape
    pieces, src_width = _w_in_layout()
    assert width == src_width
    return pl.pallas_call(
        functools.partial(_w_prep_kernel, pieces=pieces),
        grid=(depth, d // tk),
        in_specs=[pl.BlockSpec((None, width, tk), lambda l, i: (l, 0, i))],
        out_specs=pl.BlockSpec((None, NW, tk), lambda l, i: (l, 0, i)),
        out_shape=jax.ShapeDtypeStruct((depth, NW, d), BF16),
        compiler_params=_params(("parallel", "parallel"), "w_in_prep"),
        name="w_in_prep",
    )(w_t)


def _rope_tables(seq):
    def tables(dim):
        inv = 1.0 / (ROPE_THETA ** (jnp.arange(0, dim, 2, dtype=F32) / dim))
        ang = jnp.arange(seq, dtype=F32)[:, None] * inv[None, :]
        cos, sin = jnp.cos(ang), jnp.sin(ang)
        reps = LANE // dim
        return (jnp.tile(jnp.concatenate([cos, cos], axis=-1), (1, reps)),
                jnp.tile(jnp.concatenate([-sin, sin], axis=-1), (1, reps)))
    return tables(HEAD_DIM) + tables(IDX_DIM)


def _selection_constants(seq):
    nc = seq // CMP_STRIDE
    n_slc = seq // SEL_BLOCK
    c_start = np.arange(nc) * CMP_STRIDE
    n_start = np.arange(n_slc) * SEL_BLOCK
    isect = ((c_start[None, :] < n_start[:, None] + SEL_BLOCK)
             & (c_start[None, :] + CMP_BLOCK > n_start[:, None])
             & (np.arange(nc)[None, :] < nc - CMP_BLOCK // CMP_STRIDE + 1)).astype(np.float32)
    expand = np.zeros((seq, max(n_slc, LANE)), np.float32)
    expand[np.arange(seq), np.arange(seq) // SEL_BLOCK] = 1.0
    return jnp.asarray(isect, BF16), jnp.asarray(expand, BF16)


def _tiles(seq):
    return dict(tq=min(2 * LANE, seq), step_dsa=min(4 * LANE, seq), step=min(2 * LANE, seq),
                tm_proj=min(256, seq), tm_out=min(512, seq), tm_ffn=min(1024, seq), tf=512)


def kernel(x, w_in, w_out, cmp_pe_k, cmp_w1_k, cmp_w2_k, cmp_pe_v, cmp_w1_v, cmp_w2_v,
           w_gate, w_up, w_down, g_pre_mix, g_post_mix, g_pre_ffn, g_post_ffn):
    b, s, d = x.shape
    depth = w_in.shape[0]
    m = b * s
    t = _tiles(s)
    tq, step = t["tq"], t["step"]

    w_in_p = _prep_w_in(w_in)
    w_out_p = w_out.astype(BF16)
    wg, wu, wd = w_gate.astype(BF16), w_up.astype(BF16), w_down.astype(BF16)
    w1_k, w2_k = cmp_w1_k.astype(BF16), cmp_w2_k.astype(BF16)
    w1_v, w2_v = cmp_w1_v.astype(BF16), cmp_w2_v.astype(BF16)
    col_scale = jnp.asarray(_COL_SCALE)
    tabs = _rope_tables(s)
    isect, expand = _selection_constants(s)
    vec = lambda g, layer: g[layer][None, :]

    xf = x.reshape(m, d)
    y = None
    for layer in range(depth):
        if layer == 0:
            p, pc = _proj_call((xf, vec(g_pre_mix, 0)), w_in_p, layer, col_scale, tabs, s, t["tm_proj"])
        else:
            rows_in = (y, xf, vec(g_post_ffn, layer - 1), vec(g_pre_mix, layer))
            xf, p, pc = _proj_call(rows_in, w_in_p, layer, col_scale, tabs, s, t["tm_proj"])
        p3 = p.reshape(b, s, NP)
        o_a = _dsa_call(p3, tq, t["step_dsa"])
        o_b = _dil_call(p3, tq, step)
        kc, vct = _cmp_call(pc, b, layer, cmp_pe_k, w1_k, w2_k, cmp_pe_v, w1_v, w2_v)
        o_c = _nsa_call(p3, kc, vct, tabs[0], tabs[1], isect, expand, tq, step)
        xf, h = _out_call(o_a.reshape(m, -1), o_b.reshape(m, -1), o_c.reshape(m, -1), w_out_p, layer,
                          xf, vec(g_post_mix, layer), vec(g_pre_ffn, layer), t["tm_out"])
        y = _ffn_call(h, wg, wu, wd, layer, t["tm_ffn"], t["tf"])
    return _residual_call(y, xf, vec(g_post_ffn, depth - 1), t["tm_out"]).reshape(b, s, d)
```
